```python
import math
import jax, jax.numpy as jnp
from jax import lax
import numpy as np

D_MODEL = 4096
BATCH = 1
SEQ = 8192
DEPTH = 1

N_MEM = 256
Q_BLOCK = 128
NORM_EPS = 1e-6
NEG_INF = -1e30

DIFF_HEADS = 6
DIFF_HEAD_DIM = 128
DIFF_V_DIM = 2 * DIFF_HEAD_DIM
DIFF_MAPS = 2 * DIFF_HEADS
DIFF_QK_WIDTH = DIFF_MAPS * DIFF_HEAD_DIM
DIFF_WIDTH = DIFF_HEADS * DIFF_V_DIM

MLA_HEADS = 12
MLA_Q_RANK = 1536
MLA_KV_RANK = 512
MLA_NOPE_DIM = 128
MLA_ROPE_DIM = 64
MLA_QK_DIM = MLA_NOPE_DIM + MLA_ROPE_DIM
MLA_V_DIM = 128
MLA_WIDTH = MLA_HEADS * MLA_V_DIM
ROPE_THETA = 10000.0

MEM_HEADS = 4
MEM_HEAD_DIM = 256
MEM_WIDTH = MEM_HEADS * MEM_HEAD_DIM

N_BRANCHES = 3

REL_BUCKETS = 32
REL_MAX_DIST = 128

N_GROUPS = 8
EXPERTS_PER_GROUP = 8
N_EXPERTS = N_GROUPS * EXPERTS_PER_GROUP
TOP_K = 2
EXPERT_FF = 512
MOE_BLOCK = 128

IN_SPLITS = (DIFF_QK_WIDTH, DIFF_QK_WIDTH, DIFF_WIDTH, MLA_Q_RANK, MLA_KV_RANK,
             MLA_ROPE_DIM, MEM_WIDTH, N_BRANCHES * D_MODEL)
IN_COLS = sum(IN_SPLITS)

kernel_name = "hybrid_diffattn_mla_memxattn_hiermoe"


def rms_norm(x, g):
    xf = x.astype(jnp.float32)
    y = xf * lax.rsqrt(jnp.mean(xf * xf, axis=-1, keepdims=True) + NORM_EPS)
    return (y * g.astype(jnp.float32)).astype(x.dtype)


def rotary(x, positions):
    half = x.shape[-1] // 2
    inv_freq = ROPE_THETA ** (-jnp.arange(half, dtype=jnp.float32) / half)
    ang = positions.astype(jnp.float32)[..., None] * inv_freq
    cos = jnp.cos(ang)[:, :, None, :]
    sin = jnp.sin(ang)[:, :, None, :]
    xf = x.astype(jnp.float32)
    x1, x2 = xf[..., :half], xf[..., half:]
    return jnp.concatenate([x1 * cos - x2 * sin, x2 * cos + x1 * sin], axis=-1).astype(x.dtype)


def t5_bucket(dist):
    n = jnp.maximum(dist, 0)
    max_exact = REL_BUCKETS // 2
    nf = jnp.maximum(n, 1).astype(jnp.float32)
    large = max_exact + (jnp.log(nf / max_exact) / math.log(REL_MAX_DIST / max_exact)
                         * (REL_BUCKETS - max_exact)).astype(jnp.int32)
    large = jnp.minimum(large, REL_BUCKETS - 1)
    return jnp.where(n < max_exact, n, large)


def t5_bias(pos_q, pos_k, table):
    bucket = t5_bucket(pos_q[:, :, None] - pos_k[:, None, :])
    return jnp.transpose(table.astype(jnp.float32)[bucket], (0, 3, 1, 2))


def to_blocks(t):
    b, h, s, d = t.shape
    return t.reshape(b, h, s // Q_BLOCK, Q_BLOCK, d).transpose(2, 0, 1, 3, 4)


def from_blocks(t):
    nb, b, h, l, d = t.shape
    return t.transpose(1, 0, 3, 2, 4).reshape(b, nb * l, h, d)


def pos_blocks(positions):
    b, s = positions.shape
    return positions.reshape(b, s // Q_BLOCK, Q_BLOCK).transpose(1, 0, 2)


def causal_softmax(s, pos_q, pos_k):
    mask = pos_k[:, None, None, :] <= pos_q[:, None, :, None]
    return jax.nn.softmax(jnp.where(mask, s, NEG_INF), axis=-1)


def diff_attention(q, k, v, positions, rel_bias, lam_q1, lam_k1, lam_q2, lam_k2,
                   q_g, k_g, subln_g, layer_idx):
    b, s, _ = q.shape
    q = rms_norm(q.reshape(b, s, DIFF_MAPS, DIFF_HEAD_DIM), q_g).transpose(0, 2, 1, 3)
    k = rms_norm(k.reshape(b, s, DIFF_MAPS, DIFF_HEAD_DIM), k_g).transpose(0, 2, 1, 3)
    v = v.reshape(b, s, DIFF_HEADS, DIFF_V_DIM).transpose(0, 2, 1, 3)
    lam_init = 0.8 - 0.6 * math.exp(-0.3 * layer_idx)
    lam = (jnp.exp(jnp.sum(lam_q1.astype(jnp.float32) * lam_k1.astype(jnp.float32)))
           - jnp.exp(jnp.sum(lam_q2.astype(jnp.float32) * lam_k2.astype(jnp.float32)))
           + lam_init)
    scale = DIFF_HEAD_DIM ** -0.5

    def block(args):
        qb, pb = args
        sc = jnp.einsum('bmqd,bmkd->bmqk', qb, k).astype(jnp.float32) * scale
        sc = sc + t5_bias(pb, positions, rel_bias)
        p = causal_softmax(sc, pb, positions).reshape(b, 2, DIFF_HEADS, Q_BLOCK, s)
        a = (p[:, 0] - lam * p[:, 1]).astype(v.dtype)
        return jnp.einsum('bhqk,bhkv->bhqv', a, v)

    o = from_blocks(lax.map(block, (to_blocks(q), pos_blocks(positions))))
    o = rms_norm(o, subln_g) * (1.0 - lam_init)
    return o.reshape(b, s, DIFF_WIDTH)


def latent_attention(c_q, c_kv, k_rope, positions, cq_g, ckv_g, w_uq, w_ukv, q_g, k_g):
    b, s, _ = c_q.shape
    q = (rms_norm(c_q, cq_g) @ w_uq).reshape(b, s, MLA_HEADS, MLA_QK_DIM)
    kv = (rms_norm(c_kv, ckv_g) @ w_ukv).reshape(b, s, MLA_HEADS, MLA_NOPE_DIM + MLA_V_DIM)
    k_nope, v = kv[..., :MLA_NOPE_DIM], kv[..., MLA_NOPE_DIM:]
    k = jnp.concatenate(
        [k_nope, jnp.broadcast_to(k_rope[:, :, None, :], (b, s, MLA_HEADS, MLA_ROPE_DIM))], axis=-1)
    q = rms_norm(q, q_g)
    k = rms_norm(k, k_g)
    q = jnp.concatenate([q[..., :MLA_NOPE_DIM], rotary(q[..., MLA_NOPE_DIM:], positions)], axis=-1)
    k = jnp.concatenate([k[..., :MLA_NOPE_DIM], rotary(k[..., MLA_NOPE_DIM:], positions)], axis=-1)
    q = q.transpose(0, 2, 1, 3)
    k = k.transpose(0, 2, 1, 3)
    v = v.transpose(0, 2, 1, 3)
    scale = MLA_QK_DIM ** -0.5

    def block(args):
        qb, pb = args
        sc = jnp.einsum('bhqd,bhkd->bhqk', qb, k).astype(jnp.float32) * scale
        p = causal_softmax(sc, pb, positions).astype(v.dtype)
        return jnp.einsum('bhqk,bhkv->bhqv', p, v)

    o = from_blocks(lax.map(block, (to_blocks(q), pos_blocks(positions))))
    return o.reshape(b, s, MLA_WIDTH)


def memory_attention(q, mem, mem_g, w_mem_kv, q_g, k_g):
    b, s, _ = q.shape
    n = mem.shape[1]
    q = rms_norm(q.reshape(b, s, MEM_HEADS, MEM_HEAD_DIM), q_g)
    kv = rms_norm(mem, mem_g) @ w_mem_kv
    k = rms_norm(kv[..., :MEM_WIDTH].reshape(b, n, MEM_HEADS, MEM_HEAD_DIM), k_g)
    v = kv[..., MEM_WIDTH:].reshape(b, n, MEM_HEADS, MEM_HEAD_DIM)
    sc = jnp.einsum('bshd,bnhd->bhsn', q, k).astype(jnp.float32) * (MEM_HEAD_DIM ** -0.5)
    p = jax.nn.softmax(sc, axis=-1).astype(v.dtype)
    return jnp.einsum('bhsn,bnhd->bshd', p, v).reshape(b, s, MEM_WIDTH)


def hierarchical_moe(h, w_rg, b_rg, w_re, b_re, w_gate, w_up, w_down):
    t, d = h.shape
    p_group = jax.nn.softmax((h @ w_rg).astype(jnp.float32) + b_rg.astype(jnp.float32), axis=-1)
    pg_top, g_idx = lax.top_k(p_group, 1)
    logits_e = ((h @ w_re).astype(jnp.float32) + b_re.astype(jnp.float32)).reshape(
        t, N_GROUPS, EXPERTS_PER_GROUP)
    logits_sel = logits_e[jnp.arange(t), g_idx[:, 0]]
    p_e = jax.nn.softmax(logits_sel, axis=-1)
    pe_top, e_idx = lax.top_k(p_e, TOP_K)
    gate = pg_top * pe_top / jnp.sum(pe_top, axis=-1, keepdims=True)
    expert = g_idx * EXPERTS_PER_GROUP + e_idx

    a = t * TOP_K
    flat_e = expert.reshape(a).astype(jnp.int32)
    flat_tok = jnp.repeat(jnp.arange(t, dtype=jnp.int32), TOP_K)
    flat_w = gate.reshape(a)
    order = jnp.argsort(flat_e)
    sorted_e = flat_e[order]
    counts = jnp.bincount(flat_e, length=N_EXPERTS)
    padded = (counts + MOE_BLOCK - 1) // MOE_BLOCK * MOE_BLOCK
    pad_end = jnp.cumsum(padded)
    pad_start = pad_end - padded
    grp_start = jnp.cumsum(counts) - counts
    dest = pad_start[sorted_e] + jnp.arange(a, dtype=jnp.int32) - grp_start[sorted_e]
    n_blocks = (a + MOE_BLOCK - 1) // MOE_BLOCK + N_EXPERTS
    p_rows = n_blocks * MOE_BLOCK
    buf_tok = jnp.zeros((p_rows,), jnp.int32).at[dest].set(flat_tok[order])
    buf_w = jnp.zeros((p_rows,), jnp.float32).at[dest].set(flat_w[order])
    block_e = jnp.minimum(
        jnp.searchsorted(pad_end, jnp.arange(n_blocks, dtype=pad_end.dtype) * MOE_BLOCK, side='right'),
        N_EXPERTS - 1)

    def expert_block(args):
        tok, e = args
        xb = h[tok]
        act = jax.nn.silu(xb @ w_gate[e]) * (xb @ w_up[e])
        return act @ w_down[e]

    yb = lax.map(expert_block, (buf_tok.reshape(n_blocks, MOE_BLOCK), block_e)).reshape(p_rows, d)
    yb = yb * buf_w[:, None].astype(yb.dtype)
    return jnp.zeros((t, d), h.dtype).at[buf_tok].add(yb)


def setup_inputs(seed: int = 0) -> dict:
    key = jax.random.key(seed)
    ks = iter(jax.random.split(key, 48))

    def nrm(shape, scale):
        return jax.random.normal(next(ks), shape, jnp.float32) * scale

    def gain(shape):
        return 1.0 + nrm(shape, 0.02)

    L, D = DEPTH, D_MODEL
    return {
        "x": nrm((BATCH, SEQ, D), 1.0),
        "mem": nrm((BATCH, N_MEM, D), 1.0),
        "positions": jnp.broadcast_to(jnp.arange(SEQ, dtype=jnp.int32), (BATCH, SEQ)),
        "rel_bias": nrm((REL_BUCKETS, DIFF_MAPS), 0.1),
        "mix_norm_g": gain((L, D)),
        "w_in": nrm((L, D, IN_COLS), D ** -0.5),
        "diff_q_norm_g": gain((L, DIFF_HEAD_DIM)),
        "diff_k_norm_g": gain((L, DIFF_HEAD_DIM)),
        "diff_lambda_q1": nrm((L, DIFF_HEAD_DIM), 0.1),
        "diff_lambda_k1": nrm((L, DIFF_HEAD_DIM), 0.1),
        "diff_lambda_q2": nrm((L, DIFF_HEAD_DIM), 0.1),
        "diff_lambda_k2": nrm((L, DIFF_HEAD_DIM), 0.1),
        "diff_subln_g": gain((L, DIFF_V_DIM)),
        "mla_cq_norm_g": gain((L, MLA_Q_RANK)),
        "mla_ckv_norm_g": gain((L, MLA_KV_RANK)),
        "mla_w_uq": nrm((L, MLA_Q_RANK, MLA_HEADS * MLA_QK_DIM), MLA_Q_RANK ** -0.5),
        "mla_w_ukv": nrm((L, MLA_KV_RANK, MLA_HEADS * (MLA_NOPE_DIM + MLA_V_DIM)), MLA_KV_RANK ** -0.5),
        "mla_q_norm_g": gain((L, MLA_QK_DIM)),
        "mla_k_norm_g": gain((L, MLA_QK_DIM)),
        "mem_norm_g": gain((L, D)),
        "mem_w_kv": nrm((L, D, 2 * MEM_WIDTH), D ** -0.5),
        "mem_q_norm_g": gain((L, MEM_HEAD_DIM)),
        "mem_k_norm_g": gain((L, MEM_HEAD_DIM)),
        "w_o_diff": nrm((L, DIFF_WIDTH, D), DIFF_WIDTH ** -0.5),
        "w_o_mla": nrm((L, MLA_WIDTH, D), MLA_WIDTH ** -0.5),
        "w_o_mem": nrm((L, MEM_WIDTH, D), MEM_WIDTH ** -0.5),
        "w_out": nrm((L, D, D), D ** -0.5),
        "ffn_norm_g": gain((L, D)),
        "w_route_group": nrm((L, D, N_GROUPS), D ** -0.5),
        "b_route_group": nrm((L, N_GROUPS), 0.01),
        "w_route_expert": nrm((L, D, N_EXPERTS), D ** -0.5),
        "b_route_expert": nrm((L, N_EXPERTS), 0.01),
        "w_exp_gate": nrm((L, N_EXPERTS, D, EXPERT_FF), D ** -0.5),
        "w_exp_up": nrm((L, N_EXPERTS, D, EXPERT_FF), D ** -0.5),
        "w_exp_down": nrm((L, N_EXPERTS, EXPERT_FF, D), EXPERT_FF ** -0.5),
    }


def reference(x, mem, positions, rel_bias, mix_norm_g, w_in, diff_q_norm_g, diff_k_norm_g,
              diff_lambda_q1, diff_lambda_k1, diff_lambda_q2, diff_lambda_k2, diff_subln_g,
              mla_cq_norm_g, mla_ckv_norm_g, mla_w_uq, mla_w_ukv, mla_q_norm_g, mla_k_norm_g,
              mem_norm_g, mem_w_kv, mem_q_norm_g, mem_k_norm_g, w_o_diff, w_o_mla, w_o_mem,
              w_out, ffn_norm_g, w_route_group, b_route_group, w_route_expert, b_route_expert,
              w_exp_gate, w_exp_up, w_exp_down):
    b, s, d = x.shape
    offsets = [int(o) for o in np.cumsum(IN_SPLITS)[:-1]]
    for l in range(DEPTH):
        h = rms_norm(x, mix_norm_g[l])
        proj = h @ w_in[l]
        dq, dk, dv, c_q, c_kv, k_rope, mq, gate_logits = jnp.split(proj, offsets, axis=-1)
        y_diff = diff_attention(dq, dk, dv, positions, rel_bias,
                                diff_lambda_q1[l], diff_lambda_k1[l], diff_lambda_q2[l],
                                diff_lambda_k2[l], diff_q_norm_g[l], diff_k_norm_g[l],
                                diff_subln_g[l], l) @ w_o_diff[l]
        y_mla = latent_attention(c_q, c_kv, k_rope, positions, mla_cq_norm_g[l], mla_ckv_norm_g[l],
                                 mla_w_uq[l], mla_w_ukv[l], mla_q_norm_g[l],
                                 mla_k_norm_g[l]) @ w_o_mla[l]
        y_mem = memory_attention(mq, mem, mem_norm_g[l], mem_w_kv[l], mem_q_norm_g[l],
                                 mem_k_norm_g[l]) @ w_o_mem[l]
        gates = jax.nn.sigmoid(gate_logits.astype(jnp.float32)).astype(x.dtype).reshape(
            b, s, N_BRANCHES, d)
        mixed = gates[:, :, 0] * y_diff + gates[:, :, 1] * y_mla + gates[:, :, 2] * y_mem
        x = x + mixed @ w_out[l]
        h2 = rms_norm(x, ffn_norm_g[l]).reshape(b * s, d)
        y_ffn = hierarchical_moe(h2, w_route_group[l], b_route_group[l], w_route_expert[l],
                                 b_route_expert[l], w_exp_gate[l], w_exp_up[l], w_exp_down[l])
        x = x + y_ffn.reshape(b, s, d)
    return x
```

```python
import functools
import math

import jax
import jax.numpy as jnp
import numpy as np
from jax import lax
from jax.experimental import pallas as pl
from jax.experimental.pallas import tpu as pltpu

F32 = jnp.float32
BF16 = jnp.bfloat16

NORM_EPS = 1e-6
NEG_INF = -1e30

DIFF_HEADS = 6
DIFF_HEAD_DIM = 128
DIFF_V_DIM = 256
DIFF_MAPS = 12
DIFF_QK_WIDTH = 1536
DIFF_WIDTH = 1536
MLA_HEADS = 12
MLA_Q_RANK = 1536
MLA_KV_RANK = 512
MLA_NOPE_DIM = 128
MLA_ROPE_DIM = 64
MLA_QK_DIM = 192
MLA_QK_PAD = 256
MLA_V_DIM = 128
MLA_WIDTH = 1536
ROPE_THETA = 10000.0
MEM_HEADS = 4
MEM_HEAD_DIM = 256
MEM_WIDTH = 1024
REL_BUCKETS = 32
REL_MAX_DIST = 128
N_GROUPS = 8
EXPERTS_PER_GROUP = 8
N_EXPERTS = 64
TOP_K = 2
EXPERT_FF = 512

OFF_DQ = 0
OFF_DK = 1536
OFF_DV = 3072
OFF_CQ = 4608
OFF_CKV = 6144
OFF_KROPE = 6656
OFF_MQ = 6720
OFF_GATES = 7744
PROJ_WIDTH = 8192

LANE = 128
VMEM_LIMIT = 52 * 1024 * 1024

ATT_T = 512
MOE_TB = 256
MOE_SUB = 128
MOE_NF = 2
ROUTE_W = 128
MEM_WINDOW = 2048


def _cp(sem, vmem=VMEM_LIMIT):
    return pltpu.CompilerParams(dimension_semantics=sem, vmem_limit_bytes=vmem)


def _rmsnorm_kernel(x_ref, g_ref, o_ref):
    x = x_ref[...].astype(F32)
    ms = jnp.mean(x * x, axis=-1, keepdims=True)
    o_ref[...] = (x * lax.rsqrt(ms + NORM_EPS) * g_ref[...]).astype(o_ref.dtype)


def _rmsnorm_rows(x, g, tm, out_dtype=BF16):
    m, d = x.shape
    return pl.pallas_call(
        _rmsnorm_kernel,
        grid=(m // tm,),
        in_specs=[pl.BlockSpec((tm, d), lambda i: (i, 0)),
                  pl.BlockSpec((1, d), lambda i: (0, 0))],
        out_specs=pl.BlockSpec((tm, d), lambda i: (i, 0)),
        out_shape=jax.ShapeDtypeStruct((m, d), out_dtype),
        compiler_params=_cp(("parallel",)),
        name="rmsnorm_rows",
    )(x, g.reshape(1, d).astype(F32))


def _group_rmsnorm_store(acc, gain_ref, o_ref, group):
    tn = acc.shape[1]
    for c in range(tn // group):
        blk = acc[:, c * group:(c + 1) * group]
        ms = jnp.mean(blk * blk, axis=-1, keepdims=True)
        o_ref[:, c * group:(c + 1) * group] = (
            blk * lax.rsqrt(ms + NORM_EPS) * gain_ref[:, c * group:(c + 1) * group]).astype(o_ref.dtype)


def _mm_kernel(*refs, mode, cast, group, n_norm_tiles):
    a_ref, w_ref = refs[0], refs[1]
    pos = 2
    extra = None
    if mode in ("groupnorm", "residual"):
        extra = refs[pos]
        pos += 1
    o_ref = refs[pos]
    wb_ref = refs[pos + 1] if cast else None
    j = pl.program_id(0)
    i = pl.program_id(1)
    if cast:
        @pl.when(i == 0)
        def _():
            wb_ref[...] = w_ref[...].astype(BF16)
        w = wb_ref[...]
    else:
        w = w_ref[...]
    acc = jnp.dot(a_ref[...], w, preferred_element_type=F32)
    if mode == "plain":
        o_ref[...] = acc.astype(o_ref.dtype)
    elif mode == "sigmoid":
        o_ref[...] = jax.nn.sigmoid(acc).astype(o_ref.dtype)
    elif mode == "residual":
        o_ref[...] = (extra[...] + acc).astype(o_ref.dtype)
    elif mode == "groupnorm":
        @pl.when(j < n_norm_tiles)
        def _():
            _group_rmsnorm_store(acc, extra, o_ref, group)

        @pl.when(j >= n_norm_tiles)
        def _():
            o_ref[...] = acc.astype(o_ref.dtype)


def _matmul(a, w, *, n_cols, tm, tn, out_dtype, mode="plain", extra=None, group=LANE,
            n_norm_tiles=0, name="matmul"):
    m, k = a.shape
    assert m % tm == 0 and n_cols % tn == 0 and w.shape[0] == k
    cast = w.dtype != BF16
    in_specs = [pl.BlockSpec((tm, k), lambda j, i: (i, 0)),
                pl.BlockSpec((k, tn), lambda j, i: (0, j))]
    args = [a, w]
    if mode == "groupnorm":
        last = n_norm_tiles - 1
        in_specs.append(pl.BlockSpec((1, tn), lambda j, i: (0, jnp.minimum(j, last))))
        args.append(extra)
    elif mode == "residual":
        in_specs.append(pl.BlockSpec((tm, tn), lambda j, i: (i, j)))
        args.append(extra)
    scratch = [pltpu.VMEM((k, tn), BF16)] if cast else []
    return pl.pallas_call(
        functools.partial(_mm_kernel, mode=mode, cast=cast, group=group, n_norm_tiles=n_norm_tiles),
        grid=(n_cols // tn, m // tm),
        in_specs=in_specs,
        out_specs=pl.BlockSpec((tm, tn), lambda j, i: (i, j)),
        out_shape=jax.ShapeDtypeStruct((m, n_cols), out_dtype),
        scratch_shapes=scratch,
        compiler_params=_cp(("arbitrary", "arbitrary")),
        name=name,
    )(*args)


def _online_update(idx, s, v_blk, m_ref, l_ref, acc_ref):
    m_prev = m_ref[idx]
    m_new = jnp.maximum(m_prev, jnp.max(s, axis=-1, keepdims=True))
    alpha = jnp.exp(m_prev - m_new)
    p = jnp.exp(s - m_new)
    l_ref[idx] = alpha * l_ref[idx] + jnp.sum(p, axis=-1, keepdims=True)
    acc_ref[idx] = alpha * acc_ref[idx] + jnp.dot(p.astype(BF16), v_blk, preferred_element_type=F32)
    m_ref[idx] = m_new


def _qk(q, k_blk):
    return lax.dot_general(q, k_blk, (((1,), (1,)), ((), ())), preferred_element_type=F32)


def _diff_attn_kernel(lq1_ref, lk1_ref, lq2_ref, lk2_ref, q1_ref, q2_ref, k1_ref, k2_ref, v_ref,
                      b1_ref, b2_ref, g_ref, o_ref, m_ref, l_ref, acc_ref, *, lam_init):
    t = q1_ref.shape[0]
    i = pl.program_id(1)
    qs = (q1_ref[...], q2_ref[...])
    ks = (k1_ref, k2_ref)
    bs = (b1_ref, b2_ref)
    m_ref[...] = jnp.full(m_ref.shape, NEG_INF, F32)
    l_ref[...] = jnp.zeros(l_ref.shape, F32)
    acc_ref[...] = jnp.zeros(acc_ref.shape, F32)

    def block(kb, bias_idx):
        off = pl.multiple_of(kb * t, t)
        v_blk = v_ref[pl.ds(off, t), :]
        for mp in range(2):
            s = _qk(qs[mp], ks[mp][pl.ds(off, t), :])
            if bias_idx is not None:
                s = s + bs[mp][0, bias_idx]
            _online_update(mp, s, v_blk, m_ref, l_ref, acc_ref)

    def far(kb, c):
        block(kb, None)
        return c

    lax.fori_loop(0, jnp.maximum(i - 1, 0), far, 0)

    @pl.when(i >= 1)
    def _():
        block(i - 1, 1)

    block(i, 0)

    lam = (jnp.exp(jnp.sum(lq1_ref[...] * lk1_ref[...], axis=-1, keepdims=True))
           - jnp.exp(jnp.sum(lq2_ref[...] * lk2_ref[...], axis=-1, keepdims=True)) + lam_init)
    o = acc_ref[0] / l_ref[0] - lam * (acc_ref[1] / l_ref[1])
    ms = jnp.mean(o * o, axis=-1, keepdims=True)
    o_ref[...] = ((o * lax.rsqrt(ms + NORM_EPS) * g_ref[...]) * (1.0 - lam_init)).astype(o_ref.dtype)


def _diff_attention(proj, bias_tiles, lam_vecs, subln_g, lam_init):
    s = proj.shape[0]
    t = ATT_T
    hd, vd = DIFF_HEAD_DIM, DIFF_V_DIM
    vec = pl.BlockSpec((1, hd), lambda h, i: (0, 0))
    in_specs = [vec, vec, vec, vec,
                pl.BlockSpec((t, hd), lambda h, i: (i, OFF_DQ // hd + h)),
                pl.BlockSpec((t, hd), lambda h, i: (i, OFF_DQ // hd + DIFF_HEADS + h)),
                pl.BlockSpec((s, hd), lambda h, i: (0, OFF_DK // hd + h)),
                pl.BlockSpec((s, hd), lambda h, i: (0, OFF_DK // hd + DIFF_HEADS + h)),
                pl.BlockSpec((s, vd), lambda h, i: (0, OFF_DV // vd + h)),
                pl.BlockSpec((1, 2, t, t), lambda h, i: (h, 0, 0, 0)),
                pl.BlockSpec((1, 2, t, t), lambda h, i: (DIFF_HEADS + h, 0, 0, 0)),
                pl.BlockSpec((1, vd), lambda h, i: (0, 0))]
    return pl.pallas_call(
        functools.partial(_diff_attn_kernel, lam_init=lam_init),
        grid=(DIFF_HEADS, s // t),
        in_specs=in_specs,
        out_specs=pl.BlockSpec((t, vd), lambda h, i: (i, h)),
        out_shape=jax.ShapeDtypeStruct((s, DIFF_WIDTH), BF16),
        scratch_shapes=[pltpu.VMEM((2, t, 1), F32), pltpu.VMEM((2, t, 1), F32),
                        pltpu.VMEM((2, t, vd), F32)],
        compiler_params=_cp(("arbitrary", "arbitrary")),
        name="diff_attention",
    )(*lam_vecs, proj, proj, proj, proj, proj, bias_tiles, bias_tiles, subln_g)


def _t5_bucket(dist):
    n = jnp.maximum(dist, 0)
    max_exact = REL_BUCKETS // 2
    nf = jnp.maximum(n, 1).astype(F32)
    large = max_exact + (jnp.log(nf / max_exact) / math.log(REL_MAX_DIST / max_exact)
                         * (REL_BUCKETS - max_exact)).astype(jnp.int32)
    large = jnp.minimum(large, REL_BUCKETS - 1)
    return jnp.where(n < max_exact, n, large)


def _diff_bias_tiles(rel_bias, t):
    assert t >= REL_MAX_DIST
    table = rel_bias.astype(F32)
    r = jnp.arange(t, dtype=jnp.int32)[:, None]
    c = jnp.arange(t, dtype=jnp.int32)[None, :]
    far = table[REL_BUCKETS - 1][:, None, None]
    d0 = r - c
    tile0 = jnp.transpose(table[_t5_bucket(d0)], (2, 0, 1)) - far
    tile0 = jnp.where((d0 >= 0)[None], tile0, NEG_INF)
    tile1 = jnp.transpose(table[_t5_bucket(d0 + t)], (2, 0, 1)) - far
    return jnp.stack([tile0, tile1], axis=1)


def _rope_apply(tv, c_ref, s1_ref, s2_ref):
    return (tv * c_ref[...] + pltpu.roll(tv, 96, 1) * s1_ref[...] + pltpu.roll(tv, 32, 1) * s2_ref[...])


def _mla_q_kernel(cq_ref, g_ref, w_ref, qg_ref, c_ref, s1_ref, s2_ref, o_ref, xg_ref):
    h = pl.program_id(1)

    @pl.when(h == 0)
    def _():
        c = cq_ref[...].astype(F32)
        r = lax.rsqrt(jnp.mean(c * c, axis=-1, keepdims=True) + NORM_EPS)
        xg_ref[...] = (c * r * g_ref[...]).astype(BF16)

    u = jnp.dot(xg_ref[...], w_ref[0], preferred_element_type=F32)
    ms = jnp.sum(u * u, axis=-1, keepdims=True) * (1.0 / MLA_QK_DIM)
    qn = u * lax.rsqrt(ms + NORM_EPS) * qg_ref[...]
    o_ref[0, :, :MLA_NOPE_DIM] = qn[:, :MLA_NOPE_DIM].astype(o_ref.dtype)
    o_ref[0, :, MLA_NOPE_DIM:] = _rope_apply(qn[:, MLA_NOPE_DIM:], c_ref, s1_ref, s2_ref).astype(o_ref.dtype)


def _mla_q_prep(proj, cq_g, w_uq_heads, qg_pad, rope_tabs, tm):
    s = proj.shape[0]
    tab = pl.BlockSpec((tm, LANE), lambda i, h: (i, 0))
    return pl.pallas_call(
        _mla_q_kernel,
        grid=(s // tm, MLA_HEADS),
        in_specs=[pl.BlockSpec((tm, MLA_Q_RANK), lambda i, h: (i, OFF_CQ // MLA_Q_RANK)),
                  pl.BlockSpec((1, MLA_Q_RANK), lambda i, h: (0, 0)),
                  pl.BlockSpec((1, MLA_Q_RANK, MLA_QK_PAD), lambda i, h: (h, 0, 0)),
                  pl.BlockSpec((1, MLA_QK_PAD), lambda i, h: (0, 0)),
                  tab, tab, tab],
        out_specs=pl.BlockSpec((1, tm, MLA_QK_PAD), lambda i, h: (h, i, 0)),
        out_shape=jax.ShapeDtypeStruct((MLA_HEADS, s, MLA_QK_PAD), BF16),
        scratch_shapes=[pltpu.VMEM((tm, MLA_Q_RANK), BF16)],
        compiler_params=_cp(("arbitrary", "arbitrary")),
        name="mla_q_prep",
    )(proj, cq_g, w_uq_heads, qg_pad, *rope_tabs)


def _mla_kv_kernel(ckv_ref, kr_ref, g_ref, w_ref, kgn_ref, kgr_ref, c_ref, s1_ref, s2_ref,
                   k_ref, v_ref, xg_ref):
    h = pl.program_id(1)

    @pl.when(h == 0)
    def _():
        c = ckv_ref[...].astype(F32)
        r = lax.rsqrt(jnp.mean(c * c, axis=-1, keepdims=True) + NORM_EPS)
        xg_ref[...] = (c * r * g_ref[...]).astype(BF16)

    kv = jnp.dot(xg_ref[...], w_ref[...].astype(BF16), preferred_element_type=F32)
    kn = kv[:, :MLA_NOPE_DIM]
    lane = lax.broadcasted_iota(jnp.int32, kr_ref.shape, 1)
    kr = jnp.where(lane < MLA_ROPE_DIM, kr_ref[...].astype(F32), 0.0)
    ms = (jnp.sum(kn * kn, axis=-1, keepdims=True) + jnp.sum(kr * kr, axis=-1, keepdims=True)) * (1.0 / MLA_QK_DIM)
    rs = lax.rsqrt(ms + NORM_EPS)
    k_ref[0, :, :MLA_NOPE_DIM] = (kn * rs * kgn_ref[...]).astype(k_ref.dtype)
    k_ref[0, :, MLA_NOPE_DIM:] = _rope_apply(kr * rs * kgr_ref[...], c_ref, s1_ref, s2_ref).astype(k_ref.dtype)
    v_ref[0] = kv[:, MLA_NOPE_DIM:].astype(v_ref.dtype)


def _mla_kv_prep(proj, ckv_g, w_ukv, kg_nope, kg_rope_pad, rope_tabs, tm):
    s = proj.shape[0]
    tab = pl.BlockSpec((tm, LANE), lambda i, h: (i, 0))
    hw = MLA_NOPE_DIM + MLA_V_DIM
    return pl.pallas_call(
        _mla_kv_kernel,
        grid=(s // tm, MLA_HEADS),
        in_specs=[pl.BlockSpec((tm, MLA_KV_RANK), lambda i, h: (i, OFF_CKV // MLA_KV_RANK)),
                  pl.BlockSpec((tm, LANE), lambda i, h: (i, OFF_KROPE // LANE)),
                  pl.BlockSpec((1, MLA_KV_RANK), lambda i, h: (0, 0)),
                  pl.BlockSpec((MLA_KV_RANK, hw), lambda i, h: (0, h)),
                  pl.BlockSpec((1, LANE), lambda i, h: (0, 0)),
                  pl.BlockSpec((1, LANE), lambda i, h: (0, 0)),
                  tab, tab, tab],
        out_specs=[pl.BlockSpec((1, tm, MLA_QK_PAD), lambda i, h: (h, i, 0)),
                   pl.BlockSpec((1, tm, MLA_V_DIM), lambda i, h: (h, i, 0))],
        out_shape=[jax.ShapeDtypeStruct((MLA_HEADS, s, MLA_QK_PAD), BF16),
                   jax.ShapeDtypeStruct((MLA_HEADS, s, MLA_V_DIM), BF16)],
        scratch_shapes=[pltpu.VMEM((tm, MLA_KV_RANK), BF16)],
        compiler_params=_cp(("arbitrary", "arbitrary")),
        name="mla_kv_prep",
    )(proj, proj, ckv_g, w_ukv, kg_nope, kg_rope_pad, *rope_tabs)


def _mla_attn_kernel(q_ref, k_ref, v_ref, o_ref, m_ref, l_ref, acc_ref):
    t = q_ref.shape[1]
    i = pl.program_id(1)
    q = q_ref[0]
    m_ref[...] = jnp.full(m_ref.shape, NEG_INF, F32)
    l_ref[...] = jnp.zeros(l_ref.shape, F32)
    acc_ref[...] = jnp.zeros(acc_ref.shape, F32)

    def block(kb, diag):
        off = pl.multiple_of(kb * t, t)
        s = _qk(q, k_ref[0, pl.ds(off, t), :])
        if diag:
            row = lax.broadcasted_iota(jnp.int32, s.shape, 0)
            col = lax.broadcasted_iota(jnp.int32, s.shape, 1)
            s = jnp.where(col <= row, s, NEG_INF)
        _online_update(0, s, v_ref[0, pl.ds(off, t), :], m_ref, l_ref, acc_ref)

    def far(kb, c):
        block(kb, False)
        return c

    lax.fori_loop(0, i, far, 0)
    block(i, True)
    o_ref[...] = (acc_ref[0] / l_ref[0]).astype(o_ref.dtype)


def _mla_attention(q, k, v):
    s = q.shape[1]
    t = ATT_T
    return pl.pallas_call(
        _mla_attn_kernel,
        grid=(MLA_HEADS, s // t),
        in_specs=[pl.BlockSpec((1, t, MLA_QK_PAD), lambda h, i: (h, i, 0)),
                  pl.BlockSpec((1, s, MLA_QK_PAD), lambda h, i: (h, 0, 0)),
                  pl.BlockSpec((1, s, MLA_V_DIM), lambda h, i: (h, 0, 0))],
        out_specs=pl.BlockSpec((t, MLA_V_DIM), lambda h, i: (i, h)),
        out_shape=jax.ShapeDtypeStruct((s, MLA_WIDTH), BF16),
        scratch_shapes=[pltpu.VMEM((1, t, 1), F32), pltpu.VMEM((1, t, 1), F32),
                        pltpu.VMEM((1, t, MLA_V_DIM), F32)],
        compiler_params=_cp(("arbitrary", "arbitrary")),
        name="mla_attention",
    )(q, k, v)


def _mem_attn_kernel(q_ref, kv_ref, qg_ref, o_ref):
    shift = OFF_MQ % MEM_WINDOW
    qall = q_ref[...].astype(F32)[:, shift:shift + MEM_WIDTH]
    for h in range(MEM_HEADS):
        lo = h * MEM_HEAD_DIM
        qh = qall[:, lo:lo + MEM_HEAD_DIM]
        ms = jnp.mean(qh * qh, axis=-1, keepdims=True)
        qn = (qh * lax.rsqrt(ms + NORM_EPS) * qg_ref[...]).astype(BF16)
        s = _qk(qn, kv_ref[:, lo:lo + MEM_HEAD_DIM])
        p = jnp.exp(s - jnp.max(s, axis=-1, keepdims=True))
        l = jnp.sum(p, axis=-1, keepdims=True)
        o = jnp.dot(p.astype(BF16), kv_ref[:, MEM_WIDTH + lo:MEM_WIDTH + lo + MEM_HEAD_DIM],
                    preferred_element_type=F32)
        o_ref[:, lo:lo + MEM_HEAD_DIM] = (o / l).astype(o_ref.dtype)


def _mem_attention(proj, kv_mem, qg_scaled, tm):
    s = proj.shape[0]
    n_mem = kv_mem.shape[0]
    assert OFF_MQ % MEM_WINDOW + MEM_WIDTH <= MEM_WINDOW
    return pl.pallas_call(
        _mem_attn_kernel,
        grid=(s // tm,),
        in_specs=[pl.BlockSpec((tm, MEM_WINDOW), lambda i: (i, OFF_MQ // MEM_WINDOW)),
                  pl.BlockSpec((n_mem, 2 * MEM_WIDTH), lambda i: (0, 0)),
                  pl.BlockSpec((1, MEM_HEAD_DIM), lambda i: (0, 0))],
        out_specs=pl.BlockSpec((tm, MEM_WIDTH), lambda i: (i, 0)),
        out_shape=jax.ShapeDtypeStruct((s, MEM_WIDTH), BF16),
        compiler_params=_cp(("parallel",)),
        name="mem_attention",
    )(proj, kv_mem, qg_scaled)


def _mix_kernel(od_ref, om_ref, oc_ref, wd_ref, wm_ref, wc_ref, g0_ref, g1_ref, g2_ref, o_ref):
    yd = jnp.dot(od_ref[...], wd_ref[...], preferred_element_type=F32)
    ym = jnp.dot(om_ref[...], wm_ref[...], preferred_element_type=F32)
    yc = jnp.dot(oc_ref[...], wc_ref[...], preferred_element_type=F32)
    mixed = (g0_ref[...].astype(F32) * yd + g1_ref[...].astype(F32) * ym) + g2_ref[...].astype(F32) * yc
    o_ref[...] = mixed.astype(o_ref.dtype)


def _mix(o_diff, o_mla, o_mem, w_d, w_m, w_c, gates, tm, tn):
    s = o_diff.shape[0]
    d = w_d.shape[1]
    nj = d // tn
    return pl.pallas_call(
        _mix_kernel,
        grid=(s // tm, nj),
        in_specs=[pl.BlockSpec((tm, o_diff.shape[1]), lambda i, j: (i, 0)),
                  pl.BlockSpec((tm, o_mla.shape[1]), lambda i, j: (i, 0)),
                  pl.BlockSpec((tm, o_mem.shape[1]), lambda i, j: (i, 0)),
                  pl.BlockSpec((w_d.shape[0], tn), lambda i, j: (0, j)),
                  pl.BlockSpec((w_m.shape[0], tn), lambda i, j: (0, j)),
                  pl.BlockSpec((w_c.shape[0], tn), lambda i, j: (0, j)),
                  pl.BlockSpec((tm, tn), lambda i, j: (i, j)),
                  pl.BlockSpec((tm, tn), lambda i, j: (i, nj + j)),
                  pl.BlockSpec((tm, tn), lambda i, j: (i, 2 * nj + j))],
        out_specs=pl.BlockSpec((tm, tn), lambda i, j: (i, j)),
        out_shape=jax.ShapeDtypeStruct((s, d), BF16),
        compiler_params=_cp(("arbitrary", "arbitrary")),
        name="gated_mix",
    )(o_diff, o_mla, o_mem, w_d, w_m, w_c, gates, gates, gates)


def _router_kernel(x_ref, g_ref, w_ref, b_ref, h_ref, r_ref):
    x = x_ref[...]
    ms = jnp.mean(x * x, axis=-1, keepdims=True)
    h = x * lax.rsqrt(ms + NORM_EPS) * g_ref[...]
    h_ref[...] = h
    logits = jnp.dot(h, w_ref[...], preferred_element_type=F32, precision=lax.Precision.HIGHEST) + b_ref[...]
    lane = lax.broadcasted_iota(jnp.int32, logits.shape, 1)
    lane_f = lane.astype(F32)
    big = float(4 * ROUTE_W)
    lg = jnp.where(lane < N_GROUPS, logits, -jnp.inf)
    gmax = jnp.max(lg, axis=-1, keepdims=True)
    gidx = jnp.min(jnp.where(lg == gmax, lane_f, big), axis=-1, keepdims=True)
    pg_top = 1.0 / jnp.sum(jnp.exp(lg - gmax), axis=-1, keepdims=True)
    e_lane = lane - N_GROUPS
    lane_group = jnp.right_shift(e_lane, 3).astype(F32)
    in_group = (e_lane >= 0) & (e_lane < N_EXPERTS) & (lane_group == gidx)
    le = jnp.where(in_group, logits, -jnp.inf)
    e1 = jnp.max(le, axis=-1, keepdims=True)
    i1 = jnp.min(jnp.where(le == e1, lane_f, big), axis=-1, keepdims=True)
    le2 = jnp.where(lane_f == i1, -jnp.inf, le)
    e2 = jnp.max(le2, axis=-1, keepdims=True)
    i2 = jnp.min(jnp.where(le2 == e2, lane_f, big), axis=-1, keepdims=True)
    w2 = jnp.exp(e2 - e1)
    inv = 1.0 / (1.0 + w2)
    gate1 = pg_top * inv
    gate2 = pg_top * (w2 * inv)
    out = jnp.where(lane == 0, i1 - N_GROUPS,
                    jnp.where(lane == 1, i2 - N_GROUPS,
                              jnp.where(lane == 2, gate1, jnp.where(lane == 3, gate2, 0.0))))
    r_ref[...] = out


def _router(x1, g, w_r, b_r, tm):
    s, d = x1.shape
    return pl.pallas_call(
        _router_kernel,
        grid=(s // tm,),
        in_specs=[pl.BlockSpec((tm, d), lambda i: (i, 0)),
                  pl.BlockSpec((1, d), lambda i: (0, 0)),
                  pl.BlockSpec((d, ROUTE_W), lambda i: (0, 0)),
                  pl.BlockSpec((1, ROUTE_W), lambda i: (0, 0))],
        out_specs=[pl.BlockSpec((tm, d), lambda i: (i, 0)),
                   pl.BlockSpec((tm, ROUTE_W), lambda i: (i, 0))],
        out_shape=[jax.ShapeDtypeStruct((s, d), F32),
                   jax.ShapeDtypeStruct((s, ROUTE_W), F32)],
        compiler_params=_cp(("parallel",)),
        name="ffn_norm_router",
    )(x1, g, w_r, b_r)


def _pack_bf16_pairs(v):
    n = v.shape[1] // 2
    bits = lax.bitcast_convert_type(v.astype(BF16).astype(F32), jnp.uint32)
    return jnp.right_shift(bits[:, :n], jnp.uint32(16)) | bits[:, n:]


def _moe_kernel(be_ref, ns_ref, nu_ref, tok_ref, bw_ref, h_ref, wg_ref, wu_ref, wd_ref, o_ref,
                x_ref, y_ref, wgb_ref, wub_ref, wdb_ref, sem):
    b = pl.program_id(0)
    f = pl.program_id(1)
    ns = ns_ref[b]
    rows = ns * MOE_SUB

    @pl.when(f == 0)
    def _():
        def issue(r, c):
            tkn = tok_ref[0, 0, r]
            pltpu.make_async_copy(h_ref.at[pl.ds(tkn, 1), :], x_ref.at[pl.ds(r, 1), :], sem).start()
            return c

        lax.fori_loop(0, rows, issue, 0)

        def drain(r, c):
            pltpu.make_async_copy(h_ref.at[pl.ds(0, 1), :], x_ref.at[pl.ds(0, 1), :], sem).wait()
            return c

        lax.fori_loop(0, rows, drain, 0)

    @pl.when(ns > 0)
    def _():
        wgb_ref[...] = wg_ref[0].astype(BF16)
        wub_ref[...] = wu_ref[0].astype(BF16)
        wdb_ref[...] = wd_ref[0].astype(BF16)

    def sub(sb, c):
        r0 = pl.multiple_of(sb * MOE_SUB, MOE_SUB)
        xs = x_ref[pl.ds(r0, MOE_SUB), :].astype(BF16)
        g = jnp.dot(xs, wgb_ref[...], preferred_element_type=F32)
        u = jnp.dot(xs, wub_ref[...], preferred_element_type=F32)
        a = ((g * jax.nn.sigmoid(g)) * u).astype(BF16)
        yv = jnp.dot(a, wdb_ref[...], preferred_element_type=F32)

        @pl.when(f == 0)
        def _():
            y_ref[pl.ds(r0, MOE_SUB), :] = yv

        @pl.when(f > 0)
        def _():
            y_ref[pl.ds(r0, MOE_SUB), :] += yv

        return c

    lax.fori_loop(0, ns, sub, 0)

    @pl.when(f == MOE_NF - 1)
    def _():
        for sb in range(MOE_TB // MOE_SUB):
            sl = slice(sb * MOE_SUB, (sb + 1) * MOE_SUB)

            @pl.when(sb < ns)
            def _():
                o_ref[sl, :] = _pack_bf16_pairs(y_ref[sl, :] * bw_ref[sl, :])

            @pl.when(sb >= ns)
            def _():
                o_ref[sl, :] = jnp.zeros((MOE_SUB, o_ref.shape[1]), jnp.uint32)


def _moe_experts(block_e, nsub, n_used, buf_tok, buf_w, h2, w_gate, w_up, w_down):
    n_blocks = block_e.shape[0]
    d = h2.shape[1]
    fc = EXPERT_FF // MOE_NF

    def wsel(f, b, nu):
        return jnp.where(b < nu[0], f, MOE_NF - 1)

    grid_spec = pltpu.PrefetchScalarGridSpec(
        num_scalar_prefetch=3,
        grid=(n_blocks, MOE_NF),
        in_specs=[pl.BlockSpec((1, 1, MOE_TB), lambda b, f, be, ns, nu: (b, 0, 0), memory_space=pltpu.SMEM),
                  pl.BlockSpec((MOE_TB, 1), lambda b, f, be, ns, nu: (b, 0)),
                  pl.BlockSpec(memory_space=pl.ANY),
                  pl.BlockSpec((1, d, fc), lambda b, f, be, ns, nu: (be[b], 0, wsel(f, b, nu))),
                  pl.BlockSpec((1, d, fc), lambda b, f, be, ns, nu: (be[b], 0, wsel(f, b, nu))),
                  pl.BlockSpec((1, fc, d), lambda b, f, be, ns, nu: (be[b], wsel(f, b, nu), 0))],
        out_specs=pl.BlockSpec((MOE_TB, d // 2), lambda b, f, be, ns, nu: (b, 0)),
        scratch_shapes=[pltpu.VMEM((MOE_TB, d), F32), pltpu.VMEM((MOE_TB, d), F32),
                        pltpu.VMEM((d, fc), BF16), pltpu.VMEM((d, fc), BF16), pltpu.VMEM((fc, d), BF16),
                        pltpu.SemaphoreType.DMA(())],
    )
    return pl.pallas_call(
        _moe_kernel,
        grid_spec=grid_spec,
        out_shape=jax.ShapeDtypeStruct((n_blocks * MOE_TB, d // 2), jnp.uint32),
        compiler_params=_cp(("arbitrary", "arbitrary")),
        name="moe_experts",
    )(block_e, nsub, n_used, buf_tok.reshape(n_blocks, 1, MOE_TB), buf_w.reshape(-1, 1), h2, w_gate, w_up, w_down)


def _combine_kernel(slot_ref, x_ref, yb_ref, o_ref, g_ref, sem):
    tm = x_ref.shape[0]
    n = g_ref.shape[2]

    def issue(r, c):
        for k in range(TOP_K):
            sl = slot_ref[0, 0, r * TOP_K + k]
            pltpu.make_async_copy(yb_ref.at[pl.ds(sl, 1), :], g_ref.at[k, pl.ds(r, 1), :], sem).start()
        return c

    lax.fori_loop(0, tm, issue, 0)

    def drain(r, c):
        pltpu.make_async_copy(yb_ref.at[pl.ds(0, 1), :], g_ref.at[0, pl.ds(0, 1), :], sem).wait()
        return c

    lax.fori_loop(0, tm * TOP_K, drain, 0)

    g0 = g_ref[0]
    g1 = g_ref[1]
    hi_mask = jnp.uint32(0xFFFF0000)
    lo = (lax.bitcast_convert_type(jnp.left_shift(g0, jnp.uint32(16)), F32)
          + lax.bitcast_convert_type(jnp.left_shift(g1, jnp.uint32(16)), F32))
    hi = (lax.bitcast_convert_type(g0 & hi_mask, F32) + lax.bitcast_convert_type(g1 & hi_mask, F32))
    o_ref[:, :n] = x_ref[:, :n] + lo
    o_ref[:, n:] = x_ref[:, n:] + hi


def _combine(slots, x1, yb, tm):
    s, d = x1.shape
    return pl.pallas_call(
        _combine_kernel,
        grid=(s // tm,),
        in_specs=[pl.BlockSpec((1, 1, tm * TOP_K), lambda i: (i, 0, 0), memory_space=pltpu.SMEM),
                  pl.BlockSpec((tm, d), lambda i: (i, 0)),
                  pl.BlockSpec(memory_space=pl.ANY)],
        out_specs=pl.BlockSpec((tm, d), lambda i: (i, 0)),
        out_shape=jax.ShapeDtypeStruct((s, d), F32),
        scratch_shapes=[pltpu.VMEM((TOP_K, tm, d // 2), jnp.uint32), pltpu.SemaphoreType.DMA(())],
        compiler_params=_cp(("arbitrary",)),
        name="moe_combine",
    )(slots.reshape(s // tm, 1, tm * TOP_K), x1, yb)


def _dispatch_plan(route, s):
    a = s * TOP_K
    flat_e = route[:, :TOP_K].astype(jnp.int32).reshape(a)
    flat_w = route[:, TOP_K:2 * TOP_K].reshape(a)
    onehot = (flat_e[:, None] == jnp.arange(N_EXPERTS, dtype=jnp.int32)[None, :]).astype(jnp.int32)
    csum = jnp.cumsum(onehot, axis=0)
    counts = csum[-1]
    rank = jnp.sum((csum - onehot) * onehot, axis=1)
    padded = (counts + MOE_TB - 1) // MOE_TB * MOE_TB
    pad_end = jnp.cumsum(padded)
    pad_start = pad_end - padded
    dest = pad_start[flat_e] + rank
    n_blocks = a // MOE_TB + N_EXPERTS
    p_rows = n_blocks * MOE_TB
    buf_tok = jnp.zeros((p_rows,), jnp.int32).at[dest].set(jnp.arange(a, dtype=jnp.int32) // TOP_K)
    buf_w = jnp.zeros((p_rows,), F32).at[dest].set(flat_w)
    starts = jnp.arange(n_blocks, dtype=jnp.int32) * MOE_TB
    block_e = jnp.minimum(jnp.searchsorted(pad_end, starts, side='right'), N_EXPERTS - 1).astype(jnp.int32)
    valid = jnp.clip(counts[block_e] - (starts - pad_start[block_e]), 0, MOE_TB)
    valid = jnp.where(starts < pad_end[-1], valid, 0)
    nsub = ((valid + MOE_SUB - 1) // MOE_SUB).astype(jnp.int32)
    n_used = (pad_end[-1:] // MOE_TB).astype(jnp.int32)
    return block_e, nsub, n_used, buf_tok, buf_w, dest.astype(jnp.int32)


def _rope_tables(positions):
    half = MLA_ROPE_DIM // 2
    inv_freq = ROPE_THETA ** (-jnp.arange(half, dtype=F32) / half)
    ang = positions.astype(F32)[:, None] * inv_freq[None, :]
    cos, sin = jnp.cos(ang), jnp.sin(ang)
    z = jnp.zeros_like(cos)
    c = jnp.concatenate([cos, cos, z, z], axis=-1)
    s1 = jnp.concatenate([-sin, z, z, z], axis=-1)
    s2 = jnp.concatenate([z, sin, z, z], axis=-1)
    return c, s1, s2


def kernel(x, mem, positions, rel_bias, mix_norm_g, w_in, diff_q_norm_g, diff_k_norm_g, diff_lambda_q1, diff_lambda_k1, diff_lambda_q2, diff_lambda_k2, diff_subln_g, mla_cq_norm_g, mla_ckv_norm_g, mla_w_uq, mla_w_ukv, mla_q_norm_g, mla_k_norm_g, mem_norm_g, mem_w_kv, mem_q_norm_g, mem_k_norm_g, w_o_diff, w_o_mla, w_o_mem, w_out, ffn_norm_g, w_route_group, b_route_group, w_route_expert, b_route_expert, w_exp_gate, w_exp_up, w_exp_down):
    b, s, d = x.shape
    assert b == 1 and s % ATT_T == 0
    depth = mix_norm_g.shape[0]
    xs = x.reshape(s, d)
    pos = positions.reshape(s)
    rope_tabs = _rope_tables(pos)
    row = lambda v: v.reshape(1, -1).astype(F32)

    for l in range(depth):
        lam_init = 0.8 - 0.6 * math.exp(-0.3 * l)
        h = _rmsnorm_rows(xs, mix_norm_g[l], 256)
        qk_gain = jnp.concatenate([jnp.tile(diff_q_norm_g[l] * DIFF_HEAD_DIM ** -0.5, DIFF_MAPS),
                                   jnp.tile(diff_k_norm_g[l], DIFF_MAPS)]).reshape(1, -1).astype(F32)
        tn = 512
        proj = _matmul(h, w_in[l], n_cols=PROJ_WIDTH, tm=512, tn=tn, out_dtype=BF16, mode="groupnorm",
                       extra=qk_gain, group=DIFF_HEAD_DIM, n_norm_tiles=2 * DIFF_QK_WIDTH // tn, name="in_proj")
        w_gates = w_in[l][:, OFF_GATES:].astype(BF16)
        gates = _matmul(h, w_gates, n_cols=3 * d, tm=512, tn=tn, out_dtype=BF16, mode="sigmoid", name="gate_proj")

        bias_tiles = _diff_bias_tiles(rel_bias, ATT_T)
        lam_vecs = [row(diff_lambda_q1[l]), row(diff_lambda_k1[l]), row(diff_lambda_q2[l]), row(diff_lambda_k2[l])]
        o_diff = _diff_attention(proj, bias_tiles, lam_vecs, row(diff_subln_g[l]), lam_init)

        w_uq_heads = jnp.pad(
            mla_w_uq[l].reshape(MLA_Q_RANK, MLA_HEADS, MLA_QK_DIM).transpose(1, 0, 2),
            ((0, 0), (0, 0), (0, MLA_QK_PAD - MLA_QK_DIM))).astype(BF16)
        qg_pad = jnp.pad(mla_q_norm_g[l] * MLA_QK_DIM ** -0.5, (0, MLA_QK_PAD - MLA_QK_DIM)).reshape(1, -1).astype(F32)
        q_mla = _mla_q_prep(proj, row(mla_cq_norm_g[l]), w_uq_heads, qg_pad, rope_tabs, 512)
        kg = mla_k_norm_g[l]
        kg_nope = row(kg[:MLA_NOPE_DIM])
        kg_rope = jnp.pad(kg[MLA_NOPE_DIM:], (0, LANE - MLA_ROPE_DIM)).reshape(1, -1).astype(F32)
        k_mla, v_mla = _mla_kv_prep(proj, row(mla_ckv_norm_g[l]), mla_w_ukv[l], kg_nope, kg_rope, rope_tabs, 512)
        o_mla = _mla_attention(q_mla, k_mla, v_mla)

        n_mem = mem.shape[1]
        mem_h = _rmsnorm_rows(mem.reshape(n_mem, d), mem_norm_g[l], n_mem)
        kgain = jnp.tile(mem_k_norm_g[l], MEM_HEADS).reshape(1, -1).astype(F32)
        kv_mem = _matmul(mem_h, mem_w_kv[l], n_cols=2 * MEM_WIDTH, tm=n_mem, tn=512, out_dtype=BF16,
                         mode="groupnorm", extra=kgain, group=MEM_HEAD_DIM, n_norm_tiles=MEM_WIDTH // 512,
                         name="mem_kv_proj")
        o_mem = _mem_attention(proj, kv_mem, row(mem_q_norm_g[l] * MEM_HEAD_DIM ** -0.5), 512)

        mixed = _mix(o_diff, o_mla, o_mem, w_o_diff[l].astype(BF16), w_o_mla[l].astype(BF16),
                     w_o_mem[l].astype(BF16), gates, 512, 512)
        x1 = _matmul(mixed, w_out[l], n_cols=d, tm=512, tn=512, out_dtype=F32, mode="residual", extra=xs,
                     name="out_proj")

        w_r = jnp.pad(jnp.concatenate([w_route_group[l], w_route_expert[l]], axis=1),
                      ((0, 0), (0, ROUTE_W - N_GROUPS - N_EXPERTS))).astype(F32)
        b_r = jnp.pad(jnp.concatenate([b_route_group[l], b_route_expert[l]]),
                      (0, ROUTE_W - N_GROUPS - N_EXPERTS)).reshape(1, -1).astype(F32)
        h2, route = _router(x1, row(ffn_norm_g[l]), w_r, b_r, 256)
        block_e, nsub, n_used, buf_tok, buf_w, slots = _dispatch_plan(route, s)
        yb = _moe_experts(block_e, nsub, n_used, buf_tok, buf_w, h2, w_exp_gate[l], w_exp_up[l], w_exp_down[l])
        xs = _combine(slots, x1, yb, 256)
    return xs.reshape(b, s, d)
```

```python
import functools
import math

import jax
import jax.numpy as jnp
from jax import lax
from jax.experimental import pallas as pl
from jax.experimental.pallas import tpu as pltpu

F32 = jnp.float32
BF16 = jnp.bfloat16

NORM_EPS = 1e-6
NEG_INF = -1e30
LOG2E = math.log2(math.e)

DIFF_HEADS = 6
DIFF_HEAD_DIM = 128
DIFF_V_DIM = 256
DIFF_MAPS = 12
DIFF_QK_WIDTH = 1536
DIFF_WIDTH = 1536
MLA_HEADS = 12
MLA_Q_RANK = 1536
MLA_KV_RANK = 512
MLA_NOPE_DIM = 128
MLA_ROPE_DIM = 64
MLA_QK_DIM = 192
MLA_QK_PAD = 256
MLA_V_DIM = 128
MLA_WIDTH = 1536
ROPE_THETA = 10000.0
MEM_HEADS = 4
MEM_HEAD_DIM = 256
MEM_WIDTH = 1024
REL_BUCKETS = 32
REL_MAX_DIST = 128
N_GROUPS = 8
EXPERTS_PER_GROUP = 8
N_EXPERTS = 64
TOP_K = 2
EXPERT_FF = 512

OFF_DQ = 0
OFF_DK = 1536
OFF_DV = 3072
OFF_CQ = 4608
OFF_CKV = 6144
OFF_KROPE = 6656
OFF_MQ = 6720
OFF_GATES = 7744
REST_WIDTH = 4096
R_CQ = OFF_CQ - OFF_CQ
R_CKV = OFF_CKV - OFF_CQ
R_KROPE = OFF_KROPE - OFF_CQ
R_MQ = OFF_MQ - OFF_CQ

LANE = 128
VMEM_LIMIT = 52 * 1024 * 1024

ATT_T = 512
MOE_TB = 256
MOE_SUB = 128
MOE_NF = 2
ROUTE_W = 128
MEM_WINDOW = 2048


def _cp(sem, vmem=VMEM_LIMIT):
    return pltpu.CompilerParams(dimension_semantics=sem, vmem_limit_bytes=vmem)


def _rmsnorm_kernel(x_ref, g_ref, o_ref):
    x = x_ref[...].astype(F32)
    ms = jnp.mean(x * x, axis=-1, keepdims=True)
    o_ref[...] = (x * lax.rsqrt(ms + NORM_EPS) * g_ref[...]).astype(o_ref.dtype)


def _rmsnorm_rows(x, g, tm, out_dtype=BF16):
    m, d = x.shape
    return pl.pallas_call(
        _rmsnorm_kernel,
        grid=(m // tm,),
        in_specs=[pl.BlockSpec((tm, d), lambda i: (i, 0)),
                  pl.BlockSpec((1, d), lambda i: (0, 0))],
        out_specs=pl.BlockSpec((tm, d), lambda i: (i, 0)),
        out_shape=jax.ShapeDtypeStruct((m, d), out_dtype),
        compiler_params=_cp(("parallel",)),
        name="rmsnorm_rows",
    )(x, g.reshape(1, d).astype(F32))


def _cast_shifted(w_ref, w2_ref, wb_ref, shift):
    k = w_ref.shape[0]
    rows = 256

    def body(c, carry):
        r0 = pl.multiple_of(c * rows, rows)
        main = w_ref[pl.ds(r0, rows), :]
        tail = w2_ref[pl.ds(r0, rows), :]
        wb_ref[pl.ds(r0, rows), :] = jnp.concatenate([main[:, shift:], tail[:, :shift]], axis=1).astype(BF16)
        return carry

    lax.fori_loop(0, k // rows, body, 0)


def _mm_kernel(*refs, mode, cast, group, shift, transpose_out):
    a_ref, w_ref = refs[0], refs[1]
    pos = 2
    w2_ref = None
    if shift:
        w2_ref = refs[pos]
        pos += 1
    extra = None
    if mode in ("groupnorm", "residual"):
        extra = refs[pos]
        pos += 1
    o_ref = refs[pos]
    wb_ref = refs[pos + 1] if cast else None
    i = pl.program_id(1)
    if cast:
        @pl.when(i == 0)
        def _():
            if shift:
                _cast_shifted(w_ref, w2_ref, wb_ref, shift)
            else:
                wb_ref[...] = w_ref[...].astype(BF16)
        w = wb_ref[...]
    else:
        w = w_ref[...]
    acc = jnp.dot(a_ref[...], w, preferred_element_type=F32)
    if transpose_out:
        acc = acc.T
    if mode == "plain":
        o_ref[...] = acc.astype(o_ref.dtype)
    elif mode == "sigmoid":
        o_ref[...] = jax.nn.sigmoid(acc).astype(o_ref.dtype)
    elif mode == "residual":
        o_ref[...] = (extra[...] + acc).astype(o_ref.dtype)
    elif mode == "groupnorm":
        tn = w.shape[1]
        for c in range(tn // group):
            sl = slice(c * group, (c + 1) * group)
            if transpose_out:
                blk = acc[sl, :]
                ms = jnp.mean(blk * blk, axis=0, keepdims=True)
                o_ref[sl, :] = (blk * lax.rsqrt(ms + NORM_EPS) * extra[sl, :]).astype(o_ref.dtype)
            else:
                blk = acc[:, sl]
                ms = jnp.mean(blk * blk, axis=-1, keepdims=True)
                o_ref[:, sl] = (blk * lax.rsqrt(ms + NORM_EPS) * extra[:, sl]).astype(o_ref.dtype)


def _matmul(a, w, *, n_cols, tm, tn, out_dtype, col0=0, mode="plain", extra=None, group=LANE,
            transpose_out=False, name="matmul"):
    m, k = a.shape
    assert m % tm == 0 and n_cols % tn == 0 and w.shape[0] == k
    cast = w.dtype != BF16
    base, shift = divmod(col0, tn)
    assert shift <= LANE and (shift == 0 or cast)
    in_specs = [pl.BlockSpec((tm, k), lambda j, i: (i, 0)),
                pl.BlockSpec((k, tn), lambda j, i: (0, base + j))]
    args = [a, w]
    if shift:
        in_specs.append(pl.BlockSpec((k, LANE), lambda j, i: (0, (base + j + 1) * (tn // LANE))))
        args.append(w)
    if mode == "groupnorm":
        gain = extra.reshape(-1, 1) if transpose_out else extra.reshape(1, -1)
        in_specs.append(pl.BlockSpec((tn, 1), lambda j, i: (j, 0)) if transpose_out
                        else pl.BlockSpec((1, tn), lambda j, i: (0, j)))
        args.append(gain.astype(F32))
    elif mode == "residual":
        assert not transpose_out
        in_specs.append(pl.BlockSpec((tm, tn), lambda j, i: (i, j)))
        args.append(extra)
    if transpose_out:
        out_spec = pl.BlockSpec((tn, tm), lambda j, i: (j, i))
        out_shape = jax.ShapeDtypeStruct((n_cols, m), out_dtype)
    else:
        out_spec = pl.BlockSpec((tm, tn), lambda j, i: (i, j))
        out_shape = jax.ShapeDtypeStruct((m, n_cols), out_dtype)
    scratch = [pltpu.VMEM((k, tn), BF16)] if cast else []
    return pl.pallas_call(
        functools.partial(_mm_kernel, mode=mode, cast=cast, group=group, shift=shift, transpose_out=transpose_out),
        grid=(n_cols // tn, m // tm),
        in_specs=in_specs,
        out_specs=out_spec,
        out_shape=out_shape,
        scratch_shapes=scratch,
        compiler_params=_cp(("arbitrary", "arbitrary")),
        name=name,
    )(*args)


def _softmax_pv(idx, s, vt_blk, m_ref, l_ref, acc_ref):
    m_prev = m_ref[idx]
    m_new = jnp.maximum(m_prev, jnp.max(s, axis=0, keepdims=True))
    alpha = jnp.exp2(m_prev - m_new)
    p = jnp.exp2(s - m_new)
    l_ref[idx] = alpha * l_ref[idx] + jnp.sum(p, axis=0, keepdims=True)
    acc_ref[idx] = alpha * acc_ref[idx] + jnp.dot(vt_blk, p.astype(BF16), preferred_element_type=F32)
    m_ref[idx] = m_new


def _block_offset(kb, t):
    return kb * t if isinstance(kb, int) else pl.multiple_of(kb * t, t)


def _by_parity(kb, fn, even_ref, odd_ref):
    @pl.when(kb % 2 == 0)
    def _():
        fn(even_ref, odd_ref)

    @pl.when(kb % 2 == 1)
    def _():
        fn(odd_ref, even_ref)


def _init_stats(m_ref, l_ref, acc_ref):
    m_ref[...] = jnp.full(m_ref.shape, NEG_INF, F32)
    l_ref[...] = jnp.zeros(l_ref.shape, F32)
    acc_ref[...] = jnp.zeros(acc_ref.shape, F32)


def _diff_attn_kernel(lq1_ref, lk1_ref, lq2_ref, lk2_ref, q1_ref, q2_ref, k1_ref, k2_ref, vt_ref,
                      b1_ref, b2_ref, g_ref, o_ref, sa_ref, sb_ref, m_ref, l_ref, acc_ref, *, lam_init):
    t = q1_ref.shape[1]
    i = pl.program_id(1)
    qts = (q1_ref[...], q2_ref[...])
    ks = (k1_ref, k2_ref)
    bs = (b1_ref, b2_ref)
    _init_stats(m_ref, l_ref, acc_ref)

    def scores(kb, dst):
        off = _block_offset(kb, t)
        for mp in range(2):
            dst[mp] = jnp.dot(ks[mp][pl.ds(off, t), :], qts[mp], preferred_element_type=F32)

    def consume(kb, src, bias_idx):
        off = _block_offset(kb, t)
        vt_blk = vt_ref[:, pl.ds(off, t)]
        for mp in range(2):
            s = src[mp]
            if bias_idx is not None:
                s = s + bs[mp][0, bias_idx]
            _softmax_pv(mp, s, vt_blk, m_ref, l_ref, acc_ref)

    def step(kb, bias_idx, last=False):
        def run(cur, nxt):
            if not last:
                scores(kb + 1, nxt)
            consume(kb, cur, bias_idx)
        _by_parity(kb, run, sa_ref, sb_ref)

    n_far = jnp.maximum(i - 1, 0)
    scores(0, sa_ref)

    def far_pair(j, c):
        kb = 2 * j
        scores(kb + 1, sb_ref)
        consume(kb, sa_ref, None)
        scores(kb + 2, sa_ref)
        consume(kb + 1, sb_ref, None)
        return c

    lax.fori_loop(0, n_far // 2, far_pair, 0)

    @pl.when(n_far % 2 == 1)
    def _():
        scores(n_far, sb_ref)
        consume(n_far - 1, sa_ref, None)

    @pl.when(i >= 1)
    def _():
        step(i - 1, 1)

    step(i, 0, last=True)

    lam = (jnp.exp(jnp.sum(lq1_ref[...] * lk1_ref[...], axis=-1, keepdims=True))
           - jnp.exp(jnp.sum(lq2_ref[...] * lk2_ref[...], axis=-1, keepdims=True)) + lam_init)
    o = acc_ref[0] / l_ref[0] - lam * (acc_ref[1] / l_ref[1])
    ms = jnp.mean(o * o, axis=0, keepdims=True)
    o = (o * lax.rsqrt(ms + NORM_EPS) * g_ref[...]) * (1.0 - lam_init)
    o_ref[...] = o.T.astype(o_ref.dtype)


def _diff_attention(q_t, k, v_t, bias_tiles, lam_vecs, subln_g, lam_init):
    s = k.shape[0]
    t = ATT_T
    hd, vd = DIFF_HEAD_DIM, DIFF_V_DIM
    vec = pl.BlockSpec((1, hd), lambda h, i: (0, 0))
    in_specs = [vec, vec, vec, vec,
                pl.BlockSpec((hd, t), lambda h, i: (h, i)),
                pl.BlockSpec((hd, t), lambda h, i: (DIFF_HEADS + h, i)),
                pl.BlockSpec((s, hd), lambda h, i: (0, h)),
                pl.BlockSpec((s, hd), lambda h, i: (0, DIFF_HEADS + h)),
                pl.BlockSpec((vd, s), lambda h, i: (h, 0)),
                pl.BlockSpec((1, 2, t, t), lambda h, i: (h, 0, 0, 0)),
                pl.BlockSpec((1, 2, t, t), lambda h, i: (DIFF_HEADS + h, 0, 0, 0)),
                pl.BlockSpec((vd, 1), lambda h, i: (0, 0))]
    return pl.pallas_call(
        functools.partial(_diff_attn_kernel, lam_init=lam_init),
        grid=(DIFF_HEADS, s // t),
        in_specs=in_specs,
        out_specs=pl.BlockSpec((t, vd), lambda h, i: (i, h)),
        out_shape=jax.ShapeDtypeStruct((s, DIFF_WIDTH), BF16),
        scratch_shapes=[pltpu.VMEM((2, t, t), F32), pltpu.VMEM((2, t, t), F32),
                        pltpu.VMEM((2, 1, t), F32), pltpu.VMEM((2, 1, t), F32),
                        pltpu.VMEM((2, vd, t), F32)],
        compiler_params=_cp(("arbitrary", "arbitrary")),
        name="diff_attention",
    )(*lam_vecs, q_t, q_t, k, k, v_t, bias_tiles, bias_tiles, subln_g.reshape(vd, 1).astype(F32))


def _t5_bucket(dist):
    n = jnp.maximum(dist, 0)
    max_exact = REL_BUCKETS // 2
    nf = jnp.maximum(n, 1).astype(F32)
    large = max_exact + (jnp.log(nf / max_exact) / math.log(REL_MAX_DIST / max_exact)
                         * (REL_BUCKETS - max_exact)).astype(jnp.int32)
    large = jnp.minimum(large, REL_BUCKETS - 1)
    return jnp.where(n < max_exact, n, large)


def _diff_bias_tiles(rel_bias, t):
    assert t >= REL_MAX_DIST
    table = rel_bias.astype(F32)
    table = (table - table[REL_BUCKETS - 1:REL_BUCKETS]) * LOG2E
    kr = jnp.arange(t, dtype=jnp.int32)[:, None]
    qc = jnp.arange(t, dtype=jnp.int32)[None, :]
    d0 = qc - kr
    buckets = jnp.arange(REL_BUCKETS, dtype=jnp.int32)

    def lookup(dist):
        onehot = (_t5_bucket(dist)[:, :, None] == buckets).astype(F32)
        return jnp.einsum('rcb,bm->mrc', onehot, table, precision=lax.Precision.HIGHEST)

    tile0 = jnp.where((d0 >= 0)[None], lookup(d0), NEG_INF)
    return jnp.stack([tile0, lookup(d0 + t)], axis=1)


def _rope_apply(tv, c_ref, s1_ref, s2_ref):
    return (tv * c_ref[...] + pltpu.roll(tv, 96, 1) * s1_ref[...] + pltpu.roll(tv, 32, 1) * s2_ref[...])


def _mla_q_kernel(cq_ref, g_ref, w_ref, qg_ref, c_ref, s1_ref, s2_ref, o_ref, xg_ref):
    h = pl.program_id(1)

    @pl.when(h == 0)
    def _():
        c = cq_ref[...].astype(F32)
        r = lax.rsqrt(jnp.mean(c * c, axis=-1, keepdims=True) + NORM_EPS)
        xg_ref[...] = (c * r * g_ref[...]).astype(BF16)

    u = jnp.dot(xg_ref[...], w_ref[0], preferred_element_type=F32)
    ms = jnp.sum(u * u, axis=-1, keepdims=True) * (1.0 / MLA_QK_DIM)
    qn = u * lax.rsqrt(ms + NORM_EPS) * qg_ref[...]
    o_ref[0, :MLA_NOPE_DIM, :] = qn[:, :MLA_NOPE_DIM].T.astype(o_ref.dtype)
    o_ref[0, MLA_NOPE_DIM:, :] = _rope_apply(qn[:, MLA_NOPE_DIM:], c_ref, s1_ref, s2_ref).T.astype(o_ref.dtype)


def _mla_q_prep(rest, cq_g, w_uq_heads, qg_pad, rope_tabs, tm):
    s = rest.shape[0]
    tab = pl.BlockSpec((tm, LANE), lambda i, h: (i, 0))
    return pl.pallas_call(
        _mla_q_kernel,
        grid=(s // tm, MLA_HEADS),
        in_specs=[pl.BlockSpec((tm, MLA_Q_RANK), lambda i, h: (i, R_CQ // MLA_Q_RANK)),
                  pl.BlockSpec((1, MLA_Q_RANK), lambda i, h: (0, 0)),
                  pl.BlockSpec((1, MLA_Q_RANK, MLA_QK_PAD), lambda i, h: (h, 0, 0)),
                  pl.BlockSpec((1, MLA_QK_PAD), lambda i, h: (0, 0)),
                  tab, tab, tab],
        out_specs=pl.BlockSpec((1, MLA_QK_PAD, tm), lambda i, h: (h, 0, i)),
        out_shape=jax.ShapeDtypeStruct((MLA_HEADS, MLA_QK_PAD, s), BF16),
        scratch_shapes=[pltpu.VMEM((tm, MLA_Q_RANK), BF16)],
        compiler_params=_cp(("arbitrary", "arbitrary")),
        name="mla_q_prep",
    )(rest, cq_g, w_uq_heads, qg_pad, *rope_tabs)


def _mla_kv_kernel(ckv_ref, kr_ref, g_ref, w_ref, kgn_ref, kgr_ref, c_ref, s1_ref, s2_ref,
                   k_ref, vt_ref, xg_ref):
    h = pl.program_id(1)

    @pl.when(h == 0)
    def _():
        c = ckv_ref[...].astype(F32)
        r = lax.rsqrt(jnp.mean(c * c, axis=-1, keepdims=True) + NORM_EPS)
        xg_ref[...] = (c * r * g_ref[...]).astype(BF16)

    kv = jnp.dot(xg_ref[...], w_ref[...].astype(BF16), preferred_element_type=F32)
    kn = kv[:, :MLA_NOPE_DIM]
    lane = lax.broadcasted_iota(jnp.int32, kr_ref.shape, 1)
    kr = jnp.where(lane < MLA_ROPE_DIM, kr_ref[...].astype(F32), 0.0)
    ms = (jnp.sum(kn * kn, axis=-1, keepdims=True) + jnp.sum(kr * kr, axis=-1, keepdims=True)) * (1.0 / MLA_QK_DIM)
    rs = lax.rsqrt(ms + NORM_EPS)
    k_ref[0, :, :MLA_NOPE_DIM] = (kn * rs * kgn_ref[...]).astype(k_ref.dtype)
    k_ref[0, :, MLA_NOPE_DIM:] = _rope_apply(kr * rs * kgr_ref[...], c_ref, s1_ref, s2_ref).astype(k_ref.dtype)
    vt_ref[0] = kv[:, MLA_NOPE_DIM:].T.astype(vt_ref.dtype)


def _mla_kv_prep(rest, ckv_g, w_ukv, kg_nope, kg_rope_pad, rope_tabs, tm):
    s = rest.shape[0]
    tab = pl.BlockSpec((tm, LANE), lambda i, h: (i, 0))
    hw = MLA_NOPE_DIM + MLA_V_DIM
    return pl.pallas_call(
        _mla_kv_kernel,
        grid=(s // tm, MLA_HEADS),
        in_specs=[pl.BlockSpec((tm, MLA_KV_RANK), lambda i, h: (i, R_CKV // MLA_KV_RANK)),
                  pl.BlockSpec((tm, LANE), lambda i, h: (i, R_KROPE // LANE)),
                  pl.BlockSpec((1, MLA_KV_RANK), lambda i, h: (0, 0)),
                  pl.BlockSpec((MLA_KV_RANK, hw), lambda i, h: (0, h)),
                  pl.BlockSpec((1, LANE), lambda i, h: (0, 0)),
                  pl.BlockSpec((1, LANE), lambda i, h: (0, 0)),
                  tab, tab, tab],
        out_specs=[pl.BlockSpec((1, tm, MLA_QK_PAD), lambda i, h: (h, i, 0)),
                   pl.BlockSpec((1, MLA_V_DIM, tm), lambda i, h: (h, 0, i))],
        out_shape=[jax.ShapeDtypeStruct((MLA_HEADS, s, MLA_QK_PAD), BF16),
                   jax.ShapeDtypeStruct((MLA_HEADS, MLA_V_DIM, s), BF16)],
        scratch_shapes=[pltpu.VMEM((tm, MLA_KV_RANK), BF16)],
        compiler_params=_cp(("arbitrary", "arbitrary")),
        name="mla_kv_prep",
    )(rest, rest, ckv_g, w_ukv, kg_nope, kg_rope_pad, *rope_tabs)


def _mla_attn_kernel(qt_ref, k_ref, vt_ref, o_ref, sa_ref, sb_ref, m_ref, l_ref, acc_ref):
    t = qt_ref.shape[2]
    i = pl.program_id(1)
    qt = qt_ref[0]
    _init_stats(m_ref, l_ref, acc_ref)

    def scores(kb, dst):
        off = _block_offset(kb, t)
        dst[...] = jnp.dot(k_ref[0, pl.ds(off, t), :], qt, preferred_element_type=F32)

    def consume(kb, src, diag):
        off = _block_offset(kb, t)
        s = src[...]
        if diag:
            krow = lax.broadcasted_iota(jnp.int32, s.shape, 0)
            qcol = lax.broadcasted_iota(jnp.int32, s.shape, 1)
            s = jnp.where(krow <= qcol, s, NEG_INF)
        _softmax_pv(0, s, vt_ref[0, :, pl.ds(off, t)], m_ref, l_ref, acc_ref)

    scores(0, sa_ref)

    def far_pair(j, c):
        kb = 2 * j
        scores(kb + 1, sb_ref)
        consume(kb, sa_ref, False)
        scores(kb + 2, sa_ref)
        consume(kb + 1, sb_ref, False)
        return c

    lax.fori_loop(0, i // 2, far_pair, 0)

    @pl.when(i % 2 == 1)
    def _():
        scores(i, sb_ref)
        consume(i - 1, sa_ref, False)

    _by_parity(i, lambda cur, other: consume(i, cur, True), sa_ref, sb_ref)
    o_ref[...] = (acc_ref[0] / l_ref[0]).T.astype(o_ref.dtype)


def _mla_attention(q_t, k, v_t):
    s = k.shape[1]
    t = ATT_T
    return pl.pallas_call(
        _mla_attn_kernel,
        grid=(MLA_HEADS, s // t),
        in_specs=[pl.BlockSpec((1, MLA_QK_PAD, t), lambda h, i: (h, 0, i)),
                  pl.BlockSpec((1, s, MLA_QK_PAD), lambda h, i: (h, 0, 0)),
                  pl.BlockSpec((1, MLA_V_DIM, s), lambda h, i: (h, 0, 0))],
        out_specs=pl.BlockSpec((t, MLA_V_DIM), lambda h, i: (i, h)),
        out_shape=jax.ShapeDtypeStruct((s, MLA_WIDTH), BF16),
        scratch_shapes=[pltpu.VMEM((t, t), F32), pltpu.VMEM((t, t), F32),
                        pltpu.VMEM((1, 1, t), F32), pltpu.VMEM((1, 1, t), F32),
                        pltpu.VMEM((1, MLA_V_DIM, t), F32)],
        compiler_params=_cp(("arbitrary", "arbitrary")),
        name="mla_attention",
    )(q_t, k, v_t)


def _qk_nt(q, k_blk):
    return lax.dot_general(q, k_blk, (((1,), (1,)), ((), ())), preferred_element_type=F32)


def _mem_attn_kernel(q_ref, k_ref, v_ref, qg_ref, o_ref):
    shift = R_MQ % MEM_WINDOW
    qall = q_ref[...].astype(F32)[:, shift:shift + MEM_WIDTH]
    for h in range(MEM_HEADS):
        lo = h * MEM_HEAD_DIM
        qh = qall[:, lo:lo + MEM_HEAD_DIM]
        ms = jnp.mean(qh * qh, axis=-1, keepdims=True)
        qn = (qh * lax.rsqrt(ms + NORM_EPS) * qg_ref[...]).astype(BF16)
        s = _qk_nt(qn, k_ref[:, lo:lo + MEM_HEAD_DIM])
        p = jnp.exp(s - jnp.max(s, axis=-1, keepdims=True))
        l = jnp.sum(p, axis=-1, keepdims=True)
        o = jnp.dot(p.astype(BF16), v_ref[:, lo:lo + MEM_HEAD_DIM], preferred_element_type=F32)
        o_ref[:, lo:lo + MEM_HEAD_DIM] = (o / l).astype(o_ref.dtype)


def _mem_attention(rest, k_mem, v_mem, qg_scaled, tm):
    s = rest.shape[0]
    n_mem = k_mem.shape[0]
    assert R_MQ % MEM_WINDOW + MEM_WIDTH <= MEM_WINDOW
    return pl.pallas_call(
        _mem_attn_kernel,
        grid=(s // tm,),
        in_specs=[pl.BlockSpec((tm, MEM_WINDOW), lambda i: (i, R_MQ // MEM_WINDOW)),
                  pl.BlockSpec((n_mem, MEM_WIDTH), lambda i: (0, 0)),
                  pl.BlockSpec((n_mem, MEM_WIDTH), lambda i: (0, 0)),
                  pl.BlockSpec((1, MEM_HEAD_DIM), lambda i: (0, 0))],
        out_specs=pl.BlockSpec((tm, MEM_WIDTH), lambda i: (i, 0)),
        out_shape=jax.ShapeDtypeStruct((s, MEM_WIDTH), BF16),
        compiler_params=_cp(("parallel",)),
        name="mem_attention",
    )(rest, k_mem, v_mem, qg_scaled)


def _mix_kernel(od_ref, om_ref, oc_ref, wd_ref, wm_ref, wc_ref, g0_ref, g1_ref, g2_ref, o_ref):
    yd = jnp.dot(od_ref[...], wd_ref[...], preferred_element_type=F32)
    ym = jnp.dot(om_ref[...], wm_ref[...], preferred_element_type=F32)
    yc = jnp.dot(oc_ref[...], wc_ref[...], preferred_element_type=F32)
    mixed = (g0_ref[...].astype(F32) * yd + g1_ref[...].astype(F32) * ym) + g2_ref[...].astype(F32) * yc
    o_ref[...] = mixed.astype(o_ref.dtype)


def _mix(o_diff, o_mla, o_mem, w_d, w_m, w_c, gates, tm, tn):
    s = o_diff.shape[0]
    d = w_d.shape[1]
    nj = d // tn
    return pl.pallas_call(
        _mix_kernel,
        grid=(s // tm, nj),
        in_specs=[pl.BlockSpec((tm, o_diff.shape[1]), lambda i, j: (i, 0)),
                  pl.BlockSpec((tm, o_mla.shape[1]), lambda i, j: (i, 0)),
                  pl.BlockSpec((tm, o_mem.shape[1]), lambda i, j: (i, 0)),
                  pl.BlockSpec((w_d.shape[0], tn), lambda i, j: (0, j)),
                  pl.BlockSpec((w_m.shape[0], tn), lambda i, j: (0, j)),
                  pl.BlockSpec((w_c.shape[0], tn), lambda i, j: (0, j)),
                  pl.BlockSpec((tm, tn), lambda i, j: (i, j)),
                  pl.BlockSpec((tm, tn), lambda i, j: (i, nj + j)),
                  pl.BlockSpec((tm, tn), lambda i, j: (i, 2 * nj + j))],
        out_specs=pl.BlockSpec((tm, tn), lambda i, j: (i, j)),
        out_shape=jax.ShapeDtypeStruct((s, d), BF16),
        compiler_params=_cp(("arbitrary", "arbitrary")),
        name="gated_mix",
    )(o_diff, o_mla, o_mem, w_d, w_m, w_c, gates, gates, gates)


def _router_kernel(x_ref, g_ref, w_ref, b_ref, h_ref, r_ref):
    x = x_ref[...]
    ms = jnp.mean(x * x, axis=-1, keepdims=True)
    h = x * lax.rsqrt(ms + NORM_EPS) * g_ref[...]
    h_ref[...] = h
    logits = jnp.dot(h, w_ref[...], preferred_element_type=F32, precision=lax.Precision.HIGHEST) + b_ref[...]
    lane = lax.broadcasted_iota(jnp.int32, logits.shape, 1)
    lane_f = lane.astype(F32)
    big = float(4 * ROUTE_W)
    lg = jnp.where(lane < N_GROUPS, logits, -jnp.inf)
    gmax = jnp.max(lg, axis=-1, keepdims=True)
    gidx = jnp.min(jnp.where(lg == gmax, lane_f, big), axis=-1, keepdims=True)
    pg_top = 1.0 / jnp.sum(jnp.exp(lg - gmax), axis=-1, keepdims=True)
    e_lane = lane - N_GROUPS
    lane_group = jnp.right_shift(e_lane, 3).astype(F32)
    in_group = (e_lane >= 0) & (e_lane < N_EXPERTS) & (lane_group == gidx)
    le = jnp.where(in_group, logits, -jnp.inf)
    e1 = jnp.max(le, axis=-1, keepdims=True)
    i1 = jnp.min(jnp.where(le == e1, lane_f, big), axis=-1, keepdims=True)
    le2 = jnp.where(lane_f == i1, -jnp.inf, le)
    e2 = jnp.max(le2, axis=-1, keepdims=True)
    i2 = jnp.min(jnp.where(le2 == e2, lane_f, big), axis=-1, keepdims=True)
    w2 = jnp.exp(e2 - e1)
    inv = 1.0 / (1.0 + w2)
    gate1 = pg_top * inv
    gate2 = pg_top * (w2 * inv)
    out = jnp.where(lane == 0, i1 - N_GROUPS,
                    jnp.where(lane == 1, i2 - N_GROUPS,
                              jnp.where(lane == 2, gate1, jnp.where(lane == 3, gate2, 0.0))))
    r_ref[...] = out


def _router(x1, g, w_r, b_r, tm):
    s, d = x1.shape
    return pl.pallas_call(
        _router_kernel,
        grid=(s // tm,),
        in_specs=[pl.BlockSpec((tm, d), lambda i: (i, 0)),
                  pl.BlockSpec((1, d), lambda i: (0, 0)),
                  pl.BlockSpec((d, ROUTE_W), lambda i: (0, 0)),
                  pl.BlockSpec((1, ROUTE_W), lambda i: (0, 0))],
        out_specs=[pl.BlockSpec((tm, d), lambda i: (i, 0)),
                   pl.BlockSpec((tm, ROUTE_W), lambda i: (i, 0))],
        out_shape=[jax.ShapeDtypeStruct((s, d), F32),
                   jax.ShapeDtypeStruct((s, ROUTE_W), F32)],
        compiler_params=_cp(("parallel",)),
        name="ffn_norm_router",
    )(x1, g, w_r, b_r)


def _pack_bf16_pairs(v):
    n = v.shape[1] // 2
    bits = lax.bitcast_convert_type(v.astype(BF16).astype(F32), jnp.uint32)
    return jnp.right_shift(bits[:, :n], jnp.uint32(16)) | bits[:, n:]


def _moe_kernel(be_ref, ns_ref, nu_ref, tok_ref, bw_ref, h_ref, wg_ref, wu_ref, wd_ref, o_ref,
                x_ref, y_ref, wgb_ref, wub_ref, wdb_ref, sem):
    b = pl.program_id(0)
    f = pl.program_id(1)
    ns = ns_ref[b]
    rows = ns * MOE_SUB

    @pl.when(f == 0)
    def _():
        def issue(r, c):
            tkn = tok_ref[0, 0, r]
            pltpu.make_async_copy(h_ref.at[pl.ds(tkn, 1), :], x_ref.at[pl.ds(r, 1), :], sem).start()
            return c

        lax.fori_loop(0, rows, issue, 0)

        def drain(r, c):
            pltpu.make_async_copy(h_ref.at[pl.ds(0, 1), :], x_ref.at[pl.ds(0, 1), :], sem).wait()
            return c

        lax.fori_loop(0, rows, drain, 0)

    @pl.when(ns > 0)
    def _():
        wgb_ref[...] = wg_ref[0].astype(BF16)
        wub_ref[...] = wu_ref[0].astype(BF16)
        wdb_ref[...] = wd_ref[0].astype(BF16)

    def sub(sb, c):
        r0 = pl.multiple_of(sb * MOE_SUB, MOE_SUB)
        xs = x_ref[pl.ds(r0, MOE_SUB), :].astype(BF16)
        g = jnp.dot(xs, wgb_ref[...], preferred_element_type=F32)
        u = jnp.dot(xs, wub_ref[...], preferred_element_type=F32)
        a = ((g * jax.nn.sigmoid(g)) * u).astype(BF16)
        yv = jnp.dot(a, wdb_ref[...], preferred_element_type=F32)

        @pl.when(f == 0)
        def _():
            y_ref[pl.ds(r0, MOE_SUB), :] = yv

        @pl.when(f > 0)
        def _():
            y_ref[pl.ds(r0, MOE_SUB), :] += yv

        return c

    lax.fori_loop(0, ns, sub, 0)

    @pl.when(f == MOE_NF - 1)
    def _():
        for sb in range(MOE_TB // MOE_SUB):
            sl = slice(sb * MOE_SUB, (sb + 1) * MOE_SUB)

            @pl.when(sb < ns)
            def _():
                o_ref[sl, :] = _pack_bf16_pairs(y_ref[sl, :] * bw_ref[sl, :])

            @pl.when(sb >= ns)
            def _():
                o_ref[sl, :] = jnp.zeros((MOE_SUB, o_ref.shape[1]), jnp.uint32)


def _moe_experts(block_e, nsub, n_used, buf_tok, buf_w, h2, w_gate, w_up, w_down):
    n_blocks = block_e.shape[0]
    d = h2.shape[1]
    fc = EXPERT_FF // MOE_NF

    def wsel(f, b, nu):
        return jnp.where(b < nu[0], f, MOE_NF - 1)

    grid_spec = pltpu.PrefetchScalarGridSpec(
        num_scalar_prefetch=3,
        grid=(n_blocks, MOE_NF),
        in_specs=[pl.BlockSpec((1, 1, MOE_TB), lambda b, f, be, ns, nu: (b, 0, 0), memory_space=pltpu.SMEM),
                  pl.BlockSpec((MOE_TB, 1), lambda b, f, be, ns, nu: (b, 0)),
                  pl.BlockSpec(memory_space=pl.ANY),
                  pl.BlockSpec((1, d, fc), lambda b, f, be, ns, nu: (be[b], 0, wsel(f, b, nu))),
                  pl.BlockSpec((1, d, fc), lambda b, f, be, ns, nu: (be[b], 0, wsel(f, b, nu))),
                  pl.BlockSpec((1, fc, d), lambda b, f, be, ns, nu: (be[b], wsel(f, b, nu), 0))],
        out_specs=pl.BlockSpec((MOE_TB, d // 2), lambda b, f, be, ns, nu: (b, 0)),
        scratch_shapes=[pltpu.VMEM((MOE_TB, d), F32), pltpu.VMEM((MOE_TB, d), F32),
                        pltpu.VMEM((d, fc), BF16), pltpu.VMEM((d, fc), BF16), pltpu.VMEM((fc, d), BF16),
                        pltpu.SemaphoreType.DMA(())],
    )
    return pl.pallas_call(
        _moe_kernel,
        grid_spec=grid_spec,
        out_shape=jax.ShapeDtypeStruct((n_blocks * MOE_TB, d // 2), jnp.uint32),
        compiler_params=_cp(("arbitrary", "arbitrary")),
        name="moe_experts",
    )(block_e, nsub, n_used, buf_tok.reshape(n_blocks, 1, MOE_TB), buf_w.reshape(-1, 1), h2, w_gate, w_up, w_down)


def _combine_kernel(slot_ref, x_ref, yb_ref, o_ref, g_ref, sem):
    tm = x_ref.shape[0]
    n = g_ref.shape[2]

    def issue(r, c):
        for k in range(TOP_K):
            sl = slot_ref[0, 0, r * TOP_K + k]
            pltpu.make_async_copy(yb_ref.at[pl.ds(sl, 1), :], g_ref.at[k, pl.ds(r, 1), :], sem).start()
        return c

    lax.fori_loop(0, tm, issue, 0)

    def drain(r, c):
        pltpu.make_async_copy(yb_ref.at[pl.ds(0, 1), :], g_ref.at[0, pl.ds(0, 1), :], sem).wait()
        return c

    lax.fori_loop(0, tm * TOP_K, drain, 0)

    g0 = g_ref[0]
    g1 = g_ref[1]
    hi_mask = jnp.uint32(0xFFFF0000)
    lo = (lax.bitcast_convert_type(jnp.left_shift(g0, jnp.uint32(16)), F32)
          + lax.bitcast_convert_type(jnp.left_shift(g1, jnp.uint32(16)), F32))
    hi = (lax.bitcast_convert_type(g0 & hi_mask, F32) + lax.bitcast_convert_type(g1 & hi_mask, F32))
    o_ref[:, :n] = x_ref[:, :n] + lo
    o_ref[:, n:] = x_ref[:, n:] + hi


def _combine(slots, x1, yb, tm):
    s, d = x1.shape
    return pl.pallas_call(
        _combine_kernel,
        grid=(s // tm,),
        in_specs=[pl.BlockSpec((1, 1, tm * TOP_K), lambda i: (i, 0, 0), memory_space=pltpu.SMEM),
                  pl.BlockSpec((tm, d), lambda i: (i, 0)),
                  pl.BlockSpec(memory_space=pl.ANY)],
        out_specs=pl.BlockSpec((tm, d), lambda i: (i, 0)),
        out_shape=jax.ShapeDtypeStruct((s, d), F32),
        scratch_shapes=[pltpu.VMEM((TOP_K, tm, d // 2), jnp.uint32), pltpu.SemaphoreType.DMA(())],
        compiler_params=_cp(("arbitrary",)),
        name="moe_combine",
    )(slots.reshape(s // tm, 1, tm * TOP_K), x1, yb)


def _dispatch_plan(route, s):
    a = s * TOP_K
    flat_e = route[:, :TOP_K].astype(jnp.int32).reshape(a)
    flat_w = route[:, TOP_K:2 * TOP_K].reshape(a)
    onehot = (flat_e[:, None] == jnp.arange(N_EXPERTS, dtype=jnp.int32)[None, :]).astype(jnp.int32)
    csum = jnp.cumsum(onehot, axis=0)
    counts = csum[-1]
    rank = jnp.sum((csum - onehot) * onehot, axis=1)
    padded = (counts + MOE_TB - 1) // MOE_TB * MOE_TB
    pad_end = jnp.cumsum(padded)
    pad_start = pad_end - padded
    dest = pad_start[flat_e] + rank
    n_blocks = a // MOE_TB + N_EXPERTS
    p_rows = n_blocks * MOE_TB
    buf_tok = jnp.zeros((p_rows,), jnp.int32).at[dest].set(jnp.arange(a, dtype=jnp.int32) // TOP_K)
    buf_w = jnp.zeros((p_rows,), F32).at[dest].set(flat_w)
    starts = jnp.arange(n_blocks, dtype=jnp.int32) * MOE_TB
    block_e = jnp.minimum(jnp.searchsorted(pad_end, starts, side='right'), N_EXPERTS - 1).astype(jnp.int32)
    valid = jnp.clip(counts[block_e] - (starts - pad_start[block_e]), 0, MOE_TB)
    valid = jnp.where(starts < pad_end[-1], valid, 0)
    nsub = ((valid + MOE_SUB - 1) // MOE_SUB).astype(jnp.int32)
    n_used = (pad_end[-1:] // MOE_TB).astype(jnp.int32)
    return block_e, nsub, n_used, buf_tok, buf_w, dest.astype(jnp.int32)


def _rope_tables(positions):
    half = MLA_ROPE_DIM // 2
    inv_freq = ROPE_THETA ** (-jnp.arange(half, dtype=F32) / half)
    ang = positions.astype(F32)[:, None] * inv_freq[None, :]
    cos, sin = jnp.cos(ang), jnp.sin(ang)
    z = jnp.zeros_like(cos)
    c = jnp.concatenate([cos, cos, z, z], axis=-1)
    s1 = jnp.concatenate([-sin, z, z, z], axis=-1)
    s2 = jnp.concatenate([z, sin, z, z], axis=-1)
    return c, s1, s2


def kernel(x, mem, positions, rel_bias, mix_norm_g, w_in, diff_q_norm_g, diff_k_norm_g, diff_lambda_q1, diff_lambda_k1, diff_lambda_q2, diff_lambda_k2, diff_subln_g, mla_cq_norm_g, mla_ckv_norm_g, mla_w_uq, mla_w_ukv, mla_q_norm_g, mla_k_norm_g, mem_norm_g, mem_w_kv, mem_q_norm_g, mem_k_norm_g, w_o_diff, w_o_mla, w_o_mem, w_out, ffn_norm_g, w_route_group, b_route_group, w_route_expert, b_route_expert, w_exp_gate, w_exp_up, w_exp_down):
    b, s, d = x.shape
    assert b == 1 and s % ATT_T == 0
    depth = mix_norm_g.shape[0]
    xs = x.reshape(s, d)
    pos = positions.reshape(s)
    rope_tabs = _rope_tables(pos)
    row = lambda v: v.reshape(1, -1).astype(F32)

    for l in range(depth):
        lam_init = 0.8 - 0.6 * math.exp(-0.3 * l)
        h = _rmsnorm_rows(xs, mix_norm_g[l], 256)
        tn = 512
        q_gain = jnp.tile(diff_q_norm_g[l] * (DIFF_HEAD_DIM ** -0.5 * LOG2E), DIFF_MAPS)
        dq_t = _matmul(h, w_in[l], col0=OFF_DQ, n_cols=DIFF_QK_WIDTH, tm=512, tn=tn, out_dtype=BF16,
                       mode="groupnorm", extra=q_gain, group=DIFF_HEAD_DIM, transpose_out=True, name="diff_q_proj")
        dk = _matmul(h, w_in[l], col0=OFF_DK, n_cols=DIFF_QK_WIDTH, tm=512, tn=tn, out_dtype=BF16,
                     mode="groupnorm", extra=jnp.tile(diff_k_norm_g[l], DIFF_MAPS), group=DIFF_HEAD_DIM,
                     name="diff_k_proj")
        dv_t = _matmul(h, w_in[l], col0=OFF_DV, n_cols=DIFF_WIDTH, tm=512, tn=tn, out_dtype=BF16,
                       transpose_out=True, name="diff_v_proj")
        rest = _matmul(h, w_in[l], col0=OFF_CQ, n_cols=REST_WIDTH, tm=512, tn=tn, out_dtype=BF16, name="rest_proj")
        gates = _matmul(h, w_in[l], col0=OFF_GATES, n_cols=3 * d, tm=512, tn=tn, out_dtype=BF16, mode="sigmoid",
                        name="gate_proj")

        bias_tiles = _diff_bias_tiles(rel_bias, ATT_T)
        lam_vecs = [row(diff_lambda_q1[l]), row(diff_lambda_k1[l]), row(diff_lambda_q2[l]), row(diff_lambda_k2[l])]
        o_diff = _diff_attention(dq_t, dk, dv_t, bias_tiles, lam_vecs, diff_subln_g[l], lam_init)

        w_uq_heads = jnp.pad(
            mla_w_uq[l].reshape(MLA_Q_RANK, MLA_HEADS, MLA_QK_DIM).transpose(1, 0, 2),
            ((0, 0), (0, 0), (0, MLA_QK_PAD - MLA_QK_DIM))).astype(BF16)
        qg_pad = jnp.pad(mla_q_norm_g[l] * (MLA_QK_DIM ** -0.5 * LOG2E),
                         (0, MLA_QK_PAD - MLA_QK_DIM)).reshape(1, -1).astype(F32)
        q_mla_t = _mla_q_prep(rest, row(mla_cq_norm_g[l]), w_uq_heads, qg_pad, rope_tabs, 512)
        kg = mla_k_norm_g[l]
        kg_nope = row(kg[:MLA_NOPE_DIM])
        kg_rope = jnp.pad(kg[MLA_NOPE_DIM:], (0, LANE - MLA_ROPE_DIM)).reshape(1, -1).astype(F32)
        k_mla, v_mla_t = _mla_kv_prep(rest, row(mla_ckv_norm_g[l]), mla_w_ukv[l], kg_nope, kg_rope, rope_tabs, 512)
        o_mla = _mla_attention(q_mla_t, k_mla, v_mla_t)

        n_mem = mem.shape[1]
        mem_h = _rmsnorm_rows(mem.reshape(n_mem, d), mem_norm_g[l], n_mem)
        k_mem = _matmul(mem_h, mem_w_kv[l], col0=0, n_cols=MEM_WIDTH, tm=n_mem, tn=512, out_dtype=BF16,
                        mode="groupnorm", extra=jnp.tile(mem_k_norm_g[l], MEM_HEADS), group=MEM_HEAD_DIM,
                        name="mem_k_proj")
        v_mem = _matmul(mem_h, mem_w_kv[l], col0=MEM_WIDTH, n_cols=MEM_WIDTH, tm=n_mem, tn=512, out_dtype=BF16,
                        name="mem_v_proj")
        o_mem = _mem_attention(rest, k_mem, v_mem, row(mem_q_norm_g[l] * MEM_HEAD_DIM ** -0.5), 512)

        mixed = _mix(o_diff, o_mla, o_mem, w_o_diff[l].astype(BF16), w_o_mla[l].astype(BF16),
                     w_o_mem[l].astype(BF16), gates, 512, 512)
        x1 = _matmul(mixed, w_out[l], n_cols=d, tm=512, tn=512, out_dtype=F32, mode="residual", extra=xs,
                     name="out_proj")

        w_r = jnp.pad(jnp.concatenate([w_route_group[l], w_route_expert[l]], axis=1),
                      ((0, 0), (0, ROUTE_W - N_GROUPS - N_EXPERTS))).astype(F32)
        b_r = jnp.pad(jnp.concatenate([b_route_group[l], b_route_expert[l]]),
                      (0, ROUTE_W - N_GROUPS - N_EXPERTS)).reshape(1, -1).astype(F32)
        h2, route = _router(x1, row(ffn_norm_g[l]), w_r, b_r, 256)
        block_e, nsub, n_used, buf_tok, buf_w, slots = _dispatch_plan(route, s)
        yb = _moe_experts(block_e, nsub, n_used, buf_tok, buf_w, h2, w_exp_gate[l], w_exp_up[l], w_exp_down[l])
        xs = _combine(slots, x1, yb, 256)
    return xs.reshape(b, s, d)
```

```python
import functools
import math

import jax
import jax.numpy as jnp
from jax import lax
from jax.experimental import pallas as pl
from jax.experimental.pallas import tpu as pltpu

F32 = jnp.float32
BF16 = jnp.bfloat16

NORM_EPS = 1e-6
NEG_INF = -1e30
LOG2E = math.log2(math.e)

DIFF_HEADS = 6
DIFF_HEAD_DIM = 128
DIFF_V_DIM = 256
DIFF_MAPS = 12
DIFF_QK_WIDTH = 1536
DIFF_WIDTH = 1536
MLA_HEADS = 12
MLA_Q_RANK = 1536
MLA_KV_RANK = 512
MLA_NOPE_DIM = 128
MLA_ROPE_DIM = 64
MLA_QK_DIM = 192
MLA_QK_PAD = 256
MLA_V_DIM = 128
MLA_WIDTH = 1536
ROPE_THETA = 10000.0
MEM_HEADS = 4
MEM_HEAD_DIM = 256
MEM_WIDTH = 1024
REL_BUCKETS = 32
REL_MAX_DIST = 128
N_GROUPS = 8
EXPERTS_PER_GROUP = 8
N_EXPERTS = 64
TOP_K = 2
EXPERT_FF = 512

OFF_DQ = 0
OFF_DK = 1536
OFF_DV = 3072
OFF_CQ = 4608
OFF_CKV = 6144
OFF_KROPE = 6656
OFF_MQ = 6720
OFF_GATES = 7744
REST_WIDTH = 4096
R_CQ = OFF_CQ - OFF_CQ
R_CKV = OFF_CKV - OFF_CQ
R_KROPE = OFF_KROPE - OFF_CQ
R_MQ = OFF_MQ - OFF_CQ

LANE = 128
VMEM_LIMIT = 52 * 1024 * 1024

ATT_T = 512
MOE_TB = 512
MOE_RU = 128
MOE_PH = 4
ROUTE_W = 128
MEM_WINDOW = 2048


def _cp(sem, vmem=VMEM_LIMIT):
    return pltpu.CompilerParams(dimension_semantics=sem, vmem_limit_bytes=vmem)


def _rmsnorm_kernel(x_ref, g_ref, o_ref):
    x = x_ref[...].astype(F32)
    ms = jnp.mean(x * x, axis=-1, keepdims=True)
    o_ref[...] = (x * lax.rsqrt(ms + NORM_EPS) * g_ref[...]).astype(o_ref.dtype)


def _rmsnorm_rows(x, g, tm, out_dtype=BF16):
    m, d = x.shape
    return pl.pallas_call(
        _rmsnorm_kernel,
        grid=(m // tm,),
        in_specs=[pl.BlockSpec((tm, d), lambda i: (i, 0)),
                  pl.BlockSpec((1, d), lambda i: (0, 0))],
        out_specs=pl.BlockSpec((tm, d), lambda i: (i, 0)),
        out_shape=jax.ShapeDtypeStruct((m, d), out_dtype),
        compiler_params=_cp(("parallel",)),
        name="rmsnorm_rows",
    )(x, g.reshape(1, d).astype(F32))


def _cast_shifted(w_ref, w2_ref, wb_ref, shift):
    k = w_ref.shape[0]
    rows = 256

    def body(c, carry):
        r0 = pl.multiple_of(c * rows, rows)
        main = w_ref[pl.ds(r0, rows), :]
        tail = w2_ref[pl.ds(r0, rows), :]
        wb_ref[pl.ds(r0, rows), :] = jnp.concatenate([main[:, shift:], tail[:, :shift]], axis=1).astype(BF16)
        return carry

    lax.fori_loop(0, k // rows, body, 0)


def _mm_kernel(*refs, mode, cast, group, shift, transpose_out):
    a_ref, w_ref = refs[0], refs[1]
    pos = 2
    w2_ref = None
    if shift:
        w2_ref = refs[pos]
        pos += 1
    extra = None
    if mode in ("groupnorm", "residual"):
        extra = refs[pos]
        pos += 1
    o_ref = refs[pos]
    wb_ref = refs[pos + 1] if cast else None
    i = pl.program_id(1)
    if cast:
        @pl.when(i == 0)
        def _():
            if shift:
                _cast_shifted(w_ref, w2_ref, wb_ref, shift)
            else:
                wb_ref[...] = w_ref[...].astype(BF16)
        w = wb_ref[...]
    else:
        w = w_ref[...]
    acc = jnp.dot(a_ref[...], w, preferred_element_type=F32)
    if transpose_out:
        acc = acc.T
    if mode == "plain":
        o_ref[...] = acc.astype(o_ref.dtype)
    elif mode == "sigmoid":
        o_ref[...] = jax.nn.sigmoid(acc).astype(o_ref.dtype)
    elif mode == "residual":
        o_ref[...] = (extra[...] + acc).astype(o_ref.dtype)
    elif mode == "groupnorm":
        tn = w.shape[1]
        for c in range(tn // group):
            sl = slice(c * group, (c + 1) * group)
            if transpose_out:
                blk = acc[sl, :]
                ms = jnp.mean(blk * blk, axis=0, keepdims=True)
                o_ref[sl, :] = (blk * lax.rsqrt(ms + NORM_EPS) * extra[sl, :]).astype(o_ref.dtype)
            else:
                blk = acc[:, sl]
                ms = jnp.mean(blk * blk, axis=-1, keepdims=True)
                o_ref[:, sl] = (blk * lax.rsqrt(ms + NORM_EPS) * extra[:, sl]).astype(o_ref.dtype)


def _matmul(a, w, *, n_cols, tm, tn, out_dtype, col0=0, mode="plain", extra=None, group=LANE,
            transpose_out=False, name="matmul"):
    m, k = a.shape
    assert m % tm == 0 and n_cols % tn == 0 and w.shape[0] == k
    cast = w.dtype != BF16
    base, shift = divmod(col0, tn)
    assert shift <= LANE and (shift == 0 or cast)
    in_specs = [pl.BlockSpec((tm, k), lambda j, i: (i, 0)),
                pl.BlockSpec((k, tn), lambda j, i: (0, base + j))]
    args = [a, w]
    if shift:
        in_specs.append(pl.BlockSpec((k, LANE), lambda j, i: (0, (base + j + 1) * (tn // LANE))))
        args.append(w)
    if mode == "groupnorm":
        gain = extra.reshape(-1, 1) if transpose_out else extra.reshape(1, -1)
        in_specs.append(pl.BlockSpec((tn, 1), lambda j, i: (j, 0)) if transpose_out
                        else pl.BlockSpec((1, tn), lambda j, i: (0, j)))
        args.append(gain.astype(F32))
    elif mode == "residual":
        assert not transpose_out
        in_specs.append(pl.BlockSpec((tm, tn), lambda j, i: (i, j)))
        args.append(extra)
    if transpose_out:
        out_spec = pl.BlockSpec((tn, tm), lambda j, i: (j, i))
        out_shape = jax.ShapeDtypeStruct((n_cols, m), out_dtype)
    else:
        out_spec = pl.BlockSpec((tm, tn), lambda j, i: (i, j))
        out_shape = jax.ShapeDtypeStruct((m, n_cols), out_dtype)
    scratch = [pltpu.VMEM((k, tn), BF16)] if cast else []
    return pl.pallas_call(
        functools.partial(_mm_kernel, mode=mode, cast=cast, group=group, shift=shift, transpose_out=transpose_out),
        grid=(n_cols // tn, m // tm),
        in_specs=in_specs,
        out_specs=out_spec,
        out_shape=out_shape,
        scratch_shapes=scratch,
        compiler_params=_cp(("arbitrary", "arbitrary")),
        name=name,
    )(*args)


def _mm_nt_kernel(*refs, mode, group, transpose_out):
    a_ref, wt_ref = refs[0], refs[1]
    extra = refs[2] if mode == "groupnorm" else None
    o_ref, wb_ref = refs[-2], refs[-1]
    i = pl.program_id(1)

    @pl.when(i == 0)
    def _():
        wb_ref[...] = wt_ref[...].astype(BF16)

    contract_last = (((1,), (1,)), ((), ()))
    if transpose_out:
        acc = lax.dot_general(wb_ref[...], a_ref[...], contract_last, preferred_element_type=F32)
    else:
        acc = lax.dot_general(a_ref[...], wb_ref[...], contract_last, preferred_element_type=F32)
    if mode == "plain":
        o_ref[...] = acc.astype(o_ref.dtype)
    elif mode == "sigmoid":
        o_ref[...] = jax.nn.sigmoid(acc).astype(o_ref.dtype)
    elif mode == "groupnorm":
        tn = wb_ref.shape[0]
        for c in range(tn // group):
            sl = slice(c * group, (c + 1) * group)
            if transpose_out:
                blk = acc[sl, :]
                ms = jnp.mean(blk * blk, axis=0, keepdims=True)
                o_ref[sl, :] = (blk * lax.rsqrt(ms + NORM_EPS) * extra[sl, :]).astype(o_ref.dtype)
            else:
                blk = acc[:, sl]
                ms = jnp.mean(blk * blk, axis=-1, keepdims=True)
                o_ref[:, sl] = (blk * lax.rsqrt(ms + NORM_EPS) * extra[:, sl]).astype(o_ref.dtype)


def _matmul_nt(a, wt, *, row0, n_cols, tm, tn, out_dtype, mode="plain", extra=None, group=LANE,
               transpose_out=False, name="matmul_nt"):
    m, k = a.shape
    assert m % tm == 0 and n_cols % tn == 0 and wt.shape[1] == k and row0 % 8 == 0
    in_specs = [pl.BlockSpec((tm, k), lambda j, i: (i, 0)),
                pl.BlockSpec((pl.Element(tn), pl.Element(k)), lambda j, i: (pl.multiple_of(row0 + j * tn, 8), 0))]
    args = [a, wt]
    if mode == "groupnorm":
        gain = extra.reshape(-1, 1) if transpose_out else extra.reshape(1, -1)
        in_specs.append(pl.BlockSpec((tn, 1), lambda j, i: (j, 0)) if transpose_out
                        else pl.BlockSpec((1, tn), lambda j, i: (0, j)))
        args.append(gain.astype(F32))
    if transpose_out:
        out_spec = pl.BlockSpec((tn, tm), lambda j, i: (j, i))
        out_shape = jax.ShapeDtypeStruct((n_cols, m), out_dtype)
    else:
        out_spec = pl.BlockSpec((tm, tn), lambda j, i: (i, j))
        out_shape = jax.ShapeDtypeStruct((m, n_cols), out_dtype)
    return pl.pallas_call(
        functools.partial(_mm_nt_kernel, mode=mode, group=group, transpose_out=transpose_out),
        grid=(n_cols // tn, m // tm),
        in_specs=in_specs,
        out_specs=out_spec,
        out_shape=out_shape,
        scratch_shapes=[pltpu.VMEM((tn, k), BF16)],
        compiler_params=_cp(("arbitrary", "arbitrary")),
        name=name,
    )(*args)


def _softmax_pv(idx, s, vt_blk, m_ref, l_ref, acc_ref):
    m_prev = m_ref[idx]
    m_new = jnp.maximum(m_prev, jnp.max(s, axis=0, keepdims=True))
    alpha = jnp.exp2(m_prev - m_new)
    p = jnp.exp2(s - m_new)
    l_ref[idx] = alpha * l_ref[idx] + jnp.sum(p, axis=0, keepdims=True)
    acc_ref[idx] = alpha * acc_ref[idx] + jnp.dot(vt_blk, p.astype(BF16), preferred_element_type=F32)
    m_ref[idx] = m_new


def _block_offset(kb, t):
    return kb * t if isinstance(kb, int) else pl.multiple_of(kb * t, t)


def _by_parity(kb, fn, even_ref, odd_ref):
    @pl.when(kb % 2 == 0)
    def _():
        fn(even_ref, odd_ref)

    @pl.when(kb % 2 == 1)
    def _():
        fn(odd_ref, even_ref)


def _init_stats(m_ref, l_ref, acc_ref):
    m_ref[...] = jnp.full(m_ref.shape, NEG_INF, F32)
    l_ref[...] = jnp.zeros(l_ref.shape, F32)
    acc_ref[...] = jnp.zeros(acc_ref.shape, F32)


def _diff_attn_kernel(lq1_ref, lk1_ref, lq2_ref, lk2_ref, q1_ref, q2_ref, k1_ref, k2_ref, vt_ref,
                      b1_ref, b2_ref, g_ref, o_ref, sa_ref, sb_ref, m_ref, l_ref, acc_ref, *, lam_init):
    t = q1_ref.shape[1]
    i = pl.program_id(1)
    qts = (q1_ref[...], q2_ref[...])
    ks = (k1_ref, k2_ref)
    bs = (b1_ref, b2_ref)
    _init_stats(m_ref, l_ref, acc_ref)

    def scores(kb, dst):
        off = _block_offset(kb, t)
        for mp in range(2):
            dst[mp] = jnp.dot(ks[mp][pl.ds(off, t), :], qts[mp], preferred_element_type=F32)

    def consume(kb, src, bias_idx):
        off = _block_offset(kb, t)
        vt_blk = vt_ref[:, pl.ds(off, t)]
        for mp in range(2):
            s = src[mp]
            if bias_idx is not None:
                s = s + bs[mp][0, bias_idx]
            _softmax_pv(mp, s, vt_blk, m_ref, l_ref, acc_ref)

    def step(kb, bias_idx, last=False):
        def run(cur, nxt):
            if not last:
                scores(kb + 1, nxt)
            consume(kb, cur, bias_idx)
        _by_parity(kb, run, sa_ref, sb_ref)

    n_far = jnp.maximum(i - 1, 0)
    scores(0, sa_ref)

    def far_pair(j, c):
        kb = 2 * j
        scores(kb + 1, sb_ref)
        consume(kb, sa_ref, None)
        scores(kb + 2, sa_ref)
        consume(kb + 1, sb_ref, None)
        return c

    lax.fori_loop(0, n_far // 2, far_pair, 0)

    @pl.when(n_far % 2 == 1)
    def _():
        scores(n_far, sb_ref)
        consume(n_far - 1, sa_ref, None)

    @pl.when(i >= 1)
    def _():
        step(i - 1, 1)

    step(i, 0, last=True)

    lam = (jnp.exp(jnp.sum(lq1_ref[...] * lk1_ref[...], axis=-1, keepdims=True))
           - jnp.exp(jnp.sum(lq2_ref[...] * lk2_ref[...], axis=-1, keepdims=True)) + lam_init)
    o = acc_ref[0] / l_ref[0] - lam * (acc_ref[1] / l_ref[1])
    ms = jnp.mean(o * o, axis=0, keepdims=True)
    o = (o * lax.rsqrt(ms + NORM_EPS) * g_ref[...]) * (1.0 - lam_init)
    o_ref[...] = o.T.astype(o_ref.dtype)


def _diff_attention(q_t, k, v_t, bias_tiles, lam_vecs, subln_g, lam_init):
    s = k.shape[0]
    t = ATT_T
    hd, vd = DIFF_HEAD_DIM, DIFF_V_DIM
    vec = pl.BlockSpec((1, hd), lambda h, i: (0, 0))
    in_specs = [vec, vec, vec, vec,
                pl.BlockSpec((hd, t), lambda h, i: (h, i)),
                pl.BlockSpec((hd, t), lambda h, i: (DIFF_HEADS + h, i)),
                pl.BlockSpec((s, hd), lambda h, i: (0, h)),
                pl.BlockSpec((s, hd), lambda h, i: (0, DIFF_HEADS + h)),
                pl.BlockSpec((vd, s), lambda h, i: (h, 0)),
                pl.BlockSpec((1, 2, t, t), lambda h, i: (h, 0, 0, 0)),
                pl.BlockSpec((1, 2, t, t), lambda h, i: (DIFF_HEADS + h, 0, 0, 0)),
                pl.BlockSpec((vd, 1), lambda h, i: (0, 0))]
    return pl.pallas_call(
        functools.partial(_diff_attn_kernel, lam_init=lam_init),
        grid=(DIFF_HEADS, s // t),
        in_specs=in_specs,
        out_specs=pl.BlockSpec((t, vd), lambda h, i: (i, h)),
        out_shape=jax.ShapeDtypeStruct((s, DIFF_WIDTH), BF16),
        scratch_shapes=[pltpu.VMEM((2, t, t), F32), pltpu.VMEM((2, t, t), F32),
                        pltpu.VMEM((2, 1, t), F32), pltpu.VMEM((2, 1, t), F32),
                        pltpu.VMEM((2, vd, t), F32)],
        compiler_params=_cp(("arbitrary", "arbitrary")),
        name="diff_attention",
    )(*lam_vecs, q_t, q_t, k, k, v_t, bias_tiles, bias_tiles, subln_g.reshape(vd, 1).astype(F32))


def _t5_bucket(dist):
    n = jnp.maximum(dist, 0)
    max_exact = REL_BUCKETS // 2
    nf = jnp.maximum(n, 1).astype(F32)
    large = max_exact + (jnp.log(nf / max_exact) / math.log(REL_MAX_DIST / max_exact)
                         * (REL_BUCKETS - max_exact)).astype(jnp.int32)
    large = jnp.minimum(large, REL_BUCKETS - 1)
    return jnp.where(n < max_exact, n, large)


def _diff_bias_tiles(rel_bias, t):
    assert t >= REL_MAX_DIST
    table = rel_bias.astype(F32)
    table = (table - table[REL_BUCKETS - 1:REL_BUCKETS]) * LOG2E
    kr = jnp.arange(t, dtype=jnp.int32)[:, None]
    qc = jnp.arange(t, dtype=jnp.int32)[None, :]
    d0 = qc - kr
    buckets = jnp.arange(REL_BUCKETS, dtype=jnp.int32)

    def lookup(dist):
        onehot = (_t5_bucket(dist)[:, :, None] == buckets).astype(F32)
        return jnp.einsum('rcb,bm->mrc', onehot, table, precision=lax.Precision.HIGHEST)

    tile0 = jnp.where((d0 >= 0)[None], lookup(d0), NEG_INF)
    return jnp.stack([tile0, lookup(d0 + t)], axis=1)


def _rope_apply(tv, c_ref, s1_ref, s2_ref):
    return (tv * c_ref[...] + pltpu.roll(tv, 96, 1) * s1_ref[...] + pltpu.roll(tv, 32, 1) * s2_ref[...])


def _mla_q_kernel(cq_ref, g_ref, w_ref, qg_ref, c_ref, s1_ref, s2_ref, o_ref, xg_ref):
    h = pl.program_id(1)

    @pl.when(h == 0)
    def _():
        c = cq_ref[...].astype(F32)
        r = lax.rsqrt(jnp.mean(c * c, axis=-1, keepdims=True) + NORM_EPS)
        xg_ref[...] = (c * r * g_ref[...]).astype(BF16)

    u = jnp.dot(xg_ref[...], w_ref[0], preferred_element_type=F32)
    ms = jnp.sum(u * u, axis=-1, keepdims=True) * (1.0 / MLA_QK_DIM)
    qn = u * lax.rsqrt(ms + NORM_EPS) * qg_ref[...]
    o_ref[0, :MLA_NOPE_DIM, :] = qn[:, :MLA_NOPE_DIM].T.astype(o_ref.dtype)
    o_ref[0, MLA_NOPE_DIM:, :] = _rope_apply(qn[:, MLA_NOPE_DIM:], c_ref, s1_ref, s2_ref).T.astype(o_ref.dtype)


def _mla_q_prep(rest, cq_g, w_uq_heads, qg_pad, rope_tabs, tm):
    s = rest.shape[0]
    tab = pl.BlockSpec((tm, LANE), lambda i, h: (i, 0))
    return pl.pallas_call(
        _mla_q_kernel,
        grid=(s // tm, MLA_HEADS),
        in_specs=[pl.BlockSpec((tm, MLA_Q_RANK), lambda i, h: (i, R_CQ // MLA_Q_RANK)),
                  pl.BlockSpec((1, MLA_Q_RANK), lambda i, h: (0, 0)),
                  pl.BlockSpec((1, MLA_Q_RANK, MLA_QK_PAD), lambda i, h: (h, 0, 0)),
                  pl.BlockSpec((1, MLA_QK_PAD), lambda i, h: (0, 0)),
                  tab, tab, tab],
        out_specs=pl.BlockSpec((1, MLA_QK_PAD, tm), lambda i, h: (h, 0, i)),
        out_shape=jax.ShapeDtypeStruct((MLA_HEADS, MLA_QK_PAD, s), BF16),
        scratch_shapes=[pltpu.VMEM((tm, MLA_Q_RANK), BF16)],
        compiler_params=_cp(("arbitrary", "arbitrary")),
        name="mla_q_prep",
    )(rest, cq_g, w_uq_heads, qg_pad, *rope_tabs)


def _mla_kv_kernel(ckv_ref, kr_ref, g_ref, w_ref, kgn_ref, kgr_ref, c_ref, s1_ref, s2_ref,
                   k_ref, vt_ref, xg_ref):
    h = pl.program_id(1)

    @pl.when(h == 0)
    def _():
        c = ckv_ref[...].astype(F32)
        r = lax.rsqrt(jnp.mean(c * c, axis=-1, keepdims=True) + NORM_EPS)
        xg_ref[...] = (c * r * g_ref[...]).astype(BF16)

    kv = jnp.dot(xg_ref[...], w_ref[...].astype(BF16), preferred_element_type=F32)
    kn = kv[:, :MLA_NOPE_DIM]
    lane = lax.broadcasted_iota(jnp.int32, kr_ref.shape, 1)
    kr = jnp.where(lane < MLA_ROPE_DIM, kr_ref[...].astype(F32), 0.0)
    ms = (jnp.sum(kn * kn, axis=-1, keepdims=True) + jnp.sum(kr * kr, axis=-1, keepdims=True)) * (1.0 / MLA_QK_DIM)
    rs = lax.rsqrt(ms + NORM_EPS)
    k_ref[0, :, :MLA_NOPE_DIM] = (kn * rs * kgn_ref[...]).astype(k_ref.dtype)
    k_ref[0, :, MLA_NOPE_DIM:] = _rope_apply(kr * rs * kgr_ref[...], c_ref, s1_ref, s2_ref).astype(k_ref.dtype)
    vt_ref[0] = kv[:, MLA_NOPE_DIM:].T.astype(vt_ref.dtype)


def _mla_kv_prep(rest, ckv_g, w_ukv, kg_nope, kg_rope_pad, rope_tabs, tm):
    s = rest.shape[0]
    tab = pl.BlockSpec((tm, LANE), lambda i, h: (i, 0))
    hw = MLA_NOPE_DIM + MLA_V_DIM
    return pl.pallas_call(
        _mla_kv_kernel,
        grid=(s // tm, MLA_HEADS),
        in_specs=[pl.BlockSpec((tm, MLA_KV_RANK), lambda i, h: (i, R_CKV // MLA_KV_RANK)),
                  pl.BlockSpec((tm, LANE), lambda i, h: (i, R_KROPE // LANE)),
                  pl.BlockSpec((1, MLA_KV_RANK), lambda i, h: (0, 0)),
                  pl.BlockSpec((MLA_KV_RANK, hw), lambda i, h: (0, h)),
                  pl.BlockSpec((1, LANE), lambda i, h: (0, 0)),
                  pl.BlockSpec((1, LANE), lambda i, h: (0, 0)),
                  tab, tab, tab],
        out_specs=[pl.BlockSpec((1, tm, MLA_QK_PAD), lambda i, h: (h, i, 0)),
                   pl.BlockSpec((1, MLA_V_DIM, tm), lambda i, h: (h, 0, i))],
        out_shape=[jax.ShapeDtypeStruct((MLA_HEADS, s, MLA_QK_PAD), BF16),
                   jax.ShapeDtypeStruct((MLA_HEADS, MLA_V_DIM, s), BF16)],
        scratch_shapes=[pltpu.VMEM((tm, MLA_KV_RANK), BF16)],
        compiler_params=_cp(("arbitrary", "arbitrary")),
        name="mla_kv_prep",
    )(rest, rest, ckv_g, w_ukv, kg_nope, kg_rope_pad, *rope_tabs)


def _mla_attn_kernel(qt_ref, k_ref, vt_ref, o_ref, sa_ref, sb_ref, m_ref, l_ref, acc_ref):
    t = qt_ref.shape[2]
    i = pl.program_id(1)
    qt = qt_ref[0]
    _init_stats(m_ref, l_ref, acc_ref)

    def scores(kb, dst):
        off = _block_offset(kb, t)
        dst[...] = jnp.dot(k_ref[0, pl.ds(off, t), :], qt, preferred_element_type=F32)

    def consume(kb, src, diag):
        off = _block_offset(kb, t)
        s = src[...]
        if diag:
            krow = lax.broadcasted_iota(jnp.int32, s.shape, 0)
            qcol = lax.broadcasted_iota(jnp.int32, s.shape, 1)
            s = jnp.where(krow <= qcol, s, NEG_INF)
        _softmax_pv(0, s, vt_ref[0, :, pl.ds(off, t)], m_ref, l_ref, acc_ref)

    scores(0, sa_ref)

    def far_pair(j, c):
        kb = 2 * j
        scores(kb + 1, sb_ref)
        consume(kb, sa_ref, False)
        scores(kb + 2, sa_ref)
        consume(kb + 1, sb_ref, False)
        return c

    lax.fori_loop(0, i // 2, far_pair, 0)

    @pl.when(i % 2 == 1)
    def _():
        scores(i, sb_ref)
        consume(i - 1, sa_ref, False)

    _by_parity(i, lambda cur, other: consume(i, cur, True), sa_ref, sb_ref)
    o_ref[...] = (acc_ref[0] / l_ref[0]).T.astype(o_ref.dtype)


def _mla_attention(q_t, k, v_t):
    s = k.shape[1]
    t = ATT_T
    return pl.pallas_call(
        _mla_attn_kernel,
        grid=(MLA_HEADS, s // t),
        in_specs=[pl.BlockSpec((1, MLA_QK_PAD, t), lambda h, i: (h, 0, i)),
                  pl.BlockSpec((1, s, MLA_QK_PAD), lambda h, i: (h, 0, 0)),
                  pl.BlockSpec((1, MLA_V_DIM, s), lambda h, i: (h, 0, 0))],
        out_specs=pl.BlockSpec((t, MLA_V_DIM), lambda h, i: (i, h)),
        out_shape=jax.ShapeDtypeStruct((s, MLA_WIDTH), BF16),
        scratch_shapes=[pltpu.VMEM((t, t), F32), pltpu.VMEM((t, t), F32),
                        pltpu.VMEM((1, 1, t), F32), pltpu.VMEM((1, 1, t), F32),
                        pltpu.VMEM((1, MLA_V_DIM, t), F32)],
        compiler_params=_cp(("arbitrary", "arbitrary")),
        name="mla_attention",
    )(q_t, k, v_t)


def _qk_nt(q, k_blk):
    return lax.dot_general(q, k_blk, (((1,), (1,)), ((), ())), preferred_element_type=F32)


def _mem_attn_kernel(q_ref, k_ref, v_ref, qg_ref, o_ref):
    shift = R_MQ % MEM_WINDOW
    qall = q_ref[...].astype(F32)[:, shift:shift + MEM_WIDTH]
    for h in range(MEM_HEADS):
        lo = h * MEM_HEAD_DIM
        qh = qall[:, lo:lo + MEM_HEAD_DIM]
        ms = jnp.mean(qh * qh, axis=-1, keepdims=True)
        qn = (qh * lax.rsqrt(ms + NORM_EPS) * qg_ref[...]).astype(BF16)
        s = _qk_nt(qn, k_ref[:, lo:lo + MEM_HEAD_DIM])
        p = jnp.exp(s - jnp.max(s, axis=-1, keepdims=True))
        l = jnp.sum(p, axis=-1, keepdims=True)
        o = jnp.dot(p.astype(BF16), v_ref[:, lo:lo + MEM_HEAD_DIM], preferred_element_type=F32)
        o_ref[:, lo:lo + MEM_HEAD_DIM] = (o / l).astype(o_ref.dtype)


def _mem_attention(rest, k_mem, v_mem, qg_scaled, tm):
    s = rest.shape[0]
    n_mem = k_mem.shape[0]
    assert R_MQ % MEM_WINDOW + MEM_WIDTH <= MEM_WINDOW
    return pl.pallas_call(
        _mem_attn_kernel,
        grid=(s // tm,),
        in_specs=[pl.BlockSpec((tm, MEM_WINDOW), lambda i: (i, R_MQ // MEM_WINDOW)),
                  pl.BlockSpec((n_mem, MEM_WIDTH), lambda i: (0, 0)),
                  pl.BlockSpec((n_mem, MEM_WIDTH), lambda i: (0, 0)),
                  pl.BlockSpec((1, MEM_HEAD_DIM), lambda i: (0, 0))],
        out_specs=pl.BlockSpec((tm, MEM_WIDTH), lambda i: (i, 0)),
        out_shape=jax.ShapeDtypeStruct((s, MEM_WIDTH), BF16),
        compiler_params=_cp(("parallel",)),
        name="mem_attention",
    )(rest, k_mem, v_mem, qg_scaled)


def _mix_kernel(od_ref, om_ref, oc_ref, wd_ref, wm_ref, wc_ref, g0_ref, g1_ref, g2_ref, o_ref):
    yd = jnp.dot(od_ref[...], wd_ref[...], preferred_element_type=F32)
    ym = jnp.dot(om_ref[...], wm_ref[...], preferred_element_type=F32)
    yc = jnp.dot(oc_ref[...], wc_ref[...], preferred_element_type=F32)
    mixed = (g0_ref[...].astype(F32) * yd + g1_ref[...].astype(F32) * ym) + g2_ref[...].astype(F32) * yc
    o_ref[...] = mixed.astype(o_ref.dtype)


def _mix(o_diff, o_mla, o_mem, w_d, w_m, w_c, gates, tm, tn):
    s = o_diff.shape[0]
    d = w_d.shape[1]
    nj = d // tn
    return pl.pallas_call(
        _mix_kernel,
        grid=(s // tm, nj),
        in_specs=[pl.BlockSpec((tm, o_diff.shape[1]), lambda i, j: (i, 0)),
                  pl.BlockSpec((tm, o_mla.shape[1]), lambda i, j: (i, 0)),
                  pl.BlockSpec((tm, o_mem.shape[1]), lambda i, j: (i, 0)),
                  pl.BlockSpec((w_d.shape[0], tn), lambda i, j: (0, j)),
                  pl.BlockSpec((w_m.shape[0], tn), lambda i, j: (0, j)),
                  pl.BlockSpec((w_c.shape[0], tn), lambda i, j: (0, j)),
                  pl.BlockSpec((tm, tn), lambda i, j: (i, j)),
                  pl.BlockSpec((tm, tn), lambda i, j: (i, nj + j)),
                  pl.BlockSpec((tm, tn), lambda i, j: (i, 2 * nj + j))],
        out_specs=pl.BlockSpec((tm, tn), lambda i, j: (i, j)),
        out_shape=jax.ShapeDtypeStruct((s, d), BF16),
        compiler_params=_cp(("arbitrary", "arbitrary")),
        name="gated_mix",
    )(o_diff, o_mla, o_mem, w_d, w_m, w_c, gates, gates, gates)


def _router_kernel(x_ref, g_ref, w_ref, b_ref, h_ref, r_ref):
    x = x_ref[...]
    ms = jnp.mean(x * x, axis=-1, keepdims=True)
    h = x * lax.rsqrt(ms + NORM_EPS) * g_ref[...]
    h_ref[...] = _pack_bf16_pairs(h)
    logits = jnp.dot(h, w_ref[...], preferred_element_type=F32, precision=lax.Precision.HIGHEST) + b_ref[...]
    lane = lax.broadcasted_iota(jnp.int32, logits.shape, 1)
    lane_f = lane.astype(F32)
    big = float(4 * ROUTE_W)
    lg = jnp.where(lane < N_GROUPS, logits, -jnp.inf)
    gmax = jnp.max(lg, axis=-1, keepdims=True)
    gidx = jnp.min(jnp.where(lg == gmax, lane_f, big), axis=-1, keepdims=True)
    pg_top = 1.0 / jnp.sum(jnp.exp(lg - gmax), axis=-1, keepdims=True)
    e_lane = lane - N_GROUPS
    lane_group = jnp.right_shift(e_lane, 3).astype(F32)
    in_group = (e_lane >= 0) & (e_lane < N_EXPERTS) & (lane_group == gidx)
    le = jnp.where(in_group, logits, -jnp.inf)
    e1 = jnp.max(le, axis=-1, keepdims=True)
    i1 = jnp.min(jnp.where(le == e1, lane_f, big), axis=-1, keepdims=True)
    le2 = jnp.where(lane_f == i1, -jnp.inf, le)
    e2 = jnp.max(le2, axis=-1, keepdims=True)
    i2 = jnp.min(jnp.where(le2 == e2, lane_f, big), axis=-1, keepdims=True)
    w2 = jnp.exp(e2 - e1)
    inv = 1.0 / (1.0 + w2)
    gate1 = pg_top * inv
    gate2 = pg_top * (w2 * inv)
    out = jnp.where(lane == 0, i1 - N_GROUPS,
                    jnp.where(lane == 1, i2 - N_GROUPS,
                              jnp.where(lane == 2, gate1, jnp.where(lane == 3, gate2, 0.0))))
    r_ref[...] = out


def _router(x1, g, w_r, b_r, tm):
    s, d = x1.shape
    return pl.pallas_call(
        _router_kernel,
        grid=(s // tm,),
        in_specs=[pl.BlockSpec((tm, d), lambda i: (i, 0)),
                  pl.BlockSpec((1, d), lambda i: (0, 0)),
                  pl.BlockSpec((d, ROUTE_W), lambda i: (0, 0)),
                  pl.BlockSpec((1, ROUTE_W), lambda i: (0, 0))],
        out_specs=[pl.BlockSpec((tm, d // 2), lambda i: (i, 0)),
                   pl.BlockSpec((tm, ROUTE_W), lambda i: (i, 0))],
        out_shape=[jax.ShapeDtypeStruct((s, d // 2), jnp.uint32),
                   jax.ShapeDtypeStruct((s, ROUTE_W), F32)],
        compiler_params=_cp(("parallel",)),
        name="ffn_norm_router",
    )(x1, g, w_r, b_r)


def _pack_bf16_pairs(v):
    n = v.shape[1] // 2
    bits = lax.bitcast_convert_type(v.astype(BF16).astype(F32), jnp.uint32)
    return jnp.right_shift(bits[:, :n], jnp.uint32(16)) | bits[:, n:]


def _unpack_bf16_pairs(words):
    lo = lax.bitcast_convert_type(jnp.left_shift(words, jnp.uint32(16)), F32)
    hi = lax.bitcast_convert_type(words & jnp.uint32(0xFFFF0000), F32)
    return lo, hi


def _moe_kernel(be_ref, nr_ref, nu_ref, tok_ref, tokn_ref, bw_ref, h_ref, wg_ref, wu_ref, wd_ref, o_ref,
                xg_ref, xb_ref, a_ref, wgb_ref, wub_ref, wdb_ref, sem):
    b = pl.program_id(0)
    c = pl.program_id(1)
    nb = pl.num_programs(0)
    nr = nr_ref[b]
    half = xg_ref.shape[2]
    fh = wgb_ref.shape[1]

    def row_copy(src_row, slot, r):
        return pltpu.make_async_copy(h_ref.at[pl.ds(src_row, 1), :], xg_ref.at[slot, pl.ds(r, 1), :], sem.at[slot])

    def start_gather(tok, n_rows, slot):
        def issue(r, carry):
            row_copy(tok[0, 0, r], slot, r).start()
            return carry
        lax.fori_loop(0, n_rows, issue, 0)

    def for_row_count(fn):
        for units in range(1, MOE_TB // MOE_RU + 1):
            @pl.when(nr == units)
            def _():
                fn(units * MOE_RU)

    @pl.when(c == 0)
    def _():
        slot = b % 2

        @pl.when(b == 0)
        def _():
            start_gather(tok_ref, nr * MOE_RU, 0)

        def drain(r, carry):
            row_copy(0, slot, 0).wait()
            return carry
        lax.fori_loop(0, nr * MOE_RU, drain, 0)

        @pl.when(b + 1 < nb)
        def _():
            start_gather(tokn_ref, nr_ref[jnp.minimum(b + 1, nb - 1)] * MOE_RU, 1 - slot)

        def unpack(u, carry):
            r0 = pl.multiple_of(u * MOE_RU, MOE_RU)
            lo, hi = _unpack_bf16_pairs(xg_ref[slot, pl.ds(r0, MOE_RU), :])
            xb_ref[pl.ds(r0, MOE_RU), :half] = lo.astype(BF16)
            xb_ref[pl.ds(r0, MOE_RU), half:] = hi.astype(BF16)
            return carry
        lax.fori_loop(0, nr, unpack, 0)

    @pl.when((c < 2) & (nr > 0))
    def _():
        wgb_ref[...] = wg_ref[0].astype(BF16)
        wub_ref[...] = wu_ref[0].astype(BF16)

        def gate_up(m):
            x = xb_ref[:m, :]
            g = jnp.dot(x, wgb_ref[...], preferred_element_type=F32)
            u = jnp.dot(x, wub_ref[...], preferred_element_type=F32)
            a_ref[c, :m, :] = ((g * jax.nn.sigmoid(g)) * u).astype(BF16)

        for_row_count(gate_up)

    @pl.when((c >= 2) & (nr > 0))
    def _():
        wdb_ref[...] = wd_ref[0].astype(BF16)

        def down(m):
            yv = (jnp.dot(a_ref[0, :m, :], wdb_ref[:fh, :], preferred_element_type=F32)
                  + jnp.dot(a_ref[1, :m, :], wdb_ref[fh:, :], preferred_element_type=F32))
            o_ref[:m, :] = _pack_bf16_pairs(yv * bw_ref[:m, :])
            if m < MOE_TB:
                o_ref[m:, :] = jnp.zeros((MOE_TB - m, o_ref.shape[1]), jnp.uint32)

        for_row_count(down)

    @pl.when((c >= 2) & (nr == 0))
    def _():
        o_ref[...] = jnp.zeros(o_ref.shape, jnp.uint32)


def _moe_experts(block_e, nsub, n_used, buf_tok, buf_w, h2p, w_gate, w_up, w_down):
    n_blocks = block_e.shape[0]
    half = h2p.shape[1]
    d = 2 * half
    ff = w_gate.shape[2]
    fh = ff // 2
    dh = d // 2
    tok = buf_tok.reshape(n_blocks, 1, MOE_TB)

    def gate_up_idx(b, c, be, nr, nu):
        live = b < nu[0]
        return be[jnp.minimum(b, nu[0] - 1)], 0, jnp.where(live, jnp.minimum(c, 1), 1)

    def down_idx(b, c, be, nr, nu):
        bb = jnp.minimum(b, nu[0] - 1)
        cc = jnp.where(b < nu[0], c, MOE_PH - 1)
        e = jnp.where(cc >= 2, be[bb], be[jnp.maximum(bb - 1, 0)])
        return e, 0, jnp.where(cc == 2, 0, 1)

    def out_idx(b, c, be, nr, nu):
        ob = jnp.where(c >= 2, b, jnp.maximum(b - 1, 0))
        oc = jnp.where(c >= 2, c - 2, jnp.where(b > 0, 1, 0))
        return ob, oc

    grid_spec = pltpu.PrefetchScalarGridSpec(
        num_scalar_prefetch=3,
        grid=(n_blocks, MOE_PH),
        in_specs=[pl.BlockSpec((1, 1, MOE_TB), lambda b, c, be, nr, nu: (b, 0, 0), memory_space=pltpu.SMEM),
                  pl.BlockSpec((1, 1, MOE_TB), lambda b, c, be, nr, nu: (jnp.minimum(b + 1, n_blocks - 1), 0, 0),
                               memory_space=pltpu.SMEM),
                  pl.BlockSpec((MOE_TB, 1), lambda b, c, be, nr, nu: (b, 0)),
                  pl.BlockSpec(memory_space=pl.ANY),
                  pl.BlockSpec((1, d, fh), gate_up_idx),
                  pl.BlockSpec((1, d, fh), gate_up_idx),
                  pl.BlockSpec((1, ff, dh), down_idx)],
        out_specs=pl.BlockSpec((MOE_TB, dh // 2), out_idx),
        scratch_shapes=[pltpu.VMEM((2, MOE_TB, half), jnp.uint32), pltpu.VMEM((MOE_TB, d), BF16),
                        pltpu.VMEM((2, MOE_TB, fh), BF16),
                        pltpu.VMEM((d, fh), BF16), pltpu.VMEM((d, fh), BF16), pltpu.VMEM((ff, dh), BF16),
                        pltpu.SemaphoreType.DMA((2,))],
    )
    return pl.pallas_call(
        _moe_kernel,
        grid_spec=grid_spec,
        out_shape=jax.ShapeDtypeStruct((n_blocks * MOE_TB, half), jnp.uint32),
        compiler_params=_cp(("arbitrary", "arbitrary")),
        name="moe_experts",
    )(block_e, nsub, n_used, tok, tok, buf_w.reshape(-1, 1), h2p, w_gate, w_up, w_down)


def _combine_kernel(slot_ref, x_ref, yb_ref, o_ref, g_ref, sem):
    tm = x_ref.shape[0]
    n = g_ref.shape[2]

    def issue(r, c):
        for k in range(TOP_K):
            sl = slot_ref[0, 0, r * TOP_K + k]
            pltpu.make_async_copy(yb_ref.at[pl.ds(sl, 1), :], g_ref.at[k, pl.ds(r, 1), :], sem).start()
        return c

    lax.fori_loop(0, tm, issue, 0)

    def drain(r, c):
        pltpu.make_async_copy(yb_ref.at[pl.ds(0, 1), :], g_ref.at[0, pl.ds(0, 1), :], sem).wait()
        return c

    lax.fori_loop(0, tm * TOP_K, drain, 0)

    q = n // 2
    for hf in range(2):
        lo0, hi0 = _unpack_bf16_pairs(g_ref[0, :, hf * q:(hf + 1) * q])
        lo1, hi1 = _unpack_bf16_pairs(g_ref[1, :, hf * q:(hf + 1) * q])
        c0 = hf * n
        o_ref[:, c0:c0 + q] = x_ref[:, c0:c0 + q] + (lo0 + lo1)
        o_ref[:, c0 + q:c0 + n] = x_ref[:, c0 + q:c0 + n] + (hi0 + hi1)


def _combine(slots, x1, yb, tm):
    s, d = x1.shape
    return pl.pallas_call(
        _combine_kernel,
        grid=(s // tm,),
        in_specs=[pl.BlockSpec((1, 1, tm * TOP_K), lambda i: (i, 0, 0), memory_space=pltpu.SMEM),
                  pl.BlockSpec((tm, d), lambda i: (i, 0)),
                  pl.BlockSpec(memory_space=pl.ANY)],
        out_specs=pl.BlockSpec((tm, d), lambda i: (i, 0)),
        out_shape=jax.ShapeDtypeStruct((s, d), F32),
        scratch_shapes=[pltpu.VMEM((TOP_K, tm, d // 2), jnp.uint32), pltpu.SemaphoreType.DMA(())],
        compiler_params=_cp(("arbitrary",)),
        name="moe_combine",
    )(slots.reshape(s // tm, 1, tm * TOP_K), x1, yb)


def _dispatch_plan(route, s):
    a = s * TOP_K
    flat_e = route[:, :TOP_K].astype(jnp.int32).reshape(a)
    flat_w = route[:, TOP_K:2 * TOP_K].reshape(a)
    onehot = (flat_e[:, None] == jnp.arange(N_EXPERTS, dtype=jnp.int32)[None, :]).astype(jnp.int32)
    csum = jnp.cumsum(onehot, axis=0)
    counts = csum[-1]
    rank = jnp.sum((csum - onehot) * onehot, axis=1)
    padded = (counts + MOE_TB - 1) // MOE_TB * MOE_TB
    pad_end = jnp.cumsum(padded)
    pad_start = pad_end - padded
    dest = pad_start[flat_e] + rank
    n_blocks = a // MOE_TB + N_EXPERTS
    p_rows = n_blocks * MOE_TB
    buf_tok = jnp.zeros((p_rows,), jnp.int32).at[dest].set(jnp.arange(a, dtype=jnp.int32) // TOP_K)
    buf_w = jnp.zeros((p_rows,), F32).at[dest].set(flat_w)
    starts = jnp.arange(n_blocks, dtype=jnp.int32) * MOE_TB
    block_e = jnp.minimum(jnp.searchsorted(pad_end, starts, side='right'), N_EXPERTS - 1).astype(jnp.int32)
    valid = jnp.clip(counts[block_e] - (starts - pad_start[block_e]), 0, MOE_TB)
    valid = jnp.where(starts < pad_end[-1], valid, 0)
    nsub = ((valid + MOE_RU - 1) // MOE_RU).astype(jnp.int32)
    n_used = (pad_end[-1:] // MOE_TB).astype(jnp.int32)
    return block_e, nsub, n_used, buf_tok, buf_w, dest.astype(jnp.int32)


def _rope_tables(positions):
    half = MLA_ROPE_DIM // 2
    inv_freq = ROPE_THETA ** (-jnp.arange(half, dtype=F32) / half)
    ang = positions.astype(F32)[:, None] * inv_freq[None, :]
    cos, sin = jnp.cos(ang), jnp.sin(ang)
    z = jnp.zeros_like(cos)
    c = jnp.concatenate([cos, cos, z, z], axis=-1)
    s1 = jnp.concatenate([-sin, z, z, z], axis=-1)
    s2 = jnp.concatenate([z, sin, z, z], axis=-1)
    return c, s1, s2


def kernel(x, mem, positions, rel_bias, mix_norm_g, w_in, diff_q_norm_g, diff_k_norm_g, diff_lambda_q1, diff_lambda_k1, diff_lambda_q2, diff_lambda_k2, diff_subln_g, mla_cq_norm_g, mla_ckv_norm_g, mla_w_uq, mla_w_ukv, mla_q_norm_g, mla_k_norm_g, mem_norm_g, mem_w_kv, mem_q_norm_g, mem_k_norm_g, w_o_diff, w_o_mla, w_o_mem, w_out, ffn_norm_g, w_route_group, b_route_group, w_route_expert, b_route_expert, w_exp_gate, w_exp_up, w_exp_down):
    b, s, d = x.shape
    assert b == 1 and s % ATT_T == 0
    depth = mix_norm_g.shape[0]
    xs = x.reshape(s, d)
    pos = positions.reshape(s)
    rope_tabs = _rope_tables(pos)
    row = lambda v: v.reshape(1, -1).astype(F32)

    for l in range(depth):
        lam_init = 0.8 - 0.6 * math.exp(-0.3 * l)
        h = _rmsnorm_rows(xs, mix_norm_g[l], 256)
        tn = 512
        q_gain = jnp.tile(diff_q_norm_g[l] * (DIFF_HEAD_DIM ** -0.5 * LOG2E), DIFF_MAPS)
        w_in_t = jnp.transpose(w_in[l])
        dq_t = _matmul_nt(h, w_in_t, row0=OFF_DQ, n_cols=DIFF_QK_WIDTH, tm=512, tn=tn, out_dtype=BF16,
                          mode="groupnorm", extra=q_gain, group=DIFF_HEAD_DIM, transpose_out=True,
                          name="diff_q_proj")
        dk = _matmul_nt(h, w_in_t, row0=OFF_DK, n_cols=DIFF_QK_WIDTH, tm=512, tn=tn, out_dtype=BF16,
                        mode="groupnorm", extra=jnp.tile(diff_k_norm_g[l], DIFF_MAPS), group=DIFF_HEAD_DIM,
                        name="diff_k_proj")
        dv_t = _matmul_nt(h, w_in_t, row0=OFF_DV, n_cols=DIFF_WIDTH, tm=512, tn=tn, out_dtype=BF16,
                          transpose_out=True, name="diff_v_proj")
        rest = _matmul_nt(h, w_in_t, row0=OFF_CQ, n_cols=REST_WIDTH, tm=512, tn=tn, out_dtype=BF16,
                          name="rest_proj")
        gates = _matmul_nt(h, w_in_t, row0=OFF_GATES, n_cols=3 * d, tm=512, tn=tn, out_dtype=BF16,
                           mode="sigmoid", name="gate_proj")

        bias_tiles = _diff_bias_tiles(rel_bias, ATT_T)
        lam_vecs = [row(diff_lambda_q1[l]), row(diff_lambda_k1[l]), row(diff_lambda_q2[l]), row(diff_lambda_k2[l])]
        o_diff = _diff_attention(dq_t, dk, dv_t, bias_tiles, lam_vecs, diff_subln_g[l], lam_init)

        w_uq_heads = jnp.pad(
            mla_w_uq[l].reshape(MLA_Q_RANK, MLA_HEADS, MLA_QK_DIM).transpose(1, 0, 2),
            ((0, 0), (0, 0), (0, MLA_QK_PAD - MLA_QK_DIM))).astype(BF16)
        qg_pad = jnp.pad(mla_q_norm_g[l] * (MLA_QK_DIM ** -0.5 * LOG2E),
                         (0, MLA_QK_PAD - MLA_QK_DIM)).reshape(1, -1).astype(F32)
        q_mla_t = _mla_q_prep(rest, row(mla_cq_norm_g[l]), w_uq_heads, qg_pad, rope_tabs, 512)
        kg = mla_k_norm_g[l]
        kg_nope = row(kg[:MLA_NOPE_DIM])
        kg_rope = jnp.pad(kg[MLA_NOPE_DIM:], (0, LANE - MLA_ROPE_DIM)).reshape(1, -1).astype(F32)
        k_mla, v_mla_t = _mla_kv_prep(rest, row(mla_ckv_norm_g[l]), mla_w_ukv[l], kg_nope, kg_rope, rope_tabs, 512)
        o_mla = _mla_attention(q_mla_t, k_mla, v_mla_t)

        n_mem = mem.shape[1]
        mem_h = _rmsnorm_rows(mem.reshape(n_mem, d), mem_norm_g[l], n_mem)
        k_mem = _matmul(mem_h, mem_w_kv[l], col0=0, n_cols=MEM_WIDTH, tm=n_mem, tn=512, out_dtype=BF16,
                        mode="groupnorm", extra=jnp.tile(mem_k_norm_g[l], MEM_HEADS), group=MEM_HEAD_DIM,
                        name="mem_k_proj")
        v_mem = _matmul(mem_h, mem_w_kv[l], col0=MEM_WIDTH, n_cols=MEM_WIDTH, tm=n_mem, tn=512, out_dtype=BF16,
                        name="mem_v_proj")
        o_mem = _mem_attention(rest, k_mem, v_mem, row(mem_q_norm_g[l] * MEM_HEAD_DIM ** -0.5), 512)

        mixed = _mix(o_diff, o_mla, o_mem, w_o_diff[l].astype(BF16), w_o_mla[l].astype(BF16),
                     w_o_mem[l].astype(BF16), gates, 512, 512)
        x1 = _matmul(mixed, w_out[l], n_cols=d, tm=512, tn=512, out_dtype=F32, mode="residual", extra=xs,
                     name="out_proj")

        w_r = jnp.pad(jnp.concatenate([w_route_group[l], w_route_expert[l]], axis=1),
                      ((0, 0), (0, ROUTE_W - N_GROUPS - N_EXPERTS))).astype(F32)
        b_r = jnp.pad(jnp.concatenate([b_route_group[l], b_route_expert[l]]),
                      (0, ROUTE_W - N_GROUPS - N_EXPERTS)).reshape(1, -1).astype(F32)
        h2, route = _router(x1, row(ffn_norm_g[l]), w_r, b_r, 256)
        block_e, nsub, n_used, buf_tok, buf_w, slots = _dispatch_plan(route, s)
        yb = _moe_experts(block_e, nsub, n_used, buf_tok, buf_w, h2, w_exp_gate[l], w_exp_up[l], w_exp_down[l])
        xs = _combine(slots, x1, yb, 256)
    return xs.reshape(b, s, d)
```

```python
import functools
import math

import jax
import jax.numpy as jnp
from jax import lax
from jax.experimental import pallas as pl
from jax.experimental.pallas import tpu as pltpu

F32 = jnp.float32
BF16 = jnp.bfloat16

NORM_EPS = 1e-6
NEG_INF = -1e30
LOG2E = math.log2(math.e)

DIFF_HEADS = 6
DIFF_HEAD_DIM = 128
DIFF_V_DIM = 256
DIFF_MAPS = 12
DIFF_QK_WIDTH = 1536
DIFF_WIDTH = 1536
MLA_HEADS = 12
MLA_Q_RANK = 1536
MLA_KV_RANK = 512
MLA_NOPE_DIM = 128
MLA_ROPE_DIM = 64
MLA_QK_DIM = 192
MLA_QK_PAD = 256
MLA_V_DIM = 128
MLA_WIDTH = 1536
ROPE_THETA = 10000.0
MEM_HEADS = 4
MEM_HEAD_DIM = 256
MEM_WIDTH = 1024
REL_BUCKETS = 32
REL_MAX_DIST = 128
N_GROUPS = 8
EXPERTS_PER_GROUP = 8
N_EXPERTS = 64
TOP_K = 2
EXPERT_FF = 512

OFF_DQ = 0
OFF_DK = 1536
OFF_DV = 3072
OFF_CQ = 4608
OFF_CKV = 6144
OFF_KROPE = 6656
OFF_MQ = 6720
OFF_GATES = 7744
REST_WIDTH = 4096
R_CQ = OFF_CQ - OFF_CQ
R_CKV = OFF_CKV - OFF_CQ
R_KROPE = OFF_KROPE - OFF_CQ
R_MQ = OFF_MQ - OFF_CQ

LANE = 128
VMEM_LIMIT = 52 * 1024 * 1024

ATT_T = 512
MOE_TB = 512
MOE_RU = 128
MOE_PH = 4
ROUTE_W = 128
MEM_WINDOW = 2048


def _cp(sem, vmem=VMEM_LIMIT):
    return pltpu.CompilerParams(dimension_semantics=sem, vmem_limit_bytes=vmem)


def _rmsnorm_kernel(x_ref, g_ref, o_ref):
    x = x_ref[...].astype(F32)
    ms = jnp.mean(x * x, axis=-1, keepdims=True)
    o_ref[...] = (x * lax.rsqrt(ms + NORM_EPS) * g_ref[...]).astype(o_ref.dtype)


def _rmsnorm_rows(x, g, tm, out_dtype=BF16):
    m, d = x.shape
    return pl.pallas_call(
        _rmsnorm_kernel,
        grid=(m // tm,),
        in_specs=[pl.BlockSpec((tm, d), lambda i: (i, 0)),
                  pl.BlockSpec((1, d), lambda i: (0, 0))],
        out_specs=pl.BlockSpec((tm, d), lambda i: (i, 0)),
        out_shape=jax.ShapeDtypeStruct((m, d), out_dtype),
        compiler_params=_cp(("parallel",)),
        name="rmsnorm_rows",
    )(x, g.reshape(1, d).astype(F32))


def _cast_shifted(w_ref, w2_ref, wb_ref, shift):
    k = w_ref.shape[0]
    rows = 256

    def body(c, carry):
        r0 = pl.multiple_of(c * rows, rows)
        main = w_ref[pl.ds(r0, rows), :]
        tail = w2_ref[pl.ds(r0, rows), :]
        wb_ref[pl.ds(r0, rows), :] = jnp.concatenate([main[:, shift:], tail[:, :shift]], axis=1).astype(BF16)
        return carry

    lax.fori_loop(0, k // rows, body, 0)


def _mm_kernel(*refs, mode, cast, group, shift, transpose_out):
    a_ref, w_ref = refs[0], refs[1]
    pos = 2
    w2_ref = None
    if shift:
        w2_ref = refs[pos]
        pos += 1
    extra = None
    if mode in ("groupnorm", "residual"):
        extra = refs[pos]
        pos += 1
    o_ref = refs[pos]
    wb_ref = refs[pos + 1] if cast else None
    i = pl.program_id(1)
    if cast:
        @pl.when(i == 0)
        def _():
            if shift:
                _cast_shifted(w_ref, w2_ref, wb_ref, shift)
            else:
                wb_ref[...] = w_ref[...].astype(BF16)
        w = wb_ref[...]
    else:
        w = w_ref[...]
    acc = jnp.dot(a_ref[...], w, preferred_element_type=F32)
    if transpose_out:
        acc = acc.T
    if mode == "plain":
        o_ref[...] = acc.astype(o_ref.dtype)
    elif mode == "sigmoid":
        o_ref[...] = jax.nn.sigmoid(acc).astype(o_ref.dtype)
    elif mode == "residual":
        o_ref[...] = (extra[...] + acc).astype(o_ref.dtype)
    elif mode == "groupnorm":
        tn = w.shape[1]
        for c in range(tn // group):
            sl = slice(c * group, (c + 1) * group)
            if transpose_out:
                blk = acc[sl, :]
                ms = jnp.mean(blk * blk, axis=0, keepdims=True)
                o_ref[sl, :] = (blk * lax.rsqrt(ms + NORM_EPS) * extra[sl, :]).astype(o_ref.dtype)
            else:
                blk = acc[:, sl]
                ms = jnp.mean(blk * blk, axis=-1, keepdims=True)
                o_ref[:, sl] = (blk * lax.rsqrt(ms + NORM_EPS) * extra[:, sl]).astype(o_ref.dtype)


def _matmul(a, w, *, n_cols, tm, tn, out_dtype, col0=0, mode="plain", extra=None, group=LANE,
            transpose_out=False, name="matmul"):
    m, k = a.shape
    assert m % tm == 0 and n_cols % tn == 0 and w.shape[0] == k
    cast = w.dtype != BF16
    base, shift = divmod(col0, tn)
    assert shift <= LANE and (shift == 0 or cast)
    in_specs = [pl.BlockSpec((tm, k), lambda j, i: (i, 0)),
                pl.BlockSpec((k, tn), lambda j, i: (0, base + j))]
    args = [a, w]
    if shift:
        in_specs.append(pl.BlockSpec((k, LANE), lambda j, i: (0, (base + j + 1) * (tn // LANE))))
        args.append(w)
    if mode == "groupnorm":
        gain = extra.reshape(-1, 1) if transpose_out else extra.reshape(1, -1)
        in_specs.append(pl.BlockSpec((tn, 1), lambda j, i: (j, 0)) if transpose_out
                        else pl.BlockSpec((1, tn), lambda j, i: (0, j)))
        args.append(gain.astype(F32))
    elif mode == "residual":
        assert not transpose_out
        in_specs.append(pl.BlockSpec((tm, tn), lambda j, i: (i, j)))
        args.append(extra)
    if transpose_out:
        out_spec = pl.BlockSpec((tn, tm), lambda j, i: (j, i))
        out_shape = jax.ShapeDtypeStruct((n_cols, m), out_dtype)
    else:
        out_spec = pl.BlockSpec((tm, tn), lambda j, i: (i, j))
        out_shape = jax.ShapeDtypeStruct((m, n_cols), out_dtype)
    scratch = [pltpu.VMEM((k, tn), BF16)] if cast else []
    return pl.pallas_call(
        functools.partial(_mm_kernel, mode=mode, cast=cast, group=group, shift=shift, transpose_out=transpose_out),
        grid=(n_cols // tn, m // tm),
        in_specs=in_specs,
        out_specs=out_spec,
        out_shape=out_shape,
        scratch_shapes=scratch,
        compiler_params=_cp(("arbitrary", "arbitrary")),
        name=name,
    )(*args)


def _mm_nt_kernel(*refs, mode, group, transpose_out):
    a_ref, wt_ref = refs[0], refs[1]
    extra = refs[2] if mode == "groupnorm" else None
    o_ref, wb_ref = refs[-2], refs[-1]
    i = pl.program_id(1)

    @pl.when(i == 0)
    def _():
        wb_ref[...] = wt_ref[...].astype(BF16)

    contract_last = (((1,), (1,)), ((), ()))
    if transpose_out:
        acc = lax.dot_general(wb_ref[...], a_ref[...], contract_last, preferred_element_type=F32)
    else:
        acc = lax.dot_general(a_ref[...], wb_ref[...], contract_last, preferred_element_type=F32)
    if mode == "plain":
        o_ref[...] = acc.astype(o_ref.dtype)
    elif mode == "sigmoid":
        o_ref[...] = jax.nn.sigmoid(acc).astype(o_ref.dtype)
    elif mode == "groupnorm":
        tn = wb_ref.shape[0]
        for c in range(tn // group):
            sl = slice(c * group, (c + 1) * group)
            if transpose_out:
                blk = acc[sl, :]
                ms = jnp.mean(blk * blk, axis=0, keepdims=True)
                o_ref[sl, :] = (blk * lax.rsqrt(ms + NORM_EPS) * extra[sl, :]).astype(o_ref.dtype)
            else:
                blk = acc[:, sl]
                ms = jnp.mean(blk * blk, axis=-1, keepdims=True)
                o_ref[:, sl] = (blk * lax.rsqrt(ms + NORM_EPS) * extra[:, sl]).astype(o_ref.dtype)


def _matmul_nt(a, wt, *, row0, n_cols, tm, tn, out_dtype, mode="plain", extra=None, group=LANE,
               transpose_out=False, name="matmul_nt"):
    m, k = a.shape
    assert m % tm == 0 and n_cols % tn == 0 and wt.shape[1] == k and row0 % 8 == 0
    in_specs = [pl.BlockSpec((tm, k), lambda j, i: (i, 0)),
                pl.BlockSpec((pl.Element(tn), pl.Element(k)), lambda j, i: (pl.multiple_of(row0 + j * tn, 8), 0))]
    args = [a, wt]
    if mode == "groupnorm":
        gain = extra.reshape(-1, 1) if transpose_out else extra.reshape(1, -1)
        in_specs.append(pl.BlockSpec((tn, 1), lambda j, i: (j, 0)) if transpose_out
                        else pl.BlockSpec((1, tn), lambda j, i: (0, j)))
        args.append(gain.astype(F32))
    if transpose_out:
        out_spec = pl.BlockSpec((tn, tm), lambda j, i: (j, i))
        out_shape = jax.ShapeDtypeStruct((n_cols, m), out_dtype)
    else:
        out_spec = pl.BlockSpec((tm, tn), lambda j, i: (i, j))
        out_shape = jax.ShapeDtypeStruct((m, n_cols), out_dtype)
    return pl.pallas_call(
        functools.partial(_mm_nt_kernel, mode=mode, group=group, transpose_out=transpose_out),
        grid=(n_cols // tn, m // tm),
        in_specs=in_specs,
        out_specs=out_spec,
        out_shape=out_shape,
        scratch_shapes=[pltpu.VMEM((tn, k), BF16)],
        compiler_params=_cp(("arbitrary", "arbitrary")),
        name=name,
    )(*args)


def _softmax_pv(idx, s, vt_blk, m_ref, l_ref, acc_ref):
    m_prev = m_ref[idx]
    m_new = jnp.maximum(m_prev, jnp.max(s, axis=0, keepdims=True))
    alpha = jnp.exp2(m_prev - m_new)
    p = jnp.exp2(s - m_new)
    l_ref[idx] = alpha * l_ref[idx] + jnp.sum(p, axis=0, keepdims=True)
    acc_ref[idx] = alpha * acc_ref[idx] + jnp.dot(vt_blk, p.astype(BF16), preferred_element_type=F32)
    m_ref[idx] = m_new


def _block_offset(kb, t):
    return kb * t if isinstance(kb, int) else pl.multiple_of(kb * t, t)


def _by_parity(kb, fn, even_ref, odd_ref):
    @pl.when(kb % 2 == 0)
    def _():
        fn(even_ref, odd_ref)

    @pl.when(kb % 2 == 1)
    def _():
        fn(odd_ref, even_ref)


def _init_stats(m_ref, l_ref, acc_ref):
    m_ref[...] = jnp.full(m_ref.shape, NEG_INF, F32)
    l_ref[...] = jnp.zeros(l_ref.shape, F32)
    acc_ref[...] = jnp.zeros(acc_ref.shape, F32)


def _diff_attn_kernel(lq1_ref, lk1_ref, lq2_ref, lk2_ref, q1_ref, q2_ref, k1_ref, k2_ref, vt_ref,
                      b1_ref, b2_ref, g_ref, o_ref, sa_ref, sb_ref, m_ref, l_ref, acc_ref, *, lam_init):
    t = q1_ref.shape[1]
    i = pl.program_id(1)
    qts = (q1_ref[...], q2_ref[...])
    ks = (k1_ref, k2_ref)
    bs = (b1_ref, b2_ref)
    _init_stats(m_ref, l_ref, acc_ref)

    def scores(kb, dst):
        off = _block_offset(kb, t)
        for mp in range(2):
            dst[mp] = jnp.dot(ks[mp][pl.ds(off, t), :], qts[mp], preferred_element_type=F32)

    def consume(kb, src, bias_idx):
        off = _block_offset(kb, t)
        vt_blk = vt_ref[:, pl.ds(off, t)]
        for mp in range(2):
            s = src[mp]
            if bias_idx is not None:
                s = s + bs[mp][0, bias_idx]
            _softmax_pv(mp, s, vt_blk, m_ref, l_ref, acc_ref)

    def step(kb, bias_idx, last=False):
        def run(cur, nxt):
            if not last:
                scores(kb + 1, nxt)
            consume(kb, cur, bias_idx)
        _by_parity(kb, run, sa_ref, sb_ref)

    n_far = jnp.maximum(i - 1, 0)
    scores(0, sa_ref)

    def far_pair(j, c):
        kb = 2 * j
        scores(kb + 1, sb_ref)
        consume(kb, sa_ref, None)
        scores(kb + 2, sa_ref)
        consume(kb + 1, sb_ref, None)
        return c

    lax.fori_loop(0, n_far // 2, far_pair, 0)

    @pl.when(n_far % 2 == 1)
    def _():
        scores(n_far, sb_ref)
        consume(n_far - 1, sa_ref, None)

    @pl.when(i >= 1)
    def _():
        step(i - 1, 1)

    step(i, 0, last=True)

    lam = (jnp.exp(jnp.sum(lq1_ref[...] * lk1_ref[...], axis=-1, keepdims=True))
           - jnp.exp(jnp.sum(lq2_ref[...] * lk2_ref[...], axis=-1, keepdims=True)) + lam_init)
    o = acc_ref[0] / l_ref[0] - lam * (acc_ref[1] / l_ref[1])
    ms = jnp.mean(o * o, axis=0, keepdims=True)
    o = (o * lax.rsqrt(ms + NORM_EPS) * g_ref[...]) * (1.0 - lam_init)
    o_ref[...] = o.T.astype(o_ref.dtype)


def _diff_attention(q_t, k, v_t, bias_tiles, lam_vecs, subln_g, lam_init):
    s = k.shape[0]
    t = ATT_T
    hd, vd = DIFF_HEAD_DIM, DIFF_V_DIM
    vec = pl.BlockSpec((1, hd), lambda h, i: (0, 0))
    in_specs = [vec, vec, vec, vec,
                pl.BlockSpec((hd, t), lambda h, i: (h, i)),
                pl.BlockSpec((hd, t), lambda h, i: (DIFF_HEADS + h, i)),
                pl.BlockSpec((s, hd), lambda h, i: (0, h)),
                pl.BlockSpec((s, hd), lambda h, i: (0, DIFF_HEADS + h)),
                pl.BlockSpec((vd, s), lambda h, i: (h, 0)),
                pl.BlockSpec((1, 2, t, t), lambda h, i: (h, 0, 0, 0)),
                pl.BlockSpec((1, 2, t, t), lambda h, i: (DIFF_HEADS + h, 0, 0, 0)),
                pl.BlockSpec((vd, 1), lambda h, i: (0, 0))]
    return pl.pallas_call(
        functools.partial(_diff_attn_kernel, lam_init=lam_init),
        grid=(DIFF_HEADS, s // t),
        in_specs=in_specs,
        out_specs=pl.BlockSpec((t, vd), lambda h, i: (i, h)),
        out_shape=jax.ShapeDtypeStruct((s, DIFF_WIDTH), BF16),
        scratch_shapes=[pltpu.VMEM((2, t, t), F32), pltpu.VMEM((2, t, t), F32),
                        pltpu.VMEM((2, 1, t), F32), pltpu.VMEM((2, 1, t), F32),
                        pltpu.VMEM((2, vd, t), F32)],
        compiler_params=_cp(("arbitrary", "arbitrary")),
        name="diff_attention",
    )(*lam_vecs, q_t, q_t, k, k, v_t, bias_tiles, bias_tiles, subln_g.reshape(vd, 1).astype(F32))


def _t5_bucket(dist):
    n = jnp.maximum(dist, 0)
    max_exact = REL_BUCKETS // 2
    nf = jnp.maximum(n, 1).astype(F32)
    large = max_exact + (jnp.log(nf / max_exact) / math.log(REL_MAX_DIST / max_exact)
                         * (REL_BUCKETS - max_exact)).astype(jnp.int32)
    large = jnp.minimum(large, REL_BUCKETS - 1)
    return jnp.where(n < max_exact, n, large)


def _diff_bias_tiles(rel_bias, t):
    assert t >= REL_MAX_DIST
    table = rel_bias.astype(F32)
    table = (table - table[REL_BUCKETS - 1:REL_BUCKETS]) * LOG2E
    kr = jnp.arange(t, dtype=jnp.int32)[:, None]
    qc = jnp.arange(t, dtype=jnp.int32)[None, :]
    d0 = qc - kr
    buckets = jnp.arange(REL_BUCKETS, dtype=jnp.int32)

    def lookup(dist):
        onehot = (_t5_bucket(dist)[:, :, None] == buckets).astype(F32)
        return jnp.einsum('rcb,bm->mrc', onehot, table, precision=lax.Precision.HIGHEST)

    tile0 = jnp.where((d0 >= 0)[None], lookup(d0), NEG_INF)
    return jnp.stack([tile0, lookup(d0 + t)], axis=1)


def _rope_apply(tv, c_ref, s1_ref, s2_ref):
    return (tv * c_ref[...] + pltpu.roll(tv, 96, 1) * s1_ref[...] + pltpu.roll(tv, 32, 1) * s2_ref[...])


def _mla_q_kernel(cq_ref, g_ref, w_ref, qg_ref, c_ref, s1_ref, s2_ref, o_ref, xg_ref):
    h = pl.program_id(1)

    @pl.when(h == 0)
    def _():
        c = cq_ref[...].astype(F32)
        r = lax.rsqrt(jnp.mean(c * c, axis=-1, keepdims=True) + NORM_EPS)
        xg_ref[...] = (c * r * g_ref[...]).astype(BF16)

    u = jnp.dot(xg_ref[...], w_ref[0], preferred_element_type=F32)
    ms = jnp.sum(u * u, axis=-1, keepdims=True) * (1.0 / MLA_QK_DIM)
    qn = u * lax.rsqrt(ms + NORM_EPS) * qg_ref[...]
    o_ref[0, :MLA_NOPE_DIM, :] = qn[:, :MLA_NOPE_DIM].T.astype(o_ref.dtype)
    o_ref[0, MLA_NOPE_DIM:, :] = _rope_apply(qn[:, MLA_NOPE_DIM:], c_ref, s1_ref, s2_ref).T.astype(o_ref.dtype)


def _mla_q_prep(rest, cq_g, w_uq_heads, qg_pad, rope_tabs, tm):
    s = rest.shape[0]
    tab = pl.BlockSpec((tm, LANE), lambda i, h: (i, 0))
    return pl.pallas_call(
        _mla_q_kernel,
        grid=(s // tm, MLA_HEADS),
        in_specs=[pl.BlockSpec((tm, MLA_Q_RANK), lambda i, h: (i, R_CQ // MLA_Q_RANK)),
                  pl.BlockSpec((1, MLA_Q_RANK), lambda i, h: (0, 0)),
                  pl.BlockSpec((1, MLA_Q_RANK, MLA_QK_PAD), lambda i, h: (h, 0, 0)),
                  pl.BlockSpec((1, MLA_QK_PAD), lambda i, h: (0, 0)),
                  tab, tab, tab],
        out_specs=pl.BlockSpec((1, MLA_QK_PAD, tm), lambda i, h: (h, 0, i)),
        out_shape=jax.ShapeDtypeStruct((MLA_HEADS, MLA_QK_PAD, s), BF16),
        scratch_shapes=[pltpu.VMEM((tm, MLA_Q_RANK), BF16)],
        compiler_params=_cp(("arbitrary", "arbitrary")),
        name="mla_q_prep",
    )(rest, cq_g, w_uq_heads, qg_pad, *rope_tabs)


def _mla_kv_kernel(ckv_ref, kr_ref, g_ref, w_ref, kgn_ref, kgr_ref, c_ref, s1_ref, s2_ref,
                   k_ref, vt_ref, xg_ref):
    h = pl.program_id(1)

    @pl.when(h == 0)
    def _():
        c = ckv_ref[...].astype(F32)
        r = lax.rsqrt(jnp.mean(c * c, axis=-1, keepdims=True) + NORM_EPS)
        xg_ref[...] = (c * r * g_ref[...]).astype(BF16)

    kv = jnp.dot(xg_ref[...], w_ref[...].astype(BF16), preferred_element_type=F32)
    kn = kv[:, :MLA_NOPE_DIM]
    lane = lax.broadcasted_iota(jnp.int32, kr_ref.shape, 1)
    kr = jnp.where(lane < MLA_ROPE_DIM, kr_ref[...].astype(F32), 0.0)
    ms = (jnp.sum(kn * kn, axis=-1, keepdims=True) + jnp.sum(kr * kr, axis=-1, keepdims=True)) * (1.0 / MLA_QK_DIM)
    rs = lax.rsqrt(ms + NORM_EPS)
    k_ref[0, :, :MLA_NOPE_DIM] = (kn * rs * kgn_ref[...]).astype(k_ref.dtype)
    k_ref[0, :, MLA_NOPE_DIM:] = _rope_apply(kr * rs * kgr_ref[...], c_ref, s1_ref, s2_ref).astype(k_ref.dtype)
    vt_ref[0] = kv[:, MLA_NOPE_DIM:].T.astype(vt_ref.dtype)


def _mla_kv_prep(rest, ckv_g, w_ukv, kg_nope, kg_rope_pad, rope_tabs, tm):
    s = rest.shape[0]
    tab = pl.BlockSpec((tm, LANE), lambda i, h: (i, 0))
    hw = MLA_NOPE_DIM + MLA_V_DIM
    return pl.pallas_call(
        _mla_kv_kernel,
        grid=(s // tm, MLA_HEADS),
        in_specs=[pl.BlockSpec((tm, MLA_KV_RANK), lambda i, h: (i, R_CKV // MLA_KV_RANK)),
                  pl.BlockSpec((tm, LANE), lambda i, h: (i, R_KROPE // LANE)),
                  pl.BlockSpec((1, MLA_KV_RANK), lambda i, h: (0, 0)),
                  pl.BlockSpec((MLA_KV_RANK, hw), lambda i, h: (0, h)),
                  pl.BlockSpec((1, LANE), lambda i, h: (0, 0)),
                  pl.BlockSpec((1, LANE), lambda i, h: (0, 0)),
                  tab, tab, tab],
        out_specs=[pl.BlockSpec((1, tm, MLA_QK_PAD), lambda i, h: (h, i, 0)),
                   pl.BlockSpec((1, MLA_V_DIM, tm), lambda i, h: (h, 0, i))],
        out_shape=[jax.ShapeDtypeStruct((MLA_HEADS, s, MLA_QK_PAD), BF16),
                   jax.ShapeDtypeStruct((MLA_HEADS, MLA_V_DIM, s), BF16)],
        scratch_shapes=[pltpu.VMEM((tm, MLA_KV_RANK), BF16)],
        compiler_params=_cp(("arbitrary", "arbitrary")),
        name="mla_kv_prep",
    )(rest, rest, ckv_g, w_ukv, kg_nope, kg_rope_pad, *rope_tabs)


def _mla_attn_kernel(qt_ref, k_ref, vt_ref, o_ref, sa_ref, sb_ref, m_ref, l_ref, acc_ref):
    t = qt_ref.shape[2]
    i = pl.program_id(1)
    qt = qt_ref[0]
    _init_stats(m_ref, l_ref, acc_ref)

    def scores(kb, dst):
        off = _block_offset(kb, t)
        dst[...] = jnp.dot(k_ref[0, pl.ds(off, t), :], qt, preferred_element_type=F32)

    def consume(kb, src, diag):
        off = _block_offset(kb, t)
        s = src[...]
        if diag:
            krow = lax.broadcasted_iota(jnp.int32, s.shape, 0)
            qcol = lax.broadcasted_iota(jnp.int32, s.shape, 1)
            s = jnp.where(krow <= qcol, s, NEG_INF)
        _softmax_pv(0, s, vt_ref[0, :, pl.ds(off, t)], m_ref, l_ref, acc_ref)

    scores(0, sa_ref)

    def far_pair(j, c):
        kb = 2 * j
        scores(kb + 1, sb_ref)
        consume(kb, sa_ref, False)
        scores(kb + 2, sa_ref)
        consume(kb + 1, sb_ref, False)
        return c

    lax.fori_loop(0, i // 2, far_pair, 0)

    @pl.when(i % 2 == 1)
    def _():
        scores(i, sb_ref)
        consume(i - 1, sa_ref, False)

    _by_parity(i, lambda cur, other: consume(i, cur, True), sa_ref, sb_ref)
    o_ref[...] = (acc_ref[0] / l_ref[0]).T.astype(o_ref.dtype)


def _mla_attention(q_t, k, v_t):
    s = k.shape[1]
    t = ATT_T
    return pl.pallas_call(
        _mla_attn_kernel,
        grid=(MLA_HEADS, s // t),
        in_specs=[pl.BlockSpec((1, MLA_QK_PAD, t), lambda h, i: (h, 0, i)),
                  pl.BlockSpec((1, s, MLA_QK_PAD), lambda h, i: (h, 0, 0)),
                  pl.BlockSpec((1, MLA_V_DIM, s), lambda h, i: (h, 0, 0))],
        out_specs=pl.BlockSpec((t, MLA_V_DIM), lambda h, i: (i, h)),
        out_shape=jax.ShapeDtypeStruct((s, MLA_WIDTH), BF16),
        scratch_shapes=[pltpu.VMEM((t, t), F32), pltpu.VMEM((t, t), F32),
                        pltpu.VMEM((1, 1, t), F32), pltpu.VMEM((1, 1, t), F32),
                        pltpu.VMEM((1, MLA_V_DIM, t), F32)],
        compiler_params=_cp(("arbitrary", "arbitrary")),
        name="mla_attention",
    )(q_t, k, v_t)


def _qk_nt(q, k_blk):
    return lax.dot_general(q, k_blk, (((1,), (1,)), ((), ())), preferred_element_type=F32)


def _mem_attn_kernel(q_ref, k_ref, v_ref, qg_ref, o_ref):
    shift = R_MQ % MEM_WINDOW
    qall = q_ref[...].astype(F32)[:, shift:shift + MEM_WIDTH]
    for h in range(MEM_HEADS):
        lo = h * MEM_HEAD_DIM
        qh = qall[:, lo:lo + MEM_HEAD_DIM]
        ms = jnp.mean(qh * qh, axis=-1, keepdims=True)
        qn = (qh * lax.rsqrt(ms + NORM_EPS) * qg_ref[...]).astype(BF16)
        s = _qk_nt(qn, k_ref[:, lo:lo + MEM_HEAD_DIM])
        p = jnp.exp(s - jnp.max(s, axis=-1, keepdims=True))
        l = jnp.sum(p, axis=-1, keepdims=True)
        o = jnp.dot(p.astype(BF16), v_ref[:, lo:lo + MEM_HEAD_DIM], preferred_element_type=F32)
        o_ref[:, lo:lo + MEM_HEAD_DIM] = (o / l).astype(o_ref.dtype)


def _mem_attention(rest, k_mem, v_mem, qg_scaled, tm):
    s = rest.shape[0]
    n_mem = k_mem.shape[0]
    assert R_MQ % MEM_WINDOW + MEM_WIDTH <= MEM_WINDOW
    return pl.pallas_call(
        _mem_attn_kernel,
        grid=(s // tm,),
        in_specs=[pl.BlockSpec((tm, MEM_WINDOW), lambda i: (i, R_MQ // MEM_WINDOW)),
                  pl.BlockSpec((n_mem, MEM_WIDTH), lambda i: (0, 0)),
                  pl.BlockSpec((n_mem, MEM_WIDTH), lambda i: (0, 0)),
                  pl.BlockSpec((1, MEM_HEAD_DIM), lambda i: (0, 0))],
        out_specs=pl.BlockSpec((tm, MEM_WIDTH), lambda i: (i, 0)),
        out_shape=jax.ShapeDtypeStruct((s, MEM_WIDTH), BF16),
        compiler_params=_cp(("parallel",)),
        name="mem_attention",
    )(rest, k_mem, v_mem, qg_scaled)


def _mix_kernel(od_ref, om_ref, oc_ref, wd_ref, wm_ref, wc_ref, g0_ref, g1_ref, g2_ref, o_ref):
    yd = jnp.dot(od_ref[...], wd_ref[...], preferred_element_type=F32)
    ym = jnp.dot(om_ref[...], wm_ref[...], preferred_element_type=F32)
    yc = jnp.dot(oc_ref[...], wc_ref[...], preferred_element_type=F32)
    mixed = (g0_ref[...].astype(F32) * yd + g1_ref[...].astype(F32) * ym) + g2_ref[...].astype(F32) * yc
    o_ref[...] = mixed.astype(o_ref.dtype)


def _mix(o_diff, o_mla, o_mem, w_d, w_m, w_c, gates, tm, tn):
    s = o_diff.shape[0]
    d = w_d.shape[1]
    nj = d // tn
    return pl.pallas_call(
        _mix_kernel,
        grid=(s // tm, nj),
        in_specs=[pl.BlockSpec((tm, o_diff.shape[1]), lambda i, j: (i, 0)),
                  pl.BlockSpec((tm, o_mla.shape[1]), lambda i, j: (i, 0)),
                  pl.BlockSpec((tm, o_mem.shape[1]), lambda i, j: (i, 0)),
                  pl.BlockSpec((w_d.shape[0], tn), lambda i, j: (0, j)),
                  pl.BlockSpec((w_m.shape[0], tn), lambda i, j: (0, j)),
                  pl.BlockSpec((w_c.shape[0], tn), lambda i, j: (0, j)),
                  pl.BlockSpec((tm, tn), lambda i, j: (i, j)),
                  pl.BlockSpec((tm, tn), lambda i, j: (i, nj + j)),
                  pl.BlockSpec((tm, tn), lambda i, j: (i, 2 * nj + j))],
        out_specs=pl.BlockSpec((tm, tn), lambda i, j: (i, j)),
        out_shape=jax.ShapeDtypeStruct((s, d), BF16),
        compiler_params=_cp(("arbitrary", "arbitrary")),
        name="gated_mix",
    )(o_diff, o_mla, o_mem, w_d, w_m, w_c, gates, gates, gates)


def _router_kernel(x_ref, g_ref, w_ref, b_ref, h_ref, r_ref):
    x = x_ref[...]
    ms = jnp.mean(x * x, axis=-1, keepdims=True)
    h = x * lax.rsqrt(ms + NORM_EPS) * g_ref[...]
    h_ref[...] = _pack_bf16_pairs(h)
    logits = jnp.dot(h, w_ref[...], preferred_element_type=F32, precision=lax.Precision.HIGHEST) + b_ref[...]
    lane = lax.broadcasted_iota(jnp.int32, logits.shape, 1)
    lane_f = lane.astype(F32)
    big = float(4 * ROUTE_W)
    lg = jnp.where(lane < N_GROUPS, logits, -jnp.inf)
    gmax = jnp.max(lg, axis=-1, keepdims=True)
    gidx = jnp.min(jnp.where(lg == gmax, lane_f, big), axis=-1, keepdims=True)
    pg_top = 1.0 / jnp.sum(jnp.exp(lg - gmax), axis=-1, keepdims=True)
    e_lane = lane - N_GROUPS
    lane_group = jnp.right_shift(e_lane, 3).astype(F32)
    in_group = (e_lane >= 0) & (e_lane < N_EXPERTS) & (lane_group == gidx)
    le = jnp.where(in_group, logits, -jnp.inf)
    e1 = jnp.max(le, axis=-1, keepdims=True)
    i1 = jnp.min(jnp.where(le == e1, lane_f, big), axis=-1, keepdims=True)
    le2 = jnp.where(lane_f == i1, -jnp.inf, le)
    e2 = jnp.max(le2, axis=-1, keepdims=True)
    i2 = jnp.min(jnp.where(le2 == e2, lane_f, big), axis=-1, keepdims=True)
    w2 = jnp.exp(e2 - e1)
    inv = 1.0 / (1.0 + w2)
    gate1 = pg_top * inv
    gate2 = pg_top * (w2 * inv)
    out = jnp.where(lane == 0, i1 - N_GROUPS,
                    jnp.where(lane == 1, i2 - N_GROUPS,
                              jnp.where(lane == 2, gate1, jnp.where(lane == 3, gate2, 0.0))))
    r_ref[...] = out


def _router(x1, g, w_r, b_r, tm):
    s, d = x1.shape
    return pl.pallas_call(
        _router_kernel,
        grid=(s // tm,),
        in_specs=[pl.BlockSpec((tm, d), lambda i: (i, 0)),
                  pl.BlockSpec((1, d), lambda i: (0, 0)),
                  pl.BlockSpec((d, ROUTE_W), lambda i: (0, 0)),
                  pl.BlockSpec((1, ROUTE_W), lambda i: (0, 0))],
        out_specs=[pl.BlockSpec((tm, d // 2), lambda i: (i, 0)),
                   pl.BlockSpec((tm, ROUTE_W), lambda i: (i, 0))],
        out_shape=[jax.ShapeDtypeStruct((s, d // 2), jnp.uint32),
                   jax.ShapeDtypeStruct((s, ROUTE_W), F32)],
        compiler_params=_cp(("parallel",)),
        name="ffn_norm_router",
    )(x1, g, w_r, b_r)


def _pack_bf16_pairs(v):
    n = v.shape[1] // 2
    bits = lax.bitcast_convert_type(v.astype(BF16).astype(F32), jnp.uint32)
    return jnp.right_shift(bits[:, :n], jnp.uint32(16)) | bits[:, n:]


def _unpack_bf16_pairs(words):
    lo = lax.bitcast_convert_type(jnp.left_shift(words, jnp.uint32(16)), F32)
    hi = lax.bitcast_convert_type(words & jnp.uint32(0xFFFF0000), F32)
    return lo, hi


def _moe_kernel(be_ref, nr_ref, nu_ref, tok_ref, tokn_ref, h_ref, wg_ref, wu_ref, wd_ref, o_ref,
                xg_ref, xb_ref, a_ref, wgb_ref, wub_ref, wdb_ref, sem):
    b = pl.program_id(0)
    c = pl.program_id(1)
    nb = pl.num_programs(0)
    nr = nr_ref[b]
    half = xg_ref.shape[2]
    fh = wgb_ref.shape[1]

    def row_copy(src_row, slot, r):
        return pltpu.make_async_copy(h_ref.at[pl.ds(src_row, 1), :], xg_ref.at[slot, pl.ds(r, 1), :], sem.at[slot])

    def start_gather(tok, n_rows, slot):
        def issue(r, carry):
            row_copy(tok[0, 0, r], slot, r).start(priority=1)
            return carry
        lax.fori_loop(0, n_rows, issue, 0)

    def for_row_count(fn):
        for units in range(1, MOE_TB // MOE_RU + 1):
            @pl.when(nr == units)
            def _():
                fn(units * MOE_RU)

    @pl.when(c == 0)
    def _():
        slot = b % 2

        @pl.when(b == 0)
        def _():
            start_gather(tok_ref, nr * MOE_RU, 0)

        def drain(u, carry):
            pltpu.make_async_copy(h_ref.at[pl.ds(0, MOE_RU), :], xg_ref.at[slot, pl.ds(0, MOE_RU), :],
                                  sem.at[slot]).wait()
            return carry
        lax.fori_loop(0, nr, drain, 0)

        @pl.when(b + 1 < nb)
        def _():
            start_gather(tokn_ref, nr_ref[jnp.minimum(b + 1, nb - 1)] * MOE_RU, 1 - slot)

        def unpack(u, carry):
            r0 = pl.multiple_of(u * MOE_RU, MOE_RU)
            lo, hi = _unpack_bf16_pairs(xg_ref[slot, pl.ds(r0, MOE_RU), :])
            xb_ref[pl.ds(r0, MOE_RU), :half] = lo.astype(BF16)
            xb_ref[pl.ds(r0, MOE_RU), half:] = hi.astype(BF16)
            return carry
        lax.fori_loop(0, nr, unpack, 0)

    @pl.when((c < 2) & (nr > 0))
    def _():
        wgb_ref[...] = wg_ref[0].astype(BF16)
        wub_ref[...] = wu_ref[0].astype(BF16)

        def gate_up(m):
            x = xb_ref[:m, :]
            g = jnp.dot(x, wgb_ref[...], preferred_element_type=F32)
            u = jnp.dot(x, wub_ref[...], preferred_element_type=F32)
            a_ref[c, :m, :] = ((g * jax.nn.sigmoid(g)) * u).astype(BF16)

        for_row_count(gate_up)

    @pl.when((c >= 2) & (nr > 0))
    def _():
        wdb_ref[...] = wd_ref[0].astype(BF16)

        def down(m):
            yv = (jnp.dot(a_ref[0, :m, :], wdb_ref[:fh, :], preferred_element_type=F32)
                  + jnp.dot(a_ref[1, :m, :], wdb_ref[fh:, :], preferred_element_type=F32))
            o_ref[:m, :] = _pack_bf16_pairs(yv)
            if m < MOE_TB:
                o_ref[m:, :] = jnp.zeros((MOE_TB - m, o_ref.shape[1]), jnp.uint32)

        for_row_count(down)

    @pl.when((c >= 2) & (nr == 0))
    def _():
        o_ref[...] = jnp.zeros(o_ref.shape, jnp.uint32)


def _moe_experts(block_e, nsub, n_used, buf_tok, h2p, w_gate, w_up, w_down):
    n_blocks = block_e.shape[0]
    half = h2p.shape[1]
    d = 2 * half
    ff = w_gate.shape[2]
    fh = ff // 2
    dh = d // 2
    tok = buf_tok.reshape(n_blocks, 1, MOE_TB)

    def gate_up_idx(b, c, be, nr, nu):
        live = b < nu[0]
        return be[jnp.minimum(b, nu[0] - 1)], 0, jnp.where(live, jnp.minimum(c, 1), 1)

    def down_idx(b, c, be, nr, nu):
        bb = jnp.minimum(b, nu[0] - 1)
        cc = jnp.where(b < nu[0], c, MOE_PH - 1)
        e = jnp.where(cc >= 2, be[bb], be[jnp.maximum(bb - 1, 0)])
        return e, 0, jnp.where(cc == 2, 0, 1)

    def out_idx(b, c, be, nr, nu):
        ob = jnp.where(c >= 2, b, jnp.maximum(b - 1, 0))
        oc = jnp.where(c >= 2, c - 2, jnp.where(b > 0, 1, 0))
        return ob, oc

    grid_spec = pltpu.PrefetchScalarGridSpec(
        num_scalar_prefetch=3,
        grid=(n_blocks, MOE_PH),
        in_specs=[pl.BlockSpec((1, 1, MOE_TB), lambda b, c, be, nr, nu: (b, 0, 0), memory_space=pltpu.SMEM),
                  pl.BlockSpec((1, 1, MOE_TB), lambda b, c, be, nr, nu: (jnp.minimum(b + 1, n_blocks - 1), 0, 0),
                               memory_space=pltpu.SMEM),
                  pl.BlockSpec(memory_space=pl.ANY),
                  pl.BlockSpec((1, d, fh), gate_up_idx),
                  pl.BlockSpec((1, d, fh), gate_up_idx),
                  pl.BlockSpec((1, ff, dh), down_idx)],
        out_specs=pl.BlockSpec((MOE_TB, dh // 2), out_idx),
        scratch_shapes=[pltpu.VMEM((2, MOE_TB, half), jnp.uint32), pltpu.VMEM((MOE_TB, d), BF16),
                        pltpu.VMEM((2, MOE_TB, fh), BF16),
                        pltpu.VMEM((d, fh), BF16), pltpu.VMEM((d, fh), BF16), pltpu.VMEM((ff, dh), BF16),
                        pltpu.SemaphoreType.DMA((2,))],
    )
    return pl.pallas_call(
        _moe_kernel,
        grid_spec=grid_spec,
        out_shape=jax.ShapeDtypeStruct((n_blocks * MOE_TB, half), jnp.uint32),
        compiler_params=_cp(("arbitrary", "arbitrary")),
        name="moe_experts",
    )(block_e, nsub, n_used, tok, tok, h2p, w_gate, w_up, w_down)


def _combine_kernel(slot_ref, r_ref, x_ref, yb_ref, o_ref, g_ref, sem):
    tm = x_ref.shape[0]
    n = g_ref.shape[2]

    def issue(r, c):
        for k in range(TOP_K):
            sl = slot_ref[0, 0, r * TOP_K + k]
            pltpu.make_async_copy(yb_ref.at[pl.ds(sl, 1), :], g_ref.at[k, pl.ds(r, 1), :],
                                  sem.at[k]).start(priority=k)
        return c

    lax.fori_loop(0, tm, issue, 0)
    for k in range(TOP_K):
        pltpu.make_async_copy(yb_ref.at[pl.ds(0, tm), :], g_ref.at[k], sem.at[k]).wait()

    route = r_ref[...]
    w0 = route[:, TOP_K:TOP_K + 1]
    w1 = route[:, TOP_K + 1:TOP_K + 2]
    q = n // 2
    for hf in range(2):
        lo0, hi0 = _unpack_bf16_pairs(g_ref[0, :, hf * q:(hf + 1) * q])
        lo1, hi1 = _unpack_bf16_pairs(g_ref[1, :, hf * q:(hf + 1) * q])
        c0 = hf * n
        o_ref[:, c0:c0 + q] = x_ref[:, c0:c0 + q] + (lo0 * w0 + lo1 * w1)
        o_ref[:, c0 + q:c0 + n] = x_ref[:, c0 + q:c0 + n] + (hi0 * w0 + hi1 * w1)


def _combine(slots, route, x1, yb, tm):
    s, d = x1.shape
    return pl.pallas_call(
        _combine_kernel,
        grid=(s // tm,),
        in_specs=[pl.BlockSpec((1, 1, tm * TOP_K), lambda i: (i, 0, 0), memory_space=pltpu.SMEM),
                  pl.BlockSpec((tm, ROUTE_W), lambda i: (i, 0)),
                  pl.BlockSpec((tm, d), lambda i: (i, 0)),
                  pl.BlockSpec(memory_space=pl.ANY)],
        out_specs=pl.BlockSpec((tm, d), lambda i: (i, 0)),
        out_shape=jax.ShapeDtypeStruct((s, d), F32),
        scratch_shapes=[pltpu.VMEM((TOP_K, tm, d // 2), jnp.uint32), pltpu.SemaphoreType.DMA((TOP_K,))],
        compiler_params=_cp(("arbitrary",)),
        name="moe_combine",
    )(slots.reshape(s // tm, 1, tm * TOP_K), route, x1, yb)


def _dispatch_plan(route, s):
    a = s * TOP_K
    flat_e = route[:, :TOP_K].astype(jnp.int32).reshape(a)
    onehot = (flat_e[:, None] == jnp.arange(N_EXPERTS, dtype=jnp.int32)[None, :]).astype(jnp.int32)
    csum = jnp.cumsum(onehot, axis=0)
    counts = csum[-1]
    rank = jnp.sum((csum - onehot) * onehot, axis=1)
    padded = (counts + MOE_TB - 1) // MOE_TB * MOE_TB
    pad_end = jnp.cumsum(padded)
    pad_start = pad_end - padded
    dest = pad_start[flat_e] + rank
    n_blocks = a // MOE_TB + N_EXPERTS
    p_rows = n_blocks * MOE_TB
    buf_tok = jnp.zeros((p_rows,), jnp.int32).at[dest].set(jnp.arange(a, dtype=jnp.int32) // TOP_K)
    starts = jnp.arange(n_blocks, dtype=jnp.int32) * MOE_TB
    block_e = jnp.minimum(jnp.searchsorted(pad_end, starts, side='right'), N_EXPERTS - 1).astype(jnp.int32)
    valid = jnp.clip(counts[block_e] - (starts - pad_start[block_e]), 0, MOE_TB)
    valid = jnp.where(starts < pad_end[-1], valid, 0)
    nsub = ((valid + MOE_RU - 1) // MOE_RU).astype(jnp.int32)
    n_used = (pad_end[-1:] // MOE_TB).astype(jnp.int32)
    return block_e, nsub, n_used, buf_tok, dest.astype(jnp.int32)


def _rope_tables(positions):
    half = MLA_ROPE_DIM // 2
    inv_freq = ROPE_THETA ** (-jnp.arange(half, dtype=F32) / half)
    ang = positions.astype(F32)[:, None] * inv_freq[None, :]
    cos, sin = jnp.cos(ang), jnp.sin(ang)
    z = jnp.zeros_like(cos)
    c = jnp.concatenate([cos, cos, z, z], axis=-1)
    s1 = jnp.concatenate([-sin, z, z, z], axis=-1)
    s2 = jnp.concatenate([z, sin, z, z], axis=-1)
    return c, s1, s2


def kernel(x, mem, positions, rel_bias, mix_norm_g, w_in, diff_q_norm_g, diff_k_norm_g, diff_lambda_q1, diff_lambda_k1, diff_lambda_q2, diff_lambda_k2, diff_subln_g, mla_cq_norm_g, mla_ckv_norm_g, mla_w_uq, mla_w_ukv, mla_q_norm_g, mla_k_norm_g, mem_norm_g, mem_w_kv, mem_q_norm_g, mem_k_norm_g, w_o_diff, w_o_mla, w_o_mem, w_out, ffn_norm_g, w_route_group, b_route_group, w_route_expert, b_route_expert, w_exp_gate, w_exp_up, w_exp_down):
    b, s, d = x.shape
    assert b == 1 and s % ATT_T == 0
    depth = mix_norm_g.shape[0]
    xs = x.reshape(s, d)
    pos = positions.reshape(s)
    rope_tabs = _rope_tables(pos)
    row = lambda v: v.reshape(1, -1).astype(F32)

    for l in range(depth):
        lam_init = 0.8 - 0.6 * math.exp(-0.3 * l)
        h = _rmsnorm_rows(xs, mix_norm_g[l], 256)
        tn = 512
        q_gain = jnp.tile(diff_q_norm_g[l] * (DIFF_HEAD_DIM ** -0.5 * LOG2E), DIFF_MAPS)
        w_in_t = jnp.transpose(w_in[l])
        dq_t = _matmul_nt(h, w_in_t, row0=OFF_DQ, n_cols=DIFF_QK_WIDTH, tm=512, tn=tn, out_dtype=BF16,
                          mode="groupnorm", extra=q_gain, group=DIFF_HEAD_DIM, transpose_out=True,
                          name="diff_q_proj")
        dk = _matmul_nt(h, w_in_t, row0=OFF_DK, n_cols=DIFF_QK_WIDTH, tm=512, tn=tn, out_dtype=BF16,
                        mode="groupnorm", extra=jnp.tile(diff_k_norm_g[l], DIFF_MAPS), group=DIFF_HEAD_DIM,
                        name="diff_k_proj")
        dv_t = _matmul_nt(h, w_in_t, row0=OFF_DV, n_cols=DIFF_WIDTH, tm=512, tn=tn, out_dtype=BF16,
                          transpose_out=True, name="diff_v_proj")
        rest = _matmul_nt(h, w_in_t, row0=OFF_CQ, n_cols=REST_WIDTH, tm=512, tn=tn, out_dtype=BF16,
                          name="rest_proj")
        gates = _matmul_nt(h, w_in_t, row0=OFF_GATES, n_cols=3 * d, tm=512, tn=tn, out_dtype=BF16,
                           mode="sigmoid", name="gate_proj")

        bias_tiles = _diff_bias_tiles(rel_bias, ATT_T)
        lam_vecs = [row(diff_lambda_q1[l]), row(diff_lambda_k1[l]), row(diff_lambda_q2[l]), row(diff_lambda_k2[l])]
        o_diff = _diff_attention(dq_t, dk, dv_t, bias_tiles, lam_vecs, diff_subln_g[l], lam_init)

        w_uq_heads = jnp.pad(
            mla_w_uq[l].reshape(MLA_Q_RANK, MLA_HEADS, MLA_QK_DIM).transpose(1, 0, 2),
            ((0, 0), (0, 0), (0, MLA_QK_PAD - MLA_QK_DIM))).astype(BF16)
        qg_pad = jnp.pad(mla_q_norm_g[l] * (MLA_QK_DIM ** -0.5 * LOG2E),
                         (0, MLA_QK_PAD - MLA_QK_DIM)).reshape(1, -1).astype(F32)
        q_mla_t = _mla_q_prep(rest, row(mla_cq_norm_g[l]), w_uq_heads, qg_pad, rope_tabs, 512)
        kg = mla_k_norm_g[l]
        kg_nope = row(kg[:MLA_NOPE_DIM])
        kg_rope = jnp.pad(kg[MLA_NOPE_DIM:], (0, LANE - MLA_ROPE_DIM)).reshape(1, -1).astype(F32)
        k_mla, v_mla_t = _mla_kv_prep(rest, row(mla_ckv_norm_g[l]), mla_w_ukv[l], kg_nope, kg_rope, rope_tabs, 512)
        o_mla = _mla_attention(q_mla_t, k_mla, v_mla_t)

        n_mem = mem.shape[1]
        mem_h = _rmsnorm_rows(mem.reshape(n_mem, d), mem_norm_g[l], n_mem)
        k_mem = _matmul(mem_h, mem_w_kv[l], col0=0, n_cols=MEM_WIDTH, tm=n_mem, tn=512, out_dtype=BF16,
                        mode="groupnorm", extra=jnp.tile(mem_k_norm_g[l], MEM_HEADS), group=MEM_HEAD_DIM,
                        name="mem_k_proj")
        v_mem = _matmul(mem_h, mem_w_kv[l], col0=MEM_WIDTH, n_cols=MEM_WIDTH, tm=n_mem, tn=512, out_dtype=BF16,
                        name="mem_v_proj")
        o_mem = _mem_attention(rest, k_mem, v_mem, row(mem_q_norm_g[l] * MEM_HEAD_DIM ** -0.5), 512)

        mixed = _mix(o_diff, o_mla, o_mem, w_o_diff[l].astype(BF16), w_o_mla[l].astype(BF16),
                     w_o_mem[l].astype(BF16), gates, 512, 512)
        x1 = _matmul(mixed, w_out[l], n_cols=d, tm=512, tn=512, out_dtype=F32, mode="residual", extra=xs,
                     name="out_proj")

        w_r = jnp.pad(jnp.concatenate([w_route_group[l], w_route_expert[l]], axis=1),
                      ((0, 0), (0, ROUTE_W - N_GROUPS - N_EXPERTS))).astype(F32)
        b_r = jnp.pad(jnp.concatenate([b_route_group[l], b_route_expert[l]]),
                      (0, ROUTE_W - N_GROUPS - N_EXPERTS)).reshape(1, -1).astype(F32)
        h2, route = _router(x1, row(ffn_norm_g[l]), w_r, b_r, 256)
        block_e, nsub, n_used, buf_tok, slots = _dispatch_plan(route, s)
        yb = _moe_experts(block_e, nsub, n_used, buf_tok, h2, w_exp_gate[l], w_exp_up[l], w_exp_down[l])
        xs = _combine(slots, route, x1, yb, 256)
    return xs.reshape(b, s, d)
```

```python
import functools
import math

import jax
import jax.numpy as jnp
from jax import lax
from jax.experimental import pallas as pl
from jax.experimental.pallas import tpu as pltpu

F32 = jnp.float32
BF16 = jnp.bfloat16

NORM_EPS = 1e-6
NEG_INF = -1e30
LOG2E = math.log2(math.e)

DIFF_HEADS = 6
DIFF_HEAD_DIM = 128
DIFF_V_DIM = 256
DIFF_MAPS = 12
DIFF_QK_WIDTH = 1536
DIFF_WIDTH = 1536
MLA_HEADS = 12
MLA_Q_RANK = 1536
MLA_KV_RANK = 512
MLA_NOPE_DIM = 128
MLA_ROPE_DIM = 64
MLA_QK_DIM = 192
MLA_QK_PAD = 256
MLA_V_DIM = 128
MLA_WIDTH = 1536
ROPE_THETA = 10000.0
MEM_HEADS = 4
MEM_HEAD_DIM = 256
MEM_WIDTH = 1024
REL_BUCKETS = 32
REL_MAX_DIST = 128
N_GROUPS = 8
EXPERTS_PER_GROUP = 8
N_EXPERTS = 64
TOP_K = 2
EXPERT_FF = 512

OFF_DQ = 0
OFF_DK = 1536
OFF_DV = 3072
OFF_CQ = 4608
OFF_CKV = 6144
OFF_KROPE = 6656
OFF_MQ = 6720
OFF_GATES = 7744
REST_WIDTH = 4096
R_CQ = OFF_CQ - OFF_CQ
R_CKV = OFF_CKV - OFF_CQ
R_KROPE = OFF_KROPE - OFF_CQ
R_MQ = OFF_MQ - OFF_CQ

LANE = 128
VMEM_LIMIT = 52 * 1024 * 1024

MM_TM = 1024
MLA_HEAD_GROUP = 4
ATT_T = 512
MOE_TB = 512
MOE_RU = 128
MOE_PH = 4
ROUTE_W = 128
MEM_WINDOW = 2048


def _cp(sem, vmem=VMEM_LIMIT):
    return pltpu.CompilerParams(dimension_semantics=sem, vmem_limit_bytes=vmem)


def _rmsnorm_kernel(x_ref, g_ref, o_ref):
    x = x_ref[...].astype(F32)
    ms = jnp.mean(x * x, axis=-1, keepdims=True)
    o_ref[...] = (x * lax.rsqrt(ms + NORM_EPS) * g_ref[...]).astype(o_ref.dtype)


def _rmsnorm_rows(x, g, tm, out_dtype=BF16):
    m, d = x.shape
    return pl.pallas_call(
        _rmsnorm_kernel,
        grid=(m // tm,),
        in_specs=[pl.BlockSpec((tm, d), lambda i: (i, 0)),
                  pl.BlockSpec((1, d), lambda i: (0, 0))],
        out_specs=pl.BlockSpec((tm, d), lambda i: (i, 0)),
        out_shape=jax.ShapeDtypeStruct((m, d), out_dtype),
        compiler_params=_cp(("parallel",)),
        name="rmsnorm_rows",
    )(x, g.reshape(1, d).astype(F32))


def _cast_shifted(w_ref, w2_ref, wb_ref, shift):
    k = w_ref.shape[0]
    rows = 256

    def body(c, carry):
        r0 = pl.multiple_of(c * rows, rows)
        main = w_ref[pl.ds(r0, rows), :]
        tail = w2_ref[pl.ds(r0, rows), :]
        wb_ref[pl.ds(r0, rows), :] = jnp.concatenate([main[:, shift:], tail[:, :shift]], axis=1).astype(BF16)
        return carry

    lax.fori_loop(0, k // rows, body, 0)


def _mm_kernel(*refs, mode, cast, group, shift, transpose_out):
    a_ref, w_ref = refs[0], refs[1]
    pos = 2
    w2_ref = None
    if shift:
        w2_ref = refs[pos]
        pos += 1
    extra = None
    if mode in ("groupnorm", "residual"):
        extra = refs[pos]
        pos += 1
    o_ref = refs[pos]
    wb_ref = refs[pos + 1] if cast else None
    i = pl.program_id(1)
    if cast:
        @pl.when(i == 0)
        def _():
            if shift:
                _cast_shifted(w_ref, w2_ref, wb_ref, shift)
            else:
                wb_ref[...] = w_ref[...].astype(BF16)
        w = wb_ref[...]
    else:
        w = w_ref[...]
    acc = jnp.dot(a_ref[...], w, preferred_element_type=F32)
    if transpose_out:
        acc = acc.T
    if mode == "plain":
        o_ref[...] = acc.astype(o_ref.dtype)
    elif mode == "sigmoid":
        o_ref[...] = jax.nn.sigmoid(acc).astype(o_ref.dtype)
    elif mode == "residual":
        o_ref[...] = (extra[...] + acc).astype(o_ref.dtype)
    elif mode == "groupnorm":
        tn = w.shape[1]
        for c in range(tn // group):
            sl = slice(c * group, (c + 1) * group)
            if transpose_out:
                blk = acc[sl, :]
                ms = jnp.mean(blk * blk, axis=0, keepdims=True)
                o_ref[sl, :] = (blk * lax.rsqrt(ms + NORM_EPS) * extra[sl, :]).astype(o_ref.dtype)
            else:
                blk = acc[:, sl]
                ms = jnp.mean(blk * blk, axis=-1, keepdims=True)
                o_ref[:, sl] = (blk * lax.rsqrt(ms + NORM_EPS) * extra[:, sl]).astype(o_ref.dtype)


def _matmul(a, w, *, n_cols, tm, tn, out_dtype, col0=0, mode="plain", extra=None, group=LANE,
            transpose_out=False, name="matmul"):
    m, k = a.shape
    assert m % tm == 0 and n_cols % tn == 0 and w.shape[0] == k
    cast = w.dtype != BF16
    base, shift = divmod(col0, tn)
    assert shift <= LANE and (shift == 0 or cast)
    in_specs = [pl.BlockSpec((tm, k), lambda j, i: (i, 0)),
                pl.BlockSpec((k, tn), lambda j, i: (0, base + j))]
    args = [a, w]
    if shift:
        in_specs.append(pl.BlockSpec((k, LANE), lambda j, i: (0, (base + j + 1) * (tn // LANE))))
        args.append(w)
    if mode == "groupnorm":
        gain = extra.reshape(-1, 1) if transpose_out else extra.reshape(1, -1)
        in_specs.append(pl.BlockSpec((tn, 1), lambda j, i: (j, 0)) if transpose_out
                        else pl.BlockSpec((1, tn), lambda j, i: (0, j)))
        args.append(gain.astype(F32))
    elif mode == "residual":
        assert not transpose_out
        in_specs.append(pl.BlockSpec((tm, tn), lambda j, i: (i, j)))
        args.append(extra)
    if transpose_out:
        out_spec = pl.BlockSpec((tn, tm), lambda j, i: (j, i))
        out_shape = jax.ShapeDtypeStruct((n_cols, m), out_dtype)
    else:
        out_spec = pl.BlockSpec((tm, tn), lambda j, i: (i, j))
        out_shape = jax.ShapeDtypeStruct((m, n_cols), out_dtype)
    scratch = [pltpu.VMEM((k, tn), BF16)] if cast else []
    return pl.pallas_call(
        functools.partial(_mm_kernel, mode=mode, cast=cast, group=group, shift=shift, transpose_out=transpose_out),
        grid=(n_cols // tn, m // tm),
        in_specs=in_specs,
        out_specs=out_spec,
        out_shape=out_shape,
        scratch_shapes=scratch,
        compiler_params=_cp(("arbitrary", "arbitrary")),
        name=name,
    )(*args)


def _mm_nt_kernel(*refs, mode, group, transpose_out):
    a_ref, wt_ref = refs[0], refs[1]
    extra = refs[2] if mode == "groupnorm" else None
    o_ref, wb_ref = refs[-2], refs[-1]
    i = pl.program_id(1)

    @pl.when(i == 0)
    def _():
        wb_ref[...] = wt_ref[...].astype(BF16)

    contract_last = (((1,), (1,)), ((), ()))
    if transpose_out:
        acc = lax.dot_general(wb_ref[...], a_ref[...], contract_last, preferred_element_type=F32)
    else:
        acc = lax.dot_general(a_ref[...], wb_ref[...], contract_last, preferred_element_type=F32)
    if mode == "plain":
        o_ref[...] = acc.astype(o_ref.dtype)
    elif mode == "sigmoid":
        o_ref[...] = jax.nn.sigmoid(acc).astype(o_ref.dtype)
    elif mode == "groupnorm":
        tn = wb_ref.shape[0]
        for c in range(tn // group):
            sl = slice(c * group, (c + 1) * group)
            if transpose_out:
                blk = acc[sl, :]
                ms = jnp.mean(blk * blk, axis=0, keepdims=True)
                o_ref[sl, :] = (blk * lax.rsqrt(ms + NORM_EPS) * extra[sl, :]).astype(o_ref.dtype)
            else:
                blk = acc[:, sl]
                ms = jnp.mean(blk * blk, axis=-1, keepdims=True)
                o_ref[:, sl] = (blk * lax.rsqrt(ms + NORM_EPS) * extra[:, sl]).astype(o_ref.dtype)


def _matmul_nt(a, wt, *, row0, n_cols, tm, tn, out_dtype, mode="plain", extra=None, group=LANE,
               transpose_out=False, name="matmul_nt"):
    m, k = a.shape
    assert m % tm == 0 and n_cols % tn == 0 and wt.shape[1] == k and row0 % 8 == 0
    in_specs = [pl.BlockSpec((tm, k), lambda j, i: (i, 0)),
                pl.BlockSpec((pl.Element(tn), pl.Element(k)), lambda j, i: (pl.multiple_of(row0 + j * tn, 8), 0))]
    args = [a, wt]
    if mode == "groupnorm":
        gain = extra.reshape(-1, 1) if transpose_out else extra.reshape(1, -1)
        in_specs.append(pl.BlockSpec((tn, 1), lambda j, i: (j, 0)) if transpose_out
                        else pl.BlockSpec((1, tn), lambda j, i: (0, j)))
        args.append(gain.astype(F32))
    if transpose_out:
        out_spec = pl.BlockSpec((tn, tm), lambda j, i: (j, i))
        out_shape = jax.ShapeDtypeStruct((n_cols, m), out_dtype)
    else:
        out_spec = pl.BlockSpec((tm, tn), lambda j, i: (i, j))
        out_shape = jax.ShapeDtypeStruct((m, n_cols), out_dtype)
    return pl.pallas_call(
        functools.partial(_mm_nt_kernel, mode=mode, group=group, transpose_out=transpose_out),
        grid=(n_cols // tn, m // tm),
        in_specs=in_specs,
        out_specs=out_spec,
        out_shape=out_shape,
        scratch_shapes=[pltpu.VMEM((tn, k), BF16)],
        compiler_params=_cp(("arbitrary", "arbitrary")),
        name=name,
    )(*args)


def _softmax_pv(idx, s, vt_blk, m_ref, l_ref, acc_ref):
    m_prev = m_ref[idx]
    m_new = jnp.maximum(m_prev, jnp.max(s, axis=0, keepdims=True))
    alpha = jnp.exp2(m_prev - m_new)
    p = jnp.exp2(s - m_new)
    l_ref[idx] = alpha * l_ref[idx] + jnp.sum(p, axis=0, keepdims=True)
    acc_ref[idx] = alpha * acc_ref[idx] + jnp.dot(vt_blk, p.astype(BF16), preferred_element_type=F32)
    m_ref[idx] = m_new


def _block_offset(kb, t):
    return kb * t if isinstance(kb, int) else pl.multiple_of(kb * t, t)


def _by_parity(kb, fn, even_ref, odd_ref):
    @pl.when(kb % 2 == 0)
    def _():
        fn(even_ref, odd_ref)

    @pl.when(kb % 2 == 1)
    def _():
        fn(odd_ref, even_ref)


def _init_stats(m_ref, l_ref, acc_ref):
    m_ref[...] = jnp.full(m_ref.shape, NEG_INF, F32)
    l_ref[...] = jnp.zeros(l_ref.shape, F32)
    acc_ref[...] = jnp.zeros(acc_ref.shape, F32)


def _diff_attn_kernel(lq1_ref, lk1_ref, lq2_ref, lk2_ref, q1_ref, q2_ref, k1_ref, k2_ref, vt_ref,
                      b1_ref, b2_ref, g_ref, o_ref, sa_ref, sb_ref, m_ref, l_ref, acc_ref, *, lam_init):
    t = q1_ref.shape[1]
    i = pl.program_id(1)
    qts = (q1_ref[...], q2_ref[...])
    ks = (k1_ref, k2_ref)
    bs = (b1_ref, b2_ref)
    _init_stats(m_ref, l_ref, acc_ref)

    def scores(kb, dst):
        off = _block_offset(kb, t)
        for mp in range(2):
            dst[mp] = jnp.dot(ks[mp][pl.ds(off, t), :], qts[mp], preferred_element_type=F32)

    def consume(kb, src, bias_idx):
        off = _block_offset(kb, t)
        vt_blk = vt_ref[:, pl.ds(off, t)]
        for mp in range(2):
            s = src[mp]
            if bias_idx is not None:
                s = s + bs[mp][0, bias_idx]
            _softmax_pv(mp, s, vt_blk, m_ref, l_ref, acc_ref)

    def step(kb, bias_idx, last=False):
        def run(cur, nxt):
            if not last:
                scores(kb + 1, nxt)
            consume(kb, cur, bias_idx)
        _by_parity(kb, run, sa_ref, sb_ref)

    n_far = jnp.maximum(i - 1, 0)
    scores(0, sa_ref)

    def far_pair(j, c):
        kb = 2 * j
        scores(kb + 1, sb_ref)
        consume(kb, sa_ref, None)
        scores(kb + 2, sa_ref)
        consume(kb + 1, sb_ref, None)
        return c

    lax.fori_loop(0, n_far // 2, far_pair, 0)

    @pl.when(n_far % 2 == 1)
    def _():
        scores(n_far, sb_ref)
        consume(n_far - 1, sa_ref, None)

    @pl.when(i >= 1)
    def _():
        step(i - 1, 1)

    step(i, 0, last=True)

    lam = (jnp.exp(jnp.sum(lq1_ref[...] * lk1_ref[...], axis=-1, keepdims=True))
           - jnp.exp(jnp.sum(lq2_ref[...] * lk2_ref[...], axis=-1, keepdims=True)) + lam_init)
    o = acc_ref[0] / l_ref[0] - lam * (acc_ref[1] / l_ref[1])
    ms = jnp.mean(o * o, axis=0, keepdims=True)
    o = (o * lax.rsqrt(ms + NORM_EPS) * g_ref[...]) * (1.0 - lam_init)
    o_ref[...] = o.T.astype(o_ref.dtype)


def _diff_attention(q_t, k, v_t, bias_tiles, lam_vecs, subln_g, lam_init):
    s = k.shape[0]
    t = ATT_T
    hd, vd = DIFF_HEAD_DIM, DIFF_V_DIM
    vec = pl.BlockSpec((1, hd), lambda h, i: (0, 0))
    in_specs = [vec, vec, vec, vec,
                pl.BlockSpec((hd, t), lambda h, i: (h, i)),
                pl.BlockSpec((hd, t), lambda h, i: (DIFF_HEADS + h, i)),
                pl.BlockSpec((s, hd), lambda h, i: (0, h)),
                pl.BlockSpec((s, hd), lambda h, i: (0, DIFF_HEADS + h)),
                pl.BlockSpec((vd, s), lambda h, i: (h, 0)),
                pl.BlockSpec((1, 2, t, t), lambda h, i: (h, 0, 0, 0)),
                pl.BlockSpec((1, 2, t, t), lambda h, i: (DIFF_HEADS + h, 0, 0, 0)),
                pl.BlockSpec((vd, 1), lambda h, i: (0, 0))]
    return pl.pallas_call(
        functools.partial(_diff_attn_kernel, lam_init=lam_init),
        grid=(DIFF_HEADS, s // t),
        in_specs=in_specs,
        out_specs=pl.BlockSpec((t, vd), lambda h, i: (i, h)),
        out_shape=jax.ShapeDtypeStruct((s, DIFF_WIDTH), BF16),
        scratch_shapes=[pltpu.VMEM((2, t, t), F32), pltpu.VMEM((2, t, t), F32),
                        pltpu.VMEM((2, 1, t), F32), pltpu.VMEM((2, 1, t), F32),
                        pltpu.VMEM((2, vd, t), F32)],
        compiler_params=_cp(("arbitrary", "arbitrary")),
        name="diff_attention",
    )(*lam_vecs, q_t, q_t, k, k, v_t, bias_tiles, bias_tiles, subln_g.reshape(vd, 1).astype(F32))


def _t5_bucket(dist):
    n = jnp.maximum(dist, 0)
    max_exact = REL_BUCKETS // 2
    nf = jnp.maximum(n, 1).astype(F32)
    large = max_exact + (jnp.log(nf / max_exact) / math.log(REL_MAX_DIST / max_exact)
                         * (REL_BUCKETS - max_exact)).astype(jnp.int32)
    large = jnp.minimum(large, REL_BUCKETS - 1)
    return jnp.where(n < max_exact, n, large)


def _diff_bias_tiles(rel_bias, t):
    assert t >= REL_MAX_DIST
    table = rel_bias.astype(F32)
    table = (table - table[REL_BUCKETS - 1:REL_BUCKETS]) * LOG2E
    kr = jnp.arange(t, dtype=jnp.int32)[:, None]
    qc = jnp.arange(t, dtype=jnp.int32)[None, :]
    d0 = qc - kr
    buckets = jnp.arange(REL_BUCKETS, dtype=jnp.int32)

    def lookup(dist):
        onehot = (_t5_bucket(dist)[:, :, None] == buckets).astype(F32)
        return jnp.einsum('rcb,bm->mrc', onehot, table, precision=lax.Precision.HIGHEST)

    tile0 = jnp.where((d0 >= 0)[None], lookup(d0), NEG_INF)
    return jnp.stack([tile0, lookup(d0 + t)], axis=1)


def _rope_apply(tv, c_ref, s1_ref, s2_ref):
    return (tv * c_ref[...] + pltpu.roll(tv, 96, 1) * s1_ref[...] + pltpu.roll(tv, 32, 1) * s2_ref[...])


def _mla_q_kernel(cq_ref, g_ref, w_ref, qg_ref, c_ref, s1_ref, s2_ref, o_ref, xg_ref):
    h = pl.program_id(1)

    @pl.when(h == 0)
    def _():
        c = cq_ref[...].astype(F32)
        r = lax.rsqrt(jnp.mean(c * c, axis=-1, keepdims=True) + NORM_EPS)
        xg_ref[...] = (c * r * g_ref[...]).astype(BF16)

    ug = jnp.dot(xg_ref[...], w_ref[...], preferred_element_type=F32)
    for hh in range(MLA_HEAD_GROUP):
        u = ug[:, hh * MLA_QK_PAD:(hh + 1) * MLA_QK_PAD]
        ms = jnp.sum(u * u, axis=-1, keepdims=True) * (1.0 / MLA_QK_DIM)
        qn = u * lax.rsqrt(ms + NORM_EPS) * qg_ref[...]
        o_ref[hh, :MLA_NOPE_DIM, :] = qn[:, :MLA_NOPE_DIM].T.astype(o_ref.dtype)
        o_ref[hh, MLA_NOPE_DIM:, :] = _rope_apply(qn[:, MLA_NOPE_DIM:], c_ref, s1_ref, s2_ref).T.astype(o_ref.dtype)


def _mla_q_prep(rest, cq_g, w_uq_pad, qg_pad, rope_tabs, tm):
    s = rest.shape[0]
    hg = MLA_HEAD_GROUP
    tab = pl.BlockSpec((tm, LANE), lambda i, h: (i, 0))
    return pl.pallas_call(
        _mla_q_kernel,
        grid=(s // tm, MLA_HEADS // hg),
        in_specs=[pl.BlockSpec((tm, MLA_Q_RANK), lambda i, h: (i, R_CQ // MLA_Q_RANK)),
                  pl.BlockSpec((1, MLA_Q_RANK), lambda i, h: (0, 0)),
                  pl.BlockSpec((MLA_Q_RANK, hg * MLA_QK_PAD), lambda i, h: (0, h)),
                  pl.BlockSpec((1, MLA_QK_PAD), lambda i, h: (0, 0)),
                  tab, tab, tab],
        out_specs=pl.BlockSpec((hg, MLA_QK_PAD, tm), lambda i, h: (h, 0, i)),
        out_shape=jax.ShapeDtypeStruct((MLA_HEADS, MLA_QK_PAD, s), BF16),
        scratch_shapes=[pltpu.VMEM((tm, MLA_Q_RANK), BF16)],
        compiler_params=_cp(("arbitrary", "arbitrary")),
        name="mla_q_prep",
    )(rest, cq_g, w_uq_pad, qg_pad, *rope_tabs)


def _mla_kv_kernel(ckv_ref, kr_ref, g_ref, w_ref, kgn_ref, kgr_ref, c_ref, s1_ref, s2_ref,
                   k_ref, vt_ref, xg_ref):
    h = pl.program_id(1)

    @pl.when(h == 0)
    def _():
        c = ckv_ref[...].astype(F32)
        r = lax.rsqrt(jnp.mean(c * c, axis=-1, keepdims=True) + NORM_EPS)
        xg_ref[...] = (c * r * g_ref[...]).astype(BF16)

    hw = MLA_NOPE_DIM + MLA_V_DIM
    kvg = jnp.dot(xg_ref[...], w_ref[...].astype(BF16), preferred_element_type=F32)
    lane = lax.broadcasted_iota(jnp.int32, kr_ref.shape, 1)
    kr = jnp.where(lane < MLA_ROPE_DIM, kr_ref[...].astype(F32), 0.0)
    kr_ss = jnp.sum(kr * kr, axis=-1, keepdims=True)
    for hh in range(MLA_HEAD_GROUP):
        kn = kvg[:, hh * hw:hh * hw + MLA_NOPE_DIM]
        ms = (jnp.sum(kn * kn, axis=-1, keepdims=True) + kr_ss) * (1.0 / MLA_QK_DIM)
        rs = lax.rsqrt(ms + NORM_EPS)
        k_ref[hh, :, :MLA_NOPE_DIM] = (kn * rs * kgn_ref[...]).astype(k_ref.dtype)
        k_ref[hh, :, MLA_NOPE_DIM:] = _rope_apply(kr * rs * kgr_ref[...], c_ref, s1_ref, s2_ref).astype(k_ref.dtype)
        vt_ref[hh] = kvg[:, hh * hw + MLA_NOPE_DIM:(hh + 1) * hw].T.astype(vt_ref.dtype)


def _mla_kv_prep(rest, ckv_g, w_ukv, kg_nope, kg_rope_pad, rope_tabs, tm):
    s = rest.shape[0]
    hg = MLA_HEAD_GROUP
    tab = pl.BlockSpec((tm, LANE), lambda i, h: (i, 0))
    hw = MLA_NOPE_DIM + MLA_V_DIM
    return pl.pallas_call(
        _mla_kv_kernel,
        grid=(s // tm, MLA_HEADS // hg),
        in_specs=[pl.BlockSpec((tm, MLA_KV_RANK), lambda i, h: (i, R_CKV // MLA_KV_RANK)),
                  pl.BlockSpec((tm, LANE), lambda i, h: (i, R_KROPE // LANE)),
                  pl.BlockSpec((1, MLA_KV_RANK), lambda i, h: (0, 0)),
                  pl.BlockSpec((MLA_KV_RANK, hg * hw), lambda i, h: (0, h)),
                  pl.BlockSpec((1, LANE), lambda i, h: (0, 0)),
                  pl.BlockSpec((1, LANE), lambda i, h: (0, 0)),
                  tab, tab, tab],
        out_specs=[pl.BlockSpec((hg, tm, MLA_QK_PAD), lambda i, h: (h, i, 0)),
                   pl.BlockSpec((hg, MLA_V_DIM, tm), lambda i, h: (h, 0, i))],
        out_shape=[jax.ShapeDtypeStruct((MLA_HEADS, s, MLA_QK_PAD), BF16),
                   jax.ShapeDtypeStruct((MLA_HEADS, MLA_V_DIM, s), BF16)],
        scratch_shapes=[pltpu.VMEM((tm, MLA_KV_RANK), BF16)],
        compiler_params=_cp(("arbitrary", "arbitrary")),
        name="mla_kv_prep",
    )(rest, rest, ckv_g, w_ukv, kg_nope, kg_rope_pad, *rope_tabs)


def _mla_attn_kernel(qt_ref, k_ref, vt_ref, o_ref, sa_ref, sb_ref, m_ref, l_ref, acc_ref):
    t = qt_ref.shape[2]
    i = pl.program_id(1)
    qt = qt_ref[0]
    _init_stats(m_ref, l_ref, acc_ref)

    def scores(kb, dst):
        off = _block_offset(kb, t)
        dst[...] = jnp.dot(k_ref[0, pl.ds(off, t), :], qt, preferred_element_type=F32)

    def consume(kb, src, diag):
        off = _block_offset(kb, t)
        s = src[...]
        if diag:
            krow = lax.broadcasted_iota(jnp.int32, s.shape, 0)
            qcol = lax.broadcasted_iota(jnp.int32, s.shape, 1)
            s = jnp.where(krow <= qcol, s, NEG_INF)
        _softmax_pv(0, s, vt_ref[0, :, pl.ds(off, t)], m_ref, l_ref, acc_ref)

    scores(0, sa_ref)

    def far_pair(j, c):
        kb = 2 * j
        scores(kb + 1, sb_ref)
        consume(kb, sa_ref, False)
        scores(kb + 2, sa_ref)
        consume(kb + 1, sb_ref, False)
        return c

    lax.fori_loop(0, i // 2, far_pair, 0)

    @pl.when(i % 2 == 1)
    def _():
        scores(i, sb_ref)
        consume(i - 1, sa_ref, False)

    _by_parity(i, lambda cur, other: consume(i, cur, True), sa_ref, sb_ref)
    o_ref[...] = (acc_ref[0] / l_ref[0]).T.astype(o_ref.dtype)


def _mla_attention(q_t, k, v_t):
    s = k.shape[1]
    t = ATT_T
    return pl.pallas_call(
        _mla_attn_kernel,
        grid=(MLA_HEADS, s // t),
        in_specs=[pl.BlockSpec((1, MLA_QK_PAD, t), lambda h, i: (h, 0, i)),
                  pl.BlockSpec((1, s, MLA_QK_PAD), lambda h, i: (h, 0, 0)),
                  pl.BlockSpec((1, MLA_V_DIM, s), lambda h, i: (h, 0, 0))],
        out_specs=pl.BlockSpec((t, MLA_V_DIM), lambda h, i: (i, h)),
        out_shape=jax.ShapeDtypeStruct((s, MLA_WIDTH), BF16),
        scratch_shapes=[pltpu.VMEM((t, t), F32), pltpu.VMEM((t, t), F32),
                        pltpu.VMEM((1, 1, t), F32), pltpu.VMEM((1, 1, t), F32),
                        pltpu.VMEM((1, MLA_V_DIM, t), F32)],
        compiler_params=_cp(("arbitrary", "arbitrary")),
        name="mla_attention",
    )(q_t, k, v_t)


def _qk_nt(q, k_blk):
    return lax.dot_general(q, k_blk, (((1,), (1,)), ((), ())), preferred_element_type=F32)


def _mem_attn_kernel(q_ref, k_ref, v_ref, qg_ref, o_ref):
    shift = R_MQ % MEM_WINDOW
    qall = q_ref[...].astype(F32)[:, shift:shift + MEM_WIDTH]
    for h in range(MEM_HEADS):
        lo = h * MEM_HEAD_DIM
        qh = qall[:, lo:lo + MEM_HEAD_DIM]
        ms = jnp.mean(qh * qh, axis=-1, keepdims=True)
        qn = (qh * lax.rsqrt(ms + NORM_EPS) * qg_ref[...]).astype(BF16)
        s = _qk_nt(qn, k_ref[:, lo:lo + MEM_HEAD_DIM])
        p = jnp.exp(s - jnp.max(s, axis=-1, keepdims=True))
        l = jnp.sum(p, axis=-1, keepdims=True)
        o = jnp.dot(p.astype(BF16), v_ref[:, lo:lo + MEM_HEAD_DIM], preferred_element_type=F32)
        o_ref[:, lo:lo + MEM_HEAD_DIM] = (o / l).astype(o_ref.dtype)


def _mem_attention(rest, k_mem, v_mem, qg_scaled, tm):
    s = rest.shape[0]
    n_mem = k_mem.shape[0]
    assert R_MQ % MEM_WINDOW + MEM_WIDTH <= MEM_WINDOW
    return pl.pallas_call(
        _mem_attn_kernel,
        grid=(s // tm,),
        in_specs=[pl.BlockSpec((tm, MEM_WINDOW), lambda i: (i, R_MQ // MEM_WINDOW)),
                  pl.BlockSpec((n_mem, MEM_WIDTH), lambda i: (0, 0)),
                  pl.BlockSpec((n_mem, MEM_WIDTH), lambda i: (0, 0)),
                  pl.BlockSpec((1, MEM_HEAD_DIM), lambda i: (0, 0))],
        out_specs=pl.BlockSpec((tm, MEM_WIDTH), lambda i: (i, 0)),
        out_shape=jax.ShapeDtypeStruct((s, MEM_WIDTH), BF16),
        compiler_params=_cp(("parallel",)),
        name="mem_attention",
    )(rest, k_mem, v_mem, qg_scaled)


def _mix_kernel(od_ref, om_ref, oc_ref, wd_ref, wm_ref, wc_ref, g0_ref, g1_ref, g2_ref, o_ref):
    yd = jnp.dot(od_ref[...], wd_ref[...], preferred_element_type=F32)
    ym = jnp.dot(om_ref[...], wm_ref[...], preferred_element_type=F32)
    yc = jnp.dot(oc_ref[...], wc_ref[...], preferred_element_type=F32)
    mixed = (g0_ref[...].astype(F32) * yd + g1_ref[...].astype(F32) * ym) + g2_ref[...].astype(F32) * yc
    o_ref[...] = mixed.astype(o_ref.dtype)


def _mix(o_diff, o_mla, o_mem, w_d, w_m, w_c, gates, tm, tn):
    s = o_diff.shape[0]
    d = w_d.shape[1]
    nj = d // tn
    return pl.pallas_call(
        _mix_kernel,
        grid=(s // tm, nj),
        in_specs=[pl.BlockSpec((tm, o_diff.shape[1]), lambda i, j: (i, 0)),
                  pl.BlockSpec((tm, o_mla.shape[1]), lambda i, j: (i, 0)),
                  pl.BlockSpec((tm, o_mem.shape[1]), lambda i, j: (i, 0)),
                  pl.BlockSpec((w_d.shape[0], tn), lambda i, j: (0, j)),
                  pl.BlockSpec((w_m.shape[0], tn), lambda i, j: (0, j)),
                  pl.BlockSpec((w_c.shape[0], tn), lambda i, j: (0, j)),
                  pl.BlockSpec((tm, tn), lambda i, j: (i, j)),
                  pl.BlockSpec((tm, tn), lambda i, j: (i, nj + j)),
                  pl.BlockSpec((tm, tn), lambda i, j: (i, 2 * nj + j))],
        out_specs=pl.BlockSpec((tm, tn), lambda i, j: (i, j)),
        out_shape=jax.ShapeDtypeStruct((s, d), BF16),
        compiler_params=_cp(("arbitrary", "arbitrary")),
        name="gated_mix",
    )(o_diff, o_mla, o_mem, w_d, w_m, w_c, gates, gates, gates)


def _router_kernel(x_ref, g_ref, w_ref, b_ref, h_ref, r_ref):
    x = x_ref[...]
    ms = jnp.mean(x * x, axis=-1, keepdims=True)
    h = x * lax.rsqrt(ms + NORM_EPS) * g_ref[...]
    h_ref[...] = _pack_bf16_pairs(h)
    logits = jnp.dot(h, w_ref[...], preferred_element_type=F32, precision=lax.Precision.HIGHEST) + b_ref[...]
    lane = lax.broadcasted_iota(jnp.int32, logits.shape, 1)
    lane_f = lane.astype(F32)
    big = float(4 * ROUTE_W)
    lg = jnp.where(lane < N_GROUPS, logits, -jnp.inf)
    gmax = jnp.max(lg, axis=-1, keepdims=True)
    gidx = jnp.min(jnp.where(lg == gmax, lane_f, big), axis=-1, keepdims=True)
    pg_top = 1.0 / jnp.sum(jnp.exp(lg - gmax), axis=-1, keepdims=True)
    e_lane = lane - N_GROUPS
    lane_group = jnp.right_shift(e_lane, 3).astype(F32)
    in_group = (e_lane >= 0) & (e_lane < N_EXPERTS) & (lane_group == gidx)
    le = jnp.where(in_group, logits, -jnp.inf)
    e1 = jnp.max(le, axis=-1, keepdims=True)
    i1 = jnp.min(jnp.where(le == e1, lane_f, big), axis=-1, keepdims=True)
    le2 = jnp.where(lane_f == i1, -jnp.inf, le)
    e2 = jnp.max(le2, axis=-1, keepdims=True)
    i2 = jnp.min(jnp.where(le2 == e2, lane_f, big), axis=-1, keepdims=True)
    w2 = jnp.exp(e2 - e1)
    inv = 1.0 / (1.0 + w2)
    gate1 = pg_top * inv
    gate2 = pg_top * (w2 * inv)
    out = jnp.where(lane == 0, i1 - N_GROUPS,
                    jnp.where(lane == 1, i2 - N_GROUPS,
                              jnp.where(lane == 2, gate1, jnp.where(lane == 3, gate2, 0.0))))
    r_ref[...] = out


def _router(x1, g, w_r, b_r, tm):
    s, d = x1.shape
    return pl.pallas_call(
        _router_kernel,
        grid=(s // tm,),
        in_specs=[pl.BlockSpec((tm, d), lambda i: (i, 0)),
                  pl.BlockSpec((1, d), lambda i: (0, 0)),
                  pl.BlockSpec((d, ROUTE_W), lambda i: (0, 0)),
                  pl.BlockSpec((1, ROUTE_W), lambda i: (0, 0))],
        out_specs=[pl.BlockSpec((tm, d // 2), lambda i: (i, 0)),
                   pl.BlockSpec((tm, ROUTE_W), lambda i: (i, 0))],
        out_shape=[jax.ShapeDtypeStruct((s, d // 2), jnp.uint32),
                   jax.ShapeDtypeStruct((s, ROUTE_W), F32)],
        compiler_params=_cp(("parallel",)),
        name="ffn_norm_router",
    )(x1, g, w_r, b_r)


def _pack_bf16_pairs(v):
    n = v.shape[1] // 2
    bits = lax.bitcast_convert_type(v.astype(BF16).astype(F32), jnp.uint32)
    return jnp.right_shift(bits[:, :n], jnp.uint32(16)) | bits[:, n:]


def _unpack_bf16_pairs(words):
    lo = lax.bitcast_convert_type(jnp.left_shift(words, jnp.uint32(16)), F32)
    hi = lax.bitcast_convert_type(words & jnp.uint32(0xFFFF0000), F32)
    return lo, hi


def _moe_kernel(be_ref, nr_ref, nu_ref, tok_ref, tokn_ref, h_ref, wg_ref, wu_ref, wd_ref, o_ref,
                xg_ref, xb_ref, a_ref, wgb_ref, wub_ref, wdb_ref, sem):
    b = pl.program_id(0)
    c = pl.program_id(1)
    nb = pl.num_programs(0)
    nr = nr_ref[b]
    half = xg_ref.shape[2]
    fh = wgb_ref.shape[1]

    def row_copy(src_row, slot, r):
        return pltpu.make_async_copy(h_ref.at[pl.ds(src_row, 1), :], xg_ref.at[slot, pl.ds(r, 1), :], sem.at[slot])

    def start_gather(tok, n_rows, slot):
        unroll = 8

        def issue(i, carry):
            for j in range(unroll):
                r = i * unroll + j
                row_copy(tok[0, 0, r], slot, r).start(priority=1)
            return carry
        lax.fori_loop(0, n_rows // unroll, issue, 0)

    def for_row_count(fn):
        for units in range(1, MOE_TB // MOE_RU + 1):
            @pl.when(nr == units)
            def _():
                fn(units * MOE_RU)

    @pl.when(c == 0)
    def _():
        slot = b % 2

        @pl.when(b == 0)
        def _():
            start_gather(tok_ref, nr * MOE_RU, 0)

        def drain(u, carry):
            pltpu.make_async_copy(h_ref.at[pl.ds(0, MOE_RU), :], xg_ref.at[slot, pl.ds(0, MOE_RU), :],
                                  sem.at[slot]).wait()
            return carry
        lax.fori_loop(0, nr, drain, 0)

        @pl.when(b + 1 < nb)
        def _():
            start_gather(tokn_ref, nr_ref[jnp.minimum(b + 1, nb - 1)] * MOE_RU, 1 - slot)

        def unpack(u, carry):
            r0 = pl.multiple_of(u * MOE_RU, MOE_RU)
            lo, hi = _unpack_bf16_pairs(xg_ref[slot, pl.ds(r0, MOE_RU), :])
            xb_ref[pl.ds(r0, MOE_RU), :half] = lo.astype(BF16)
            xb_ref[pl.ds(r0, MOE_RU), half:] = hi.astype(BF16)
            return carry
        lax.fori_loop(0, nr, unpack, 0)

    @pl.when((c < 2) & (nr > 0))
    def _():
        wgb_ref[...] = wg_ref[0].astype(BF16)
        wub_ref[...] = wu_ref[0].astype(BF16)

        def gate_up(m):
            x = xb_ref[:m, :]
            g = jnp.dot(x, wgb_ref[...], preferred_element_type=F32)
            u = jnp.dot(x, wub_ref[...], preferred_element_type=F32)
            a_ref[c, :m, :] = ((g * jax.nn.sigmoid(g)) * u).astype(BF16)

        for_row_count(gate_up)

    @pl.when((c >= 2) & (nr > 0))
    def _():
        wdb_ref[...] = wd_ref[0].astype(BF16)

        def down(m):
            yv = (jnp.dot(a_ref[0, :m, :], wdb_ref[:fh, :], preferred_element_type=F32)
                  + jnp.dot(a_ref[1, :m, :], wdb_ref[fh:, :], preferred_element_type=F32))
            o_ref[:m, :] = _pack_bf16_pairs(yv)
            if m < MOE_TB:
                o_ref[m:, :] = jnp.zeros((MOE_TB - m, o_ref.shape[1]), jnp.uint32)

        for_row_count(down)

    @pl.when((c >= 2) & (nr == 0))
    def _():
        o_ref[...] = jnp.zeros(o_ref.shape, jnp.uint32)


def _moe_experts(block_e, nsub, n_used, buf_tok, h2p, w_gate, w_up, w_down):
    n_blocks = block_e.shape[0]
    half = h2p.shape[1]
    d = 2 * half
    ff = w_gate.shape[2]
    fh = ff // 2
    dh = d // 2
    tok = buf_tok.reshape(n_blocks, 1, MOE_TB)

    def gate_up_idx(b, c, be, nr, nu):
        live = b < nu[0]
        return be[jnp.minimum(b, nu[0] - 1)], 0, jnp.where(live, jnp.minimum(c, 1), 1)

    def down_idx(b, c, be, nr, nu):
        bb = jnp.minimum(b, nu[0] - 1)
        cc = jnp.where(b < nu[0], c, MOE_PH - 1)
        e = jnp.where(cc >= 2, be[bb], be[jnp.maximum(bb - 1, 0)])
        return e, 0, jnp.where(cc == 2, 0, 1)

    def out_idx(b, c, be, nr, nu):
        ob = jnp.where(c >= 2, b, jnp.maximum(b - 1, 0))
        oc = jnp.where(c >= 2, c - 2, jnp.where(b > 0, 1, 0))
        return ob, oc

    grid_spec = pltpu.PrefetchScalarGridSpec(
        num_scalar_prefetch=3,
        grid=(n_blocks, MOE_PH),
        in_specs=[pl.BlockSpec((1, 1, MOE_TB), lambda b, c, be, nr, nu: (b, 0, 0), memory_space=pltpu.SMEM),
                  pl.BlockSpec((1, 1, MOE_TB), lambda b, c, be, nr, nu: (jnp.minimum(b + 1, n_blocks - 1), 0, 0),
                               memory_space=pltpu.SMEM),
                  pl.BlockSpec(memory_space=pl.ANY),
                  pl.BlockSpec((1, d, fh), gate_up_idx),
                  pl.BlockSpec((1, d, fh), gate_up_idx),
                  pl.BlockSpec((1, ff, dh), down_idx)],
        out_specs=pl.BlockSpec((MOE_TB, dh // 2), out_idx),
        scratch_shapes=[pltpu.VMEM((2, MOE_TB, half), jnp.uint32), pltpu.VMEM((MOE_TB, d), BF16),
                        pltpu.VMEM((2, MOE_TB, fh), BF16),
                        pltpu.VMEM((d, fh), BF16), pltpu.VMEM((d, fh), BF16), pltpu.VMEM((ff, dh), BF16),
                        pltpu.SemaphoreType.DMA((2,))],
    )
    return pl.pallas_call(
        _moe_kernel,
        grid_spec=grid_spec,
        out_shape=jax.ShapeDtypeStruct((n_blocks * MOE_TB, half), jnp.uint32),
        compiler_params=_cp(("arbitrary", "arbitrary")),
        name="moe_experts",
    )(block_e, nsub, n_used, tok, tok, h2p, w_gate, w_up, w_down)


def _combine_kernel(slot_ref, r_ref, x_ref, yb_ref, o_ref, g_ref, sem):
    tm = x_ref.shape[0]
    n = g_ref.shape[2]

    unroll = 4

    def issue(i, c):
        for j in range(unroll):
            r = i * unroll + j
            for k in range(TOP_K):
                sl = slot_ref[0, 0, r * TOP_K + k]
                pltpu.make_async_copy(yb_ref.at[pl.ds(sl, 1), :], g_ref.at[k, pl.ds(r, 1), :],
                                      sem.at[k]).start(priority=k)
        return c

    lax.fori_loop(0, tm // unroll, issue, 0)
    for k in range(TOP_K):
        pltpu.make_async_copy(yb_ref.at[pl.ds(0, tm), :], g_ref.at[k], sem.at[k]).wait()

    route = r_ref[...]
    w0 = route[:, TOP_K:TOP_K + 1]
    w1 = route[:, TOP_K + 1:TOP_K + 2]
    q = n // 2
    for hf in range(2):
        lo0, hi0 = _unpack_bf16_pairs(g_ref[0, :, hf * q:(hf + 1) * q])
        lo1, hi1 = _unpack_bf16_pairs(g_ref[1, :, hf * q:(hf + 1) * q])
        c0 = hf * n
        o_ref[:, c0:c0 + q] = x_ref[:, c0:c0 + q] + (lo0 * w0 + lo1 * w1)
        o_ref[:, c0 + q:c0 + n] = x_ref[:, c0 + q:c0 + n] + (hi0 * w0 + hi1 * w1)


def _combine(slots, route, x1, yb, tm):
    s, d = x1.shape
    return pl.pallas_call(
        _combine_kernel,
        grid=(s // tm,),
        in_specs=[pl.BlockSpec((1, 1, tm * TOP_K), lambda i: (i, 0, 0), memory_space=pltpu.SMEM),
                  pl.BlockSpec((tm, ROUTE_W), lambda i: (i, 0)),
                  pl.BlockSpec((tm, d), lambda i: (i, 0)),
                  pl.BlockSpec(memory_space=pl.ANY)],
        out_specs=pl.BlockSpec((tm, d), lambda i: (i, 0)),
        out_shape=jax.ShapeDtypeStruct((s, d), F32),
        scratch_shapes=[pltpu.VMEM((TOP_K, tm, d // 2), jnp.uint32), pltpu.SemaphoreType.DMA((TOP_K,))],
        compiler_params=_cp(("arbitrary",)),
        name="moe_combine",
    )(slots.reshape(s // tm, 1, tm * TOP_K), route, x1, yb)


def _dispatch_plan(route, s):
    a = s * TOP_K
    flat_e = route[:, :TOP_K].astype(jnp.int32).reshape(a)
    onehot = (flat_e[:, None] == jnp.arange(N_EXPERTS, dtype=jnp.int32)[None, :]).astype(jnp.int32)
    csum = jnp.cumsum(onehot, axis=0)
    counts = csum[-1]
    rank = jnp.sum((csum - onehot) * onehot, axis=1)
    padded = (counts + MOE_TB - 1) // MOE_TB * MOE_TB
    pad_end = jnp.cumsum(padded)
    pad_start = pad_end - padded
    dest = pad_start[flat_e] + rank
    n_blocks = a // MOE_TB + N_EXPERTS
    p_rows = n_blocks * MOE_TB
    buf_tok = jnp.zeros((p_rows,), jnp.int32).at[dest].set(jnp.arange(a, dtype=jnp.int32) // TOP_K)
    starts = jnp.arange(n_blocks, dtype=jnp.int32) * MOE_TB
    block_e = jnp.minimum(jnp.searchsorted(pad_end, starts, side='right'), N_EXPERTS - 1).astype(jnp.int32)
    valid = jnp.clip(counts[block_e] - (starts - pad_start[block_e]), 0, MOE_TB)
    valid = jnp.where(starts < pad_end[-1], valid, 0)
    nsub = ((valid + MOE_RU - 1) // MOE_RU).astype(jnp.int32)
    n_used = (pad_end[-1:] // MOE_TB).astype(jnp.int32)
    return block_e, nsub, n_used, buf_tok, dest.astype(jnp.int32)


def _rope_tables(positions):
    half = MLA_ROPE_DIM // 2
    inv_freq = ROPE_THETA ** (-jnp.arange(half, dtype=F32) / half)
    ang = positions.astype(F32)[:, None] * inv_freq[None, :]
    cos, sin = jnp.cos(ang), jnp.sin(ang)
    z = jnp.zeros_like(cos)
    c = jnp.concatenate([cos, cos, z, z], axis=-1)
    s1 = jnp.concatenate([-sin, z, z, z], axis=-1)
    s2 = jnp.concatenate([z, sin, z, z], axis=-1)
    return c, s1, s2


def kernel(x, mem, positions, rel_bias, mix_norm_g, w_in, diff_q_norm_g, diff_k_norm_g, diff_lambda_q1, diff_lambda_k1, diff_lambda_q2, diff_lambda_k2, diff_subln_g, mla_cq_norm_g, mla_ckv_norm_g, mla_w_uq, mla_w_ukv, mla_q_norm_g, mla_k_norm_g, mem_norm_g, mem_w_kv, mem_q_norm_g, mem_k_norm_g, w_o_diff, w_o_mla, w_o_mem, w_out, ffn_norm_g, w_route_group, b_route_group, w_route_expert, b_route_expert, w_exp_gate, w_exp_up, w_exp_down):
    b, s, d = x.shape
    assert b == 1 and s % ATT_T == 0
    depth = mix_norm_g.shape[0]
    xs = x.reshape(s, d)
    pos = positions.reshape(s)
    rope_tabs = _rope_tables(pos)
    row = lambda v: v.reshape(1, -1).astype(F32)

    for l in range(depth):
        lam_init = 0.8 - 0.6 * math.exp(-0.3 * l)
        h = _rmsnorm_rows(xs, mix_norm_g[l], 256)
        tn = 512
        q_gain = jnp.tile(diff_q_norm_g[l] * (DIFF_HEAD_DIM ** -0.5 * LOG2E), DIFF_MAPS)
        w_in_t = jnp.transpose(w_in[l])
        dq_t = _matmul_nt(h, w_in_t, row0=OFF_DQ, n_cols=DIFF_QK_WIDTH, tm=MM_TM, tn=tn, out_dtype=BF16,
                          mode="groupnorm", extra=q_gain, group=DIFF_HEAD_DIM, transpose_out=True,
                          name="diff_q_proj")
        dk = _matmul_nt(h, w_in_t, row0=OFF_DK, n_cols=DIFF_QK_WIDTH, tm=MM_TM, tn=tn, out_dtype=BF16,
                        mode="groupnorm", extra=jnp.tile(diff_k_norm_g[l], DIFF_MAPS), group=DIFF_HEAD_DIM,
                        name="diff_k_proj")
        dv_t = _matmul_nt(h, w_in_t, row0=OFF_DV, n_cols=DIFF_WIDTH, tm=MM_TM, tn=tn, out_dtype=BF16,
                          transpose_out=True, name="diff_v_proj")
        rest = _matmul_nt(h, w_in_t, row0=OFF_CQ, n_cols=REST_WIDTH, tm=MM_TM, tn=tn, out_dtype=BF16,
                          name="rest_proj")
        gates = _matmul_nt(h, w_in_t, row0=OFF_GATES, n_cols=3 * d, tm=MM_TM, tn=tn, out_dtype=BF16,
                           mode="sigmoid", name="gate_proj")

        bias_tiles = _diff_bias_tiles(rel_bias, ATT_T)
        lam_vecs = [row(diff_lambda_q1[l]), row(diff_lambda_k1[l]), row(diff_lambda_q2[l]), row(diff_lambda_k2[l])]
        o_diff = _diff_attention(dq_t, dk, dv_t, bias_tiles, lam_vecs, diff_subln_g[l], lam_init)

        w_uq_heads = jnp.pad(
            mla_w_uq[l].reshape(MLA_Q_RANK, MLA_HEADS, MLA_QK_DIM),
            ((0, 0), (0, 0), (0, MLA_QK_PAD - MLA_QK_DIM))).reshape(MLA_Q_RANK, MLA_HEADS * MLA_QK_PAD).astype(BF16)
        qg_pad = jnp.pad(mla_q_norm_g[l] * (MLA_QK_DIM ** -0.5 * LOG2E),
                         (0, MLA_QK_PAD - MLA_QK_DIM)).reshape(1, -1).astype(F32)
        q_mla_t = _mla_q_prep(rest, row(mla_cq_norm_g[l]), w_uq_heads, qg_pad, rope_tabs, 512)
        kg = mla_k_norm_g[l]
        kg_nope = row(kg[:MLA_NOPE_DIM])
        kg_rope = jnp.pad(kg[MLA_NOPE_DIM:], (0, LANE - MLA_ROPE_DIM)).reshape(1, -1).astype(F32)
        k_mla, v_mla_t = _mla_kv_prep(rest, row(mla_ckv_norm_g[l]), mla_w_ukv[l], kg_nope, kg_rope, rope_tabs, 512)
        o_mla = _mla_attention(q_mla_t, k_mla, v_mla_t)

        n_mem = mem.shape[1]
        mem_h = _rmsnorm_rows(mem.reshape(n_mem, d), mem_norm_g[l], n_mem)
        k_mem = _matmul(mem_h, mem_w_kv[l], col0=0, n_cols=MEM_WIDTH, tm=n_mem, tn=512, out_dtype=BF16,
                        mode="groupnorm", extra=jnp.tile(mem_k_norm_g[l], MEM_HEADS), group=MEM_HEAD_DIM,
                        name="mem_k_proj")
        v_mem = _matmul(mem_h, mem_w_kv[l], col0=MEM_WIDTH, n_cols=MEM_WIDTH, tm=n_mem, tn=512, out_dtype=BF16,
                        name="mem_v_proj")
        o_mem = _mem_attention(rest, k_mem, v_mem, row(mem_q_norm_g[l] * MEM_HEAD_DIM ** -0.5), 512)

        mixed = _mix(o_diff, o_mla, o_mem, w_o_diff[l].astype(BF16), w_o_mla[l].astype(BF16),
                     w_o_mem[l].astype(BF16), gates, MM_TM, 512)
        x1 = _matmul(mixed, w_out[l], n_cols=d, tm=MM_TM, tn=512, out_dtype=F32, mode="residual", extra=xs,
                     name="out_proj")

        w_r = jnp.pad(jnp.concatenate([w_route_group[l], w_route_expert[l]], axis=1),
                      ((0, 0), (0, ROUTE_W - N_GROUPS - N_EXPERTS))).astype(F32)
        b_r = jnp.pad(jnp.concatenate([b_route_group[l], b_route_expert[l]]),
                      (0, ROUTE_W - N_GROUPS - N_EXPERTS)).reshape(1, -1).astype(F32)
        h2, route = _router(x1, row(ffn_norm_g[l]), w_r, b_r, 256)
        block_e, nsub, n_used, buf_tok, slots = _dispatch_plan(route, s)
        yb = _moe_experts(block_e, nsub, n_used, buf_tok, h2, w_exp_gate[l], w_exp_up[l], w_exp_down[l])
        xs = _combine(slots, route, x1, yb, 256)
    return xs.reshape(b, s, d)
```

```python
import functools
import math

import jax
import jax.numpy as jnp
from jax import lax
from jax.experimental import pallas as pl
from jax.experimental.pallas import tpu as pltpu

F32 = jnp.float32
BF16 = jnp.bfloat16

NORM_EPS = 1e-6
NEG_INF = -1e30
LOG2E = math.log2(math.e)

DIFF_HEADS = 6
DIFF_HEAD_DIM = 128
DIFF_V_DIM = 256
DIFF_MAPS = 12
DIFF_QK_WIDTH = 1536
DIFF_WIDTH = 1536
MLA_HEADS = 12
MLA_Q_RANK = 1536
MLA_KV_RANK = 512
MLA_NOPE_DIM = 128
MLA_ROPE_DIM = 64
MLA_QK_DIM = 192
MLA_QK_PAD = 256
MLA_V_DIM = 128
MLA_WIDTH = 1536
ROPE_THETA = 10000.0
MEM_HEADS = 4
MEM_HEAD_DIM = 256
MEM_WIDTH = 1024
REL_BUCKETS = 32
REL_MAX_DIST = 128
N_GROUPS = 8
EXPERTS_PER_GROUP = 8
N_EXPERTS = 64
TOP_K = 2
EXPERT_FF = 512

OFF_DQ = 0
OFF_DK = 1536
OFF_DV = 3072
OFF_CQ = 4608
OFF_CKV = 6144
OFF_KROPE = 6656
OFF_MQ = 6720
OFF_GATES = 7744
REST_WIDTH = 4096
R_CQ = OFF_CQ - OFF_CQ
R_CKV = OFF_CKV - OFF_CQ
R_KROPE = OFF_KROPE - OFF_CQ
R_MQ = OFF_MQ - OFF_CQ

LANE = 128
VMEM_LIMIT = 52 * 1024 * 1024

MM_TM = 1024
MLA_HEAD_GROUP = 4
ATT_T = 512
MOE_TB = 512
MOE_RU = 128
MOE_PH = 4
ROUTE_W = 128
MEM_WINDOW = 2048


def _cp(sem, vmem=VMEM_LIMIT):
    return pltpu.CompilerParams(dimension_semantics=sem, vmem_limit_bytes=vmem)


def _rmsnorm_kernel(x_ref, g_ref, o_ref):
    x = x_ref[...].astype(F32)
    ms = jnp.mean(x * x, axis=-1, keepdims=True)
    o_ref[...] = (x * lax.rsqrt(ms + NORM_EPS) * g_ref[...]).astype(o_ref.dtype)


def _rmsnorm_rows(x, g, tm, out_dtype=BF16):
    m, d = x.shape
    return pl.pallas_call(
        _rmsnorm_kernel,
        grid=(m // tm,),
        in_specs=[pl.BlockSpec((tm, d), lambda i: (i, 0)),
                  pl.BlockSpec((1, d), lambda i: (0, 0))],
        out_specs=pl.BlockSpec((tm, d), lambda i: (i, 0)),
        out_shape=jax.ShapeDtypeStruct((m, d), out_dtype),
        compiler_params=_cp(("parallel",)),
        name="rmsnorm_rows",
    )(x, g.reshape(1, d).astype(F32))


def _cast_shifted(w_ref, w2_ref, wb_ref, shift):
    k = w_ref.shape[0]
    rows = 256

    def body(c, carry):
        r0 = pl.multiple_of(c * rows, rows)
        main = w_ref[pl.ds(r0, rows), :]
        tail = w2_ref[pl.ds(r0, rows), :]
        wb_ref[pl.ds(r0, rows), :] = jnp.concatenate([main[:, shift:], tail[:, :shift]], axis=1).astype(BF16)
        return carry

    lax.fori_loop(0, k // rows, body, 0)


def _mm_kernel(*refs, mode, cast, group, shift, transpose_out):
    a_ref, w_ref = refs[0], refs[1]
    pos = 2
    w2_ref = None
    if shift:
        w2_ref = refs[pos]
        pos += 1
    extra = None
    if mode in ("groupnorm", "residual"):
        extra = refs[pos]
        pos += 1
    o_ref = refs[pos]
    wb_ref = refs[pos + 1] if cast else None
    i = pl.program_id(1)
    if cast:
        @pl.when(i == 0)
        def _():
            if shift:
                _cast_shifted(w_ref, w2_ref, wb_ref, shift)
            else:
                wb_ref[...] = w_ref[...].astype(BF16)
        w = wb_ref[...]
    else:
        w = w_ref[...]
    acc = jnp.dot(a_ref[...], w, preferred_element_type=F32)
    if transpose_out:
        acc = acc.T
    if mode == "plain":
        o_ref[...] = acc.astype(o_ref.dtype)
    elif mode == "sigmoid":
        o_ref[...] = jax.nn.sigmoid(acc).astype(o_ref.dtype)
    elif mode == "residual":
        o_ref[...] = (extra[...] + acc).astype(o_ref.dtype)
    elif mode == "groupnorm":
        tn = w.shape[1]
        for c in range(tn // group):
            sl = slice(c * group, (c + 1) * group)
            if transpose_out:
                blk = acc[sl, :]
                ms = jnp.mean(blk * blk, axis=0, keepdims=True)
                o_ref[sl, :] = (blk * lax.rsqrt(ms + NORM_EPS) * extra[sl, :]).astype(o_ref.dtype)
            else:
                blk = acc[:, sl]
                ms = jnp.mean(blk * blk, axis=-1, keepdims=True)
                o_ref[:, sl] = (blk * lax.rsqrt(ms + NORM_EPS) * extra[:, sl]).astype(o_ref.dtype)


def _matmul(a, w, *, n_cols, tm, tn, out_dtype, col0=0, mode="plain", extra=None, group=LANE,
            transpose_out=False, name="matmul"):
    m, k = a.shape
    assert m % tm == 0 and n_cols % tn == 0 and w.shape[0] == k
    cast = w.dtype != BF16
    base, shift = divmod(col0, tn)
    assert shift <= LANE and (shift == 0 or cast)
    in_specs = [pl.BlockSpec((tm, k), lambda j, i: (i, 0)),
                pl.BlockSpec((k, tn), lambda j, i: (0, base + j))]
    args = [a, w]
    if shift:
        in_specs.append(pl.BlockSpec((k, LANE), lambda j, i: (0, (base + j + 1) * (tn // LANE))))
        args.append(w)
    if mode == "groupnorm":
        gain = extra.reshape(-1, 1) if transpose_out else extra.reshape(1, -1)
        in_specs.append(pl.BlockSpec((tn, 1), lambda j, i: (j, 0)) if transpose_out
                        else pl.BlockSpec((1, tn), lambda j, i: (0, j)))
        args.append(gain.astype(F32))
    elif mode == "residual":
        assert not transpose_out
        in_specs.append(pl.BlockSpec((tm, tn), lambda j, i: (i, j)))
        args.append(extra)
    if transpose_out:
        out_spec = pl.BlockSpec((tn, tm), lambda j, i: (j, i))
        out_shape = jax.ShapeDtypeStruct((n_cols, m), out_dtype)
    else:
        out_spec = pl.BlockSpec((tm, tn), lambda j, i: (i, j))
        out_shape = jax.ShapeDtypeStruct((m, n_cols), out_dtype)
    scratch = [pltpu.VMEM((k, tn), BF16)] if cast else []
    return pl.pallas_call(
        functools.partial(_mm_kernel, mode=mode, cast=cast, group=group, shift=shift, transpose_out=transpose_out),
        grid=(n_cols // tn, m // tm),
        in_specs=in_specs,
        out_specs=out_spec,
        out_shape=out_shape,
        scratch_shapes=scratch,
        compiler_params=_cp(("arbitrary", "arbitrary")),
        name=name,
    )(*args)


def _mm_nt_kernel(*refs, mode, group, transpose_out):
    a_ref, wt_ref = refs[0], refs[1]
    extra = refs[2] if mode == "groupnorm" else None
    o_ref, wb_ref = refs[-2], refs[-1]
    i = pl.program_id(1)

    @pl.when(i == 0)
    def _():
        wb_ref[...] = wt_ref[...].astype(BF16)

    contract_last = (((1,), (1,)), ((), ()))
    if transpose_out:
        acc = lax.dot_general(wb_ref[...], a_ref[...], contract_last, preferred_element_type=F32)
    else:
        acc = lax.dot_general(a_ref[...], wb_ref[...], contract_last, preferred_element_type=F32)
    if mode == "plain":
        o_ref[...] = acc.astype(o_ref.dtype)
    elif mode == "sigmoid":
        o_ref[...] = jax.nn.sigmoid(acc).astype(o_ref.dtype)
    elif mode == "groupnorm":
        tn = wb_ref.shape[0]
        for c in range(tn // group):
            sl = slice(c * group, (c + 1) * group)
            if transpose_out:
                blk = acc[sl, :]
                ms = jnp.mean(blk * blk, axis=0, keepdims=True)
                o_ref[sl, :] = (blk * lax.rsqrt(ms + NORM_EPS) * extra[sl, :]).astype(o_ref.dtype)
            else:
                blk = acc[:, sl]
                ms = jnp.mean(blk * blk, axis=-1, keepdims=True)
                o_ref[:, sl] = (blk * lax.rsqrt(ms + NORM_EPS) * extra[:, sl]).astype(o_ref.dtype)


def _matmul_nt(a, wt, *, row0, n_cols, tm, tn, out_dtype, mode="plain", extra=None, group=LANE,
               transpose_out=False, name="matmul_nt"):
    m, k = a.shape
    assert m % tm == 0 and n_cols % tn == 0 and wt.shape[1] == k and row0 % 8 == 0
    in_specs = [pl.BlockSpec((tm, k), lambda j, i: (i, 0)),
                pl.BlockSpec((pl.Element(tn), pl.Element(k)), lambda j, i: (pl.multiple_of(row0 + j * tn, 8), 0))]
    args = [a, wt]
    if mode == "groupnorm":
        gain = extra.reshape(-1, 1) if transpose_out else extra.reshape(1, -1)
        in_specs.append(pl.BlockSpec((tn, 1), lambda j, i: (j, 0)) if transpose_out
                        else pl.BlockSpec((1, tn), lambda j, i: (0, j)))
        args.append(gain.astype(F32))
    if transpose_out:
        out_spec = pl.BlockSpec((tn, tm), lambda j, i: (j, i))
        out_shape = jax.ShapeDtypeStruct((n_cols, m), out_dtype)
    else:
        out_spec = pl.BlockSpec((tm, tn), lambda j, i: (i, j))
        out_shape = jax.ShapeDtypeStruct((m, n_cols), out_dtype)
    return pl.pallas_call(
        functools.partial(_mm_nt_kernel, mode=mode, group=group, transpose_out=transpose_out),
        grid=(n_cols // tn, m // tm),
        in_specs=in_specs,
        out_specs=out_spec,
        out_shape=out_shape,
        scratch_shapes=[pltpu.VMEM((tn, k), BF16)],
        compiler_params=_cp(("arbitrary", "arbitrary")),
        name=name,
    )(*args)


def _softmax_pv(idx, s, vt_blk, m_ref, l_ref, acc_ref):
    m_prev = m_ref[idx]
    m_new = jnp.maximum(m_prev, jnp.max(s, axis=0, keepdims=True))
    alpha = jnp.exp2(m_prev - m_new)
    p = jnp.exp2(s - m_new)
    l_ref[idx] = alpha * l_ref[idx] + jnp.sum(p, axis=0, keepdims=True)
    acc_ref[idx] = alpha * acc_ref[idx] + jnp.dot(vt_blk, p.astype(BF16), preferred_element_type=F32)
    m_ref[idx] = m_new


def _block_offset(kb, t):
    return kb * t if isinstance(kb, int) else pl.multiple_of(kb * t, t)


def _init_stats(m_ref, l_ref, acc_ref):
    m_ref[...] = jnp.full(m_ref.shape, NEG_INF, F32)
    l_ref[...] = jnp.zeros(l_ref.shape, F32)
    acc_ref[...] = jnp.zeros(acc_ref.shape, F32)


def _diff_attn_kernel(lq1_ref, lk1_ref, lq2_ref, lk2_ref, q1_ref, q2_ref, k1_ref, k2_ref, vt_ref,
                      b1_ref, b2_ref, g_ref, o_ref, sa_ref, sb_ref, m_ref, l_ref, acc_ref, *, lam_init):
    _, tk, tq = sa_ref.shape
    i = pl.program_id(1)
    qts = (q1_ref[...], q2_ref[...])
    ks = (k1_ref, k2_ref)
    bs = (b1_ref, b2_ref)
    _init_stats(m_ref, l_ref, acc_ref)

    def scores(kb, dst):
        off = _block_offset(kb, tk)
        for mp in range(2):
            dst[mp] = jnp.dot(ks[mp][pl.ds(off, tk), :], qts[mp], preferred_element_type=F32)

    def consume(kb, src, strip_off):
        off = _block_offset(kb, tk)
        vt_blk = vt_ref[:, pl.ds(off, tk)]
        for mp in range(2):
            s = src[mp]
            if strip_off is not None:
                s = s + bs[mp][0, :, strip_off:strip_off + tq]
            _softmax_pv(mp, s, vt_blk, m_ref, l_ref, acc_ref)

    scores(0, sa_ref)

    def far_pair(j, c):
        kb = 2 * j
        scores(kb + 1, sb_ref)
        consume(kb, sa_ref, None)
        scores(kb + 2, sa_ref)
        consume(kb + 1, sb_ref, None)
        return c

    lax.fori_loop(0, jnp.maximum(i - 1, 0), far_pair, 0)

    @pl.when(i >= 1)
    def _():
        scores(2 * i - 1, sb_ref)
        consume(2 * i - 2, sa_ref, None)
        scores(2 * i, sa_ref)
        consume(2 * i - 1, sb_ref, 2 * tk)

    scores(2 * i + 1, sb_ref)
    consume(2 * i, sa_ref, tk)
    consume(2 * i + 1, sb_ref, 0)

    lam = (jnp.exp(jnp.sum(lq1_ref[...] * lk1_ref[...], axis=-1, keepdims=True))
           - jnp.exp(jnp.sum(lq2_ref[...] * lk2_ref[...], axis=-1, keepdims=True)) + lam_init)
    o = acc_ref[0] / l_ref[0] - lam * (acc_ref[1] / l_ref[1])
    ms = jnp.mean(o * o, axis=0, keepdims=True)
    o = (o * lax.rsqrt(ms + NORM_EPS) * g_ref[...]) * (1.0 - lam_init)
    o_ref[...] = o.T.astype(o_ref.dtype)


def _diff_attention(q_t, k, v_t, bias_strips, lam_vecs, subln_g, lam_init):
    s = k.shape[0]
    tk = ATT_T
    tq = 2 * tk
    assert s % tq == 0 and bias_strips.shape[1:] == (tk, 4 * tk)
    hd, vd = DIFF_HEAD_DIM, DIFF_V_DIM
    vec = pl.BlockSpec((1, hd), lambda h, i: (0, 0))
    in_specs = [vec, vec, vec, vec,
                pl.BlockSpec((hd, tq), lambda h, i: (h, i)),
                pl.BlockSpec((hd, tq), lambda h, i: (DIFF_HEADS + h, i)),
                pl.BlockSpec((s, hd), lambda h, i: (0, h)),
                pl.BlockSpec((s, hd), lambda h, i: (0, DIFF_HEADS + h)),
                pl.BlockSpec((vd, s), lambda h, i: (h, 0)),
                pl.BlockSpec((1, tk, 4 * tk), lambda h, i: (h, 0, 0), pipeline_mode=pl.Buffered(1)),
                pl.BlockSpec((1, tk, 4 * tk), lambda h, i: (DIFF_HEADS + h, 0, 0), pipeline_mode=pl.Buffered(1)),
                pl.BlockSpec((vd, 1), lambda h, i: (0, 0))]
    return pl.pallas_call(
        functools.partial(_diff_attn_kernel, lam_init=lam_init),
        grid=(DIFF_HEADS, s // tq),
        in_specs=in_specs,
        out_specs=pl.BlockSpec((tq, vd), lambda h, i: (i, h)),
        out_shape=jax.ShapeDtypeStruct((s, DIFF_WIDTH), BF16),
        scratch_shapes=[pltpu.VMEM((2, tk, tq), F32), pltpu.VMEM((2, tk, tq), F32),
                        pltpu.VMEM((2, 1, tq), F32), pltpu.VMEM((2, 1, tq), F32),
                        pltpu.VMEM((2, vd, tq), F32)],
        compiler_params=_cp(("arbitrary", "arbitrary")),
        name="diff_attention",
    )(*lam_vecs, q_t, q_t, k, k, v_t, bias_strips, bias_strips, subln_g.reshape(vd, 1).astype(F32))


def _t5_bucket(dist):
    n = jnp.maximum(dist, 0)
    max_exact = REL_BUCKETS // 2
    nf = jnp.maximum(n, 1).astype(F32)
    large = max_exact + (jnp.log(nf / max_exact) / math.log(REL_MAX_DIST / max_exact)
                         * (REL_BUCKETS - max_exact)).astype(jnp.int32)
    large = jnp.minimum(large, REL_BUCKETS - 1)
    return jnp.where(n < max_exact, n, large)


def _diff_bias_strips(rel_bias, tk):
    assert tk >= REL_MAX_DIST
    width = 4 * tk
    n = width + tk - 1
    table = rel_bias.astype(F32)
    table = (table - table[REL_BUCKETS - 1:REL_BUCKETS]) * LOG2E
    kk = jnp.arange(n, dtype=jnp.int32)
    dist = kk - tk
    onehot = (_t5_bucket(dist)[:, None] == jnp.arange(REL_BUCKETS, dtype=jnp.int32)[None, :]).astype(F32)
    g = jnp.einsum('kb,bm->mk', onehot, table, precision=lax.Precision.HIGHEST)
    w = jnp.where(((dist >= 0) & (kk < width))[None, :], g, NEG_INF)
    maps = w.shape[0]
    flat = jnp.tile(w, (1, tk + 1))[:, :tk * (n - 1)]
    return flat.reshape(maps, tk, n - 1)[:, :, :width]


def _rope_apply(tv, c_ref, s1_ref, s2_ref):
    return (tv * c_ref[...] + pltpu.roll(tv, 96, 1) * s1_ref[...] + pltpu.roll(tv, 32, 1) * s2_ref[...])


def _mla_q_kernel(cq_ref, g_ref, w_ref, qg_ref, c_ref, s1_ref, s2_ref, o_ref, xg_ref):
    h = pl.program_id(1)

    @pl.when(h == 0)
    def _():
        c = cq_ref[...].astype(F32)
        r = lax.rsqrt(jnp.mean(c * c, axis=-1, keepdims=True) + NORM_EPS)
        xg_ref[...] = (c * r * g_ref[...]).astype(BF16)

    ug = jnp.dot(xg_ref[...], w_ref[...], preferred_element_type=F32)
    for hh in range(MLA_HEAD_GROUP):
        u = ug[:, hh * MLA_QK_PAD:(hh + 1) * MLA_QK_PAD]
        ms = jnp.sum(u * u, axis=-1, keepdims=True) * (1.0 / MLA_QK_DIM)
        qn = u * lax.rsqrt(ms + NORM_EPS) * qg_ref[...]
        o_ref[hh, :MLA_NOPE_DIM, :] = qn[:, :MLA_NOPE_DIM].T.astype(o_ref.dtype)
        o_ref[hh, MLA_NOPE_DIM:, :] = _rope_apply(qn[:, MLA_NOPE_DIM:], c_ref, s1_ref, s2_ref).T.astype(o_ref.dtype)


def _mla_q_prep(rest, cq_g, w_uq_pad, qg_pad, rope_tabs, tm):
    s = rest.shape[0]
    hg = MLA_HEAD_GROUP
    tab = pl.BlockSpec((tm, LANE), lambda i, h: (i, 0))
    return pl.pallas_call(
        _mla_q_kernel,
        grid=(s // tm, MLA_HEADS // hg),
        in_specs=[pl.BlockSpec((tm, MLA_Q_RANK), lambda i, h: (i, R_CQ // MLA_Q_RANK)),
                  pl.BlockSpec((1, MLA_Q_RANK), lambda i, h: (0, 0)),
                  pl.BlockSpec((MLA_Q_RANK, hg * MLA_QK_PAD), lambda i, h: (0, h)),
                  pl.BlockSpec((1, MLA_QK_PAD), lambda i, h: (0, 0)),
                  tab, tab, tab],
        out_specs=pl.BlockSpec((hg, MLA_QK_PAD, tm), lambda i, h: (h, 0, i)),
        out_shape=jax.ShapeDtypeStruct((MLA_HEADS, MLA_QK_PAD, s), BF16),
        scratch_shapes=[pltpu.VMEM((tm, MLA_Q_RANK), BF16)],
        compiler_params=_cp(("arbitrary", "arbitrary")),
        name="mla_q_prep",
    )(rest, cq_g, w_uq_pad, qg_pad, *rope_tabs)


def _mla_kv_kernel(ckv_ref, kr_ref, g_ref, w_ref, kgn_ref, kgr_ref, c_ref, s1_ref, s2_ref,
                   k_ref, vt_ref, xg_ref):
    h = pl.program_id(1)

    @pl.when(h == 0)
    def _():
        c = ckv_ref[...].astype(F32)
        r = lax.rsqrt(jnp.mean(c * c, axis=-1, keepdims=True) + NORM_EPS)
        xg_ref[...] = (c * r * g_ref[...]).astype(BF16)

    hw = MLA_NOPE_DIM + MLA_V_DIM
    kvg = jnp.dot(xg_ref[...], w_ref[...].astype(BF16), preferred_element_type=F32)
    lane = lax.broadcasted_iota(jnp.int32, kr_ref.shape, 1)
    kr = jnp.where(lane < MLA_ROPE_DIM, kr_ref[...].astype(F32), 0.0)
    kr_ss = jnp.sum(kr * kr, axis=-1, keepdims=True)
    for hh in range(MLA_HEAD_GROUP):
        kn = kvg[:, hh * hw:hh * hw + MLA_NOPE_DIM]
        ms = (jnp.sum(kn * kn, axis=-1, keepdims=True) + kr_ss) * (1.0 / MLA_QK_DIM)
        rs = lax.rsqrt(ms + NORM_EPS)
        k_ref[hh, :, :MLA_NOPE_DIM] = (kn * rs * kgn_ref[...]).astype(k_ref.dtype)
        k_ref[hh, :, MLA_NOPE_DIM:] = _rope_apply(kr * rs * kgr_ref[...], c_ref, s1_ref, s2_ref).astype(k_ref.dtype)
        vt_ref[hh] = kvg[:, hh * hw + MLA_NOPE_DIM:(hh + 1) * hw].T.astype(vt_ref.dtype)


def _mla_kv_prep(rest, ckv_g, w_ukv, kg_nope, kg_rope_pad, rope_tabs, tm):
    s = rest.shape[0]
    hg = MLA_HEAD_GROUP
    tab = pl.BlockSpec((tm, LANE), lambda i, h: (i, 0))
    hw = MLA_NOPE_DIM + MLA_V_DIM
    return pl.pallas_call(
        _mla_kv_kernel,
        grid=(s // tm, MLA_HEADS // hg),
        in_specs=[pl.BlockSpec((tm, MLA_KV_RANK), lambda i, h: (i, R_CKV // MLA_KV_RANK)),
                  pl.BlockSpec((tm, LANE), lambda i, h: (i, R_KROPE // LANE)),
                  pl.BlockSpec((1, MLA_KV_RANK), lambda i, h: (0, 0)),
                  pl.BlockSpec((MLA_KV_RANK, hg * hw), lambda i, h: (0, h)),
                  pl.BlockSpec((1, LANE), lambda i, h: (0, 0)),
                  pl.BlockSpec((1, LANE), lambda i, h: (0, 0)),
                  tab, tab, tab],
        out_specs=[pl.BlockSpec((hg, tm, MLA_QK_PAD), lambda i, h: (h, i, 0)),
                   pl.BlockSpec((hg, MLA_V_DIM, tm), lambda i, h: (h, 0, i))],
        out_shape=[jax.ShapeDtypeStruct((MLA_HEADS, s, MLA_QK_PAD), BF16),
                   jax.ShapeDtypeStruct((MLA_HEADS, MLA_V_DIM, s), BF16)],
        scratch_shapes=[pltpu.VMEM((tm, MLA_KV_RANK), BF16)],
        compiler_params=_cp(("arbitrary", "arbitrary")),
        name="mla_kv_prep",
    )(rest, rest, ckv_g, w_ukv, kg_nope, kg_rope_pad, *rope_tabs)


def _mla_attn_kernel(qt_ref, k_ref, vt_ref, o_ref, sa_ref, sb_ref, m_ref, l_ref, acc_ref):
    tk, tq = sa_ref.shape
    i = pl.program_id(1)
    qt = qt_ref[0]
    _init_stats(m_ref, l_ref, acc_ref)

    def scores(kb, dst):
        off = _block_offset(kb, tk)
        dst[...] = jnp.dot(k_ref[0, pl.ds(off, tk), :], qt, preferred_element_type=F32)

    def consume(kb, src, diag_shift):
        off = _block_offset(kb, tk)
        s = src[...]
        if diag_shift is not None:
            krow = lax.broadcasted_iota(jnp.int32, s.shape, 0)
            qcol = lax.broadcasted_iota(jnp.int32, s.shape, 1)
            s = jnp.where(krow + diag_shift <= qcol, s, NEG_INF)
        _softmax_pv(0, s, vt_ref[0, :, pl.ds(off, tk)], m_ref, l_ref, acc_ref)

    scores(0, sa_ref)

    def far_pair(j, c):
        kb = 2 * j
        scores(kb + 1, sb_ref)
        consume(kb, sa_ref, None)
        scores(kb + 2, sa_ref)
        consume(kb + 1, sb_ref, None)
        return c

    lax.fori_loop(0, i, far_pair, 0)
    scores(2 * i + 1, sb_ref)
    consume(2 * i, sa_ref, 0)
    consume(2 * i + 1, sb_ref, tk)
    o_ref[...] = (acc_ref[0] / l_ref[0]).T.astype(o_ref.dtype)


def _mla_attention(q_t, k, v_t):
    s = k.shape[1]
    tk = ATT_T
    tq = 2 * tk
    assert s % tq == 0
    return pl.pallas_call(
        _mla_attn_kernel,
        grid=(MLA_HEADS, s // tq),
        in_specs=[pl.BlockSpec((1, MLA_QK_PAD, tq), lambda h, i: (h, 0, i)),
                  pl.BlockSpec((1, s, MLA_QK_PAD), lambda h, i: (h, 0, 0)),
                  pl.BlockSpec((1, MLA_V_DIM, s), lambda h, i: (h, 0, 0))],
        out_specs=pl.BlockSpec((tq, MLA_V_DIM), lambda h, i: (i, h)),
        out_shape=jax.ShapeDtypeStruct((s, MLA_WIDTH), BF16),
        scratch_shapes=[pltpu.VMEM((tk, tq), F32), pltpu.VMEM((tk, tq), F32),
                        pltpu.VMEM((1, 1, tq), F32), pltpu.VMEM((1, 1, tq), F32),
                        pltpu.VMEM((1, MLA_V_DIM, tq), F32)],
        compiler_params=_cp(("arbitrary", "arbitrary")),
        name="mla_attention",
    )(q_t, k, v_t)


def _qk_nt(q, k_blk):
    return lax.dot_general(q, k_blk, (((1,), (1,)), ((), ())), preferred_element_type=F32)


def _mem_attn_kernel(q_ref, k_ref, v_ref, qg_ref, o_ref):
    shift = R_MQ % MEM_WINDOW
    qall = q_ref[...].astype(F32)[:, shift:shift + MEM_WIDTH]
    for h in range(MEM_HEADS):
        lo = h * MEM_HEAD_DIM
        qh = qall[:, lo:lo + MEM_HEAD_DIM]
        ms = jnp.mean(qh * qh, axis=-1, keepdims=True)
        qn = (qh * lax.rsqrt(ms + NORM_EPS) * qg_ref[...]).astype(BF16)
        s = _qk_nt(qn, k_ref[:, lo:lo + MEM_HEAD_DIM])
        p = jnp.exp(s - jnp.max(s, axis=-1, keepdims=True))
        l = jnp.sum(p, axis=-1, keepdims=True)
        o = jnp.dot(p.astype(BF16), v_ref[:, lo:lo + MEM_HEAD_DIM], preferred_element_type=F32)
        o_ref[:, lo:lo + MEM_HEAD_DIM] = (o / l).astype(o_ref.dtype)


def _mem_attention(rest, k_mem, v_mem, qg_scaled, tm):
    s = rest.shape[0]
    n_mem = k_mem.shape[0]
    assert R_MQ % MEM_WINDOW + MEM_WIDTH <= MEM_WINDOW
    return pl.pallas_call(
        _mem_attn_kernel,
        grid=(s // tm,),
        in_specs=[pl.BlockSpec((tm, MEM_WINDOW), lambda i: (i, R_MQ // MEM_WINDOW)),
                  pl.BlockSpec((n_mem, MEM_WIDTH), lambda i: (0, 0)),
                  pl.BlockSpec((n_mem, MEM_WIDTH), lambda i: (0, 0)),
                  pl.BlockSpec((1, MEM_HEAD_DIM), lambda i: (0, 0))],
        out_specs=pl.BlockSpec((tm, MEM_WIDTH), lambda i: (i, 0)),
        out_shape=jax.ShapeDtypeStruct((s, MEM_WIDTH), BF16),
        compiler_params=_cp(("parallel",)),
        name="mem_attention",
    )(rest, k_mem, v_mem, qg_scaled)


def _mix_kernel(od_ref, om_ref, oc_ref, wd_ref, wm_ref, wc_ref, g0_ref, g1_ref, g2_ref, o_ref):
    yd = jnp.dot(od_ref[...], wd_ref[...], preferred_element_type=F32)
    ym = jnp.dot(om_ref[...], wm_ref[...], preferred_element_type=F32)
    yc = jnp.dot(oc_ref[...], wc_ref[...], preferred_element_type=F32)
    mixed = (g0_ref[...].astype(F32) * yd + g1_ref[...].astype(F32) * ym) + g2_ref[...].astype(F32) * yc
    o_ref[...] = mixed.astype(o_ref.dtype)


def _mix(o_diff, o_mla, o_mem, w_d, w_m, w_c, gates, tm, tn):
    s = o_diff.shape[0]
    d = w_d.shape[1]
    nj = d // tn
    return pl.pallas_call(
        _mix_kernel,
        grid=(s // tm, nj),
        in_specs=[pl.BlockSpec((tm, o_diff.shape[1]), lambda i, j: (i, 0)),
                  pl.BlockSpec((tm, o_mla.shape[1]), lambda i, j: (i, 0)),
                  pl.BlockSpec((tm, o_mem.shape[1]), lambda i, j: (i, 0)),
                  pl.BlockSpec((w_d.shape[0], tn), lambda i, j: (0, j)),
                  pl.BlockSpec((w_m.shape[0], tn), lambda i, j: (0, j)),
                  pl.BlockSpec((w_c.shape[0], tn), lambda i, j: (0, j)),
                  pl.BlockSpec((tm, tn), lambda i, j: (i, j)),
                  pl.BlockSpec((tm, tn), lambda i, j: (i, nj + j)),
                  pl.BlockSpec((tm, tn), lambda i, j: (i, 2 * nj + j))],
        out_specs=pl.BlockSpec((tm, tn), lambda i, j: (i, j)),
        out_shape=jax.ShapeDtypeStruct((s, d), BF16),
        compiler_params=_cp(("arbitrary", "arbitrary")),
        name="gated_mix",
    )(o_diff, o_mla, o_mem, w_d, w_m, w_c, gates, gates, gates)


def _router_kernel(x_ref, g_ref, w_ref, b_ref, h_ref, r_ref):
    x = x_ref[...]
    ms = jnp.mean(x * x, axis=-1, keepdims=True)
    h = x * lax.rsqrt(ms + NORM_EPS) * g_ref[...]
    h_ref[...] = _pack_bf16_pairs(h)
    logits = jnp.dot(h, w_ref[...], preferred_element_type=F32, precision=lax.Precision.HIGHEST) + b_ref[...]
    lane = lax.broadcasted_iota(jnp.int32, logits.shape, 1)
    lane_f = lane.astype(F32)
    big = float(4 * ROUTE_W)
    lg = jnp.where(lane < N_GROUPS, logits, -jnp.inf)
    gmax = jnp.max(lg, axis=-1, keepdims=True)
    gidx = jnp.min(jnp.where(lg == gmax, lane_f, big), axis=-1, keepdims=True)
    pg_top = 1.0 / jnp.sum(jnp.exp(lg - gmax), axis=-1, keepdims=True)
    e_lane = lane - N_GROUPS
    lane_group = jnp.right_shift(e_lane, 3).astype(F32)
    in_group = (e_lane >= 0) & (e_lane < N_EXPERTS) & (lane_group == gidx)
    le = jnp.where(in_group, logits, -jnp.inf)
    e1 = jnp.max(le, axis=-1, keepdims=True)
    i1 = jnp.min(jnp.where(le == e1, lane_f, big), axis=-1, keepdims=True)
    le2 = jnp.where(lane_f == i1, -jnp.inf, le)
    e2 = jnp.max(le2, axis=-1, keepdims=True)
    i2 = jnp.min(jnp.where(le2 == e2, lane_f, big), axis=-1, keepdims=True)
    w2 = jnp.exp(e2 - e1)
    inv = 1.0 / (1.0 + w2)
    gate1 = pg_top * inv
    gate2 = pg_top * (w2 * inv)
    out = jnp.where(lane == 0, i1 - N_GROUPS,
                    jnp.where(lane == 1, i2 - N_GROUPS,
                              jnp.where(lane == 2, gate1, jnp.where(lane == 3, gate2, 0.0))))
    r_ref[...] = out


def _router(x1, g, w_r, b_r, tm):
    s, d = x1.shape
    return pl.pallas_call(
        _router_kernel,
        grid=(s // tm,),
        in_specs=[pl.BlockSpec((tm, d), lambda i: (i, 0)),
                  pl.BlockSpec((1, d), lambda i: (0, 0)),
                  pl.BlockSpec((d, ROUTE_W), lambda i: (0, 0)),
                  pl.BlockSpec((1, ROUTE_W), lambda i: (0, 0))],
        out_specs=[pl.BlockSpec((tm, d // 2), lambda i: (i, 0)),
                   pl.BlockSpec((tm, ROUTE_W), lambda i: (i, 0))],
        out_shape=[jax.ShapeDtypeStruct((s, d // 2), jnp.uint32),
                   jax.ShapeDtypeStruct((s, ROUTE_W), F32)],
        compiler_params=_cp(("parallel",)),
        name="ffn_norm_router",
    )(x1, g, w_r, b_r)


def _pack_bf16_pairs(v):
    n = v.shape[1] // 2
    bits = lax.bitcast_convert_type(v.astype(BF16).astype(F32), jnp.uint32)
    return jnp.right_shift(bits[:, :n], jnp.uint32(16)) | bits[:, n:]


def _unpack_bf16_pairs(words):
    lo = lax.bitcast_convert_type(jnp.left_shift(words, jnp.uint32(16)), F32)
    hi = lax.bitcast_convert_type(words & jnp.uint32(0xFFFF0000), F32)
    return lo, hi


def _moe_kernel(be_ref, nr_ref, nu_ref, tok_ref, tokn_ref, h_ref, wg_ref, wu_ref, wd_ref, o_ref,
                xg_ref, xb_ref, a_ref, wgb_ref, wub_ref, wdb_ref, sem):
    b = pl.program_id(0)
    c = pl.program_id(1)
    nb = pl.num_programs(0)
    nr = nr_ref[b]
    half = xg_ref.shape[2]
    fh = wgb_ref.shape[1]

    def row_copy(src_row, slot, r):
        return pltpu.make_async_copy(h_ref.at[pl.ds(src_row, 1), :], xg_ref.at[slot, pl.ds(r, 1), :], sem.at[slot])

    def start_gather(tok, n_rows, slot):
        unroll = 8

        def issue(i, carry):
            for j in range(unroll):
                r = i * unroll + j
                row_copy(tok[0, 0, r], slot, r).start(priority=1)
            return carry
        lax.fori_loop(0, n_rows // unroll, issue, 0)

    def for_row_count(fn):
        for units in range(1, MOE_TB // MOE_RU + 1):
            @pl.when(nr == units)
            def _():
                fn(units * MOE_RU)

    @pl.when(c == 0)
    def _():
        slot = b % 2

        @pl.when(b == 0)
        def _():
            start_gather(tok_ref, nr * MOE_RU, 0)

        def drain(u, carry):
            pltpu.make_async_copy(h_ref.at[pl.ds(0, MOE_RU), :], xg_ref.at[slot, pl.ds(0, MOE_RU), :],
                                  sem.at[slot]).wait()
            return carry
        lax.fori_loop(0, nr, drain, 0)

        @pl.when(b + 1 < nb)
        def _():
            start_gather(tokn_ref, nr_ref[jnp.minimum(b + 1, nb - 1)] * MOE_RU, 1 - slot)

        def unpack(u, carry):
            r0 = pl.multiple_of(u * MOE_RU, MOE_RU)
            lo, hi = _unpack_bf16_pairs(xg_ref[slot, pl.ds(r0, MOE_RU), :])
            xb_ref[pl.ds(r0, MOE_RU), :half] = lo.astype(BF16)
            xb_ref[pl.ds(r0, MOE_RU), half:] = hi.astype(BF16)
            return carry
        lax.fori_loop(0, nr, unpack, 0)

    @pl.when((c < 2) & (nr > 0))
    def _():
        wgb_ref[...] = wg_ref[0].astype(BF16)
        wub_ref[...] = wu_ref[0].astype(BF16)

        def gate_up(m):
            x = xb_ref[:m, :]
            g = jnp.dot(x, wgb_ref[...], preferred_element_type=F32)
            u = jnp.dot(x, wub_ref[...], preferred_element_type=F32)
            a_ref[c, :m, :] = ((g * jax.nn.sigmoid(g)) * u).astype(BF16)

        for_row_count(gate_up)

    @pl.when((c >= 2) & (nr > 0))
    def _():
        wdb_ref[...] = wd_ref[0].astype(BF16)

        def down(m):
            yv = (jnp.dot(a_ref[0, :m, :], wdb_ref[:fh, :], preferred_element_type=F32)
                  + jnp.dot(a_ref[1, :m, :], wdb_ref[fh:, :], preferred_element_type=F32))
            o_ref[:m, :] = _pack_bf16_pairs(yv)
            if m < MOE_TB:
                o_ref[m:, :] = jnp.zeros((MOE_TB - m, o_ref.shape[1]), jnp.uint32)

        for_row_count(down)

    @pl.when((c >= 2) & (nr == 0))
    def _():
        o_ref[...] = jnp.zeros(o_ref.shape, jnp.uint32)


def _moe_experts(block_e, nsub, n_used, buf_tok, h2p, w_gate, w_up, w_down):
    n_blocks = block_e.shape[0]
    half = h2p.shape[1]
    d = 2 * half
    ff = w_gate.shape[2]
    fh = ff // 2
    dh = d // 2
    tok = buf_tok.reshape(n_blocks, 1, MOE_TB)

    def gate_up_idx(b, c, be, nr, nu):
        live = b < nu[0]
        return be[jnp.minimum(b, nu[0] - 1)], 0, jnp.where(live, jnp.minimum(c, 1), 1)

    def down_idx(b, c, be, nr, nu):
        bb = jnp.minimum(b, nu[0] - 1)
        cc = jnp.where(b < nu[0], c, MOE_PH - 1)
        e = jnp.where(cc >= 2, be[bb], be[jnp.maximum(bb - 1, 0)])
        return e, 0, jnp.where(cc == 2, 0, 1)

    def out_idx(b, c, be, nr, nu):
        ob = jnp.where(c >= 2, b, jnp.maximum(b - 1, 0))
        oc = jnp.where(c >= 2, c - 2, jnp.where(b > 0, 1, 0))
        return ob, oc

    grid_spec = pltpu.PrefetchScalarGridSpec(
        num_scalar_prefetch=3,
        grid=(n_blocks, MOE_PH),
        in_specs=[pl.BlockSpec((1, 1, MOE_TB), lambda b, c, be, nr, nu: (b, 0, 0), memory_space=pltpu.SMEM),
                  pl.BlockSpec((1, 1, MOE_TB), lambda b, c, be, nr, nu: (jnp.minimum(b + 1, n_blocks - 1), 0, 0),
                               memory_space=pltpu.SMEM),
                  pl.BlockSpec(memory_space=pl.ANY),
                  pl.BlockSpec((1, d, fh), gate_up_idx),
                  pl.BlockSpec((1, d, fh), gate_up_idx),
                  pl.BlockSpec((1, ff, dh), down_idx)],
        out_specs=pl.BlockSpec((MOE_TB, dh // 2), out_idx),
        scratch_shapes=[pltpu.VMEM((2, MOE_TB, half), jnp.uint32), pltpu.VMEM((MOE_TB, d), BF16),
                        pltpu.VMEM((2, MOE_TB, fh), BF16),
                        pltpu.VMEM((d, fh), BF16), pltpu.VMEM((d, fh), BF16), pltpu.VMEM((ff, dh), BF16),
                        pltpu.SemaphoreType.DMA((2,))],
    )
    return pl.pallas_call(
        _moe_kernel,
        grid_spec=grid_spec,
        out_shape=jax.ShapeDtypeStruct((n_blocks * MOE_TB, half), jnp.uint32),
        compiler_params=_cp(("arbitrary", "arbitrary")),
        name="moe_experts",
    )(block_e, nsub, n_used, tok, tok, h2p, w_gate, w_up, w_down)


def _combine_kernel(slot_ref, r_ref, x_ref, yb_ref, o_ref, g_ref, sem):
    tm = x_ref.shape[0]
    n = g_ref.shape[2]

    unroll = 4

    def issue(i, c):
        for j in range(unroll):
            r = i * unroll + j
            for k in range(TOP_K):
                sl = slot_ref[0, 0, r * TOP_K + k]
                pltpu.make_async_copy(yb_ref.at[pl.ds(sl, 1), :], g_ref.at[k, pl.ds(r, 1), :],
                                      sem.at[k]).start(priority=k)
        return c

    lax.fori_loop(0, tm // unroll, issue, 0)
    for k in range(TOP_K):
        pltpu.make_async_copy(yb_ref.at[pl.ds(0, tm), :], g_ref.at[k], sem.at[k]).wait()

    route = r_ref[...]
    w0 = route[:, TOP_K:TOP_K + 1]
    w1 = route[:, TOP_K + 1:TOP_K + 2]
    q = n // 2
    for hf in range(2):
        lo0, hi0 = _unpack_bf16_pairs(g_ref[0, :, hf * q:(hf + 1) * q])
        lo1, hi1 = _unpack_bf16_pairs(g_ref[1, :, hf * q:(hf + 1) * q])
        c0 = hf * n
        o_ref[:, c0:c0 + q] = x_ref[:, c0:c0 + q] + (lo0 * w0 + lo1 * w1)
        o_ref[:, c0 + q:c0 + n] = x_ref[:, c0 + q:c0 + n] + (hi0 * w0 + hi1 * w1)


def _combine(slots, route, x1, yb, tm):
    s, d = x1.shape
    return pl.pallas_call(
        _combine_kernel,
        grid=(s // tm,),
        in_specs=[pl.BlockSpec((1, 1, tm * TOP_K), lambda i: (i, 0, 0), memory_space=pltpu.SMEM),
                  pl.BlockSpec((tm, ROUTE_W), lambda i: (i, 0)),
                  pl.BlockSpec((tm, d), lambda i: (i, 0)),
                  pl.BlockSpec(memory_space=pl.ANY)],
        out_specs=pl.BlockSpec((tm, d), lambda i: (i, 0)),
        out_shape=jax.ShapeDtypeStruct((s, d), F32),
        scratch_shapes=[pltpu.VMEM((TOP_K, tm, d // 2), jnp.uint32), pltpu.SemaphoreType.DMA((TOP_K,))],
        compiler_params=_cp(("arbitrary",)),
        name="moe_combine",
    )(slots.reshape(s // tm, 1, tm * TOP_K), route, x1, yb)


def _dispatch_plan(route, s):
    a = s * TOP_K
    flat_e = route[:, :TOP_K].astype(jnp.int32).reshape(a)
    onehot = (flat_e[:, None] == jnp.arange(N_EXPERTS, dtype=jnp.int32)[None, :]).astype(jnp.int32)
    csum = jnp.cumsum(onehot, axis=0)
    counts = csum[-1]
    rank = jnp.sum((csum - onehot) * onehot, axis=1)
    padded = (counts + MOE_TB - 1) // MOE_TB * MOE_TB
    pad_end = jnp.cumsum(padded)
    pad_start = pad_end - padded
    dest = pad_start[flat_e] + rank
    n_blocks = a // MOE_TB + N_EXPERTS
    p_rows = n_blocks * MOE_TB
    buf_tok = jnp.zeros((p_rows,), jnp.int32).at[dest].set(jnp.arange(a, dtype=jnp.int32) // TOP_K)
    starts = jnp.arange(n_blocks, dtype=jnp.int32) * MOE_TB
    block_e = jnp.minimum(jnp.searchsorted(pad_end, starts, side='right'), N_EXPERTS - 1).astype(jnp.int32)
    valid = jnp.clip(counts[block_e] - (starts - pad_start[block_e]), 0, MOE_TB)
    valid = jnp.where(starts < pad_end[-1], valid, 0)
    nsub = ((valid + MOE_RU - 1) // MOE_RU).astype(jnp.int32)
    n_used = (pad_end[-1:] // MOE_TB).astype(jnp.int32)
    return block_e, nsub, n_used, buf_tok, dest.astype(jnp.int32)


def _rope_tables(positions):
    half = MLA_ROPE_DIM // 2
    inv_freq = ROPE_THETA ** (-jnp.arange(half, dtype=F32) / half)
    ang = positions.astype(F32)[:, None] * inv_freq[None, :]
    cos, sin = jnp.cos(ang), jnp.sin(ang)
    z = jnp.zeros_like(cos)
    c = jnp.concatenate([cos, cos, z, z], axis=-1)
    s1 = jnp.concatenate([-sin, z, z, z], axis=-1)
    s2 = jnp.concatenate([z, sin, z, z], axis=-1)
    return c, s1, s2


def kernel(x, mem, positions, rel_bias, mix_norm_g, w_in, diff_q_norm_g, diff_k_norm_g, diff_lambda_q1, diff_lambda_k1, diff_lambda_q2, diff_lambda_k2, diff_subln_g, mla_cq_norm_g, mla_ckv_norm_g, mla_w_uq, mla_w_ukv, mla_q_norm_g, mla_k_norm_g, mem_norm_g, mem_w_kv, mem_q_norm_g, mem_k_norm_g, w_o_diff, w_o_mla, w_o_mem, w_out, ffn_norm_g, w_route_group, b_route_group, w_route_expert, b_route_expert, w_exp_gate, w_exp_up, w_exp_down):
    b, s, d = x.shape
    assert b == 1 and s % ATT_T == 0
    depth = mix_norm_g.shape[0]
    xs = x.reshape(s, d)
    pos = positions.reshape(s)
    rope_tabs = _rope_tables(pos)
    row = lambda v: v.reshape(1, -1).astype(F32)

    for l in range(depth):
        lam_init = 0.8 - 0.6 * math.exp(-0.3 * l)
        h = _rmsnorm_rows(xs, mix_norm_g[l], 256)
        tn = 512
        q_gain = jnp.tile(diff_q_norm_g[l] * (DIFF_HEAD_DIM ** -0.5 * LOG2E), DIFF_MAPS)
        w_in_t = jnp.transpose(w_in[l])
        dq_t = _matmul_nt(h, w_in_t, row0=OFF_DQ, n_cols=DIFF_QK_WIDTH, tm=MM_TM, tn=tn, out_dtype=BF16,
                          mode="groupnorm", extra=q_gain, group=DIFF_HEAD_DIM, transpose_out=True,
                          name="diff_q_proj")
        dk = _matmul_nt(h, w_in_t, row0=OFF_DK, n_cols=DIFF_QK_WIDTH, tm=MM_TM, tn=tn, out_dtype=BF16,
                        mode="groupnorm", extra=jnp.tile(diff_k_norm_g[l], DIFF_MAPS), group=DIFF_HEAD_DIM,
                        name="diff_k_proj")
        dv_t = _matmul_nt(h, w_in_t, row0=OFF_DV, n_cols=DIFF_WIDTH, tm=MM_TM, tn=tn, out_dtype=BF16,
                          transpose_out=True, name="diff_v_proj")
        rest = _matmul_nt(h, w_in_t, row0=OFF_CQ, n_cols=REST_WIDTH, tm=MM_TM, tn=tn, out_dtype=BF16,
                          name="rest_proj")
        gates = _matmul_nt(h, w_in_t, row0=OFF_GATES, n_cols=3 * d, tm=MM_TM, tn=tn, out_dtype=BF16,
                           mode="sigmoid", name="gate_proj")

        bias_strips = _diff_bias_strips(rel_bias, ATT_T)
        lam_vecs = [row(diff_lambda_q1[l]), row(diff_lambda_k1[l]), row(diff_lambda_q2[l]), row(diff_lambda_k2[l])]
        o_diff = _diff_attention(dq_t, dk, dv_t, bias_strips, lam_vecs, diff_subln_g[l], lam_init)

        w_uq_heads = jnp.pad(
            mla_w_uq[l].reshape(MLA_Q_RANK, MLA_HEADS, MLA_QK_DIM),
            ((0, 0), (0, 0), (0, MLA_QK_PAD - MLA_QK_DIM))).reshape(MLA_Q_RANK, MLA_HEADS * MLA_QK_PAD).astype(BF16)
        qg_pad = jnp.pad(mla_q_norm_g[l] * (MLA_QK_DIM ** -0.5 * LOG2E),
                         (0, MLA_QK_PAD - MLA_QK_DIM)).reshape(1, -1).astype(F32)
        q_mla_t = _mla_q_prep(rest, row(mla_cq_norm_g[l]), w_uq_heads, qg_pad, rope_tabs, 512)
        kg = mla_k_norm_g[l]
        kg_nope = row(kg[:MLA_NOPE_DIM])
        kg_rope = jnp.pad(kg[MLA_NOPE_DIM:], (0, LANE - MLA_ROPE_DIM)).reshape(1, -1).astype(F32)
        k_mla, v_mla_t = _mla_kv_prep(rest, row(mla_ckv_norm_g[l]), mla_w_ukv[l], kg_nope, kg_rope, rope_tabs, 512)
        o_mla = _mla_attention(q_mla_t, k_mla, v_mla_t)

        n_mem = mem.shape[1]
        mem_h = _rmsnorm_rows(mem.reshape(n_mem, d), mem_norm_g[l], n_mem)
        k_mem = _matmul(mem_h, mem_w_kv[l], col0=0, n_cols=MEM_WIDTH, tm=n_mem, tn=512, out_dtype=BF16,
                        mode="groupnorm", extra=jnp.tile(mem_k_norm_g[l], MEM_HEADS), group=MEM_HEAD_DIM,
                        name="mem_k_proj")
        v_mem = _matmul(mem_h, mem_w_kv[l], col0=MEM_WIDTH, n_cols=MEM_WIDTH, tm=n_mem, tn=512, out_dtype=BF16,
                        name="mem_v_proj")
        o_mem = _mem_attention(rest, k_mem, v_mem, row(mem_q_norm_g[l] * MEM_HEAD_DIM ** -0.5), 512)

        mixed = _mix(o_diff, o_mla, o_mem, w_o_diff[l].astype(BF16), w_o_mla[l].astype(BF16),
                     w_o_mem[l].astype(BF16), gates, MM_TM, 512)
        x1 = _matmul(mixed, w_out[l], n_cols=d, tm=MM_TM, tn=512, out_dtype=F32, mode="residual", extra=xs,
                     name="out_proj")

        w_r = jnp.pad(jnp.concatenate([w_route_group[l], w_route_expert[l]], axis=1),
                      ((0, 0), (0, ROUTE_W - N_GROUPS - N_EXPERTS))).astype(F32)
        b_r = jnp.pad(jnp.concatenate([b_route_group[l], b_route_expert[l]]),
                      (0, ROUTE_W - N_GROUPS - N_EXPERTS)).reshape(1, -1).astype(F32)
        h2, route = _router(x1, row(ffn_norm_g[l]), w_r, b_r, 256)
        block_e, nsub, n_used, buf_tok, slots = _dispatch_plan(route, s)
        yb = _moe_experts(block_e, nsub, n_used, buf_tok, h2, w_exp_gate[l], w_exp_up[l], w_exp_down[l])
        xs = _combine(slots, route, x1, yb, 256)
    return xs.reshape(b, s, d)
```

```python
import functools
import math

import jax
import jax.numpy as jnp
from jax import lax
from jax.experimental import pallas as pl
from jax.experimental.pallas import tpu as pltpu

F32 = jnp.float32
BF16 = jnp.bfloat16

NORM_EPS = 1e-6
NEG_INF = -1e30
LOG2E = math.log2(math.e)

DIFF_HEADS = 6
DIFF_HEAD_DIM = 128
DIFF_V_DIM = 256
DIFF_MAPS = 12
DIFF_QK_WIDTH = 1536
DIFF_WIDTH = 1536
MLA_HEADS = 12
MLA_Q_RANK = 1536
MLA_KV_RANK = 512
MLA_NOPE_DIM = 128
MLA_ROPE_DIM = 64
MLA_QK_DIM = 192
MLA_QK_PAD = 256
MLA_V_DIM = 128
MLA_WIDTH = 1536
ROPE_THETA = 10000.0
MEM_HEADS = 4
MEM_HEAD_DIM = 256
MEM_WIDTH = 1024
REL_BUCKETS = 32
REL_MAX_DIST = 128
N_GROUPS = 8
EXPERTS_PER_GROUP = 8
N_EXPERTS = 64
TOP_K = 2
EXPERT_FF = 512

OFF_DQ = 0
OFF_DK = 1536
OFF_DV = 3072
OFF_CQ = 4608
OFF_CKV = 6144
OFF_KROPE = 6656
OFF_MQ = 6720
OFF_GATES = 7744
REST_WIDTH = 4096
R_CQ = OFF_CQ - OFF_CQ
R_CKV = OFF_CKV - OFF_CQ
R_KROPE = OFF_KROPE - OFF_CQ
R_MQ = OFF_MQ - OFF_CQ

LANE = 128
VMEM_LIMIT = 52 * 1024 * 1024

MM_TM = 1024
MLA_HEAD_GROUP = 4
ATT_T = 512
MOE_TB = 512
MOE_RU = 128
MOE_PH = 4
ROUTE_W = 128
MEM_WINDOW = 2048


def _cp(sem, vmem=VMEM_LIMIT):
    return pltpu.CompilerParams(dimension_semantics=sem, vmem_limit_bytes=vmem)


def _rmsnorm_kernel(x_ref, g_ref, o_ref):
    x = x_ref[...].astype(F32)
    ms = jnp.mean(x * x, axis=-1, keepdims=True)
    o_ref[...] = (x * lax.rsqrt(ms + NORM_EPS) * g_ref[...]).astype(o_ref.dtype)


def _rmsnorm_rows(x, g, tm, out_dtype=BF16):
    m, d = x.shape
    return pl.pallas_call(
        _rmsnorm_kernel,
        grid=(m // tm,),
        in_specs=[pl.BlockSpec((tm, d), lambda i: (i, 0)),
                  pl.BlockSpec((1, d), lambda i: (0, 0))],
        out_specs=pl.BlockSpec((tm, d), lambda i: (i, 0)),
        out_shape=jax.ShapeDtypeStruct((m, d), out_dtype),
        compiler_params=_cp(("parallel",)),
        name="rmsnorm_rows",
    )(x, g.reshape(1, d).astype(F32))


def _cast_shifted(w_ref, w2_ref, wb_ref, shift):
    k = w_ref.shape[0]
    rows = 256

    def body(c, carry):
        r0 = pl.multiple_of(c * rows, rows)
        main = w_ref[pl.ds(r0, rows), :]
        tail = w2_ref[pl.ds(r0, rows), :]
        wb_ref[pl.ds(r0, rows), :] = jnp.concatenate([main[:, shift:], tail[:, :shift]], axis=1).astype(BF16)
        return carry

    lax.fori_loop(0, k // rows, body, 0)


def _mm_kernel(*refs, mode, cast, group, shift, transpose_out):
    a_ref, w_ref = refs[0], refs[1]
    pos = 2
    w2_ref = None
    if shift:
        w2_ref = refs[pos]
        pos += 1
    extra = None
    if mode in ("groupnorm", "residual"):
        extra = refs[pos]
        pos += 1
    o_ref = refs[pos]
    wb_ref = refs[pos + 1] if cast else None
    i = pl.program_id(1)
    if cast:
        @pl.when(i == 0)
        def _():
            if shift:
                _cast_shifted(w_ref, w2_ref, wb_ref, shift)
            else:
                wb_ref[...] = w_ref[...].astype(BF16)
        w = wb_ref[...]
    else:
        w = w_ref[...]
    acc = jnp.dot(a_ref[...], w, preferred_element_type=F32)
    if transpose_out:
        acc = acc.T
    if mode == "plain":
        o_ref[...] = acc.astype(o_ref.dtype)
    elif mode == "sigmoid":
        o_ref[...] = jax.nn.sigmoid(acc).astype(o_ref.dtype)
    elif mode == "residual":
        o_ref[...] = (extra[...] + acc).astype(o_ref.dtype)
    elif mode == "groupnorm":
        tn = w.shape[1]
        for c in range(tn // group):
            sl = slice(c * group, (c + 1) * group)
            if transpose_out:
                blk = acc[sl, :]
                ms = jnp.mean(blk * blk, axis=0, keepdims=True)
                o_ref[sl, :] = (blk * lax.rsqrt(ms + NORM_EPS) * extra[sl, :]).astype(o_ref.dtype)
            else:
                blk = acc[:, sl]
                ms = jnp.mean(blk * blk, axis=-1, keepdims=True)
                o_ref[:, sl] = (blk * lax.rsqrt(ms + NORM_EPS) * extra[:, sl]).astype(o_ref.dtype)


def _matmul(a, w, *, n_cols, tm, tn, out_dtype, col0=0, mode="plain", extra=None, group=LANE,
            transpose_out=False, name="matmul"):
    m, k = a.shape
    assert m % tm == 0 and n_cols % tn == 0 and w.shape[0] == k
    cast = w.dtype != BF16
    base, shift = divmod(col0, tn)
    assert shift <= LANE and (shift == 0 or cast)
    in_specs = [pl.BlockSpec((tm, k), lambda j, i: (i, 0)),
                pl.BlockSpec((k, tn), lambda j, i: (0, base + j))]
    args = [a, w]
    if shift:
        in_specs.append(pl.BlockSpec((k, LANE), lambda j, i: (0, (base + j + 1) * (tn // LANE))))
        args.append(w)
    if mode == "groupnorm":
        gain = extra.reshape(-1, 1) if transpose_out else extra.reshape(1, -1)
        in_specs.append(pl.BlockSpec((tn, 1), lambda j, i: (j, 0)) if transpose_out
                        else pl.BlockSpec((1, tn), lambda j, i: (0, j)))
        args.append(gain.astype(F32))
    elif mode == "residual":
        assert not transpose_out
        in_specs.append(pl.BlockSpec((tm, tn), lambda j, i: (i, j)))
        args.append(extra)
    if transpose_out:
        out_spec = pl.BlockSpec((tn, tm), lambda j, i: (j, i))
        out_shape = jax.ShapeDtypeStruct((n_cols, m), out_dtype)
    else:
        out_spec = pl.BlockSpec((tm, tn), lambda j, i: (i, j))
        out_shape = jax.ShapeDtypeStruct((m, n_cols), out_dtype)
    scratch = [pltpu.VMEM((k, tn), BF16)] if cast else []
    return pl.pallas_call(
        functools.partial(_mm_kernel, mode=mode, cast=cast, group=group, shift=shift, transpose_out=transpose_out),
        grid=(n_cols // tn, m // tm),
        in_specs=in_specs,
        out_specs=out_spec,
        out_shape=out_shape,
        scratch_shapes=scratch,
        compiler_params=_cp(("arbitrary", "arbitrary")),
        name=name,
    )(*args)


def _mm_nt_kernel(*refs, mode, group, transpose_out):
    a_ref, wt_ref = refs[0], refs[1]
    extra = refs[2] if mode == "groupnorm" else None
    o_ref, wb_ref = refs[-2], refs[-1]
    i = pl.program_id(1)

    @pl.when(i == 0)
    def _():
        wb_ref[...] = wt_ref[...].astype(BF16)

    contract_last = (((1,), (1,)), ((), ()))
    if transpose_out:
        acc = lax.dot_general(wb_ref[...], a_ref[...], contract_last, preferred_element_type=F32)
    else:
        acc = lax.dot_general(a_ref[...], wb_ref[...], contract_last, preferred_element_type=F32)
    if mode == "plain":
        o_ref[...] = acc.astype(o_ref.dtype)
    elif mode == "sigmoid":
        o_ref[...] = jax.nn.sigmoid(acc).astype(o_ref.dtype)
    elif mode == "groupnorm":
        tn = wb_ref.shape[0]
        for c in range(tn // group):
            sl = slice(c * group, (c + 1) * group)
            if transpose_out:
                blk = acc[sl, :]
                ms = jnp.mean(blk * blk, axis=0, keepdims=True)
                o_ref[sl, :] = (blk * lax.rsqrt(ms + NORM_EPS) * extra[sl, :]).astype(o_ref.dtype)
            else:
                blk = acc[:, sl]
                ms = jnp.mean(blk * blk, axis=-1, keepdims=True)
                o_ref[:, sl] = (blk * lax.rsqrt(ms + NORM_EPS) * extra[:, sl]).astype(o_ref.dtype)


def _matmul_nt(a, wt, *, row0, n_cols, tm, tn, out_dtype, mode="plain", extra=None, group=LANE,
               transpose_out=False, name="matmul_nt"):
    m, k = a.shape
    assert m % tm == 0 and n_cols % tn == 0 and wt.shape[1] == k and row0 % 8 == 0
    in_specs = [pl.BlockSpec((tm, k), lambda j, i: (i, 0)),
                pl.BlockSpec((pl.Element(tn), pl.Element(k)), lambda j, i: (pl.multiple_of(row0 + j * tn, 8), 0))]
    args = [a, wt]
    if mode == "groupnorm":
        gain = extra.reshape(-1, 1) if transpose_out else extra.reshape(1, -1)
        in_specs.append(pl.BlockSpec((tn, 1), lambda j, i: (j, 0)) if transpose_out
                        else pl.BlockSpec((1, tn), lambda j, i: (0, j)))
        args.append(gain.astype(F32))
    if transpose_out:
        out_spec = pl.BlockSpec((tn, tm), lambda j, i: (j, i))
        out_shape = jax.ShapeDtypeStruct((n_cols, m), out_dtype)
    else:
        out_spec = pl.BlockSpec((tm, tn), lambda j, i: (i, j))
        out_shape = jax.ShapeDtypeStruct((m, n_cols), out_dtype)
    return pl.pallas_call(
        functools.partial(_mm_nt_kernel, mode=mode, group=group, transpose_out=transpose_out),
        grid=(n_cols // tn, m // tm),
        in_specs=in_specs,
        out_specs=out_spec,
        out_shape=out_shape,
        scratch_shapes=[pltpu.VMEM((tn, k), BF16)],
        compiler_params=_cp(("arbitrary", "arbitrary")),
        name=name,
    )(*args)


def _softmax_pv(idx, s, vt_blk, m_ref, l_ref, acc_ref):
    m_prev = m_ref[idx]
    m_new = jnp.maximum(m_prev, jnp.max(s, axis=0, keepdims=True))
    alpha = jnp.exp2(m_prev - m_new)
    p = jnp.exp2(s - m_new)
    l_ref[idx] = alpha * l_ref[idx] + jnp.sum(p, axis=0, keepdims=True)
    acc_ref[idx] = alpha * acc_ref[idx] + jnp.dot(vt_blk, p.astype(BF16), preferred_element_type=F32)
    m_ref[idx] = m_new


def _block_offset(kb, t):
    return kb * t if isinstance(kb, int) else pl.multiple_of(kb * t, t)


def _init_stats(m_ref, l_ref, acc_ref):
    m_ref[...] = jnp.full(m_ref.shape, NEG_INF, F32)
    l_ref[...] = jnp.zeros(l_ref.shape, F32)
    acc_ref[...] = jnp.zeros(acc_ref.shape, F32)


def _diff_attn_kernel(lq1_ref, lk1_ref, lq2_ref, lk2_ref, q1_ref, q2_ref, k1_ref, k2_ref, vt_ref,
                      b1_ref, b2_ref, g_ref, o_ref, sa_ref, sb_ref, m_ref, l_ref, acc_ref, *, lam_init):
    _, tk, tq = sa_ref.shape
    i = pl.program_id(1)
    qts = (q1_ref[...], q2_ref[...])
    ks = (k1_ref, k2_ref)
    bs = (b1_ref, b2_ref)
    _init_stats(m_ref, l_ref, acc_ref)

    def scores(kb, dst):
        off = _block_offset(kb, tk)
        for mp in range(2):
            dst[mp] = jnp.dot(ks[mp][pl.ds(off, tk), :], qts[mp], preferred_element_type=F32)

    def consume(kb, src, strip_off):
        off = _block_offset(kb, tk)
        vt_blk = vt_ref[:, pl.ds(off, tk)]
        for mp in range(2):
            s = src[mp]
            if strip_off is not None:
                s = s + bs[mp][0, :, strip_off:strip_off + tq]
            _softmax_pv(mp, s, vt_blk, m_ref, l_ref, acc_ref)

    scores(0, sa_ref)

    def far_pair(j, c):
        kb = 2 * j
        scores(kb + 1, sb_ref)
        consume(kb, sa_ref, None)
        scores(kb + 2, sa_ref)
        consume(kb + 1, sb_ref, None)
        return c

    lax.fori_loop(0, jnp.maximum(i - 1, 0), far_pair, 0)

    @pl.when(i >= 1)
    def _():
        scores(2 * i - 1, sb_ref)
        consume(2 * i - 2, sa_ref, None)
        scores(2 * i, sa_ref)
        consume(2 * i - 1, sb_ref, 2 * tk)

    scores(2 * i + 1, sb_ref)
    consume(2 * i, sa_ref, tk)
    consume(2 * i + 1, sb_ref, 0)

    lam = (jnp.exp(jnp.sum(lq1_ref[...] * lk1_ref[...], axis=-1, keepdims=True))
           - jnp.exp(jnp.sum(lq2_ref[...] * lk2_ref[...], axis=-1, keepdims=True)) + lam_init)
    o = acc_ref[0] / l_ref[0] - lam * (acc_ref[1] / l_ref[1])
    ms = jnp.mean(o * o, axis=0, keepdims=True)
    o = (o * lax.rsqrt(ms + NORM_EPS) * g_ref[...]) * (1.0 - lam_init)
    o_ref[...] = o.T.astype(o_ref.dtype)


def _diff_attention(q_t, k, v_t, bias_strips, lam_vecs, subln_g, lam_init):
    s = k.shape[0]
    tk = ATT_T
    tq = 2 * tk
    assert s % tq == 0 and bias_strips.shape[1:] == (tk, 4 * tk)
    hd, vd = DIFF_HEAD_DIM, DIFF_V_DIM
    vec = pl.BlockSpec((1, hd), lambda h, i: (0, 0))
    in_specs = [vec, vec, vec, vec,
                pl.BlockSpec((hd, tq), lambda h, i: (h, i)),
                pl.BlockSpec((hd, tq), lambda h, i: (DIFF_HEADS + h, i)),
                pl.BlockSpec((s, hd), lambda h, i: (0, h)),
                pl.BlockSpec((s, hd), lambda h, i: (0, DIFF_HEADS + h)),
                pl.BlockSpec((vd, s), lambda h, i: (h, 0)),
                pl.BlockSpec((1, tk, 4 * tk), lambda h, i: (h, 0, 0), pipeline_mode=pl.Buffered(1)),
                pl.BlockSpec((1, tk, 4 * tk), lambda h, i: (DIFF_HEADS + h, 0, 0), pipeline_mode=pl.Buffered(1)),
                pl.BlockSpec((vd, 1), lambda h, i: (0, 0))]
    return pl.pallas_call(
        functools.partial(_diff_attn_kernel, lam_init=lam_init),
        grid=(DIFF_HEADS, s // tq),
        in_specs=in_specs,
        out_specs=pl.BlockSpec((tq, vd), lambda h, i: (i, h)),
        out_shape=jax.ShapeDtypeStruct((s, DIFF_WIDTH), BF16),
        scratch_shapes=[pltpu.VMEM((2, tk, tq), F32), pltpu.VMEM((2, tk, tq), F32),
                        pltpu.VMEM((2, 1, tq), F32), pltpu.VMEM((2, 1, tq), F32),
                        pltpu.VMEM((2, vd, tq), F32)],
        compiler_params=_cp(("arbitrary", "arbitrary")),
        name="diff_attention",
    )(*lam_vecs, q_t, q_t, k, k, v_t, bias_strips, bias_strips, subln_g.reshape(vd, 1).astype(F32))


def _t5_bucket(dist):
    n = jnp.maximum(dist, 0)
    max_exact = REL_BUCKETS // 2
    nf = jnp.maximum(n, 1).astype(F32)
    large = max_exact + (jnp.log(nf / max_exact) / math.log(REL_MAX_DIST / max_exact)
                         * (REL_BUCKETS - max_exact)).astype(jnp.int32)
    large = jnp.minimum(large, REL_BUCKETS - 1)
    return jnp.where(n < max_exact, n, large)


def _diff_bias_strips(rel_bias, tk):
    assert tk >= REL_MAX_DIST
    width = 4 * tk
    n = width + tk
    table = rel_bias.astype(F32)
    table = (table - table[REL_BUCKETS - 1:REL_BUCKETS]) * LOG2E
    kk = jnp.arange(n, dtype=jnp.int32)
    dist = kk - tk
    onehot = (_t5_bucket(dist)[:, None] == jnp.arange(REL_BUCKETS, dtype=jnp.int32)[None, :]).astype(F32)
    g = jnp.einsum('kb,bm->mk', onehot, table, precision=lax.Precision.HIGHEST)
    w = jnp.where(((dist >= 0) & (kk < width))[None, :], g, NEG_INF)
    maps = w.shape[0]

    def strip_kernel(w_ref, o_ref):
        rows = jnp.broadcast_to(w_ref[0], (tk, n))
        o_ref[0] = pltpu.roll(rows, 0, 1, stride=1, stride_axis=0)[:, :width]

    return pl.pallas_call(
        strip_kernel,
        grid=(maps,),
        in_specs=[pl.BlockSpec((1, 1, n), lambda m: (m, 0, 0))],
        out_specs=pl.BlockSpec((1, tk, width), lambda m: (m, 0, 0)),
        out_shape=jax.ShapeDtypeStruct((maps, tk, width), F32),
        compiler_params=_cp(("parallel",)),
        name="bias_strips",
    )(w.reshape(maps, 1, n))


def _rope_apply(tv, c_ref, s1_ref, s2_ref):
    return (tv * c_ref[...] + pltpu.roll(tv, 96, 1) * s1_ref[...] + pltpu.roll(tv, 32, 1) * s2_ref[...])


def _mla_q_kernel(cq_ref, g_ref, w_ref, qg_ref, c_ref, s1_ref, s2_ref, o_ref, xg_ref):
    h = pl.program_id(1)

    @pl.when(h == 0)
    def _():
        c = cq_ref[...].astype(F32)
        r = lax.rsqrt(jnp.mean(c * c, axis=-1, keepdims=True) + NORM_EPS)
        xg_ref[...] = (c * r * g_ref[...]).astype(BF16)

    ug = jnp.dot(xg_ref[...], w_ref[...], preferred_element_type=F32)
    for hh in range(MLA_HEAD_GROUP):
        u = ug[:, hh * MLA_QK_PAD:(hh + 1) * MLA_QK_PAD]
        ms = jnp.sum(u * u, axis=-1, keepdims=True) * (1.0 / MLA_QK_DIM)
        qn = u * lax.rsqrt(ms + NORM_EPS) * qg_ref[...]
        o_ref[hh, :MLA_NOPE_DIM, :] = qn[:, :MLA_NOPE_DIM].T.astype(o_ref.dtype)
        o_ref[hh, MLA_NOPE_DIM:, :] = _rope_apply(qn[:, MLA_NOPE_DIM:], c_ref, s1_ref, s2_ref).T.astype(o_ref.dtype)


def _mla_q_prep(rest, cq_g, w_uq_pad, qg_pad, rope_tabs, tm):
    s = rest.shape[0]
    hg = MLA_HEAD_GROUP
    tab = pl.BlockSpec((tm, LANE), lambda i, h: (i, 0))
    return pl.pallas_call(
        _mla_q_kernel,
        grid=(s // tm, MLA_HEADS // hg),
        in_specs=[pl.BlockSpec((tm, MLA_Q_RANK), lambda i, h: (i, R_CQ // MLA_Q_RANK)),
                  pl.BlockSpec((1, MLA_Q_RANK), lambda i, h: (0, 0)),
                  pl.BlockSpec((MLA_Q_RANK, hg * MLA_QK_PAD), lambda i, h: (0, h)),
                  pl.BlockSpec((1, MLA_QK_PAD), lambda i, h: (0, 0)),
                  tab, tab, tab],
        out_specs=pl.BlockSpec((hg, MLA_QK_PAD, tm), lambda i, h: (h, 0, i)),
        out_shape=jax.ShapeDtypeStruct((MLA_HEADS, MLA_QK_PAD, s), BF16),
        scratch_shapes=[pltpu.VMEM((tm, MLA_Q_RANK), BF16)],
        compiler_params=_cp(("arbitrary", "arbitrary")),
        name="mla_q_prep",
    )(rest, cq_g, w_uq_pad, qg_pad, *rope_tabs)


def _mla_kv_kernel(ckv_ref, kr_ref, g_ref, w_ref, kgn_ref, kgr_ref, c_ref, s1_ref, s2_ref,
                   k_ref, vt_ref, xg_ref):
    h = pl.program_id(1)

    @pl.when(h == 0)
    def _():
        c = ckv_ref[...].astype(F32)
        r = lax.rsqrt(jnp.mean(c * c, axis=-1, keepdims=True) + NORM_EPS)
        xg_ref[...] = (c * r * g_ref[...]).astype(BF16)

    hw = MLA_NOPE_DIM + MLA_V_DIM
    kvg = jnp.dot(xg_ref[...], w_ref[...].astype(BF16), preferred_element_type=F32)
    lane = lax.broadcasted_iota(jnp.int32, kr_ref.shape, 1)
    kr = jnp.where(lane < MLA_ROPE_DIM, kr_ref[...].astype(F32), 0.0)
    kr_ss = jnp.sum(kr * kr, axis=-1, keepdims=True)
    for hh in range(MLA_HEAD_GROUP):
        kn = kvg[:, hh * hw:hh * hw + MLA_NOPE_DIM]
        ms = (jnp.sum(kn * kn, axis=-1, keepdims=True) + kr_ss) * (1.0 / MLA_QK_DIM)
        rs = lax.rsqrt(ms + NORM_EPS)
        k_ref[hh, :, :MLA_NOPE_DIM] = (kn * rs * kgn_ref[...]).astype(k_ref.dtype)
        k_ref[hh, :, MLA_NOPE_DIM:] = _rope_apply(kr * rs * kgr_ref[...], c_ref, s1_ref, s2_ref).astype(k_ref.dtype)
        vt_ref[hh] = kvg[:, hh * hw + MLA_NOPE_DIM:(hh + 1) * hw].T.astype(vt_ref.dtype)


def _mla_kv_prep(rest, ckv_g, w_ukv, kg_nope, kg_rope_pad, rope_tabs, tm):
    s = rest.shape[0]
    hg = MLA_HEAD_GROUP
    tab = pl.BlockSpec((tm, LANE), lambda i, h: (i, 0))
    hw = MLA_NOPE_DIM + MLA_V_DIM
    return pl.pallas_call(
        _mla_kv_kernel,
        grid=(s // tm, MLA_HEADS // hg),
        in_specs=[pl.BlockSpec((tm, MLA_KV_RANK), lambda i, h: (i, R_CKV // MLA_KV_RANK)),
                  pl.BlockSpec((tm, LANE), lambda i, h: (i, R_KROPE // LANE)),
                  pl.BlockSpec((1, MLA_KV_RANK), lambda i, h: (0, 0)),
                  pl.BlockSpec((MLA_KV_RANK, hg * hw), lambda i, h: (0, h)),
                  pl.BlockSpec((1, LANE), lambda i, h: (0, 0)),
                  pl.BlockSpec((1, LANE), lambda i, h: (0, 0)),
                  tab, tab, tab],
        out_specs=[pl.BlockSpec((hg, tm, MLA_QK_PAD), lambda i, h: (h, i, 0)),
                   pl.BlockSpec((hg, MLA_V_DIM, tm), lambda i, h: (h, 0, i))],
        out_shape=[jax.ShapeDtypeStruct((MLA_HEADS, s, MLA_QK_PAD), BF16),
                   jax.ShapeDtypeStruct((MLA_HEADS, MLA_V_DIM, s), BF16)],
        scratch_shapes=[pltpu.VMEM((tm, MLA_KV_RANK), BF16)],
        compiler_params=_cp(("arbitrary", "arbitrary")),
        name="mla_kv_prep",
    )(rest, rest, ckv_g, w_ukv, kg_nope, kg_rope_pad, *rope_tabs)


def _mla_attn_kernel(qt_ref, k_ref, vt_ref, o_ref, sa_ref, sb_ref, m_ref, l_ref, acc_ref):
    tk, tq = sa_ref.shape
    i = pl.program_id(1)
    qt = qt_ref[0]
    _init_stats(m_ref, l_ref, acc_ref)

    def scores(kb, dst):
        off = _block_offset(kb, tk)
        dst[...] = jnp.dot(k_ref[0, pl.ds(off, tk), :], qt, preferred_element_type=F32)

    def consume(kb, src, diag_shift):
        off = _block_offset(kb, tk)
        s = src[...]
        if diag_shift is not None:
            krow = lax.broadcasted_iota(jnp.int32, s.shape, 0)
            qcol = lax.broadcasted_iota(jnp.int32, s.shape, 1)
            s = jnp.where(krow + diag_shift <= qcol, s, NEG_INF)
        _softmax_pv(0, s, vt_ref[0, :, pl.ds(off, tk)], m_ref, l_ref, acc_ref)

    scores(0, sa_ref)

    def far_pair(j, c):
        kb = 2 * j
        scores(kb + 1, sb_ref)
        consume(kb, sa_ref, None)
        scores(kb + 2, sa_ref)
        consume(kb + 1, sb_ref, None)
        return c

    lax.fori_loop(0, i, far_pair, 0)
    scores(2 * i + 1, sb_ref)
    consume(2 * i, sa_ref, 0)
    consume(2 * i + 1, sb_ref, tk)
    o_ref[...] = (acc_ref[0] / l_ref[0]).T.astype(o_ref.dtype)


def _mla_attention(q_t, k, v_t):
    s = k.shape[1]
    tk = ATT_T
    tq = 2 * tk
    assert s % tq == 0
    return pl.pallas_call(
        _mla_attn_kernel,
        grid=(MLA_HEADS, s // tq),
        in_specs=[pl.BlockSpec((1, MLA_QK_PAD, tq), lambda h, i: (h, 0, i)),
                  pl.BlockSpec((1, s, MLA_QK_PAD), lambda h, i: (h, 0, 0)),
                  pl.BlockSpec((1, MLA_V_DIM, s), lambda h, i: (h, 0, 0))],
        out_specs=pl.BlockSpec((tq, MLA_V_DIM), lambda h, i: (i, h)),
        out_shape=jax.ShapeDtypeStruct((s, MLA_WIDTH), BF16),
        scratch_shapes=[pltpu.VMEM((tk, tq), F32), pltpu.VMEM((tk, tq), F32),
                        pltpu.VMEM((1, 1, tq), F32), pltpu.VMEM((1, 1, tq), F32),
                        pltpu.VMEM((1, MLA_V_DIM, tq), F32)],
        compiler_params=_cp(("arbitrary", "arbitrary")),
        name="mla_attention",
    )(q_t, k, v_t)


def _qk_nt(q, k_blk):
    return lax.dot_general(q, k_blk, (((1,), (1,)), ((), ())), preferred_element_type=F32)


def _mem_attn_kernel(q_ref, k_ref, v_ref, qg_ref, o_ref):
    shift = R_MQ % MEM_WINDOW
    qall = q_ref[...].astype(F32)[:, shift:shift + MEM_WIDTH]
    for h in range(MEM_HEADS):
        lo = h * MEM_HEAD_DIM
        qh = qall[:, lo:lo + MEM_HEAD_DIM]
        ms = jnp.mean(qh * qh, axis=-1, keepdims=True)
        qn = (qh * lax.rsqrt(ms + NORM_EPS) * qg_ref[...]).astype(BF16)
        s = _qk_nt(qn, k_ref[:, lo:lo + MEM_HEAD_DIM])
        p = jnp.exp(s - jnp.max(s, axis=-1, keepdims=True))
        l = jnp.sum(p, axis=-1, keepdims=True)
        o = jnp.dot(p.astype(BF16), v_ref[:, lo:lo + MEM_HEAD_DIM], preferred_element_type=F32)
        o_ref[:, lo:lo + MEM_HEAD_DIM] = (o / l).astype(o_ref.dtype)


def _mem_attention(rest, k_mem, v_mem, qg_scaled, tm):
    s = rest.shape[0]
    n_mem = k_mem.shape[0]
    assert R_MQ % MEM_WINDOW + MEM_WIDTH <= MEM_WINDOW
    return pl.pallas_call(
        _mem_attn_kernel,
        grid=(s // tm,),
        in_specs=[pl.BlockSpec((tm, MEM_WINDOW), lambda i: (i, R_MQ // MEM_WINDOW)),
                  pl.BlockSpec((n_mem, MEM_WIDTH), lambda i: (0, 0)),
                  pl.BlockSpec((n_mem, MEM_WIDTH), lambda i: (0, 0)),
                  pl.BlockSpec((1, MEM_HEAD_DIM), lambda i: (0, 0))],
        out_specs=pl.BlockSpec((tm, MEM_WIDTH), lambda i: (i, 0)),
        out_shape=jax.ShapeDtypeStruct((s, MEM_WIDTH), BF16),
        compiler_params=_cp(("parallel",)),
        name="mem_attention",
    )(rest, k_mem, v_mem, qg_scaled)


def _mix_kernel(od_ref, om_ref, oc_ref, wd_ref, wm_ref, wc_ref, g0_ref, g1_ref, g2_ref, o_ref):
    yd = jnp.dot(od_ref[...], wd_ref[...], preferred_element_type=F32)
    ym = jnp.dot(om_ref[...], wm_ref[...], preferred_element_type=F32)
    yc = jnp.dot(oc_ref[...], wc_ref[...], preferred_element_type=F32)
    mixed = (g0_ref[...].astype(F32) * yd + g1_ref[...].astype(F32) * ym) + g2_ref[...].astype(F32) * yc
    o_ref[...] = mixed.astype(o_ref.dtype)


def _mix(o_diff, o_mla, o_mem, w_d, w_m, w_c, gates, tm, tn):
    s = o_diff.shape[0]
    d = w_d.shape[1]
    nj = d // tn
    return pl.pallas_call(
        _mix_kernel,
        grid=(s // tm, nj),
        in_specs=[pl.BlockSpec((tm, o_diff.shape[1]), lambda i, j: (i, 0)),
                  pl.BlockSpec((tm, o_mla.shape[1]), lambda i, j: (i, 0)),
                  pl.BlockSpec((tm, o_mem.shape[1]), lambda i, j: (i, 0)),
                  pl.BlockSpec((w_d.shape[0], tn), lambda i, j: (0, j)),
                  pl.BlockSpec((w_m.shape[0], tn), lambda i, j: (0, j)),
                  pl.BlockSpec((w_c.shape[0], tn), lambda i, j: (0, j)),
                  pl.BlockSpec((tm, tn), lambda i, j: (i, j)),
                  pl.BlockSpec((tm, tn), lambda i, j: (i, nj + j)),
                  pl.BlockSpec((tm, tn), lambda i, j: (i, 2 * nj + j))],
        out_specs=pl.BlockSpec((tm, tn), lambda i, j: (i, j)),
        out_shape=jax.ShapeDtypeStruct((s, d), BF16),
        compiler_params=_cp(("arbitrary", "arbitrary")),
        name="gated_mix",
    )(o_diff, o_mla, o_mem, w_d, w_m, w_c, gates, gates, gates)


def _router_kernel(x_ref, g_ref, w_ref, b_ref, h_ref, r_ref):
    x = x_ref[...]
    ms = jnp.mean(x * x, axis=-1, keepdims=True)
    h = x * lax.rsqrt(ms + NORM_EPS) * g_ref[...]
    h_ref[...] = _pack_bf16_pairs(h)
    logits = jnp.dot(h, w_ref[...], preferred_element_type=F32, precision=lax.Precision.HIGHEST) + b_ref[...]
    lane = lax.broadcasted_iota(jnp.int32, logits.shape, 1)
    lane_f = lane.astype(F32)
    big = float(4 * ROUTE_W)
    lg = jnp.where(lane < N_GROUPS, logits, -jnp.inf)
    gmax = jnp.max(lg, axis=-1, keepdims=True)
    gidx = jnp.min(jnp.where(lg == gmax, lane_f, big), axis=-1, keepdims=True)
    pg_top = 1.0 / jnp.sum(jnp.exp(lg - gmax), axis=-1, keepdims=True)
    e_lane = lane - N_GROUPS
    lane_group = jnp.right_shift(e_lane, 3).astype(F32)
    in_group = (e_lane >= 0) & (e_lane < N_EXPERTS) & (lane_group == gidx)
    le = jnp.where(in_group, logits, -jnp.inf)
    e1 = jnp.max(le, axis=-1, keepdims=True)
    i1 = jnp.min(jnp.where(le == e1, lane_f, big), axis=-1, keepdims=True)
    le2 = jnp.where(lane_f == i1, -jnp.inf, le)
    e2 = jnp.max(le2, axis=-1, keepdims=True)
    i2 = jnp.min(jnp.where(le2 == e2, lane_f, big), axis=-1, keepdims=True)
    w2 = jnp.exp(e2 - e1)
    inv = 1.0 / (1.0 + w2)
    gate1 = pg_top * inv
    gate2 = pg_top * (w2 * inv)
    out = jnp.where(lane == 0, i1 - N_GROUPS,
                    jnp.where(lane == 1, i2 - N_GROUPS,
                              jnp.where(lane == 2, gate1, jnp.where(lane == 3, gate2, 0.0))))
    r_ref[...] = out


def _router(x1, g, w_r, b_r, tm):
    s, d = x1.shape
    return pl.pallas_call(
        _router_kernel,
        grid=(s // tm,),
        in_specs=[pl.BlockSpec((tm, d), lambda i: (i, 0)),
                  pl.BlockSpec((1, d), lambda i: (0, 0)),
                  pl.BlockSpec((d, ROUTE_W), lambda i: (0, 0)),
                  pl.BlockSpec((1, ROUTE_W), lambda i: (0, 0))],
        out_specs=[pl.BlockSpec((tm, d // 2), lambda i: (i, 0)),
                   pl.BlockSpec((tm, ROUTE_W), lambda i: (i, 0))],
        out_shape=[jax.ShapeDtypeStruct((s, d // 2), jnp.uint32),
                   jax.ShapeDtypeStruct((s, ROUTE_W), F32)],
        compiler_params=_cp(("parallel",)),
        name="ffn_norm_router",
    )(x1, g, w_r, b_r)


def _pack_bf16_pairs(v):
    n = v.shape[1] // 2
    bits = lax.bitcast_convert_type(v.astype(BF16).astype(F32), jnp.uint32)
    return jnp.right_shift(bits[:, :n], jnp.uint32(16)) | bits[:, n:]


def _unpack_bf16_pairs(words):
    lo = lax.bitcast_convert_type(jnp.left_shift(words, jnp.uint32(16)), F32)
    hi = lax.bitcast_convert_type(words & jnp.uint32(0xFFFF0000), F32)
    return lo, hi


def _moe_kernel(be_ref, nr_ref, nu_ref, tok_ref, tokn_ref, h_ref, wg_ref, wu_ref, wd_ref, o_ref,
                xg_ref, xb_ref, a_ref, wgb_ref, wub_ref, wdb_ref, sem):
    b = pl.program_id(0)
    c = pl.program_id(1)
    nb = pl.num_programs(0)
    nr = nr_ref[b]
    half = xg_ref.shape[2]
    fh = wgb_ref.shape[1]

    def row_copy(src_row, slot, r):
        return pltpu.make_async_copy(h_ref.at[pl.ds(src_row, 1), :], xg_ref.at[slot, pl.ds(r, 1), :], sem.at[slot])

    def start_gather(tok, n_rows, slot):
        unroll = 8

        def issue(i, carry):
            for j in range(unroll):
                r = i * unroll + j
                row_copy(tok[0, 0, r], slot, r).start(priority=1)
            return carry
        lax.fori_loop(0, n_rows // unroll, issue, 0)

    def for_row_count(fn):
        for units in range(1, MOE_TB // MOE_RU + 1):
            @pl.when(nr == units)
            def _():
                fn(units * MOE_RU)

    @pl.when(c == 0)
    def _():
        slot = b % 2

        @pl.when(b == 0)
        def _():
            start_gather(tok_ref, nr * MOE_RU, 0)

        def drain(u, carry):
            pltpu.make_async_copy(h_ref.at[pl.ds(0, MOE_RU), :], xg_ref.at[slot, pl.ds(0, MOE_RU), :],
                                  sem.at[slot]).wait()
            return carry
        lax.fori_loop(0, nr, drain, 0)

        @pl.when(b + 1 < nb)
        def _():
            start_gather(tokn_ref, nr_ref[jnp.minimum(b + 1, nb - 1)] * MOE_RU, 1 - slot)

        def unpack(u, carry):
            r0 = pl.multiple_of(u * MOE_RU, MOE_RU)
            lo, hi = _unpack_bf16_pairs(xg_ref[slot, pl.ds(r0, MOE_RU), :])
            xb_ref[pl.ds(r0, MOE_RU), :half] = lo.astype(BF16)
            xb_ref[pl.ds(r0, MOE_RU), half:] = hi.astype(BF16)
            return carry
        lax.fori_loop(0, nr, unpack, 0)

    @pl.when((c < 2) & (nr > 0))
    def _():
        wgb_ref[...] = wg_ref[0].astype(BF16)
        wub_ref[...] = wu_ref[0].astype(BF16)

        def gate_up(m):
            x = xb_ref[:m, :]
            g = jnp.dot(x, wgb_ref[...], preferred_element_type=F32)
            u = jnp.dot(x, wub_ref[...], preferred_element_type=F32)
            a_ref[c, :m, :] = ((g * jax.nn.sigmoid(g)) * u).astype(BF16)

        for_row_count(gate_up)

    @pl.when((c >= 2) & (nr > 0))
    def _():
        wdb_ref[...] = wd_ref[0].astype(BF16)

        def down(m):
            yv = (jnp.dot(a_ref[0, :m, :], wdb_ref[:fh, :], preferred_element_type=F32)
                  + jnp.dot(a_ref[1, :m, :], wdb_ref[fh:, :], preferred_element_type=F32))
            o_ref[:m, :] = _pack_bf16_pairs(yv)
            if m < MOE_TB:
                o_ref[m:, :] = jnp.zeros((MOE_TB - m, o_ref.shape[1]), jnp.uint32)

        for_row_count(down)

    @pl.when((c >= 2) & (nr == 0))
    def _():
        o_ref[...] = jnp.zeros(o_ref.shape, jnp.uint32)


def _moe_experts(block_e, nsub, n_used, buf_tok, h2p, w_gate, w_up, w_down):
    n_blocks = block_e.shape[0]
    half = h2p.shape[1]
    d = 2 * half
    ff = w_gate.shape[2]
    fh = ff // 2
    dh = d // 2
    tok = buf_tok.reshape(n_blocks, 1, MOE_TB)

    def gate_up_idx(b, c, be, nr, nu):
        live = b < nu[0]
        return be[jnp.minimum(b, nu[0] - 1)], 0, jnp.where(live, jnp.minimum(c, 1), 1)

    def down_idx(b, c, be, nr, nu):
        bb = jnp.minimum(b, nu[0] - 1)
        cc = jnp.where(b < nu[0], c, MOE_PH - 1)
        e = jnp.where(cc >= 2, be[bb], be[jnp.maximum(bb - 1, 0)])
        return e, 0, jnp.where(cc == 2, 0, 1)

    def out_idx(b, c, be, nr, nu):
        ob = jnp.where(c >= 2, b, jnp.maximum(b - 1, 0))
        oc = jnp.where(c >= 2, c - 2, jnp.where(b > 0, 1, 0))
        return ob, oc

    grid_spec = pltpu.PrefetchScalarGridSpec(
        num_scalar_prefetch=3,
        grid=(n_blocks, MOE_PH),
        in_specs=[pl.BlockSpec((1, 1, MOE_TB), lambda b, c, be, nr, nu: (b, 0, 0), memory_space=pltpu.SMEM),
                  pl.BlockSpec((1, 1, MOE_TB), lambda b, c, be, nr, nu: (jnp.minimum(b + 1, n_blocks - 1), 0, 0),
                               memory_space=pltpu.SMEM),
                  pl.BlockSpec(memory_space=pl.ANY),
                  pl.BlockSpec((1, d, fh), gate_up_idx),
                  pl.BlockSpec((1, d, fh), gate_up_idx),
                  pl.BlockSpec((1, ff, dh), down_idx)],
        out_specs=pl.BlockSpec((MOE_TB, dh // 2), out_idx),
        scratch_shapes=[pltpu.VMEM((2, MOE_TB, half), jnp.uint32), pltpu.VMEM((MOE_TB, d), BF16),
                        pltpu.VMEM((2, MOE_TB, fh), BF16),
                        pltpu.VMEM((d, fh), BF16), pltpu.VMEM((d, fh), BF16), pltpu.VMEM((ff, dh), BF16),
                        pltpu.SemaphoreType.DMA((2,))],
    )
    return pl.pallas_call(
        _moe_kernel,
        grid_spec=grid_spec,
        out_shape=jax.ShapeDtypeStruct((n_blocks * MOE_TB, half), jnp.uint32),
        compiler_params=_cp(("arbitrary", "arbitrary")),
        name="moe_experts",
    )(block_e, nsub, n_used, tok, tok, h2p, w_gate, w_up, w_down)


def _combine_kernel(slot_ref, r_ref, x_ref, yb_ref, o_ref, g_ref, sem):
    tm = x_ref.shape[0]
    n = g_ref.shape[2]

    unroll = 4

    def issue(i, c):
        for j in range(unroll):
            r = i * unroll + j
            for k in range(TOP_K):
                sl = slot_ref[0, 0, r * TOP_K + k]
                pltpu.make_async_copy(yb_ref.at[pl.ds(sl, 1), :], g_ref.at[k, pl.ds(r, 1), :],
                                      sem.at[k]).start(priority=k)
        return c

    lax.fori_loop(0, tm // unroll, issue, 0)
    for k in range(TOP_K):
        pltpu.make_async_copy(yb_ref.at[pl.ds(0, tm), :], g_ref.at[k], sem.at[k]).wait()

    route = r_ref[...]
    w0 = route[:, TOP_K:TOP_K + 1]
    w1 = route[:, TOP_K + 1:TOP_K + 2]
    q = n // 2
    for hf in range(2):
        lo0, hi0 = _unpack_bf16_pairs(g_ref[0, :, hf * q:(hf + 1) * q])
        lo1, hi1 = _unpack_bf16_pairs(g_ref[1, :, hf * q:(hf + 1) * q])
        c0 = hf * n
        o_ref[:, c0:c0 + q] = x_ref[:, c0:c0 + q] + (lo0 * w0 + lo1 * w1)
        o_ref[:, c0 + q:c0 + n] = x_ref[:, c0 + q:c0 + n] + (hi0 * w0 + hi1 * w1)


def _combine(slots, route, x1, yb, tm):
    s, d = x1.shape
    return pl.pallas_call(
        _combine_kernel,
        grid=(s // tm,),
        in_specs=[pl.BlockSpec((1, 1, tm * TOP_K), lambda i: (i, 0, 0), memory_space=pltpu.SMEM),
                  pl.BlockSpec((tm, ROUTE_W), lambda i: (i, 0)),
                  pl.BlockSpec((tm, d), lambda i: (i, 0)),
                  pl.BlockSpec(memory_space=pl.ANY)],
        out_specs=pl.BlockSpec((tm, d), lambda i: (i, 0)),
        out_shape=jax.ShapeDtypeStruct((s, d), F32),
        scratch_shapes=[pltpu.VMEM((TOP_K, tm, d // 2), jnp.uint32), pltpu.SemaphoreType.DMA((TOP_K,))],
        compiler_params=_cp(("arbitrary",)),
        name="moe_combine",
    )(slots.reshape(s // tm, 1, tm * TOP_K), route, x1, yb)


def _dispatch_plan(route, s):
    a = s * TOP_K
    flat_e = route[:, :TOP_K].astype(jnp.int32).reshape(a)
    onehot = (flat_e[:, None] == jnp.arange(N_EXPERTS, dtype=jnp.int32)[None, :]).astype(jnp.int32)
    csum = jnp.cumsum(onehot, axis=0)
    counts = csum[-1]
    rank = jnp.sum((csum - onehot) * onehot, axis=1)
    padded = (counts + MOE_TB - 1) // MOE_TB * MOE_TB
    pad_end = jnp.cumsum(padded)
    pad_start = pad_end - padded
    dest = pad_start[flat_e] + rank
    n_blocks = a // MOE_TB + N_EXPERTS
    p_rows = n_blocks * MOE_TB
    buf_tok = jnp.zeros((p_rows,), jnp.int32).at[dest].set(jnp.arange(a, dtype=jnp.int32) // TOP_K)
    starts = jnp.arange(n_blocks, dtype=jnp.int32) * MOE_TB
    block_e = jnp.minimum(jnp.searchsorted(pad_end, starts, side='right'), N_EXPERTS - 1).astype(jnp.int32)
    valid = jnp.clip(counts[block_e] - (starts - pad_start[block_e]), 0, MOE_TB)
    valid = jnp.where(starts < pad_end[-1], valid, 0)
    nsub = ((valid + MOE_RU - 1) // MOE_RU).astype(jnp.int32)
    n_used = (pad_end[-1:] // MOE_TB).astype(jnp.int32)
    return block_e, nsub, n_used, buf_tok, dest.astype(jnp.int32)


def _rope_tables(positions):
    half = MLA_ROPE_DIM // 2
    inv_freq = ROPE_THETA ** (-jnp.arange(half, dtype=F32) / half)
    ang = positions.astype(F32)[:, None] * inv_freq[None, :]
    cos, sin = jnp.cos(ang), jnp.sin(ang)
    z = jnp.zeros_like(cos)
    c = jnp.concatenate([cos, cos, z, z], axis=-1)
    s1 = jnp.concatenate([-sin, z, z, z], axis=-1)
    s2 = jnp.concatenate([z, sin, z, z], axis=-1)
    return c, s1, s2


def kernel(x, mem, positions, rel_bias, mix_norm_g, w_in, diff_q_norm_g, diff_k_norm_g, diff_lambda_q1, diff_lambda_k1, diff_lambda_q2, diff_lambda_k2, diff_subln_g, mla_cq_norm_g, mla_ckv_norm_g, mla_w_uq, mla_w_ukv, mla_q_norm_g, mla_k_norm_g, mem_norm_g, mem_w_kv, mem_q_norm_g, mem_k_norm_g, w_o_diff, w_o_mla, w_o_mem, w_out, ffn_norm_g, w_route_group, b_route_group, w_route_expert, b_route_expert, w_exp_gate, w_exp_up, w_exp_down):
    b, s, d = x.shape
    assert b == 1 and s % ATT_T == 0
    depth = mix_norm_g.shape[0]
    xs = x.reshape(s, d)
    pos = positions.reshape(s)
    rope_tabs = _rope_tables(pos)
    row = lambda v: v.reshape(1, -1).astype(F32)

    for l in range(depth):
        lam_init = 0.8 - 0.6 * math.exp(-0.3 * l)
        h = _rmsnorm_rows(xs, mix_norm_g[l], 256)
        tn = 512
        q_gain = jnp.tile(diff_q_norm_g[l] * (DIFF_HEAD_DIM ** -0.5 * LOG2E), DIFF_MAPS)
        w_in_t = jnp.transpose(w_in[l])
        dq_t = _matmul_nt(h, w_in_t, row0=OFF_DQ, n_cols=DIFF_QK_WIDTH, tm=MM_TM, tn=tn, out_dtype=BF16,
                          mode="groupnorm", extra=q_gain, group=DIFF_HEAD_DIM, transpose_out=True,
                          name="diff_q_proj")
        dk = _matmul_nt(h, w_in_t, row0=OFF_DK, n_cols=DIFF_QK_WIDTH, tm=MM_TM, tn=tn, out_dtype=BF16,
                        mode="groupnorm", extra=jnp.tile(diff_k_norm_g[l], DIFF_MAPS), group=DIFF_HEAD_DIM,
                        name="diff_k_proj")
        dv_t = _matmul_nt(h, w_in_t, row0=OFF_DV, n_cols=DIFF_WIDTH, tm=MM_TM, tn=tn, out_dtype=BF16,
                          transpose_out=True, name="diff_v_proj")
        rest = _matmul_nt(h, w_in_t, row0=OFF_CQ, n_cols=REST_WIDTH, tm=MM_TM, tn=tn, out_dtype=BF16,
                          name="rest_proj")
        gates = _matmul_nt(h, w_in_t, row0=OFF_GATES, n_cols=3 * d, tm=MM_TM, tn=tn, out_dtype=BF16,
                           mode="sigmoid", name="gate_proj")

        bias_strips = _diff_bias_strips(rel_bias, ATT_T)
        lam_vecs = [row(diff_lambda_q1[l]), row(diff_lambda_k1[l]), row(diff_lambda_q2[l]), row(diff_lambda_k2[l])]
        o_diff = _diff_attention(dq_t, dk, dv_t, bias_strips, lam_vecs, diff_subln_g[l], lam_init)

        w_uq_heads = jnp.pad(
            mla_w_uq[l].reshape(MLA_Q_RANK, MLA_HEADS, MLA_QK_DIM),
            ((0, 0), (0, 0), (0, MLA_QK_PAD - MLA_QK_DIM))).reshape(MLA_Q_RANK, MLA_HEADS * MLA_QK_PAD).astype(BF16)
        qg_pad = jnp.pad(mla_q_norm_g[l] * (MLA_QK_DIM ** -0.5 * LOG2E),
                         (0, MLA_QK_PAD - MLA_QK_DIM)).reshape(1, -1).astype(F32)
        q_mla_t = _mla_q_prep(rest, row(mla_cq_norm_g[l]), w_uq_heads, qg_pad, rope_tabs, 512)
        kg = mla_k_norm_g[l]
        kg_nope = row(kg[:MLA_NOPE_DIM])
        kg_rope = jnp.pad(kg[MLA_NOPE_DIM:], (0, LANE - MLA_ROPE_DIM)).reshape(1, -1).astype(F32)
        k_mla, v_mla_t = _mla_kv_prep(rest, row(mla_ckv_norm_g[l]), mla_w_ukv[l], kg_nope, kg_rope, rope_tabs, 512)
        o_mla = _mla_attention(q_mla_t, k_mla, v_mla_t)

        n_mem = mem.shape[1]
        mem_h = _rmsnorm_rows(mem.reshape(n_mem, d), mem_norm_g[l], n_mem)
        k_mem = _matmul(mem_h, mem_w_kv[l], col0=0, n_cols=MEM_WIDTH, tm=n_mem, tn=512, out_dtype=BF16,
                        mode="groupnorm", extra=jnp.tile(mem_k_norm_g[l], MEM_HEADS), group=MEM_HEAD_DIM,
                        name="mem_k_proj")
        v_mem = _matmul(mem_h, mem_w_kv[l], col0=MEM_WIDTH, n_cols=MEM_WIDTH, tm=n_mem, tn=512, out_dtype=BF16,
                        name="mem_v_proj")
        o_mem = _mem_attention(rest, k_mem, v_mem, row(mem_q_norm_g[l] * MEM_HEAD_DIM ** -0.5), 512)

        mixed = _mix(o_diff, o_mla, o_mem, w_o_diff[l].astype(BF16), w_o_mla[l].astype(BF16),
                     w_o_mem[l].astype(BF16), gates, MM_TM, 512)
        x1 = _matmul(mixed, w_out[l], n_cols=d, tm=MM_TM, tn=512, out_dtype=F32, mode="residual", extra=xs,
                     name="out_proj")

        w_r = jnp.pad(jnp.concatenate([w_route_group[l], w_route_expert[l]], axis=1),
                      ((0, 0), (0, ROUTE_W - N_GROUPS - N_EXPERTS))).astype(F32)
        b_r = jnp.pad(jnp.concatenate([b_route_group[l], b_route_expert[l]]),
                      (0, ROUTE_W - N_GROUPS - N_EXPERTS)).reshape(1, -1).astype(F32)
        h2, route = _router(x1, row(ffn_norm_g[l]), w_r, b_r, 256)
        block_e, nsub, n_used, buf_tok, slots = _dispatch_plan(route, s)
        yb = _moe_experts(block_e, nsub, n_used, buf_tok, h2, w_exp_gate[l], w_exp_up[l], w_exp_down[l])
        xs = _combine(slots, route, x1, yb, 256)
    return xs.reshape(b, s, d)
```

```python
import functools
import math

import jax
import jax.numpy as jnp
from jax import lax
from jax.experimental import pallas as pl
from jax.experimental.pallas import tpu as pltpu

F32 = jnp.float32
BF16 = jnp.bfloat16

NORM_EPS = 1e-6
NEG_INF = -1e30
LOG2E = math.log2(math.e)

DIFF_HEADS = 6
DIFF_HEAD_DIM = 128
DIFF_V_DIM = 256
DIFF_MAPS = 12
DIFF_QK_WIDTH = 1536
DIFF_WIDTH = 1536
MLA_HEADS = 12
MLA_Q_RANK = 1536
MLA_KV_RANK = 512
MLA_NOPE_DIM = 128
MLA_ROPE_DIM = 64
MLA_QK_DIM = 192
MLA_QK_PAD = 256
MLA_V_DIM = 128
MLA_WIDTH = 1536
ROPE_THETA = 10000.0
MEM_HEADS = 4
MEM_HEAD_DIM = 256
MEM_WIDTH = 1024
REL_BUCKETS = 32
REL_MAX_DIST = 128
N_GROUPS = 8
EXPERTS_PER_GROUP = 8
N_EXPERTS = 64
TOP_K = 2
EXPERT_FF = 512

OFF_DQ = 0
OFF_DK = 1536
OFF_DV = 3072
OFF_CQ = 4608
OFF_CKV = 6144
OFF_KROPE = 6656
OFF_MQ = 6720
OFF_GATES = 7744
REST_WIDTH = 4096
R_CQ = OFF_CQ - OFF_CQ
R_CKV = OFF_CKV - OFF_CQ
R_KROPE = OFF_KROPE - OFF_CQ
R_MQ = OFF_MQ - OFF_CQ

LANE = 128
VMEM_LIMIT = 52 * 1024 * 1024

MM_TM = 1024
MLA_HEAD_GROUP = 4
ATT_T = 512
MOE_TB = 512
MOE_RU = 128
MOE_PH = 4
ROUTE_W = 128
MEM_WINDOW = 2048


def _cp(sem, vmem=VMEM_LIMIT):
    return pltpu.CompilerParams(dimension_semantics=sem, vmem_limit_bytes=vmem)


def _rmsnorm_kernel(x_ref, g_ref, o_ref):
    x = x_ref[...].astype(F32)
    ms = jnp.mean(x * x, axis=-1, keepdims=True)
    o_ref[...] = (x * lax.rsqrt(ms + NORM_EPS) * g_ref[...]).astype(o_ref.dtype)


def _rmsnorm_rows(x, g, tm, out_dtype=BF16):
    m, d = x.shape
    return pl.pallas_call(
        _rmsnorm_kernel,
        grid=(m // tm,),
        in_specs=[pl.BlockSpec((tm, d), lambda i: (i, 0)),
                  pl.BlockSpec((1, d), lambda i: (0, 0))],
        out_specs=pl.BlockSpec((tm, d), lambda i: (i, 0)),
        out_shape=jax.ShapeDtypeStruct((m, d), out_dtype),
        compiler_params=_cp(("parallel",)),
        name="rmsnorm_rows",
    )(x, g.reshape(1, d).astype(F32))


def _cast_shifted(w_ref, w2_ref, wb_ref, shift):
    k = w_ref.shape[0]
    rows = 256

    def body(c, carry):
        r0 = pl.multiple_of(c * rows, rows)
        main = w_ref[pl.ds(r0, rows), :]
        tail = w2_ref[pl.ds(r0, rows), :]
        wb_ref[pl.ds(r0, rows), :] = jnp.concatenate([main[:, shift:], tail[:, :shift]], axis=1).astype(BF16)
        return carry

    lax.fori_loop(0, k // rows, body, 0)


def _mm_kernel(*refs, mode, cast, group, shift, transpose_out):
    a_ref, w_ref = refs[0], refs[1]
    pos = 2
    w2_ref = None
    if shift:
        w2_ref = refs[pos]
        pos += 1
    extra = None
    if mode in ("groupnorm", "residual"):
        extra = refs[pos]
        pos += 1
    o_ref = refs[pos]
    wb_ref = refs[pos + 1] if cast else None
    i = pl.program_id(1)
    if cast:
        @pl.when(i == 0)
        def _():
            if shift:
                _cast_shifted(w_ref, w2_ref, wb_ref, shift)
            else:
                wb_ref[...] = w_ref[...].astype(BF16)
        w = wb_ref[...]
    else:
        w = w_ref[...]
    acc = jnp.dot(a_ref[...], w, preferred_element_type=F32)
    if transpose_out:
        acc = acc.T
    if mode == "plain":
        o_ref[...] = acc.astype(o_ref.dtype)
    elif mode == "sigmoid":
        o_ref[...] = jax.nn.sigmoid(acc).astype(o_ref.dtype)
    elif mode == "residual":
        o_ref[...] = (extra[...] + acc).astype(o_ref.dtype)
    elif mode == "groupnorm":
        tn = w.shape[1]
        for c in range(tn // group):
            sl = slice(c * group, (c + 1) * group)
            if transpose_out:
                blk = acc[sl, :]
                ms = jnp.mean(blk * blk, axis=0, keepdims=True)
                o_ref[sl, :] = (blk * lax.rsqrt(ms + NORM_EPS) * extra[sl, :]).astype(o_ref.dtype)
            else:
                blk = acc[:, sl]
                ms = jnp.mean(blk * blk, axis=-1, keepdims=True)
                o_ref[:, sl] = (blk * lax.rsqrt(ms + NORM_EPS) * extra[:, sl]).astype(o_ref.dtype)


def _matmul(a, w, *, n_cols, tm, tn, out_dtype, col0=0, mode="plain", extra=None, group=LANE,
            transpose_out=False, name="matmul"):
    m, k = a.shape
    assert m % tm == 0 and n_cols % tn == 0 and w.shape[0] == k
    cast = w.dtype != BF16
    base, shift = divmod(col0, tn)
    assert shift <= LANE and (shift == 0 or cast)
    in_specs = [pl.BlockSpec((tm, k), lambda j, i: (i, 0)),
                pl.BlockSpec((k, tn), lambda j, i: (0, base + j))]
    args = [a, w]
    if shift:
        in_specs.append(pl.BlockSpec((k, LANE), lambda j, i: (0, (base + j + 1) * (tn // LANE))))
        args.append(w)
    if mode == "groupnorm":
        gain = extra.reshape(-1, 1) if transpose_out else extra.reshape(1, -1)
        in_specs.append(pl.BlockSpec((tn, 1), lambda j, i: (j, 0)) if transpose_out
                        else pl.BlockSpec((1, tn), lambda j, i: (0, j)))
        args.append(gain.astype(F32))
    elif mode == "residual":
        assert not transpose_out
        in_specs.append(pl.BlockSpec((tm, tn), lambda j, i: (i, j)))
        args.append(extra)
    if transpose_out:
        out_spec = pl.BlockSpec((tn, tm), lambda j, i: (j, i))
        out_shape = jax.ShapeDtypeStruct((n_cols, m), out_dtype)
    else:
        out_spec = pl.BlockSpec((tm, tn), lambda j, i: (i, j))
        out_shape = jax.ShapeDtypeStruct((m, n_cols), out_dtype)
    scratch = [pltpu.VMEM((k, tn), BF16)] if cast else []
    return pl.pallas_call(
        functools.partial(_mm_kernel, mode=mode, cast=cast, group=group, shift=shift, transpose_out=transpose_out),
        grid=(n_cols // tn, m // tm),
        in_specs=in_specs,
        out_specs=out_spec,
        out_shape=out_shape,
        scratch_shapes=scratch,
        compiler_params=_cp(("arbitrary", "arbitrary")),
        name=name,
    )(*args)


def _mm_nt_kernel(*refs, mode, group, transpose_out):
    a_ref, wt_ref = refs[0], refs[1]
    extra = refs[2] if mode == "groupnorm" else None
    o_ref, wb_ref = refs[-2], refs[-1]
    i = pl.program_id(1)

    @pl.when(i == 0)
    def _():
        wb_ref[...] = wt_ref[...].astype(BF16)

    contract_last = (((1,), (1,)), ((), ()))
    if transpose_out:
        acc = lax.dot_general(wb_ref[...], a_ref[...], contract_last, preferred_element_type=F32)
    else:
        acc = lax.dot_general(a_ref[...], wb_ref[...], contract_last, preferred_element_type=F32)
    if mode == "plain":
        o_ref[...] = acc.astype(o_ref.dtype)
    elif mode == "sigmoid":
        o_ref[...] = jax.nn.sigmoid(acc).astype(o_ref.dtype)
    elif mode == "groupnorm":
        tn = wb_ref.shape[0]
        for c in range(tn // group):
            sl = slice(c * group, (c + 1) * group)
            if transpose_out:
                blk = acc[sl, :]
                ms = jnp.mean(blk * blk, axis=0, keepdims=True)
                o_ref[sl, :] = (blk * lax.rsqrt(ms + NORM_EPS) * extra[sl, :]).astype(o_ref.dtype)
            else:
                blk = acc[:, sl]
                ms = jnp.mean(blk * blk, axis=-1, keepdims=True)
                o_ref[:, sl] = (blk * lax.rsqrt(ms + NORM_EPS) * extra[:, sl]).astype(o_ref.dtype)


def _matmul_nt(a, wt, *, row0, n_cols, tm, tn, out_dtype, mode="plain", extra=None, group=LANE,
               transpose_out=False, name="matmul_nt"):
    m, k = a.shape
    assert m % tm == 0 and n_cols % tn == 0 and wt.shape[1] == k and row0 % 8 == 0
    in_specs = [pl.BlockSpec((tm, k), lambda j, i: (i, 0)),
                pl.BlockSpec((pl.Element(tn), pl.Element(k)), lambda j, i: (pl.multiple_of(row0 + j * tn, 8), 0))]
    args = [a, wt]
    if mode == "groupnorm":
        gain = extra.reshape(-1, 1) if transpose_out else extra.reshape(1, -1)
        in_specs.append(pl.BlockSpec((tn, 1), lambda j, i: (j, 0)) if transpose_out
                        else pl.BlockSpec((1, tn), lambda j, i: (0, j)))
        args.append(gain.astype(F32))
    if transpose_out:
        out_spec = pl.BlockSpec((tn, tm), lambda j, i: (j, i))
        out_shape = jax.ShapeDtypeStruct((n_cols, m), out_dtype)
    else:
        out_spec = pl.BlockSpec((tm, tn), lambda j, i: (i, j))
        out_shape = jax.ShapeDtypeStruct((m, n_cols), out_dtype)
    return pl.pallas_call(
        functools.partial(_mm_nt_kernel, mode=mode, group=group, transpose_out=transpose_out),
        grid=(n_cols // tn, m // tm),
        in_specs=in_specs,
        out_specs=out_spec,
        out_shape=out_shape,
        scratch_shapes=[pltpu.VMEM((tn, k), BF16)],
        compiler_params=_cp(("arbitrary", "arbitrary")),
        name=name,
    )(*args)


def _softmax_pv(idx, s, vt_blk, m_ref, l_ref, acc_ref, col0=0):
    cols = slice(col0, col0 + s.shape[1])
    m_prev = m_ref[idx, :, cols]
    m_new = jnp.maximum(m_prev, jnp.max(s, axis=0, keepdims=True))
    alpha = jnp.exp2(m_prev - m_new)
    p = jnp.exp2(s - m_new)
    l_ref[idx, :, cols] = alpha * l_ref[idx, :, cols] + jnp.sum(p, axis=0, keepdims=True)
    acc_ref[idx, :, cols] = (alpha * acc_ref[idx, :, cols]
                             + jnp.dot(vt_blk, p.astype(BF16), preferred_element_type=F32))
    m_ref[idx, :, cols] = m_new


def _block_offset(kb, t):
    return kb * t if isinstance(kb, int) else pl.multiple_of(kb * t, t)


def _init_stats(m_ref, l_ref, acc_ref):
    m_ref[...] = jnp.full(m_ref.shape, NEG_INF, F32)
    l_ref[...] = jnp.zeros(l_ref.shape, F32)
    acc_ref[...] = jnp.zeros(acc_ref.shape, F32)


def _diff_attn_kernel(lq1_ref, lk1_ref, lq2_ref, lk2_ref, q1_ref, q2_ref, k1_ref, k2_ref, vt_ref,
                      b1_ref, b2_ref, g_ref, o_ref, sa_ref, sb_ref, m_ref, l_ref, acc_ref, *, lam_init):
    _, tk, tq = sa_ref.shape
    i = pl.program_id(1)
    qts = (q1_ref[...], q2_ref[...])
    ks = (k1_ref, k2_ref)
    bs = (b1_ref, b2_ref)
    _init_stats(m_ref, l_ref, acc_ref)

    def scores(kb, dst):
        off = _block_offset(kb, tk)
        for mp in range(2):
            dst[mp] = jnp.dot(ks[mp][pl.ds(off, tk), :], qts[mp], preferred_element_type=F32)

    def scores_right(kb, dst):
        off = _block_offset(kb, tk)
        for mp in range(2):
            dst[mp, :, tk:] = jnp.dot(ks[mp][pl.ds(off, tk), :], qts[mp][:, tk:], preferred_element_type=F32)

    def consume(kb, src, strip_off, col0=0):
        off = _block_offset(kb, tk)
        vt_blk = vt_ref[:, pl.ds(off, tk)]
        for mp in range(2):
            s = src[mp, :, col0:]
            if strip_off is not None:
                s = s + bs[mp][0, :, strip_off + col0:strip_off + tq]
            _softmax_pv(mp, s, vt_blk, m_ref, l_ref, acc_ref, col0)

    scores(0, sa_ref)

    def far_pair(j, c):
        kb = 2 * j
        scores(kb + 1, sb_ref)
        consume(kb, sa_ref, None)
        scores(kb + 2, sa_ref)
        consume(kb + 1, sb_ref, None)
        return c

    lax.fori_loop(0, jnp.maximum(i - 1, 0), far_pair, 0)

    @pl.when(i >= 1)
    def _():
        scores(2 * i - 1, sb_ref)
        consume(2 * i - 2, sa_ref, None)
        scores(2 * i, sa_ref)
        consume(2 * i - 1, sb_ref, 2 * tk)

    scores_right(2 * i + 1, sb_ref)
    consume(2 * i, sa_ref, tk)
    consume(2 * i + 1, sb_ref, 0, col0=tk)

    lam = (jnp.exp(jnp.sum(lq1_ref[...] * lk1_ref[...], axis=-1, keepdims=True))
           - jnp.exp(jnp.sum(lq2_ref[...] * lk2_ref[...], axis=-1, keepdims=True)) + lam_init)
    o = acc_ref[0] / l_ref[0] - lam * (acc_ref[1] / l_ref[1])
    ms = jnp.mean(o * o, axis=0, keepdims=True)
    o = (o * lax.rsqrt(ms + NORM_EPS) * g_ref[...]) * (1.0 - lam_init)
    o_ref[...] = o.T.astype(o_ref.dtype)


def _diff_attention(q_t, k, v_t, bias_strips, lam_vecs, subln_g, lam_init):
    s = k.shape[0]
    tk = ATT_T
    tq = 2 * tk
    assert s % tq == 0 and bias_strips.shape[1:] == (tk, 4 * tk)
    hd, vd = DIFF_HEAD_DIM, DIFF_V_DIM
    vec = pl.BlockSpec((1, hd), lambda h, i: (0, 0))
    in_specs = [vec, vec, vec, vec,
                pl.BlockSpec((hd, tq), lambda h, i: (h, i)),
                pl.BlockSpec((hd, tq), lambda h, i: (DIFF_HEADS + h, i)),
                pl.BlockSpec((s, hd), lambda h, i: (0, h)),
                pl.BlockSpec((s, hd), lambda h, i: (0, DIFF_HEADS + h)),
                pl.BlockSpec((vd, s), lambda h, i: (h, 0)),
                pl.BlockSpec((1, tk, 4 * tk), lambda h, i: (h, 0, 0), pipeline_mode=pl.Buffered(1)),
                pl.BlockSpec((1, tk, 4 * tk), lambda h, i: (DIFF_HEADS + h, 0, 0), pipeline_mode=pl.Buffered(1)),
                pl.BlockSpec((vd, 1), lambda h, i: (0, 0))]
    return pl.pallas_call(
        functools.partial(_diff_attn_kernel, lam_init=lam_init),
        grid=(DIFF_HEADS, s // tq),
        in_specs=in_specs,
        out_specs=pl.BlockSpec((tq, vd), lambda h, i: (i, h)),
        out_shape=jax.ShapeDtypeStruct((s, DIFF_WIDTH), BF16),
        scratch_shapes=[pltpu.VMEM((2, tk, tq), F32), pltpu.VMEM((2, tk, tq), F32),
                        pltpu.VMEM((2, 1, tq), F32), pltpu.VMEM((2, 1, tq), F32),
                        pltpu.VMEM((2, vd, tq), F32)],
        compiler_params=_cp(("arbitrary", "arbitrary")),
        name="diff_attention",
    )(*lam_vecs, q_t, q_t, k, k, v_t, bias_strips, bias_strips, subln_g.reshape(vd, 1).astype(F32))


def _t5_bucket(dist):
    n = jnp.maximum(dist, 0)
    max_exact = REL_BUCKETS // 2
    nf = jnp.maximum(n, 1).astype(F32)
    large = max_exact + (jnp.log(nf / max_exact) / math.log(REL_MAX_DIST / max_exact)
                         * (REL_BUCKETS - max_exact)).astype(jnp.int32)
    large = jnp.minimum(large, REL_BUCKETS - 1)
    return jnp.where(n < max_exact, n, large)


def _diff_bias_strips(rel_bias, tk):
    assert tk >= REL_MAX_DIST
    width = 4 * tk
    n = width + tk
    table = rel_bias.astype(F32)
    table = (table - table[REL_BUCKETS - 1:REL_BUCKETS]) * LOG2E
    kk = jnp.arange(n, dtype=jnp.int32)
    dist = kk - tk
    onehot = (_t5_bucket(dist)[:, None] == jnp.arange(REL_BUCKETS, dtype=jnp.int32)[None, :]).astype(F32)
    g = jnp.einsum('kb,bm->mk', onehot, table, precision=lax.Precision.HIGHEST)
    w = jnp.where(((dist >= 0) & (kk < width))[None, :], g, NEG_INF)
    maps = w.shape[0]

    def strip_kernel(w_ref, o_ref):
        rows = jnp.broadcast_to(w_ref[0], (tk, n))
        o_ref[0] = pltpu.roll(rows, 0, 1, stride=1, stride_axis=0)[:, :width]

    return pl.pallas_call(
        strip_kernel,
        grid=(maps,),
        in_specs=[pl.BlockSpec((1, 1, n), lambda m: (m, 0, 0))],
        out_specs=pl.BlockSpec((1, tk, width), lambda m: (m, 0, 0)),
        out_shape=jax.ShapeDtypeStruct((maps, tk, width), F32),
        compiler_params=_cp(("parallel",)),
        name="bias_strips",
    )(w.reshape(maps, 1, n))


def _rope_apply(tv, c_ref, s1_ref, s2_ref):
    return (tv * c_ref[...] + pltpu.roll(tv, 96, 1) * s1_ref[...] + pltpu.roll(tv, 32, 1) * s2_ref[...])


def _mla_q_kernel(cq_ref, g_ref, w_ref, qg_ref, c_ref, s1_ref, s2_ref, o_ref, xg_ref):
    h = pl.program_id(1)

    @pl.when(h == 0)
    def _():
        c = cq_ref[...].astype(F32)
        r = lax.rsqrt(jnp.mean(c * c, axis=-1, keepdims=True) + NORM_EPS)
        xg_ref[...] = (c * r * g_ref[...]).astype(BF16)

    ug = jnp.dot(xg_ref[...], w_ref[...], preferred_element_type=F32)
    for hh in range(MLA_HEAD_GROUP):
        u = ug[:, hh * MLA_QK_PAD:(hh + 1) * MLA_QK_PAD]
        ms = jnp.sum(u * u, axis=-1, keepdims=True) * (1.0 / MLA_QK_DIM)
        qn = u * lax.rsqrt(ms + NORM_EPS) * qg_ref[...]
        o_ref[hh, :MLA_NOPE_DIM, :] = qn[:, :MLA_NOPE_DIM].T.astype(o_ref.dtype)
        o_ref[hh, MLA_NOPE_DIM:, :] = _rope_apply(qn[:, MLA_NOPE_DIM:], c_ref, s1_ref, s2_ref).T.astype(o_ref.dtype)


def _mla_q_prep(rest, cq_g, w_uq_pad, qg_pad, rope_tabs, tm):
    s = rest.shape[0]
    hg = MLA_HEAD_GROUP
    tab = pl.BlockSpec((tm, LANE), lambda i, h: (i, 0))
    return pl.pallas_call(
        _mla_q_kernel,
        grid=(s // tm, MLA_HEADS // hg),
        in_specs=[pl.BlockSpec((tm, MLA_Q_RANK), lambda i, h: (i, R_CQ // MLA_Q_RANK)),
                  pl.BlockSpec((1, MLA_Q_RANK), lambda i, h: (0, 0)),
                  pl.BlockSpec((MLA_Q_RANK, hg * MLA_QK_PAD), lambda i, h: (0, h)),
                  pl.BlockSpec((1, MLA_QK_PAD), lambda i, h: (0, 0)),
                  tab, tab, tab],
        out_specs=pl.BlockSpec((hg, MLA_QK_PAD, tm), lambda i, h: (h, 0, i)),
        out_shape=jax.ShapeDtypeStruct((MLA_HEADS, MLA_QK_PAD, s), BF16),
        scratch_shapes=[pltpu.VMEM((tm, MLA_Q_RANK), BF16)],
        compiler_params=_cp(("arbitrary", "arbitrary")),
        name="mla_q_prep",
    )(rest, cq_g, w_uq_pad, qg_pad, *rope_tabs)


def _mla_kv_kernel(ckv_ref, kr_ref, g_ref, w_ref, kgn_ref, kgr_ref, c_ref, s1_ref, s2_ref,
                   k_ref, vt_ref, xg_ref):
    h = pl.program_id(1)

    @pl.when(h == 0)
    def _():
        c = ckv_ref[...].astype(F32)
        r = lax.rsqrt(jnp.mean(c * c, axis=-1, keepdims=True) + NORM_EPS)
        xg_ref[...] = (c * r * g_ref[...]).astype(BF16)

    hw = MLA_NOPE_DIM + MLA_V_DIM
    kvg = jnp.dot(xg_ref[...], w_ref[...].astype(BF16), preferred_element_type=F32)
    lane = lax.broadcasted_iota(jnp.int32, kr_ref.shape, 1)
    kr = jnp.where(lane < MLA_ROPE_DIM, kr_ref[...].astype(F32), 0.0)
    kr_ss = jnp.sum(kr * kr, axis=-1, keepdims=True)
    for hh in range(MLA_HEAD_GROUP):
        kn = kvg[:, hh * hw:hh * hw + MLA_NOPE_DIM]
        ms = (jnp.sum(kn * kn, axis=-1, keepdims=True) + kr_ss) * (1.0 / MLA_QK_DIM)
        rs = lax.rsqrt(ms + NORM_EPS)
        k_ref[hh, :, :MLA_NOPE_DIM] = (kn * rs * kgn_ref[...]).astype(k_ref.dtype)
        k_ref[hh, :, MLA_NOPE_DIM:] = _rope_apply(kr * rs * kgr_ref[...], c_ref, s1_ref, s2_ref).astype(k_ref.dtype)
        vt_ref[hh] = kvg[:, hh * hw + MLA_NOPE_DIM:(hh + 1) * hw].T.astype(vt_ref.dtype)


def _mla_kv_prep(rest, ckv_g, w_ukv, kg_nope, kg_rope_pad, rope_tabs, tm):
    s = rest.shape[0]
    hg = MLA_HEAD_GROUP
    tab = pl.BlockSpec((tm, LANE), lambda i, h: (i, 0))
    hw = MLA_NOPE_DIM + MLA_V_DIM
    return pl.pallas_call(
        _mla_kv_kernel,
        grid=(s // tm, MLA_HEADS // hg),
        in_specs=[pl.BlockSpec((tm, MLA_KV_RANK), lambda i, h: (i, R_CKV // MLA_KV_RANK)),
                  pl.BlockSpec((tm, LANE), lambda i, h: (i, R_KROPE // LANE)),
                  pl.BlockSpec((1, MLA_KV_RANK), lambda i, h: (0, 0)),
                  pl.BlockSpec((MLA_KV_RANK, hg * hw), lambda i, h: (0, h)),
                  pl.BlockSpec((1, LANE), lambda i, h: (0, 0)),
                  pl.BlockSpec((1, LANE), lambda i, h: (0, 0)),
                  tab, tab, tab],
        out_specs=[pl.BlockSpec((hg, tm, MLA_QK_PAD), lambda i, h: (h, i, 0)),
                   pl.BlockSpec((hg, MLA_V_DIM, tm), lambda i, h: (h, 0, i))],
        out_shape=[jax.ShapeDtypeStruct((MLA_HEADS, s, MLA_QK_PAD), BF16),
                   jax.ShapeDtypeStruct((MLA_HEADS, MLA_V_DIM, s), BF16)],
        scratch_shapes=[pltpu.VMEM((tm, MLA_KV_RANK), BF16)],
        compiler_params=_cp(("arbitrary", "arbitrary")),
        name="mla_kv_prep",
    )(rest, rest, ckv_g, w_ukv, kg_nope, kg_rope_pad, *rope_tabs)


def _mla_attn_kernel(qt_ref, k_ref, vt_ref, o_ref, sa_ref, sb_ref, m_ref, l_ref, acc_ref):
    tk, tq = sa_ref.shape
    i = pl.program_id(1)
    qt = qt_ref[0]
    _init_stats(m_ref, l_ref, acc_ref)

    def scores(kb, dst):
        off = _block_offset(kb, tk)
        dst[...] = jnp.dot(k_ref[0, pl.ds(off, tk), :], qt, preferred_element_type=F32)

    def scores_right(kb, dst):
        off = _block_offset(kb, tk)
        dst[:, tk:] = jnp.dot(k_ref[0, pl.ds(off, tk), :], qt[:, tk:], preferred_element_type=F32)

    def consume(kb, src, diag, col0=0):
        off = _block_offset(kb, tk)
        s = src[:, col0:]
        if diag:
            krow = lax.broadcasted_iota(jnp.int32, s.shape, 0)
            qcol = lax.broadcasted_iota(jnp.int32, s.shape, 1)
            s = jnp.where(krow <= qcol, s, NEG_INF)
        _softmax_pv(0, s, vt_ref[0, :, pl.ds(off, tk)], m_ref, l_ref, acc_ref, col0)

    scores(0, sa_ref)

    def far_pair(j, c):
        kb = 2 * j
        scores(kb + 1, sb_ref)
        consume(kb, sa_ref, False)
        scores(kb + 2, sa_ref)
        consume(kb + 1, sb_ref, False)
        return c

    lax.fori_loop(0, i, far_pair, 0)
    scores_right(2 * i + 1, sb_ref)
    consume(2 * i, sa_ref, True)
    consume(2 * i + 1, sb_ref, True, col0=tk)
    o_ref[...] = (acc_ref[0] / l_ref[0]).T.astype(o_ref.dtype)


def _mla_attention(q_t, k, v_t):
    s = k.shape[1]
    tk = ATT_T
    tq = 2 * tk
    assert s % tq == 0
    return pl.pallas_call(
        _mla_attn_kernel,
        grid=(MLA_HEADS, s // tq),
        in_specs=[pl.BlockSpec((1, MLA_QK_PAD, tq), lambda h, i: (h, 0, i)),
                  pl.BlockSpec((1, s, MLA_QK_PAD), lambda h, i: (h, 0, 0)),
                  pl.BlockSpec((1, MLA_V_DIM, s), lambda h, i: (h, 0, 0))],
        out_specs=pl.BlockSpec((tq, MLA_V_DIM), lambda h, i: (i, h)),
        out_shape=jax.ShapeDtypeStruct((s, MLA_WIDTH), BF16),
        scratch_shapes=[pltpu.VMEM((tk, tq), F32), pltpu.VMEM((tk, tq), F32),
                        pltpu.VMEM((1, 1, tq), F32), pltpu.VMEM((1, 1, tq), F32),
                        pltpu.VMEM((1, MLA_V_DIM, tq), F32)],
        compiler_params=_cp(("arbitrary", "arbitrary")),
        name="mla_attention",
    )(q_t, k, v_t)


def _qk_nt(q, k_blk):
    return lax.dot_general(q, k_blk, (((1,), (1,)), ((), ())), preferred_element_type=F32)


def _mem_attn_kernel(q_ref, k_ref, v_ref, qg_ref, o_ref):
    shift = R_MQ % MEM_WINDOW
    qall = q_ref[...].astype(F32)[:, shift:shift + MEM_WIDTH]
    for h in range(MEM_HEADS):
        lo = h * MEM_HEAD_DIM
        qh = qall[:, lo:lo + MEM_HEAD_DIM]
        ms = jnp.mean(qh * qh, axis=-1, keepdims=True)
        qn = (qh * lax.rsqrt(ms + NORM_EPS) * qg_ref[...]).astype(BF16)
        s = _qk_nt(qn, k_ref[:, lo:lo + MEM_HEAD_DIM])
        p = jnp.exp(s - jnp.max(s, axis=-1, keepdims=True))
        l = jnp.sum(p, axis=-1, keepdims=True)
        o = jnp.dot(p.astype(BF16), v_ref[:, lo:lo + MEM_HEAD_DIM], preferred_element_type=F32)
        o_ref[:, lo:lo + MEM_HEAD_DIM] = (o / l).astype(o_ref.dtype)


def _mem_attention(rest, k_mem, v_mem, qg_scaled, tm):
    s = rest.shape[0]
    n_mem = k_mem.shape[0]
    assert R_MQ % MEM_WINDOW + MEM_WIDTH <= MEM_WINDOW
    return pl.pallas_call(
        _mem_attn_kernel,
        grid=(s // tm,),
        in_specs=[pl.BlockSpec((tm, MEM_WINDOW), lambda i: (i, R_MQ // MEM_WINDOW)),
                  pl.BlockSpec((n_mem, MEM_WIDTH), lambda i: (0, 0)),
                  pl.BlockSpec((n_mem, MEM_WIDTH), lambda i: (0, 0)),
                  pl.BlockSpec((1, MEM_HEAD_DIM), lambda i: (0, 0))],
        out_specs=pl.BlockSpec((tm, MEM_WIDTH), lambda i: (i, 0)),
        out_shape=jax.ShapeDtypeStruct((s, MEM_WIDTH), BF16),
        compiler_params=_cp(("parallel",)),
        name="mem_attention",
    )(rest, k_mem, v_mem, qg_scaled)


def _mix_kernel(od_ref, om_ref, oc_ref, wd_ref, wm_ref, wc_ref, g0_ref, g1_ref, g2_ref, o_ref):
    yd = jnp.dot(od_ref[...], wd_ref[...], preferred_element_type=F32)
    ym = jnp.dot(om_ref[...], wm_ref[...], preferred_element_type=F32)
    yc = jnp.dot(oc_ref[...], wc_ref[...], preferred_element_type=F32)
    mixed = (g0_ref[...].astype(F32) * yd + g1_ref[...].astype(F32) * ym) + g2_ref[...].astype(F32) * yc
    o_ref[...] = mixed.astype(o_ref.dtype)


def _mix(o_diff, o_mla, o_mem, w_d, w_m, w_c, gates, tm, tn):
    s = o_diff.shape[0]
    d = w_d.shape[1]
    nj = d // tn
    return pl.pallas_call(
        _mix_kernel,
        grid=(s // tm, nj),
        in_specs=[pl.BlockSpec((tm, o_diff.shape[1]), lambda i, j: (i, 0)),
                  pl.BlockSpec((tm, o_mla.shape[1]), lambda i, j: (i, 0)),
                  pl.BlockSpec((tm, o_mem.shape[1]), lambda i, j: (i, 0)),
                  pl.BlockSpec((w_d.shape[0], tn), lambda i, j: (0, j)),
                  pl.BlockSpec((w_m.shape[0], tn), lambda i, j: (0, j)),
                  pl.BlockSpec((w_c.shape[0], tn), lambda i, j: (0, j)),
                  pl.BlockSpec((tm, tn), lambda i, j: (i, j)),
                  pl.BlockSpec((tm, tn), lambda i, j: (i, nj + j)),
                  pl.BlockSpec((tm, tn), lambda i, j: (i, 2 * nj + j))],
        out_specs=pl.BlockSpec((tm, tn), lambda i, j: (i, j)),
        out_shape=jax.ShapeDtypeStruct((s, d), BF16),
        compiler_params=_cp(("arbitrary", "arbitrary")),
        name="gated_mix",
    )(o_diff, o_mla, o_mem, w_d, w_m, w_c, gates, gates, gates)


def _router_kernel(x_ref, g_ref, w_ref, b_ref, h_ref, r_ref):
    x = x_ref[...]
    ms = jnp.mean(x * x, axis=-1, keepdims=True)
    h = x * lax.rsqrt(ms + NORM_EPS) * g_ref[...]
    h_ref[...] = _pack_bf16_pairs(h)
    h_hi = h.astype(BF16)
    h_lo = (h - h_hi.astype(F32)).astype(BF16)
    logits = (jnp.dot(h_hi, w_ref[0], preferred_element_type=F32)
              + (jnp.dot(h_lo, w_ref[0], preferred_element_type=F32)
                 + jnp.dot(h_hi, w_ref[1], preferred_element_type=F32))) + b_ref[...]
    lane = lax.broadcasted_iota(jnp.int32, logits.shape, 1)
    lane_f = lane.astype(F32)
    big = float(4 * ROUTE_W)
    lg = jnp.where(lane < N_GROUPS, logits, -jnp.inf)
    gmax = jnp.max(lg, axis=-1, keepdims=True)
    gidx = jnp.min(jnp.where(lg == gmax, lane_f, big), axis=-1, keepdims=True)
    pg_top = 1.0 / jnp.sum(jnp.exp(lg - gmax), axis=-1, keepdims=True)
    e_lane = lane - N_GROUPS
    lane_group = jnp.right_shift(e_lane, 3).astype(F32)
    in_group = (e_lane >= 0) & (e_lane < N_EXPERTS) & (lane_group == gidx)
    le = jnp.where(in_group, logits, -jnp.inf)
    e1 = jnp.max(le, axis=-1, keepdims=True)
    i1 = jnp.min(jnp.where(le == e1, lane_f, big), axis=-1, keepdims=True)
    le2 = jnp.where(lane_f == i1, -jnp.inf, le)
    e2 = jnp.max(le2, axis=-1, keepdims=True)
    i2 = jnp.min(jnp.where(le2 == e2, lane_f, big), axis=-1, keepdims=True)
    w2 = jnp.exp(e2 - e1)
    inv = 1.0 / (1.0 + w2)
    gate1 = pg_top * inv
    gate2 = pg_top * (w2 * inv)
    out = jnp.where(lane == 0, i1 - N_GROUPS,
                    jnp.where(lane == 1, i2 - N_GROUPS,
                              jnp.where(lane == 2, gate1, jnp.where(lane == 3, gate2, 0.0))))
    r_ref[...] = out


def _router(x1, g, w_r, b_r, tm):
    s, d = x1.shape
    return pl.pallas_call(
        _router_kernel,
        grid=(s // tm,),
        in_specs=[pl.BlockSpec((tm, d), lambda i: (i, 0)),
                  pl.BlockSpec((1, d), lambda i: (0, 0)),
                  pl.BlockSpec((2, d, ROUTE_W), lambda i: (0, 0, 0)),
                  pl.BlockSpec((1, ROUTE_W), lambda i: (0, 0))],
        out_specs=[pl.BlockSpec((tm, d // 2), lambda i: (i, 0)),
                   pl.BlockSpec((tm, ROUTE_W), lambda i: (i, 0))],
        out_shape=[jax.ShapeDtypeStruct((s, d // 2), jnp.uint32),
                   jax.ShapeDtypeStruct((s, ROUTE_W), F32)],
        compiler_params=_cp(("parallel",)),
        name="ffn_norm_router",
    )(x1, g, w_r, b_r)


def _pack_bf16_pairs(v):
    n = v.shape[1] // 2
    bits = lax.bitcast_convert_type(v.astype(BF16).astype(F32), jnp.uint32)
    return jnp.right_shift(bits[:, :n], jnp.uint32(16)) | bits[:, n:]


def _unpack_bf16_pairs(words):
    lo = lax.bitcast_convert_type(jnp.left_shift(words, jnp.uint32(16)), F32)
    hi = lax.bitcast_convert_type(words & jnp.uint32(0xFFFF0000), F32)
    return lo, hi


def _moe_kernel(be_ref, nr_ref, nu_ref, tok_ref, tokn_ref, h_ref, wg_ref, wu_ref, wd_ref, o_ref,
                xg_ref, xb_ref, gp_ref, a_ref, wgb_ref, wub_ref, wdb_ref, sem):
    b = pl.program_id(0)
    c = pl.program_id(1)
    nb = pl.num_programs(0)
    nr = nr_ref[b]

    def row_copy(src_row, slot, r):
        return pltpu.make_async_copy(h_ref.at[pl.ds(src_row, 1), :], xg_ref.at[slot, pl.ds(r, 1), :], sem.at[slot])

    def start_gather(tok, n_rows, slot):
        unroll = 8

        def issue(i, carry):
            for j in range(unroll):
                r = i * unroll + j
                row_copy(tok[0, 0, r], slot, r).start(priority=1)
            return carry
        lax.fori_loop(0, n_rows // unroll, issue, 0)

    def for_row_count(fn):
        for units in range(1, MOE_TB // MOE_RU + 1):
            @pl.when(nr == units)
            def _():
                fn(units * MOE_RU)

    @pl.when(c == 0)
    def _():
        slot = b % 2

        @pl.when(b == 0)
        def _():
            start_gather(tok_ref, nr * MOE_RU, 0)

        def drain(u, carry):
            pltpu.make_async_copy(h_ref.at[pl.ds(0, MOE_RU), :], xg_ref.at[slot, pl.ds(0, MOE_RU), :],
                                  sem.at[slot]).wait()
            return carry
        lax.fori_loop(0, nr, drain, 0)

        @pl.when(b + 1 < nb)
        def _():
            start_gather(tokn_ref, nr_ref[jnp.minimum(b + 1, nb - 1)] * MOE_RU, 1 - slot)

        def unpack(u, carry):
            r0 = pl.multiple_of(u * MOE_RU, MOE_RU)
            lo, hi = _unpack_bf16_pairs(xg_ref[slot, pl.ds(r0, MOE_RU), :])
            xb_ref[0, pl.ds(r0, MOE_RU), :] = lo.astype(BF16)
            xb_ref[1, pl.ds(r0, MOE_RU), :] = hi.astype(BF16)
            return carry
        lax.fori_loop(0, nr, unpack, 0)

    @pl.when((c < 2) & (nr > 0))
    def _():
        wgb_ref[...] = wg_ref[0].astype(BF16)
        wub_ref[...] = wu_ref[0].astype(BF16)

        def gate_up(m):
            x = xb_ref[c, :m, :]
            g = jnp.dot(x, wgb_ref[...], preferred_element_type=F32)
            u = jnp.dot(x, wub_ref[...], preferred_element_type=F32)

            @pl.when(c == 0)
            def _():
                gp_ref[0, :m, :] = g
                gp_ref[1, :m, :] = u

            @pl.when(c == 1)
            def _():
                gs = gp_ref[0, :m, :] + g
                a_ref[:m, :] = ((gs * jax.nn.sigmoid(gs)) * (gp_ref[1, :m, :] + u)).astype(BF16)

        for_row_count(gate_up)

    @pl.when((c >= 2) & (nr > 0))
    def _():
        wdb_ref[...] = wd_ref[0].astype(BF16)

        def down(m):
            yv = jnp.dot(a_ref[:m, :], wdb_ref[...], preferred_element_type=F32)
            o_ref[:m, :] = _pack_bf16_pairs(yv)
            if m < MOE_TB:
                o_ref[m:, :] = jnp.zeros((MOE_TB - m, o_ref.shape[1]), jnp.uint32)

        for_row_count(down)

    @pl.when((c >= 2) & (nr == 0))
    def _():
        o_ref[...] = jnp.zeros(o_ref.shape, jnp.uint32)


def _moe_experts(block_e, nsub, n_used, buf_tok, h2p, w_gate, w_up, w_down):
    n_blocks = block_e.shape[0]
    half = h2p.shape[1]
    d = 2 * half
    ff = w_gate.shape[2]
    dh = d // 2
    tok = buf_tok.reshape(n_blocks, 1, MOE_TB)

    def gate_up_idx(b, c, be, nr, nu):
        live = b < nu[0]
        return be[jnp.minimum(b, nu[0] - 1)], jnp.where(live, jnp.minimum(c, 1), 1), 0

    def down_idx(b, c, be, nr, nu):
        bb = jnp.minimum(b, nu[0] - 1)
        cc = jnp.where(b < nu[0], c, MOE_PH - 1)
        e = jnp.where(cc >= 2, be[bb], be[jnp.maximum(bb - 1, 0)])
        return e, 0, jnp.where(cc == 2, 0, 1)

    def out_idx(b, c, be, nr, nu):
        ob = jnp.where(c >= 2, b, jnp.maximum(b - 1, 0))
        oc = jnp.where(c >= 2, c - 2, jnp.where(b > 0, 1, 0))
        return ob, oc

    grid_spec = pltpu.PrefetchScalarGridSpec(
        num_scalar_prefetch=3,
        grid=(n_blocks, MOE_PH),
        in_specs=[pl.BlockSpec((1, 1, MOE_TB), lambda b, c, be, nr, nu: (b, 0, 0), memory_space=pltpu.SMEM),
                  pl.BlockSpec((1, 1, MOE_TB), lambda b, c, be, nr, nu: (jnp.minimum(b + 1, n_blocks - 1), 0, 0),
                               memory_space=pltpu.SMEM),
                  pl.BlockSpec(memory_space=pl.ANY),
                  pl.BlockSpec((1, half, ff), gate_up_idx),
                  pl.BlockSpec((1, half, ff), gate_up_idx),
                  pl.BlockSpec((1, ff, dh), down_idx)],
        out_specs=pl.BlockSpec((MOE_TB, dh // 2), out_idx),
        scratch_shapes=[pltpu.VMEM((2, MOE_TB, half), jnp.uint32), pltpu.VMEM((2, MOE_TB, half), BF16),
                        pltpu.VMEM((2, MOE_TB, ff), F32), pltpu.VMEM((MOE_TB, ff), BF16),
                        pltpu.VMEM((half, ff), BF16), pltpu.VMEM((half, ff), BF16), pltpu.VMEM((ff, dh), BF16),
                        pltpu.SemaphoreType.DMA((2,))],
    )
    return pl.pallas_call(
        _moe_kernel,
        grid_spec=grid_spec,
        out_shape=jax.ShapeDtypeStruct((n_blocks * MOE_TB, half), jnp.uint32),
        compiler_params=_cp(("arbitrary", "arbitrary")),
        name="moe_experts",
    )(block_e, nsub, n_used, tok, tok, h2p, w_gate, w_up, w_down)


def _combine_kernel(slot_ref, r_ref, x_ref, yb_ref, o_ref, g_ref, sem):
    tm = x_ref.shape[0]
    n = g_ref.shape[2]

    unroll = 4

    def issue(i, c):
        for j in range(unroll):
            r = i * unroll + j
            for k in range(TOP_K):
                sl = slot_ref[0, 0, r * TOP_K + k]
                pltpu.make_async_copy(yb_ref.at[pl.ds(sl, 1), :], g_ref.at[k, pl.ds(r, 1), :],
                                      sem.at[k]).start(priority=k)
        return c

    lax.fori_loop(0, tm // unroll, issue, 0)
    for k in range(TOP_K):
        pltpu.make_async_copy(yb_ref.at[pl.ds(0, tm), :], g_ref.at[k], sem.at[k]).wait()

    route = r_ref[...]
    w0 = route[:, TOP_K:TOP_K + 1]
    w1 = route[:, TOP_K + 1:TOP_K + 2]
    q = n // 2
    for hf in range(2):
        lo0, hi0 = _unpack_bf16_pairs(g_ref[0, :, hf * q:(hf + 1) * q])
        lo1, hi1 = _unpack_bf16_pairs(g_ref[1, :, hf * q:(hf + 1) * q])
        c0 = hf * n
        o_ref[:, c0:c0 + q] = x_ref[:, c0:c0 + q] + (lo0 * w0 + lo1 * w1)
        o_ref[:, c0 + q:c0 + n] = x_ref[:, c0 + q:c0 + n] + (hi0 * w0 + hi1 * w1)


def _combine(slots, route, x1, yb, tm):
    s, d = x1.shape
    return pl.pallas_call(
        _combine_kernel,
        grid=(s // tm,),
        in_specs=[pl.BlockSpec((1, 1, tm * TOP_K), lambda i: (i, 0, 0), memory_space=pltpu.SMEM),
                  pl.BlockSpec((tm, ROUTE_W), lambda i: (i, 0)),
                  pl.BlockSpec((tm, d), lambda i: (i, 0)),
                  pl.BlockSpec(memory_space=pl.ANY)],
        out_specs=pl.BlockSpec((tm, d), lambda i: (i, 0)),
        out_shape=jax.ShapeDtypeStruct((s, d), F32),
        scratch_shapes=[pltpu.VMEM((TOP_K, tm, d // 2), jnp.uint32), pltpu.SemaphoreType.DMA((TOP_K,))],
        compiler_params=_cp(("arbitrary",)),
        name="moe_combine",
    )(slots.reshape(s // tm, 1, tm * TOP_K), route, x1, yb)


def _dispatch_plan(route, s):
    a = s * TOP_K
    flat_e = route[:, :TOP_K].astype(jnp.int32).reshape(a)
    onehot = (flat_e[:, None] == jnp.arange(N_EXPERTS, dtype=jnp.int32)[None, :]).astype(jnp.int32)
    csum = jnp.cumsum(onehot, axis=0)
    counts = csum[-1]
    rank = jnp.sum((csum - onehot) * onehot, axis=1)
    padded = (counts + MOE_TB - 1) // MOE_TB * MOE_TB
    pad_end = jnp.cumsum(padded)
    pad_start = pad_end - padded
    dest = pad_start[flat_e] + rank
    n_blocks = a // MOE_TB + N_EXPERTS
    p_rows = n_blocks * MOE_TB
    buf_tok = jnp.zeros((p_rows,), jnp.int32).at[dest].set(jnp.arange(a, dtype=jnp.int32) // TOP_K)
    starts = jnp.arange(n_blocks, dtype=jnp.int32) * MOE_TB
    block_e = jnp.minimum(jnp.searchsorted(pad_end, starts, side='right'), N_EXPERTS - 1).astype(jnp.int32)
    valid = jnp.clip(counts[block_e] - (starts - pad_start[block_e]), 0, MOE_TB)
    valid = jnp.where(starts < pad_end[-1], valid, 0)
    nsub = ((valid + MOE_RU - 1) // MOE_RU).astype(jnp.int32)
    n_used = (pad_end[-1:] // MOE_TB).astype(jnp.int32)
    return block_e, nsub, n_used, buf_tok, dest.astype(jnp.int32)


def _rope_tables(positions):
    half = MLA_ROPE_DIM // 2
    inv_freq = ROPE_THETA ** (-jnp.arange(half, dtype=F32) / half)
    ang = positions.astype(F32)[:, None] * inv_freq[None, :]
    cos, sin = jnp.cos(ang), jnp.sin(ang)
    z = jnp.zeros_like(cos)
    c = jnp.concatenate([cos, cos, z, z], axis=-1)
    s1 = jnp.concatenate([-sin, z, z, z], axis=-1)
    s2 = jnp.concatenate([z, sin, z, z], axis=-1)
    return c, s1, s2


def kernel(x, mem, positions, rel_bias, mix_norm_g, w_in, diff_q_norm_g, diff_k_norm_g, diff_lambda_q1, diff_lambda_k1, diff_lambda_q2, diff_lambda_k2, diff_subln_g, mla_cq_norm_g, mla_ckv_norm_g, mla_w_uq, mla_w_ukv, mla_q_norm_g, mla_k_norm_g, mem_norm_g, mem_w_kv, mem_q_norm_g, mem_k_norm_g, w_o_diff, w_o_mla, w_o_mem, w_out, ffn_norm_g, w_route_group, b_route_group, w_route_expert, b_route_expert, w_exp_gate, w_exp_up, w_exp_down):
    b, s, d = x.shape
    assert b == 1 and s % ATT_T == 0
    depth = mix_norm_g.shape[0]
    xs = x.reshape(s, d)
    pos = positions.reshape(s)
    rope_tabs = _rope_tables(pos)
    row = lambda v: v.reshape(1, -1).astype(F32)

    for l in range(depth):
        lam_init = 0.8 - 0.6 * math.exp(-0.3 * l)
        h = _rmsnorm_rows(xs, mix_norm_g[l], 256)
        tn = 512
        q_gain = jnp.tile(diff_q_norm_g[l] * (DIFF_HEAD_DIM ** -0.5 * LOG2E), DIFF_MAPS)
        w_in_t = jnp.transpose(w_in[l])
        dq_t = _matmul_nt(h, w_in_t, row0=OFF_DQ, n_cols=DIFF_QK_WIDTH, tm=MM_TM, tn=tn, out_dtype=BF16,
                          mode="groupnorm", extra=q_gain, group=DIFF_HEAD_DIM, transpose_out=True,
                          name="diff_q_proj")
        dk = _matmul_nt(h, w_in_t, row0=OFF_DK, n_cols=DIFF_QK_WIDTH, tm=MM_TM, tn=tn, out_dtype=BF16,
                        mode="groupnorm", extra=jnp.tile(diff_k_norm_g[l], DIFF_MAPS), group=DIFF_HEAD_DIM,
                        name="diff_k_proj")
        dv_t = _matmul_nt(h, w_in_t, row0=OFF_DV, n_cols=DIFF_WIDTH, tm=MM_TM, tn=tn, out_dtype=BF16,
                          transpose_out=True, name="diff_v_proj")
        rest = _matmul_nt(h, w_in_t, row0=OFF_CQ, n_cols=REST_WIDTH, tm=MM_TM, tn=tn, out_dtype=BF16,
                          name="rest_proj")
        gates = _matmul_nt(h, w_in_t, row0=OFF_GATES, n_cols=3 * d, tm=MM_TM, tn=tn, out_dtype=BF16,
                           mode="sigmoid", name="gate_proj")

        bias_strips = _diff_bias_strips(rel_bias, ATT_T)
        lam_vecs = [row(diff_lambda_q1[l]), row(diff_lambda_k1[l]), row(diff_lambda_q2[l]), row(diff_lambda_k2[l])]
        o_diff = _diff_attention(dq_t, dk, dv_t, bias_strips, lam_vecs, diff_subln_g[l], lam_init)

        w_uq_heads = jnp.pad(
            mla_w_uq[l].reshape(MLA_Q_RANK, MLA_HEADS, MLA_QK_DIM),
            ((0, 0), (0, 0), (0, MLA_QK_PAD - MLA_QK_DIM))).reshape(MLA_Q_RANK, MLA_HEADS * MLA_QK_PAD).astype(BF16)
        qg_pad = jnp.pad(mla_q_norm_g[l] * (MLA_QK_DIM ** -0.5 * LOG2E),
                         (0, MLA_QK_PAD - MLA_QK_DIM)).reshape(1, -1).astype(F32)
        q_mla_t = _mla_q_prep(rest, row(mla_cq_norm_g[l]), w_uq_heads, qg_pad, rope_tabs, 512)
        kg = mla_k_norm_g[l]
        kg_nope = row(kg[:MLA_NOPE_DIM])
        kg_rope = jnp.pad(kg[MLA_NOPE_DIM:], (0, LANE - MLA_ROPE_DIM)).reshape(1, -1).astype(F32)
        k_mla, v_mla_t = _mla_kv_prep(rest, row(mla_ckv_norm_g[l]), mla_w_ukv[l], kg_nope, kg_rope, rope_tabs, 512)
        o_mla = _mla_attention(q_mla_t, k_mla, v_mla_t)

        n_mem = mem.shape[1]
        mem_h = _rmsnorm_rows(mem.reshape(n_mem, d), mem_norm_g[l], n_mem)
        k_mem = _matmul(mem_h, mem_w_kv[l], col0=0, n_cols=MEM_WIDTH, tm=n_mem, tn=512, out_dtype=BF16,
                        mode="groupnorm", extra=jnp.tile(mem_k_norm_g[l], MEM_HEADS), group=MEM_HEAD_DIM,
                        name="mem_k_proj")
        v_mem = _matmul(mem_h, mem_w_kv[l], col0=MEM_WIDTH, n_cols=MEM_WIDTH, tm=n_mem, tn=512, out_dtype=BF16,
                        name="mem_v_proj")
        o_mem = _mem_attention(rest, k_mem, v_mem, row(mem_q_norm_g[l] * MEM_HEAD_DIM ** -0.5), 512)

        mixed = _mix(o_diff, o_mla, o_mem, w_o_diff[l].astype(BF16), w_o_mla[l].astype(BF16),
                     w_o_mem[l].astype(BF16), gates, MM_TM, 512)
        x1 = _matmul(mixed, w_out[l], n_cols=d, tm=MM_TM, tn=512, out_dtype=F32, mode="residual", extra=xs,
                     name="out_proj")

        w_r = jnp.pad(jnp.concatenate([w_route_group[l], w_route_expert[l]], axis=1),
                      ((0, 0), (0, ROUTE_W - N_GROUPS - N_EXPERTS))).astype(F32)
        w_r_hi = w_r.astype(BF16)
        w_r = jnp.stack([w_r_hi, (w_r - w_r_hi.astype(F32)).astype(BF16)])
        b_r = jnp.pad(jnp.concatenate([b_route_group[l], b_route_expert[l]]),
                      (0, ROUTE_W - N_GROUPS - N_EXPERTS)).reshape(1, -1).astype(F32)
        h2, route = _router(x1, row(ffn_norm_g[l]), w_r, b_r, 256)
        block_e, nsub, n_used, buf_tok, slots = _dispatch_plan(route, s)
        yb = _moe_experts(block_e, nsub, n_used, buf_tok, h2, w_exp_gate[l], w_exp_up[l], w_exp_down[l])
        xs = _combine(slots, route, x1, yb, 256)
    return xs.reshape(b, s, d)
```

```python
import functools
import math

import jax
import jax.numpy as jnp
from jax import lax
from jax.experimental import pallas as pl
from jax.experimental.pallas import tpu as pltpu

F32 = jnp.float32
BF16 = jnp.bfloat16

NORM_EPS = 1e-6
NEG_INF = -1e30
LOG2E = math.log2(math.e)

DIFF_HEADS = 6
DIFF_HEAD_DIM = 128
DIFF_V_DIM = 256
DIFF_MAPS = 12
DIFF_QK_WIDTH = 1536
DIFF_WIDTH = 1536
MLA_HEADS = 12
MLA_Q_RANK = 1536
MLA_KV_RANK = 512
MLA_NOPE_DIM = 128
MLA_ROPE_DIM = 64
MLA_QK_DIM = 192
MLA_QK_PAD = 256
MLA_V_DIM = 128
MLA_WIDTH = 1536
ROPE_THETA = 10000.0
MEM_HEADS = 4
MEM_HEAD_DIM = 256
MEM_WIDTH = 1024
REL_BUCKETS = 32
REL_MAX_DIST = 128
N_GROUPS = 8
EXPERTS_PER_GROUP = 8
N_EXPERTS = 64
TOP_K = 2
EXPERT_FF = 512

OFF_DQ = 0
OFF_DK = 1536
OFF_DV = 3072
OFF_CQ = 4608
OFF_CKV = 6144
OFF_KROPE = 6656
OFF_MQ = 6720
OFF_GATES = 7744
REST_WIDTH = 4096
R_CQ = OFF_CQ - OFF_CQ
R_CKV = OFF_CKV - OFF_CQ
R_KROPE = OFF_KROPE - OFF_CQ
R_MQ = OFF_MQ - OFF_CQ

LANE = 128
VMEM_LIMIT = 52 * 1024 * 1024

MM_TM = 1024
MLA_HEAD_GROUP = 4
ATT_T = 512
MOE_TB = 512
MOE_RU = 128
MOE_PH = 4
ROUTE_W = 128
MEM_WINDOW = 2048


def _cp(sem, vmem=VMEM_LIMIT):
    return pltpu.CompilerParams(dimension_semantics=sem, vmem_limit_bytes=vmem)


def _rmsnorm_kernel(x_ref, g_ref, o_ref):
    x = x_ref[...].astype(F32)
    ms = jnp.mean(x * x, axis=-1, keepdims=True)
    o_ref[...] = (x * lax.rsqrt(ms + NORM_EPS) * g_ref[...]).astype(o_ref.dtype)


def _rmsnorm_rows(x, g, tm, out_dtype=BF16):
    m, d = x.shape
    return pl.pallas_call(
        _rmsnorm_kernel,
        grid=(m // tm,),
        in_specs=[pl.BlockSpec((tm, d), lambda i: (i, 0)),
                  pl.BlockSpec((1, d), lambda i: (0, 0))],
        out_specs=pl.BlockSpec((tm, d), lambda i: (i, 0)),
        out_shape=jax.ShapeDtypeStruct((m, d), out_dtype),
        compiler_params=_cp(("parallel",)),
        name="rmsnorm_rows",
    )(x, g.reshape(1, d).astype(F32))


def _cast_shifted(w_ref, w2_ref, wb_ref, shift):
    k = w_ref.shape[0]
    rows = 256

    def body(c, carry):
        r0 = pl.multiple_of(c * rows, rows)
        main = w_ref[pl.ds(r0, rows), :]
        tail = w2_ref[pl.ds(r0, rows), :]
        wb_ref[pl.ds(r0, rows), :] = jnp.concatenate([main[:, shift:], tail[:, :shift]], axis=1).astype(BF16)
        return carry

    lax.fori_loop(0, k // rows, body, 0)


def _mm_kernel(*refs, mode, cast, group, shift, transpose_out):
    a_ref, w_ref = refs[0], refs[1]
    pos = 2
    w2_ref = None
    if shift:
        w2_ref = refs[pos]
        pos += 1
    extra = None
    if mode in ("groupnorm", "residual"):
        extra = refs[pos]
        pos += 1
    o_ref = refs[pos]
    wb_ref = refs[pos + 1] if cast else None
    i = pl.program_id(1)
    if cast:
        @pl.when(i == 0)
        def _():
            if shift:
                _cast_shifted(w_ref, w2_ref, wb_ref, shift)
            else:
                wb_ref[...] = w_ref[...].astype(BF16)
        w = wb_ref[...]
    else:
        w = w_ref[...]
    acc = jnp.dot(a_ref[...], w, preferred_element_type=F32)
    if transpose_out:
        acc = acc.T
    if mode == "plain":
        o_ref[...] = acc.astype(o_ref.dtype)
    elif mode == "sigmoid":
        o_ref[...] = jax.nn.sigmoid(acc).astype(o_ref.dtype)
    elif mode == "residual":
        o_ref[...] = (extra[...] + acc).astype(o_ref.dtype)
    elif mode == "groupnorm":
        tn = w.shape[1]
        for c in range(tn // group):
            sl = slice(c * group, (c + 1) * group)
            if transpose_out:
                blk = acc[sl, :]
                ms = jnp.mean(blk * blk, axis=0, keepdims=True)
                o_ref[sl, :] = (blk * lax.rsqrt(ms + NORM_EPS) * extra[sl, :]).astype(o_ref.dtype)
            else:
                blk = acc[:, sl]
                ms = jnp.mean(blk * blk, axis=-1, keepdims=True)
                o_ref[:, sl] = (blk * lax.rsqrt(ms + NORM_EPS) * extra[:, sl]).astype(o_ref.dtype)


def _matmul(a, w, *, n_cols, tm, tn, out_dtype, col0=0, mode="plain", extra=None, group=LANE,
            transpose_out=False, name="matmul"):
    m, k = a.shape
    assert m % tm == 0 and n_cols % tn == 0 and w.shape[0] == k
    cast = w.dtype != BF16
    base, shift = divmod(col0, tn)
    assert shift <= LANE and (shift == 0 or cast)
    in_specs = [pl.BlockSpec((tm, k), lambda j, i: (i, 0)),
                pl.BlockSpec((k, tn), lambda j, i: (0, base + j))]
    args = [a, w]
    if shift:
        in_specs.append(pl.BlockSpec((k, LANE), lambda j, i: (0, (base + j + 1) * (tn // LANE))))
        args.append(w)
    if mode == "groupnorm":
        gain = extra.reshape(-1, 1) if transpose_out else extra.reshape(1, -1)
        in_specs.append(pl.BlockSpec((tn, 1), lambda j, i: (j, 0)) if transpose_out
                        else pl.BlockSpec((1, tn), lambda j, i: (0, j)))
        args.append(gain.astype(F32))
    elif mode == "residual":
        assert not transpose_out
        in_specs.append(pl.BlockSpec((tm, tn), lambda j, i: (i, j)))
        args.append(extra)
    if transpose_out:
        out_spec = pl.BlockSpec((tn, tm), lambda j, i: (j, i))
        out_shape = jax.ShapeDtypeStruct((n_cols, m), out_dtype)
    else:
        out_spec = pl.BlockSpec((tm, tn), lambda j, i: (i, j))
        out_shape = jax.ShapeDtypeStruct((m, n_cols), out_dtype)
    scratch = [pltpu.VMEM((k, tn), BF16)] if cast else []
    return pl.pallas_call(
        functools.partial(_mm_kernel, mode=mode, cast=cast, group=group, shift=shift, transpose_out=transpose_out),
        grid=(n_cols // tn, m // tm),
        in_specs=in_specs,
        out_specs=out_spec,
        out_shape=out_shape,
        scratch_shapes=scratch,
        compiler_params=_cp(("arbitrary", "arbitrary")),
        name=name,
    )(*args)


def _mm_nt_kernel(*refs, mode, group, transpose_out):
    a_ref, wt_ref = refs[0], refs[1]
    extra = refs[2] if mode == "groupnorm" else None
    o_ref, wb_ref = refs[-2], refs[-1]
    i = pl.program_id(1)

    @pl.when(i == 0)
    def _():
        wb_ref[...] = wt_ref[...].astype(BF16)

    contract_last = (((1,), (1,)), ((), ()))
    if transpose_out:
        acc = lax.dot_general(wb_ref[...], a_ref[...], contract_last, preferred_element_type=F32)
    else:
        acc = lax.dot_general(a_ref[...], wb_ref[...], contract_last, preferred_element_type=F32)
    if mode == "plain":
        o_ref[...] = acc.astype(o_ref.dtype)
    elif mode == "sigmoid":
        o_ref[...] = jax.nn.sigmoid(acc).astype(o_ref.dtype)
    elif mode == "groupnorm":
        tn = wb_ref.shape[0]
        for c in range(tn // group):
            sl = slice(c * group, (c + 1) * group)
            if transpose_out:
                blk = acc[sl, :]
                ms = jnp.mean(blk * blk, axis=0, keepdims=True)
                o_ref[sl, :] = (blk * lax.rsqrt(ms + NORM_EPS) * extra[sl, :]).astype(o_ref.dtype)
            else:
                blk = acc[:, sl]
                ms = jnp.mean(blk * blk, axis=-1, keepdims=True)
                o_ref[:, sl] = (blk * lax.rsqrt(ms + NORM_EPS) * extra[:, sl]).astype(o_ref.dtype)


def _matmul_nt(a, wt, *, row0, n_cols, tm, tn, out_dtype, mode="plain", extra=None, group=LANE,
               transpose_out=False, name="matmul_nt"):
    m, k = a.shape
    assert m % tm == 0 and n_cols % tn == 0 and wt.shape[1] == k and row0 % 8 == 0
    in_specs = [pl.BlockSpec((tm, k), lambda j, i: (i, 0)),
                pl.BlockSpec((pl.Element(tn), pl.Element(k)), lambda j, i: (pl.multiple_of(row0 + j * tn, 8), 0))]
    args = [a, wt]
    if mode == "groupnorm":
        gain = extra.reshape(-1, 1) if transpose_out else extra.reshape(1, -1)
        in_specs.append(pl.BlockSpec((tn, 1), lambda j, i: (j, 0)) if transpose_out
                        else pl.BlockSpec((1, tn), lambda j, i: (0, j)))
        args.append(gain.astype(F32))
    if transpose_out:
        out_spec = pl.BlockSpec((tn, tm), lambda j, i: (j, i))
        out_shape = jax.ShapeDtypeStruct((n_cols, m), out_dtype)
    else:
        out_spec = pl.BlockSpec((tm, tn), lambda j, i: (i, j))
        out_shape = jax.ShapeDtypeStruct((m, n_cols), out_dtype)
    return pl.pallas_call(
        functools.partial(_mm_nt_kernel, mode=mode, group=group, transpose_out=transpose_out),
        grid=(n_cols // tn, m // tm),
        in_specs=in_specs,
        out_specs=out_spec,
        out_shape=out_shape,
        scratch_shapes=[pltpu.VMEM((tn, k), BF16)],
        compiler_params=_cp(("arbitrary", "arbitrary")),
        name=name,
    )(*args)


def _softmax_pv(idx, s, vt_blk, m_ref, l_ref, acc_ref, col0=0):
    cols = slice(col0, col0 + s.shape[1])
    m_prev = m_ref[idx, :, cols]
    m_new = jnp.maximum(m_prev, jnp.max(s, axis=0, keepdims=True))
    alpha = jnp.exp2(m_prev - m_new)
    p = jnp.exp2(s - m_new)
    l_ref[idx, :, cols] = alpha * l_ref[idx, :, cols] + jnp.sum(p, axis=0, keepdims=True)
    acc_ref[idx, :, cols] = (alpha * acc_ref[idx, :, cols]
                             + jnp.dot(vt_blk, p.astype(BF16), preferred_element_type=F32))
    m_ref[idx, :, cols] = m_new


def _block_offset(kb, t):
    return kb * t if isinstance(kb, int) else pl.multiple_of(kb * t, t)


def _init_stats(m_ref, l_ref, acc_ref):
    m_ref[...] = jnp.full(m_ref.shape, NEG_INF, F32)
    l_ref[...] = jnp.zeros(l_ref.shape, F32)
    acc_ref[...] = jnp.zeros(acc_ref.shape, F32)


def _diff_attn_kernel(lq1_ref, lk1_ref, lq2_ref, lk2_ref, q1_ref, q2_ref, k1_ref, k2_ref, vt_ref,
                      b1_ref, b2_ref, g_ref, o_ref, sa_ref, sb_ref, m_ref, l_ref, acc_ref, *, lam_init):
    _, tk, tq = sa_ref.shape
    i = pl.program_id(1)
    qts = (q1_ref[...], q2_ref[...])
    ks = (k1_ref, k2_ref)
    bs = (b1_ref, b2_ref)
    _init_stats(m_ref, l_ref, acc_ref)

    def scores(kb, dst):
        off = _block_offset(kb, tk)
        for mp in range(2):
            dst[mp] = jnp.dot(ks[mp][pl.ds(off, tk), :], qts[mp], preferred_element_type=F32)

    def scores_right(kb, dst):
        off = _block_offset(kb, tk)
        for mp in range(2):
            dst[mp, :, tk:] = jnp.dot(ks[mp][pl.ds(off, tk), :], qts[mp][:, tk:], preferred_element_type=F32)

    def consume(kb, src, strip_off, col0=0):
        off = _block_offset(kb, tk)
        vt_blk = vt_ref[:, pl.ds(off, tk)]
        for mp in range(2):
            s = src[mp, :, col0:]
            if strip_off is not None:
                s = s + bs[mp][0, :, strip_off + col0:strip_off + tq]
            _softmax_pv(mp, s, vt_blk, m_ref, l_ref, acc_ref, col0)

    scores(0, sa_ref)

    def far_pair(j, c):
        kb = 2 * j
        scores(kb + 1, sb_ref)
        consume(kb, sa_ref, None)
        scores(kb + 2, sa_ref)
        consume(kb + 1, sb_ref, None)
        return c

    lax.fori_loop(0, jnp.maximum(i - 1, 0), far_pair, 0)

    @pl.when(i >= 1)
    def _():
        scores(2 * i - 1, sb_ref)
        consume(2 * i - 2, sa_ref, None)
        scores(2 * i, sa_ref)
        consume(2 * i - 1, sb_ref, 2 * tk)

    scores_right(2 * i + 1, sb_ref)
    consume(2 * i, sa_ref, tk)
    consume(2 * i + 1, sb_ref, 0, col0=tk)

    lam = (jnp.exp(jnp.sum(lq1_ref[...] * lk1_ref[...], axis=-1, keepdims=True))
           - jnp.exp(jnp.sum(lq2_ref[...] * lk2_ref[...], axis=-1, keepdims=True)) + lam_init)
    o = acc_ref[0] / l_ref[0] - lam * (acc_ref[1] / l_ref[1])
    ms = jnp.mean(o * o, axis=0, keepdims=True)
    o = (o * lax.rsqrt(ms + NORM_EPS) * g_ref[...]) * (1.0 - lam_init)
    o_ref[...] = o.T.astype(o_ref.dtype)


def _diff_attention(q_t, k, v_t, bias_strips, lam_vecs, subln_g, lam_init):
    s = k.shape[0]
    tk = ATT_T
    tq = 2 * tk
    assert s % tq == 0 and bias_strips.shape[1:] == (tk, 4 * tk)
    hd, vd = DIFF_HEAD_DIM, DIFF_V_DIM
    vec = pl.BlockSpec((1, hd), lambda h, i: (0, 0))
    in_specs = [vec, vec, vec, vec,
                pl.BlockSpec((hd, tq), lambda h, i: (h, i)),
                pl.BlockSpec((hd, tq), lambda h, i: (DIFF_HEADS + h, i)),
                pl.BlockSpec((s, hd), lambda h, i: (0, h)),
                pl.BlockSpec((s, hd), lambda h, i: (0, DIFF_HEADS + h)),
                pl.BlockSpec((vd, s), lambda h, i: (h, 0)),
                pl.BlockSpec((1, tk, 4 * tk), lambda h, i: (h, 0, 0), pipeline_mode=pl.Buffered(1)),
                pl.BlockSpec((1, tk, 4 * tk), lambda h, i: (DIFF_HEADS + h, 0, 0), pipeline_mode=pl.Buffered(1)),
                pl.BlockSpec((vd, 1), lambda h, i: (0, 0))]
    return pl.pallas_call(
        functools.partial(_diff_attn_kernel, lam_init=lam_init),
        grid=(DIFF_HEADS, s // tq),
        in_specs=in_specs,
        out_specs=pl.BlockSpec((tq, vd), lambda h, i: (i, h)),
        out_shape=jax.ShapeDtypeStruct((s, DIFF_WIDTH), BF16),
        scratch_shapes=[pltpu.VMEM((2, tk, tq), F32), pltpu.VMEM((2, tk, tq), F32),
                        pltpu.VMEM((2, 1, tq), F32), pltpu.VMEM((2, 1, tq), F32),
                        pltpu.VMEM((2, vd, tq), F32)],
        compiler_params=_cp(("arbitrary", "arbitrary")),
        name="diff_attention",
    )(*lam_vecs, q_t, q_t, k, k, v_t, bias_strips, bias_strips, subln_g.reshape(vd, 1).astype(F32))


def _t5_bucket(dist):
    n = jnp.maximum(dist, 0)
    max_exact = REL_BUCKETS // 2
    nf = jnp.maximum(n, 1).astype(F32)
    large = max_exact + (jnp.log(nf / max_exact) / math.log(REL_MAX_DIST / max_exact)
                         * (REL_BUCKETS - max_exact)).astype(jnp.int32)
    large = jnp.minimum(large, REL_BUCKETS - 1)
    return jnp.where(n < max_exact, n, large)


def _diff_bias_strips(rel_bias, tk):
    assert tk >= REL_MAX_DIST
    width = 4 * tk
    n = width + tk
    table = rel_bias.astype(F32)
    table = (table - table[REL_BUCKETS - 1:REL_BUCKETS]) * LOG2E
    kk = jnp.arange(n, dtype=jnp.int32)
    dist = kk - tk
    onehot = (_t5_bucket(dist)[:, None] == jnp.arange(REL_BUCKETS, dtype=jnp.int32)[None, :]).astype(F32)
    g = jnp.einsum('kb,bm->mk', onehot, table, precision=lax.Precision.HIGHEST)
    w = jnp.where(((dist >= 0) & (kk < width))[None, :], g, NEG_INF)
    maps = w.shape[0]

    def strip_kernel(w_ref, o_ref):
        rows = jnp.broadcast_to(w_ref[0], (tk, n))
        o_ref[0] = pltpu.roll(rows, 0, 1, stride=1, stride_axis=0)[:, :width]

    return pl.pallas_call(
        strip_kernel,
        grid=(maps,),
        in_specs=[pl.BlockSpec((1, 1, n), lambda m: (m, 0, 0))],
        out_specs=pl.BlockSpec((1, tk, width), lambda m: (m, 0, 0)),
        out_shape=jax.ShapeDtypeStruct((maps, tk, width), F32),
        compiler_params=_cp(("parallel",)),
        name="bias_strips",
    )(w.reshape(maps, 1, n))


def _rope_apply(tv, c_ref, s1_ref, s2_ref):
    return (tv * c_ref[...] + pltpu.roll(tv, 96, 1) * s1_ref[...] + pltpu.roll(tv, 32, 1) * s2_ref[...])


def _mla_q_kernel(cq_ref, g_ref, w_ref, qg_ref, c_ref, s1_ref, s2_ref, o_ref, xg_ref):
    h = pl.program_id(1)

    @pl.when(h == 0)
    def _():
        c = cq_ref[...].astype(F32)
        r = lax.rsqrt(jnp.mean(c * c, axis=-1, keepdims=True) + NORM_EPS)
        xg_ref[...] = (c * r * g_ref[...]).astype(BF16)

    ug = jnp.dot(xg_ref[...], w_ref[...], preferred_element_type=F32)
    for hh in range(MLA_HEAD_GROUP):
        u = ug[:, hh * MLA_QK_PAD:(hh + 1) * MLA_QK_PAD]
        ms = jnp.sum(u * u, axis=-1, keepdims=True) * (1.0 / MLA_QK_DIM)
        qn = u * lax.rsqrt(ms + NORM_EPS) * qg_ref[...]
        o_ref[hh, :MLA_NOPE_DIM, :] = qn[:, :MLA_NOPE_DIM].T.astype(o_ref.dtype)
        o_ref[hh, MLA_NOPE_DIM:, :] = _rope_apply(qn[:, MLA_NOPE_DIM:], c_ref, s1_ref, s2_ref).T.astype(o_ref.dtype)


def _mla_q_prep(rest, cq_g, w_uq_pad, qg_pad, rope_tabs, tm):
    s = rest.shape[0]
    hg = MLA_HEAD_GROUP
    tab = pl.BlockSpec((tm, LANE), lambda i, h: (i, 0))
    return pl.pallas_call(
        _mla_q_kernel,
        grid=(s // tm, MLA_HEADS // hg),
        in_specs=[pl.BlockSpec((tm, MLA_Q_RANK), lambda i, h: (i, R_CQ // MLA_Q_RANK)),
                  pl.BlockSpec((1, MLA_Q_RANK), lambda i, h: (0, 0)),
                  pl.BlockSpec((MLA_Q_RANK, hg * MLA_QK_PAD), lambda i, h: (0, h)),
                  pl.BlockSpec((1, MLA_QK_PAD), lambda i, h: (0, 0)),
                  tab, tab, tab],
        out_specs=pl.BlockSpec((hg, MLA_QK_PAD, tm), lambda i, h: (h, 0, i)),
        out_shape=jax.ShapeDtypeStruct((MLA_HEADS, MLA_QK_PAD, s), BF16),
        scratch_shapes=[pltpu.VMEM((tm, MLA_Q_RANK), BF16)],
        compiler_params=_cp(("arbitrary", "arbitrary")),
        name="mla_q_prep",
    )(rest, cq_g, w_uq_pad, qg_pad, *rope_tabs)


def _mla_kv_kernel(ckv_ref, kr_ref, g_ref, w_ref, kgn_ref, kgr_ref, c_ref, s1_ref, s2_ref,
                   k_ref, vt_ref, xg_ref):
    h = pl.program_id(1)

    @pl.when(h == 0)
    def _():
        c = ckv_ref[...].astype(F32)
        r = lax.rsqrt(jnp.mean(c * c, axis=-1, keepdims=True) + NORM_EPS)
        xg_ref[...] = (c * r * g_ref[...]).astype(BF16)

    hw = MLA_NOPE_DIM + MLA_V_DIM
    kvg = jnp.dot(xg_ref[...], w_ref[...].astype(BF16), preferred_element_type=F32)
    lane = lax.broadcasted_iota(jnp.int32, kr_ref.shape, 1)
    kr = jnp.where(lane < MLA_ROPE_DIM, kr_ref[...].astype(F32), 0.0)
    kr_ss = jnp.sum(kr * kr, axis=-1, keepdims=True)
    for hh in range(MLA_HEAD_GROUP):
        kn = kvg[:, hh * hw:hh * hw + MLA_NOPE_DIM]
        ms = (jnp.sum(kn * kn, axis=-1, keepdims=True) + kr_ss) * (1.0 / MLA_QK_DIM)
        rs = lax.rsqrt(ms + NORM_EPS)
        k_ref[hh, :, :MLA_NOPE_DIM] = (kn * rs * kgn_ref[...]).astype(k_ref.dtype)
        k_ref[hh, :, MLA_NOPE_DIM:] = _rope_apply(kr * rs * kgr_ref[...], c_ref, s1_ref, s2_ref).astype(k_ref.dtype)
        vt_ref[hh] = kvg[:, hh * hw + MLA_NOPE_DIM:(hh + 1) * hw].T.astype(vt_ref.dtype)


def _mla_kv_prep(rest, ckv_g, w_ukv, kg_nope, kg_rope_pad, rope_tabs, tm):
    s = rest.shape[0]
    hg = MLA_HEAD_GROUP
    tab = pl.BlockSpec((tm, LANE), lambda i, h: (i, 0))
    hw = MLA_NOPE_DIM + MLA_V_DIM
    return pl.pallas_call(
        _mla_kv_kernel,
        grid=(s // tm, MLA_HEADS // hg),
        in_specs=[pl.BlockSpec((tm, MLA_KV_RANK), lambda i, h: (i, R_CKV // MLA_KV_RANK)),
                  pl.BlockSpec((tm, LANE), lambda i, h: (i, R_KROPE // LANE)),
                  pl.BlockSpec((1, MLA_KV_RANK), lambda i, h: (0, 0)),
                  pl.BlockSpec((MLA_KV_RANK, hg * hw), lambda i, h: (0, h)),
                  pl.BlockSpec((1, LANE), lambda i, h: (0, 0)),
                  pl.BlockSpec((1, LANE), lambda i, h: (0, 0)),
                  tab, tab, tab],
        out_specs=[pl.BlockSpec((hg, tm, MLA_QK_PAD), lambda i, h: (h, i, 0)),
                   pl.BlockSpec((hg, MLA_V_DIM, tm), lambda i, h: (h, 0, i))],
        out_shape=[jax.ShapeDtypeStruct((MLA_HEADS, s, MLA_QK_PAD), BF16),
                   jax.ShapeDtypeStruct((MLA_HEADS, MLA_V_DIM, s), BF16)],
        scratch_shapes=[pltpu.VMEM((tm, MLA_KV_RANK), BF16)],
        compiler_params=_cp(("arbitrary", "arbitrary")),
        name="mla_kv_prep",
    )(rest, rest, ckv_g, w_ukv, kg_nope, kg_rope_pad, *rope_tabs)


def _mla_attn_kernel(qt_ref, k_ref, vt_ref, o_ref, sa_ref, sb_ref, m_ref, l_ref, acc_ref):
    tk, tq = sa_ref.shape
    i = pl.program_id(1)
    qt = qt_ref[0]
    _init_stats(m_ref, l_ref, acc_ref)

    def scores(kb, dst):
        off = _block_offset(kb, tk)
        dst[...] = jnp.dot(k_ref[0, pl.ds(off, tk), :], qt, preferred_element_type=F32)

    def scores_right(kb, dst):
        off = _block_offset(kb, tk)
        dst[:, tk:] = jnp.dot(k_ref[0, pl.ds(off, tk), :], qt[:, tk:], preferred_element_type=F32)

    def consume(kb, src, diag, col0=0):
        off = _block_offset(kb, tk)
        s = src[:, col0:]
        if diag:
            krow = lax.broadcasted_iota(jnp.int32, s.shape, 0)
            qcol = lax.broadcasted_iota(jnp.int32, s.shape, 1)
            s = jnp.where(krow <= qcol, s, NEG_INF)
        _softmax_pv(0, s, vt_ref[0, :, pl.ds(off, tk)], m_ref, l_ref, acc_ref, col0)

    scores(0, sa_ref)

    def far_pair(j, c):
        kb = 2 * j
        scores(kb + 1, sb_ref)
        consume(kb, sa_ref, False)
        scores(kb + 2, sa_ref)
        consume(kb + 1, sb_ref, False)
        return c

    lax.fori_loop(0, i, far_pair, 0)
    scores_right(2 * i + 1, sb_ref)
    consume(2 * i, sa_ref, True)
    consume(2 * i + 1, sb_ref, True, col0=tk)
    o_ref[...] = (acc_ref[0] / l_ref[0]).T.astype(o_ref.dtype)


def _mla_attention(q_t, k, v_t):
    s = k.shape[1]
    tk = ATT_T
    tq = 2 * tk
    assert s % tq == 0
    return pl.pallas_call(
        _mla_attn_kernel,
        grid=(MLA_HEADS, s // tq),
        in_specs=[pl.BlockSpec((1, MLA_QK_PAD, tq), lambda h, i: (h, 0, i)),
                  pl.BlockSpec((1, s, MLA_QK_PAD), lambda h, i: (h, 0, 0)),
                  pl.BlockSpec((1, MLA_V_DIM, s), lambda h, i: (h, 0, 0))],
        out_specs=pl.BlockSpec((tq, MLA_V_DIM), lambda h, i: (i, h)),
        out_shape=jax.ShapeDtypeStruct((s, MLA_WIDTH), BF16),
        scratch_shapes=[pltpu.VMEM((tk, tq), F32), pltpu.VMEM((tk, tq), F32),
                        pltpu.VMEM((1, 1, tq), F32), pltpu.VMEM((1, 1, tq), F32),
                        pltpu.VMEM((1, MLA_V_DIM, tq), F32)],
        compiler_params=_cp(("arbitrary", "arbitrary")),
        name="mla_attention",
    )(q_t, k, v_t)


def _qk_nt(q, k_blk):
    return lax.dot_general(q, k_blk, (((1,), (1,)), ((), ())), preferred_element_type=F32)


def _mem_attn_kernel(q_ref, k_ref, v_ref, qg_ref, o_ref):
    shift = R_MQ % MEM_WINDOW
    qall = q_ref[...].astype(F32)[:, shift:shift + MEM_WIDTH]
    for h in range(MEM_HEADS):
        lo = h * MEM_HEAD_DIM
        qh = qall[:, lo:lo + MEM_HEAD_DIM]
        ms = jnp.mean(qh * qh, axis=-1, keepdims=True)
        qn = (qh * lax.rsqrt(ms + NORM_EPS) * qg_ref[...]).astype(BF16)
        s = _qk_nt(qn, k_ref[:, lo:lo + MEM_HEAD_DIM])
        p = jnp.exp(s - jnp.max(s, axis=-1, keepdims=True))
        l = jnp.sum(p, axis=-1, keepdims=True)
        o = jnp.dot(p.astype(BF16), v_ref[:, lo:lo + MEM_HEAD_DIM], preferred_element_type=F32)
        o_ref[:, lo:lo + MEM_HEAD_DIM] = (o / l).astype(o_ref.dtype)


def _mem_attention(rest, k_mem, v_mem, qg_scaled, tm):
    s = rest.shape[0]
    n_mem = k_mem.shape[0]
    assert R_MQ % MEM_WINDOW + MEM_WIDTH <= MEM_WINDOW
    return pl.pallas_call(
        _mem_attn_kernel,
        grid=(s // tm,),
        in_specs=[pl.BlockSpec((tm, MEM_WINDOW), lambda i: (i, R_MQ // MEM_WINDOW)),
                  pl.BlockSpec((n_mem, MEM_WIDTH), lambda i: (0, 0)),
                  pl.BlockSpec((n_mem, MEM_WIDTH), lambda i: (0, 0)),
                  pl.BlockSpec((1, MEM_HEAD_DIM), lambda i: (0, 0))],
        out_specs=pl.BlockSpec((tm, MEM_WIDTH), lambda i: (i, 0)),
        out_shape=jax.ShapeDtypeStruct((s, MEM_WIDTH), BF16),
        compiler_params=_cp(("parallel",)),
        name="mem_attention",
    )(rest, k_mem, v_mem, qg_scaled)


def _mix_kernel(od_ref, om_ref, oc_ref, wd_ref, wm_ref, wc_ref, g0_ref, g1_ref, g2_ref, o_ref):
    yd = jnp.dot(od_ref[...], wd_ref[...], preferred_element_type=F32)
    ym = jnp.dot(om_ref[...], wm_ref[...], preferred_element_type=F32)
    yc = jnp.dot(oc_ref[...], wc_ref[...], preferred_element_type=F32)
    mixed = (g0_ref[...].astype(F32) * yd + g1_ref[...].astype(F32) * ym) + g2_ref[...].astype(F32) * yc
    o_ref[...] = mixed.astype(o_ref.dtype)


def _mix(o_diff, o_mla, o_mem, w_d, w_m, w_c, gates, tm, tn):
    s = o_diff.shape[0]
    d = w_d.shape[1]
    nj = d // tn
    return pl.pallas_call(
        _mix_kernel,
        grid=(s // tm, nj),
        in_specs=[pl.BlockSpec((tm, o_diff.shape[1]), lambda i, j: (i, 0)),
                  pl.BlockSpec((tm, o_mla.shape[1]), lambda i, j: (i, 0)),
                  pl.BlockSpec((tm, o_mem.shape[1]), lambda i, j: (i, 0)),
                  pl.BlockSpec((w_d.shape[0], tn), lambda i, j: (0, j)),
                  pl.BlockSpec((w_m.shape[0], tn), lambda i, j: (0, j)),
                  pl.BlockSpec((w_c.shape[0], tn), lambda i, j: (0, j)),
                  pl.BlockSpec((tm, tn), lambda i, j: (i, j)),
                  pl.BlockSpec((tm, tn), lambda i, j: (i, nj + j)),
                  pl.BlockSpec((tm, tn), lambda i, j: (i, 2 * nj + j))],
        out_specs=pl.BlockSpec((tm, tn), lambda i, j: (i, j)),
        out_shape=jax.ShapeDtypeStruct((s, d), BF16),
        compiler_params=_cp(("arbitrary", "arbitrary")),
        name="gated_mix",
    )(o_diff, o_mla, o_mem, w_d, w_m, w_c, gates, gates, gates)


def _router_kernel(x_ref, g_ref, w_ref, b_ref, h_ref, r_ref):
    x = x_ref[...]
    ms = jnp.mean(x * x, axis=-1, keepdims=True)
    h = x * lax.rsqrt(ms + NORM_EPS) * g_ref[...]
    h_ref[...] = _pack_bf16_pairs(h)
    h_hi = h.astype(BF16)
    h_lo = (h - h_hi.astype(F32)).astype(BF16)
    logits = (jnp.dot(h_hi, w_ref[0], preferred_element_type=F32)
              + (jnp.dot(h_lo, w_ref[0], preferred_element_type=F32)
                 + jnp.dot(h_hi, w_ref[1], preferred_element_type=F32))) + b_ref[...]
    lane = lax.broadcasted_iota(jnp.int32, logits.shape, 1)
    lane_f = lane.astype(F32)
    big = float(4 * ROUTE_W)
    lg = jnp.where(lane < N_GROUPS, logits, -jnp.inf)
    gmax = jnp.max(lg, axis=-1, keepdims=True)
    gidx = jnp.min(jnp.where(lg == gmax, lane_f, big), axis=-1, keepdims=True)
    pg_top = 1.0 / jnp.sum(jnp.exp(lg - gmax), axis=-1, keepdims=True)
    e_lane = lane - N_GROUPS
    lane_group = jnp.right_shift(e_lane, 3).astype(F32)
    in_group = (e_lane >= 0) & (e_lane < N_EXPERTS) & (lane_group == gidx)
    le = jnp.where(in_group, logits, -jnp.inf)
    e1 = jnp.max(le, axis=-1, keepdims=True)
    i1 = jnp.min(jnp.where(le == e1, lane_f, big), axis=-1, keepdims=True)
    le2 = jnp.where(lane_f == i1, -jnp.inf, le)
    e2 = jnp.max(le2, axis=-1, keepdims=True)
    i2 = jnp.min(jnp.where(le2 == e2, lane_f, big), axis=-1, keepdims=True)
    w2 = jnp.exp(e2 - e1)
    inv = 1.0 / (1.0 + w2)
    gate1 = pg_top * inv
    gate2 = pg_top * (w2 * inv)
    out = jnp.where(lane == 0, i1 - N_GROUPS,
                    jnp.where(lane == 1, i2 - N_GROUPS,
                              jnp.where(lane == 2, gate1, jnp.where(lane == 3, gate2, 0.0))))
    r_ref[...] = out


def _router(x1, g, w_r, b_r, tm):
    s, d = x1.shape
    return pl.pallas_call(
        _router_kernel,
        grid=(s // tm,),
        in_specs=[pl.BlockSpec((tm, d), lambda i: (i, 0)),
                  pl.BlockSpec((1, d), lambda i: (0, 0)),
                  pl.BlockSpec((2, d, ROUTE_W), lambda i: (0, 0, 0)),
                  pl.BlockSpec((1, ROUTE_W), lambda i: (0, 0))],
        out_specs=[pl.BlockSpec((tm, d // 2), lambda i: (i, 0)),
                   pl.BlockSpec((tm, ROUTE_W), lambda i: (i, 0))],
        out_shape=[jax.ShapeDtypeStruct((s, d // 2), jnp.uint32),
                   jax.ShapeDtypeStruct((s, ROUTE_W), F32)],
        compiler_params=_cp(("parallel",)),
        name="ffn_norm_router",
    )(x1, g, w_r, b_r)


def _pack_bf16_pairs(v):
    n = v.shape[1] // 2
    bits = lax.bitcast_convert_type(v.astype(BF16).astype(F32), jnp.uint32)
    return jnp.right_shift(bits[:, :n], jnp.uint32(16)) | bits[:, n:]


def _unpack_bf16_pairs(words):
    lo = lax.bitcast_convert_type(jnp.left_shift(words, jnp.uint32(16)), F32)
    hi = lax.bitcast_convert_type(words & jnp.uint32(0xFFFF0000), F32)
    return lo, hi


def _moe_kernel(be_ref, nr_ref, nu_ref, tok_ref, tokn_ref, h_ref, wg_ref, wu_ref, wd_ref, o_ref,
                xg_ref, xb_ref, gp_ref, a_ref, wgb_ref, wub_ref, wdb_ref, sem):
    b = pl.program_id(0)
    c = pl.program_id(1)
    nb = pl.num_programs(0)
    nr = nr_ref[b]

    def row_copy(src_row, slot, r):
        return pltpu.make_async_copy(h_ref.at[pl.ds(src_row, 1), :], xg_ref.at[slot, pl.ds(r, 1), :], sem.at[slot])

    def start_gather(tok, n_rows, slot):
        unroll = 8

        def issue(i, carry):
            for j in range(unroll):
                r = i * unroll + j
                row_copy(tok[0, 0, r], slot, r).start(priority=1)
            return carry
        lax.fori_loop(0, n_rows // unroll, issue, 0)

    def for_row_count(fn):
        for units in range(1, MOE_TB // MOE_RU + 1):
            @pl.when(nr == units)
            def _():
                fn(units * MOE_RU)

    @pl.when(c == 0)
    def _():
        slot = b % 2

        @pl.when(b == 0)
        def _():
            start_gather(tok_ref, nr * MOE_RU, 0)

        def drain(u, carry):
            pltpu.make_async_copy(h_ref.at[pl.ds(0, MOE_RU), :], xg_ref.at[slot, pl.ds(0, MOE_RU), :],
                                  sem.at[slot]).wait()
            return carry
        lax.fori_loop(0, nr, drain, 0)

    @pl.when((c == MOE_PH - 1) & (b + 1 < nb))
    def _():
        start_gather(tokn_ref, nr_ref[jnp.minimum(b + 1, nb - 1)] * MOE_RU, 1 - b % 2)

    @pl.when(c < 2)
    def _():
        def unpack(u, carry):
            r0 = pl.multiple_of(u * MOE_RU, MOE_RU)
            lo, hi = _unpack_bf16_pairs(xg_ref[b % 2, pl.ds(r0, MOE_RU), :])

            @pl.when(c == 0)
            def _():
                xb_ref[0, pl.ds(r0, MOE_RU), :] = lo.astype(BF16)

            @pl.when(c == 1)
            def _():
                xb_ref[1, pl.ds(r0, MOE_RU), :] = hi.astype(BF16)

            return carry
        lax.fori_loop(0, nr, unpack, 0)

    @pl.when((c < 2) & (nr > 0))
    def _():
        wgb_ref[...] = wg_ref[0].astype(BF16)
        wub_ref[...] = wu_ref[0].astype(BF16)

        def gate_up(m):
            x = xb_ref[c, :m, :]
            g = jnp.dot(x, wgb_ref[...], preferred_element_type=F32)
            u = jnp.dot(x, wub_ref[...], preferred_element_type=F32)

            @pl.when(c == 0)
            def _():
                gp_ref[0, :m, :] = g
                gp_ref[1, :m, :] = u

            @pl.when(c == 1)
            def _():
                gs = gp_ref[0, :m, :] + g
                a_ref[:m, :] = ((gs * jax.nn.sigmoid(gs)) * (gp_ref[1, :m, :] + u)).astype(BF16)

        for_row_count(gate_up)

    @pl.when((c >= 2) & (nr > 0))
    def _():
        wdb_ref[...] = wd_ref[0].astype(BF16)

        def down(m):
            yv = jnp.dot(a_ref[:m, :], wdb_ref[...], preferred_element_type=F32)
            o_ref[:m, :] = _pack_bf16_pairs(yv)
            if m < MOE_TB:
                o_ref[m:, :] = jnp.zeros((MOE_TB - m, o_ref.shape[1]), jnp.uint32)

        for_row_count(down)

    @pl.when((c >= 2) & (nr == 0))
    def _():
        o_ref[...] = jnp.zeros(o_ref.shape, jnp.uint32)


def _moe_experts(block_e, nsub, n_used, buf_tok, h2p, w_gate, w_up, w_down):
    n_blocks = block_e.shape[0]
    half = h2p.shape[1]
    d = 2 * half
    ff = w_gate.shape[2]
    dh = d // 2
    tok = buf_tok.reshape(n_blocks, 1, MOE_TB)

    def gate_up_idx(b, c, be, nr, nu):
        live = b < nu[0]
        return be[jnp.minimum(b, nu[0] - 1)], jnp.where(live, jnp.minimum(c, 1), 1), 0

    def down_idx(b, c, be, nr, nu):
        bb = jnp.minimum(b, nu[0] - 1)
        cc = jnp.where(b < nu[0], c, MOE_PH - 1)
        e = jnp.where(cc >= 2, be[bb], be[jnp.maximum(bb - 1, 0)])
        return e, 0, jnp.where(cc == 2, 0, 1)

    def out_idx(b, c, be, nr, nu):
        ob = jnp.where(c >= 2, b, jnp.maximum(b - 1, 0))
        oc = jnp.where(c >= 2, c - 2, jnp.where(b > 0, 1, 0))
        return ob, oc

    grid_spec = pltpu.PrefetchScalarGridSpec(
        num_scalar_prefetch=3,
        grid=(n_blocks, MOE_PH),
        in_specs=[pl.BlockSpec((1, 1, MOE_TB), lambda b, c, be, nr, nu: (b, 0, 0), memory_space=pltpu.SMEM),
                  pl.BlockSpec((1, 1, MOE_TB), lambda b, c, be, nr, nu: (jnp.minimum(b + 1, n_blocks - 1), 0, 0),
                               memory_space=pltpu.SMEM),
                  pl.BlockSpec(memory_space=pl.ANY),
                  pl.BlockSpec((1, half, ff), gate_up_idx),
                  pl.BlockSpec((1, half, ff), gate_up_idx),
                  pl.BlockSpec((1, ff, dh), down_idx)],
        out_specs=pl.BlockSpec((MOE_TB, dh // 2), out_idx),
        scratch_shapes=[pltpu.VMEM((2, MOE_TB, half), jnp.uint32), pltpu.VMEM((2, MOE_TB, half), BF16),
                        pltpu.VMEM((2, MOE_TB, ff), F32), pltpu.VMEM((MOE_TB, ff), BF16),
                        pltpu.VMEM((half, ff), BF16), pltpu.VMEM((half, ff), BF16), pltpu.VMEM((ff, dh), BF16),
                        pltpu.SemaphoreType.DMA((2,))],
    )
    return pl.pallas_call(
        _moe_kernel,
        grid_spec=grid_spec,
        out_shape=jax.ShapeDtypeStruct((n_blocks * MOE_TB, half), jnp.uint32),
        compiler_params=_cp(("arbitrary", "arbitrary")),
        name="moe_experts",
    )(block_e, nsub, n_used, tok, tok, h2p, w_gate, w_up, w_down)


def _combine_kernel(slot_ref, slotn_ref, r_ref, x_ref, yb_ref, o_ref, g_ref, sem):
    tm = x_ref.shape[0]
    n = g_ref.shape[3]
    step = pl.program_id(0)
    buf = step % 2
    unroll = 4

    def start_gather(slots, dst):
        def issue(i, c):
            for j in range(unroll):
                r = i * unroll + j
                for k in range(TOP_K):
                    sl = slots[0, 0, r * TOP_K + k]
                    pltpu.make_async_copy(yb_ref.at[pl.ds(sl, 1), :], g_ref.at[dst, k, pl.ds(r, 1), :],
                                          sem.at[dst, k]).start(priority=k)
            return c
        lax.fori_loop(0, tm // unroll, issue, 0)

    @pl.when(step == 0)
    def _():
        start_gather(slot_ref, 0)

    @pl.when(step + 1 < pl.num_programs(0))
    def _():
        start_gather(slotn_ref, 1 - buf)

    for k in range(TOP_K):
        pltpu.make_async_copy(yb_ref.at[pl.ds(0, tm), :], g_ref.at[buf, k], sem.at[buf, k]).wait()

    route = r_ref[...]
    w0 = route[:, TOP_K:TOP_K + 1]
    w1 = route[:, TOP_K + 1:TOP_K + 2]
    q = n // 2
    for hf in range(2):
        lo0, hi0 = _unpack_bf16_pairs(g_ref[buf, 0, :, hf * q:(hf + 1) * q])
        lo1, hi1 = _unpack_bf16_pairs(g_ref[buf, 1, :, hf * q:(hf + 1) * q])
        c0 = hf * n
        o_ref[:, c0:c0 + q] = x_ref[:, c0:c0 + q] + (lo0 * w0 + lo1 * w1)
        o_ref[:, c0 + q:c0 + n] = x_ref[:, c0 + q:c0 + n] + (hi0 * w0 + hi1 * w1)


def _combine(slots, route, x1, yb, tm):
    s, d = x1.shape
    n_steps = s // tm
    slot_blocks = slots.reshape(n_steps, 1, tm * TOP_K)
    return pl.pallas_call(
        _combine_kernel,
        grid=(n_steps,),
        in_specs=[pl.BlockSpec((1, 1, tm * TOP_K), lambda i: (i, 0, 0), memory_space=pltpu.SMEM),
                  pl.BlockSpec((1, 1, tm * TOP_K), lambda i: (jnp.minimum(i + 1, n_steps - 1), 0, 0),
                               memory_space=pltpu.SMEM),
                  pl.BlockSpec((tm, ROUTE_W), lambda i: (i, 0)),
                  pl.BlockSpec((tm, d), lambda i: (i, 0)),
                  pl.BlockSpec(memory_space=pl.ANY)],
        out_specs=pl.BlockSpec((tm, d), lambda i: (i, 0)),
        out_shape=jax.ShapeDtypeStruct((s, d), F32),
        scratch_shapes=[pltpu.VMEM((2, TOP_K, tm, d // 2), jnp.uint32), pltpu.SemaphoreType.DMA((2, TOP_K))],
        compiler_params=_cp(("arbitrary",)),
        name="moe_combine",
    )(slot_blocks, slot_blocks, route, x1, yb)


def _dispatch_plan(route, s):
    a = s * TOP_K
    flat_e = route[:, :TOP_K].astype(jnp.int32).reshape(a)
    onehot = (flat_e[:, None] == jnp.arange(N_EXPERTS, dtype=jnp.int32)[None, :]).astype(jnp.int32)
    csum = jnp.cumsum(onehot, axis=0)
    counts = csum[-1]
    rank = jnp.sum((csum - onehot) * onehot, axis=1)
    padded = (counts + MOE_TB - 1) // MOE_TB * MOE_TB
    pad_end = jnp.cumsum(padded)
    pad_start = pad_end - padded
    dest = pad_start[flat_e] + rank
    n_blocks = a // MOE_TB + N_EXPERTS
    p_rows = n_blocks * MOE_TB
    buf_tok = jnp.zeros((p_rows,), jnp.int32).at[dest].set(jnp.arange(a, dtype=jnp.int32) // TOP_K)
    starts = jnp.arange(n_blocks, dtype=jnp.int32) * MOE_TB
    block_e = jnp.minimum(jnp.searchsorted(pad_end, starts, side='right'), N_EXPERTS - 1).astype(jnp.int32)
    valid = jnp.clip(counts[block_e] - (starts - pad_start[block_e]), 0, MOE_TB)
    valid = jnp.where(starts < pad_end[-1], valid, 0)
    nsub = ((valid + MOE_RU - 1) // MOE_RU).astype(jnp.int32)
    n_used = (pad_end[-1:] // MOE_TB).astype(jnp.int32)
    return block_e, nsub, n_used, buf_tok, dest.astype(jnp.int32)


def _rope_tables(positions):
    half = MLA_ROPE_DIM // 2
    inv_freq = ROPE_THETA ** (-jnp.arange(half, dtype=F32) / half)
    ang = positions.astype(F32)[:, None] * inv_freq[None, :]
    cos, sin = jnp.cos(ang), jnp.sin(ang)
    z = jnp.zeros_like(cos)
    c = jnp.concatenate([cos, cos, z, z], axis=-1)
    s1 = jnp.concatenate([-sin, z, z, z], axis=-1)
    s2 = jnp.concatenate([z, sin, z, z], axis=-1)
    return c, s1, s2


def kernel(x, mem, positions, rel_bias, mix_norm_g, w_in, diff_q_norm_g, diff_k_norm_g, diff_lambda_q1, diff_lambda_k1, diff_lambda_q2, diff_lambda_k2, diff_subln_g, mla_cq_norm_g, mla_ckv_norm_g, mla_w_uq, mla_w_ukv, mla_q_norm_g, mla_k_norm_g, mem_norm_g, mem_w_kv, mem_q_norm_g, mem_k_norm_g, w_o_diff, w_o_mla, w_o_mem, w_out, ffn_norm_g, w_route_group, b_route_group, w_route_expert, b_route_expert, w_exp_gate, w_exp_up, w_exp_down):
    b, s, d = x.shape
    assert b == 1 and s % ATT_T == 0
    depth = mix_norm_g.shape[0]
    xs = x.reshape(s, d)
    pos = positions.reshape(s)
    rope_tabs = _rope_tables(pos)
    row = lambda v: v.reshape(1, -1).astype(F32)

    for l in range(depth):
        lam_init = 0.8 - 0.6 * math.exp(-0.3 * l)
        h = _rmsnorm_rows(xs, mix_norm_g[l], 256)
        tn = 512
        q_gain = jnp.tile(diff_q_norm_g[l] * (DIFF_HEAD_DIM ** -0.5 * LOG2E), DIFF_MAPS)
        w_in_t = jnp.transpose(w_in[l])
        dq_t = _matmul_nt(h, w_in_t, row0=OFF_DQ, n_cols=DIFF_QK_WIDTH, tm=MM_TM, tn=tn, out_dtype=BF16,
                          mode="groupnorm", extra=q_gain, group=DIFF_HEAD_DIM, transpose_out=True,
                          name="diff_q_proj")
        dk = _matmul_nt(h, w_in_t, row0=OFF_DK, n_cols=DIFF_QK_WIDTH, tm=MM_TM, tn=tn, out_dtype=BF16,
                        mode="groupnorm", extra=jnp.tile(diff_k_norm_g[l], DIFF_MAPS), group=DIFF_HEAD_DIM,
                        name="diff_k_proj")
        dv_t = _matmul_nt(h, w_in_t, row0=OFF_DV, n_cols=DIFF_WIDTH, tm=MM_TM, tn=tn, out_dtype=BF16,
                          transpose_out=True, name="diff_v_proj")
        rest = _matmul_nt(h, w_in_t, row0=OFF_CQ, n_cols=REST_WIDTH, tm=MM_TM, tn=tn, out_dtype=BF16,
                          name="rest_proj")
        gates = _matmul_nt(h, w_in_t, row0=OFF_GATES, n_cols=3 * d, tm=MM_TM, tn=tn, out_dtype=BF16,
                           mode="sigmoid", name="gate_proj")

        bias_strips = _diff_bias_strips(rel_bias, ATT_T)
        lam_vecs = [row(diff_lambda_q1[l]), row(diff_lambda_k1[l]), row(diff_lambda_q2[l]), row(diff_lambda_k2[l])]
        o_diff = _diff_attention(dq_t, dk, dv_t, bias_strips, lam_vecs, diff_subln_g[l], lam_init)

        w_uq_heads = jnp.pad(
            mla_w_uq[l].reshape(MLA_Q_RANK, MLA_HEADS, MLA_QK_DIM),
            ((0, 0), (0, 0), (0, MLA_QK_PAD - MLA_QK_DIM))).reshape(MLA_Q_RANK, MLA_HEADS * MLA_QK_PAD).astype(BF16)
        qg_pad = jnp.pad(mla_q_norm_g[l] * (MLA_QK_DIM ** -0.5 * LOG2E),
                         (0, MLA_QK_PAD - MLA_QK_DIM)).reshape(1, -1).astype(F32)
        q_mla_t = _mla_q_prep(rest, row(mla_cq_norm_g[l]), w_uq_heads, qg_pad, rope_tabs, 512)
        kg = mla_k_norm_g[l]
        kg_nope = row(kg[:MLA_NOPE_DIM])
        kg_rope = jnp.pad(kg[MLA_NOPE_DIM:], (0, LANE - MLA_ROPE_DIM)).reshape(1, -1).astype(F32)
        k_mla, v_mla_t = _mla_kv_prep(rest, row(mla_ckv_norm_g[l]), mla_w_ukv[l], kg_nope, kg_rope, rope_tabs, 512)
        o_mla = _mla_attention(q_mla_t, k_mla, v_mla_t)

        n_mem = mem.shape[1]
        mem_h = _rmsnorm_rows(mem.reshape(n_mem, d), mem_norm_g[l], n_mem)
        k_mem = _matmul(mem_h, mem_w_kv[l], col0=0, n_cols=MEM_WIDTH, tm=n_mem, tn=512, out_dtype=BF16,
                        mode="groupnorm", extra=jnp.tile(mem_k_norm_g[l], MEM_HEADS), group=MEM_HEAD_DIM,
                        name="mem_k_proj")
        v_mem = _matmul(mem_h, mem_w_kv[l], col0=MEM_WIDTH, n_cols=MEM_WIDTH, tm=n_mem, tn=512, out_dtype=BF16,
                        name="mem_v_proj")
        o_mem = _mem_attention(rest, k_mem, v_mem, row(mem_q_norm_g[l] * MEM_HEAD_DIM ** -0.5), 512)

        mixed = _mix(o_diff, o_mla, o_mem, w_o_diff[l].astype(BF16), w_o_mla[l].astype(BF16),
                     w_o_mem[l].astype(BF16), gates, MM_TM, 512)
        x1 = _matmul(mixed, w_out[l], n_cols=d, tm=MM_TM, tn=512, out_dtype=F32, mode="residual", extra=xs,
                     name="out_proj")

        w_r = jnp.pad(jnp.concatenate([w_route_group[l], w_route_expert[l]], axis=1),
                      ((0, 0), (0, ROUTE_W - N_GROUPS - N_EXPERTS))).astype(F32)
        w_r_hi = w_r.astype(BF16)
        w_r = jnp.stack([w_r_hi, (w_r - w_r_hi.astype(F32)).astype(BF16)])
        b_r = jnp.pad(jnp.concatenate([b_route_group[l], b_route_expert[l]]),
                      (0, ROUTE_W - N_GROUPS - N_EXPERTS)).reshape(1, -1).astype(F32)
        h2, route = _router(x1, row(ffn_norm_g[l]), w_r, b_r, 256)
        block_e, nsub, n_used, buf_tok, slots = _dispatch_plan(route, s)
        yb = _moe_experts(block_e, nsub, n_used, buf_tok, h2, w_exp_gate[l], w_exp_up[l], w_exp_down[l])
        xs = _combine(slots, route, x1, yb, 256)
    return xs.reshape(b, s, d)
```

```python
import functools
import math

import jax
import jax.numpy as jnp
from jax import lax
from jax.experimental import pallas as pl
from jax.experimental.pallas import tpu as pltpu

F32 = jnp.float32
BF16 = jnp.bfloat16

NORM_EPS = 1e-6
NEG_INF = -1e30
LOG2E = math.log2(math.e)

DIFF_HEADS = 6
DIFF_HEAD_DIM = 128
DIFF_V_DIM = 256
DIFF_MAPS = 12
DIFF_QK_WIDTH = 1536
DIFF_WIDTH = 1536
MLA_HEADS = 12
MLA_Q_RANK = 1536
MLA_KV_RANK = 512
MLA_NOPE_DIM = 128
MLA_ROPE_DIM = 64
MLA_QK_DIM = 192
MLA_QK_PAD = 256
MLA_V_DIM = 128
MLA_WIDTH = 1536
ROPE_THETA = 10000.0
MEM_HEADS = 4
MEM_HEAD_DIM = 256
MEM_WIDTH = 1024
REL_BUCKETS = 32
REL_MAX_DIST = 128
N_GROUPS = 8
EXPERTS_PER_GROUP = 8
N_EXPERTS = 64
TOP_K = 2
EXPERT_FF = 512

OFF_DQ = 0
OFF_DK = 1536
OFF_DV = 3072
OFF_CQ = 4608
OFF_CKV = 6144
OFF_KROPE = 6656
OFF_MQ = 6720
OFF_GATES = 7744
REST_WIDTH = 4096
R_CQ = OFF_CQ - OFF_CQ
R_CKV = OFF_CKV - OFF_CQ
R_KROPE = OFF_KROPE - OFF_CQ
R_MQ = OFF_MQ - OFF_CQ

LANE = 128
VMEM_LIMIT = 52 * 1024 * 1024

MM_TM = 1024
MLA_HEAD_GROUP = 4
ATT_T = 512
MOE_TB = 512
MOE_RU = 128
MOE_PH = 4
ROUTE_W = 128
MEM_WINDOW = 2048


def _cp(sem, vmem=VMEM_LIMIT):
    return pltpu.CompilerParams(dimension_semantics=sem, vmem_limit_bytes=vmem)


def _rmsnorm_kernel(x_ref, g_ref, o_ref):
    x = x_ref[...].astype(F32)
    ms = jnp.mean(x * x, axis=-1, keepdims=True)
    o_ref[...] = (x * lax.rsqrt(ms + NORM_EPS) * g_ref[...]).astype(o_ref.dtype)


def _rmsnorm_rows(x, g, tm, out_dtype=BF16):
    m, d = x.shape
    return pl.pallas_call(
        _rmsnorm_kernel,
        grid=(m // tm,),
        in_specs=[pl.BlockSpec((tm, d), lambda i: (i, 0)),
                  pl.BlockSpec((1, d), lambda i: (0, 0))],
        out_specs=pl.BlockSpec((tm, d), lambda i: (i, 0)),
        out_shape=jax.ShapeDtypeStruct((m, d), out_dtype),
        compiler_params=_cp(("parallel",)),
        name="rmsnorm_rows",
    )(x, g.reshape(1, d).astype(F32))


def _cast_shifted(w_ref, w2_ref, wb_ref, shift):
    k = w_ref.shape[0]
    rows = 256

    def body(c, carry):
        r0 = pl.multiple_of(c * rows, rows)
        main = w_ref[pl.ds(r0, rows), :]
        tail = w2_ref[pl.ds(r0, rows), :]
        wb_ref[pl.ds(r0, rows), :] = jnp.concatenate([main[:, shift:], tail[:, :shift]], axis=1).astype(BF16)
        return carry

    lax.fori_loop(0, k // rows, body, 0)


def _mm_kernel(*refs, mode, cast, group, shift, transpose_out):
    a_ref, w_ref = refs[0], refs[1]
    pos = 2
    w2_ref = None
    if shift:
        w2_ref = refs[pos]
        pos += 1
    extra = None
    if mode in ("groupnorm", "residual"):
        extra = refs[pos]
        pos += 1
    o_ref = refs[pos]
    wb_ref = refs[pos + 1] if cast else None
    i = pl.program_id(1)
    if cast:
        @pl.when(i == 0)
        def _():
            if shift:
                _cast_shifted(w_ref, w2_ref, wb_ref, shift)
            else:
                wb_ref[...] = w_ref[...].astype(BF16)
        w = wb_ref[...]
    else:
        w = w_ref[...]
    acc = jnp.dot(a_ref[...], w, preferred_element_type=F32)
    if transpose_out:
        acc = acc.T
    if mode == "plain":
        o_ref[...] = acc.astype(o_ref.dtype)
    elif mode == "sigmoid":
        o_ref[...] = jax.nn.sigmoid(acc).astype(o_ref.dtype)
    elif mode == "residual":
        o_ref[...] = (extra[...] + acc).astype(o_ref.dtype)
    elif mode == "groupnorm":
        tn = w.shape[1]
        for c in range(tn // group):
            sl = slice(c * group, (c + 1) * group)
            if transpose_out:
                blk = acc[sl, :]
                ms = jnp.mean(blk * blk, axis=0, keepdims=True)
                o_ref[sl, :] = (blk * lax.rsqrt(ms + NORM_EPS) * extra[sl, :]).astype(o_ref.dtype)
            else:
                blk = acc[:, sl]
                ms = jnp.mean(blk * blk, axis=-1, keepdims=True)
                o_ref[:, sl] = (blk * lax.rsqrt(ms + NORM_EPS) * extra[:, sl]).astype(o_ref.dtype)


def _matmul(a, w, *, n_cols, tm, tn, out_dtype, col0=0, mode="plain", extra=None, group=LANE,
            transpose_out=False, name="matmul"):
    m, k = a.shape
    assert m % tm == 0 and n_cols % tn == 0 and w.shape[0] == k
    cast = w.dtype != BF16
    base, shift = divmod(col0, tn)
    assert shift <= LANE and (shift == 0 or cast)
    in_specs = [pl.BlockSpec((tm, k), lambda j, i: (i, 0)),
                pl.BlockSpec((k, tn), lambda j, i: (0, base + j))]
    args = [a, w]
    if shift:
        in_specs.append(pl.BlockSpec((k, LANE), lambda j, i: (0, (base + j + 1) * (tn // LANE))))
        args.append(w)
    if mode == "groupnorm":
        gain = extra.reshape(-1, 1) if transpose_out else extra.reshape(1, -1)
        in_specs.append(pl.BlockSpec((tn, 1), lambda j, i: (j, 0)) if transpose_out
                        else pl.BlockSpec((1, tn), lambda j, i: (0, j)))
        args.append(gain.astype(F32))
    elif mode == "residual":
        assert not transpose_out
        in_specs.append(pl.BlockSpec((tm, tn), lambda j, i: (i, j)))
        args.append(extra)
    if transpose_out:
        out_spec = pl.BlockSpec((tn, tm), lambda j, i: (j, i))
        out_shape = jax.ShapeDtypeStruct((n_cols, m), out_dtype)
    else:
        out_spec = pl.BlockSpec((tm, tn), lambda j, i: (i, j))
        out_shape = jax.ShapeDtypeStruct((m, n_cols), out_dtype)
    scratch = [pltpu.VMEM((k, tn), BF16)] if cast else []
    return pl.pallas_call(
        functools.partial(_mm_kernel, mode=mode, cast=cast, group=group, shift=shift, transpose_out=transpose_out),
        grid=(n_cols // tn, m // tm),
        in_specs=in_specs,
        out_specs=out_spec,
        out_shape=out_shape,
        scratch_shapes=scratch,
        compiler_params=_cp(("arbitrary", "arbitrary")),
        name=name,
    )(*args)


def _mm_nt_kernel(*refs, mode, group, transpose_out):
    a_ref, wt_ref = refs[0], refs[1]
    extra = refs[2] if mode == "groupnorm" else None
    o_ref, wb_ref = refs[-2], refs[-1]
    i = pl.program_id(1)

    @pl.when(i == 0)
    def _():
        wb_ref[...] = wt_ref[...].astype(BF16)

    contract_last = (((1,), (1,)), ((), ()))
    if transpose_out:
        acc = lax.dot_general(wb_ref[...], a_ref[...], contract_last, preferred_element_type=F32)
    else:
        acc = lax.dot_general(a_ref[...], wb_ref[...], contract_last, preferred_element_type=F32)
    if mode == "plain":
        o_ref[...] = acc.astype(o_ref.dtype)
    elif mode == "sigmoid":
        o_ref[...] = jax.nn.sigmoid(acc).astype(o_ref.dtype)
    elif mode == "groupnorm":
        tn = wb_ref.shape[0]
        for c in range(tn // group):
            sl = slice(c * group, (c + 1) * group)
            if transpose_out:
                blk = acc[sl, :]
                ms = jnp.mean(blk * blk, axis=0, keepdims=True)
                o_ref[sl, :] = (blk * lax.rsqrt(ms + NORM_EPS) * extra[sl, :]).astype(o_ref.dtype)
            else:
                blk = acc[:, sl]
                ms = jnp.mean(blk * blk, axis=-1, keepdims=True)
                o_ref[:, sl] = (blk * lax.rsqrt(ms + NORM_EPS) * extra[:, sl]).astype(o_ref.dtype)


def _matmul_nt(a, wt, *, row0, n_cols, tm, tn, out_dtype, mode="plain", extra=None, group=LANE,
               transpose_out=False, name="matmul_nt"):
    m, k = a.shape
    assert m % tm == 0 and n_cols % tn == 0 and wt.shape[1] == k and row0 % 8 == 0
    in_specs = [pl.BlockSpec((tm, k), lambda j, i: (i, 0)),
                pl.BlockSpec((pl.Element(tn), pl.Element(k)), lambda j, i: (pl.multiple_of(row0 + j * tn, 8), 0))]
    args = [a, wt]
    if mode == "groupnorm":
        gain = extra.reshape(-1, 1) if transpose_out else extra.reshape(1, -1)
        in_specs.append(pl.BlockSpec((tn, 1), lambda j, i: (j, 0)) if transpose_out
                        else pl.BlockSpec((1, tn), lambda j, i: (0, j)))
        args.append(gain.astype(F32))
    if transpose_out:
        out_spec = pl.BlockSpec((tn, tm), lambda j, i: (j, i))
        out_shape = jax.ShapeDtypeStruct((n_cols, m), out_dtype)
    else:
        out_spec = pl.BlockSpec((tm, tn), lambda j, i: (i, j))
        out_shape = jax.ShapeDtypeStruct((m, n_cols), out_dtype)
    return pl.pallas_call(
        functools.partial(_mm_nt_kernel, mode=mode, group=group, transpose_out=transpose_out),
        grid=(n_cols // tn, m // tm),
        in_specs=in_specs,
        out_specs=out_spec,
        out_shape=out_shape,
        scratch_shapes=[pltpu.VMEM((tn, k), BF16)],
        compiler_params=_cp(("arbitrary", "arbitrary")),
        name=name,
    )(*args)


def _softmax_pv(idx, s, vt_blk, m_ref, l_ref, acc_ref, col0=0):
    cols = slice(col0, col0 + s.shape[1])
    m_prev = m_ref[idx, :, cols]
    m_new = jnp.maximum(m_prev, jnp.max(s, axis=0, keepdims=True))
    alpha = jnp.exp2(m_prev - m_new)
    p = jnp.exp2(s - m_new)
    l_ref[idx, :, cols] = alpha * l_ref[idx, :, cols] + jnp.sum(p, axis=0, keepdims=True)
    acc_ref[idx, :, cols] = (alpha * acc_ref[idx, :, cols]
                             + jnp.dot(vt_blk, p.astype(BF16), preferred_element_type=F32))
    m_ref[idx, :, cols] = m_new


def _block_offset(kb, t):
    return kb * t if isinstance(kb, int) else pl.multiple_of(kb * t, t)


def _init_stats(m_ref, l_ref, acc_ref):
    m_ref[...] = jnp.full(m_ref.shape, NEG_INF, F32)
    l_ref[...] = jnp.zeros(l_ref.shape, F32)
    acc_ref[...] = jnp.zeros(acc_ref.shape, F32)


def _diff_attn_kernel(lq1_ref, lk1_ref, lq2_ref, lk2_ref, q1_ref, q2_ref, k1_ref, k2_ref, vt_ref,
                      b1_ref, b2_ref, g_ref, o_ref, sa_ref, sb_ref, m_ref, l_ref, acc_ref, *, lam_init):
    _, tk, tq = sa_ref.shape
    i = pl.program_id(1)
    qts = (q1_ref[...], q2_ref[...])
    ks = (k1_ref, k2_ref)
    bs = (b1_ref, b2_ref)
    _init_stats(m_ref, l_ref, acc_ref)

    def scores(kb, dst):
        off = _block_offset(kb, tk)
        for mp in range(2):
            dst[mp] = jnp.dot(ks[mp][pl.ds(off, tk), :], qts[mp], preferred_element_type=F32)

    def scores_right(kb, dst):
        off = _block_offset(kb, tk)
        for mp in range(2):
            dst[mp, :, tk:] = jnp.dot(ks[mp][pl.ds(off, tk), :], qts[mp][:, tk:], preferred_element_type=F32)

    def consume(kb, src, strip_off, col0=0):
        off = _block_offset(kb, tk)
        vt_blk = vt_ref[:, pl.ds(off, tk)]
        for mp in range(2):
            s = src[mp, :, col0:]
            if strip_off is not None:
                s = s + bs[mp][0, :, strip_off + col0:strip_off + tq]
            _softmax_pv(mp, s, vt_blk, m_ref, l_ref, acc_ref, col0)

    scores(0, sa_ref)

    def far_pair(j, c):
        kb = 2 * j
        scores(kb + 1, sb_ref)
        consume(kb, sa_ref, None)
        scores(kb + 2, sa_ref)
        consume(kb + 1, sb_ref, None)
        return c

    lax.fori_loop(0, jnp.maximum(i - 1, 0), far_pair, 0)

    @pl.when(i >= 1)
    def _():
        scores(2 * i - 1, sb_ref)
        consume(2 * i - 2, sa_ref, None)
        scores(2 * i, sa_ref)
        consume(2 * i - 1, sb_ref, 2 * tk)

    scores_right(2 * i + 1, sb_ref)
    consume(2 * i, sa_ref, tk)
    consume(2 * i + 1, sb_ref, 0, col0=tk)

    lam = (jnp.exp(jnp.sum(lq1_ref[...] * lk1_ref[...], axis=-1, keepdims=True))
           - jnp.exp(jnp.sum(lq2_ref[...] * lk2_ref[...], axis=-1, keepdims=True)) + lam_init)
    o = acc_ref[0] / l_ref[0] - lam * (acc_ref[1] / l_ref[1])
    ms = jnp.mean(o * o, axis=0, keepdims=True)
    o = (o * lax.rsqrt(ms + NORM_EPS) * g_ref[...]) * (1.0 - lam_init)
    o_ref[...] = o.T.astype(o_ref.dtype)


def _diff_attention(q_t, k, v_t, bias_strips, lam_vecs, subln_g, lam_init):
    s = k.shape[0]
    tk = ATT_T
    tq = 2 * tk
    assert s % tq == 0 and bias_strips.shape[1:] == (tk, 4 * tk)
    hd, vd = DIFF_HEAD_DIM, DIFF_V_DIM
    vec = pl.BlockSpec((1, hd), lambda h, i: (0, 0))
    in_specs = [vec, vec, vec, vec,
                pl.BlockSpec((hd, tq), lambda h, i: (h, i)),
                pl.BlockSpec((hd, tq), lambda h, i: (DIFF_HEADS + h, i)),
                pl.BlockSpec((s, hd), lambda h, i: (0, h)),
                pl.BlockSpec((s, hd), lambda h, i: (0, DIFF_HEADS + h)),
                pl.BlockSpec((vd, s), lambda h, i: (h, 0)),
                pl.BlockSpec((1, tk, 4 * tk), lambda h, i: (h, 0, 0), pipeline_mode=pl.Buffered(1)),
                pl.BlockSpec((1, tk, 4 * tk), lambda h, i: (DIFF_HEADS + h, 0, 0), pipeline_mode=pl.Buffered(1)),
                pl.BlockSpec((vd, 1), lambda h, i: (0, 0))]
    return pl.pallas_call(
        functools.partial(_diff_attn_kernel, lam_init=lam_init),
        grid=(DIFF_HEADS, s // tq),
        in_specs=in_specs,
        out_specs=pl.BlockSpec((tq, vd), lambda h, i: (i, h)),
        out_shape=jax.ShapeDtypeStruct((s, DIFF_WIDTH), BF16),
        scratch_shapes=[pltpu.VMEM((2, tk, tq), F32), pltpu.VMEM((2, tk, tq), F32),
                        pltpu.VMEM((2, 1, tq), F32), pltpu.VMEM((2, 1, tq), F32),
                        pltpu.VMEM((2, vd, tq), F32)],
        compiler_params=_cp(("arbitrary", "arbitrary")),
        name="diff_attention",
    )(*lam_vecs, q_t, q_t, k, k, v_t, bias_strips, bias_strips, subln_g.reshape(vd, 1).astype(F32))


def _t5_bucket(dist):
    n = jnp.maximum(dist, 0)
    max_exact = REL_BUCKETS // 2
    nf = jnp.maximum(n, 1).astype(F32)
    large = max_exact + (jnp.log(nf / max_exact) / math.log(REL_MAX_DIST / max_exact)
                         * (REL_BUCKETS - max_exact)).astype(jnp.int32)
    large = jnp.minimum(large, REL_BUCKETS - 1)
    return jnp.where(n < max_exact, n, large)


def _diff_bias_strips(rel_bias, tk):
    assert tk >= REL_MAX_DIST
    width = 4 * tk
    n = width + tk
    table = rel_bias.astype(F32)
    table = (table - table[REL_BUCKETS - 1:REL_BUCKETS]) * LOG2E
    kk = jnp.arange(n, dtype=jnp.int32)
    dist = kk - tk
    onehot = (_t5_bucket(dist)[:, None] == jnp.arange(REL_BUCKETS, dtype=jnp.int32)[None, :]).astype(F32)
    g = jnp.einsum('kb,bm->mk', onehot, table, precision=lax.Precision.HIGHEST)
    w = jnp.where(((dist >= 0) & (kk < width))[None, :], g, NEG_INF)
    maps = w.shape[0]

    def strip_kernel(w_ref, o_ref):
        rows = jnp.broadcast_to(w_ref[0], (tk, n))
        o_ref[0] = pltpu.roll(rows, 0, 1, stride=1, stride_axis=0)[:, :width]

    return pl.pallas_call(
        strip_kernel,
        grid=(maps,),
        in_specs=[pl.BlockSpec((1, 1, n), lambda m: (m, 0, 0))],
        out_specs=pl.BlockSpec((1, tk, width), lambda m: (m, 0, 0)),
        out_shape=jax.ShapeDtypeStruct((maps, tk, width), F32),
        compiler_params=_cp(("parallel",)),
        name="bias_strips",
    )(w.reshape(maps, 1, n))


def _rope_apply(tv, c_ref, s1_ref, s2_ref):
    return (tv * c_ref[...] + pltpu.roll(tv, 96, 1) * s1_ref[...] + pltpu.roll(tv, 32, 1) * s2_ref[...])


def _mla_q_kernel(cq_ref, g_ref, w_ref, qg_ref, c_ref, s1_ref, s2_ref, o_ref, xg_ref):
    h = pl.program_id(1)

    @pl.when(h == 0)
    def _():
        c = cq_ref[...].astype(F32)
        r = lax.rsqrt(jnp.mean(c * c, axis=-1, keepdims=True) + NORM_EPS)
        xg_ref[...] = (c * r * g_ref[...]).astype(BF16)

    ug = jnp.dot(xg_ref[...], w_ref[...], preferred_element_type=F32)
    for hh in range(MLA_HEAD_GROUP):
        u = ug[:, hh * MLA_QK_PAD:(hh + 1) * MLA_QK_PAD]
        ms = jnp.sum(u * u, axis=-1, keepdims=True) * (1.0 / MLA_QK_DIM)
        qn = u * lax.rsqrt(ms + NORM_EPS) * qg_ref[...]
        o_ref[hh, :MLA_NOPE_DIM, :] = qn[:, :MLA_NOPE_DIM].T.astype(o_ref.dtype)
        o_ref[hh, MLA_NOPE_DIM:, :] = _rope_apply(qn[:, MLA_NOPE_DIM:], c_ref, s1_ref, s2_ref).T.astype(o_ref.dtype)


def _mla_q_prep(rest, cq_g, w_uq_pad, qg_pad, rope_tabs, tm):
    s = rest.shape[0]
    hg = MLA_HEAD_GROUP
    tab = pl.BlockSpec((tm, LANE), lambda i, h: (i, 0))
    return pl.pallas_call(
        _mla_q_kernel,
        grid=(s // tm, MLA_HEADS // hg),
        in_specs=[pl.BlockSpec((tm, MLA_Q_RANK), lambda i, h: (i, R_CQ // MLA_Q_RANK)),
                  pl.BlockSpec((1, MLA_Q_RANK), lambda i, h: (0, 0)),
                  pl.BlockSpec((MLA_Q_RANK, hg * MLA_QK_PAD), lambda i, h: (0, h)),
                  pl.BlockSpec((1, MLA_QK_PAD), lambda i, h: (0, 0)),
                  tab, tab, tab],
        out_specs=pl.BlockSpec((hg, MLA_QK_PAD, tm), lambda i, h: (h, 0, i)),
        out_shape=jax.ShapeDtypeStruct((MLA_HEADS, MLA_QK_PAD, s), BF16),
        scratch_shapes=[pltpu.VMEM((tm, MLA_Q_RANK), BF16)],
        compiler_params=_cp(("arbitrary", "arbitrary")),
        name="mla_q_prep",
    )(rest, cq_g, w_uq_pad, qg_pad, *rope_tabs)


def _mla_kv_kernel(ckv_ref, kr_ref, g_ref, w_ref, kgn_ref, kgr_ref, c_ref, s1_ref, s2_ref,
                   k_ref, vt_ref, xg_ref):
    h = pl.program_id(1)

    @pl.when(h == 0)
    def _():
        c = ckv_ref[...].astype(F32)
        r = lax.rsqrt(jnp.mean(c * c, axis=-1, keepdims=True) + NORM_EPS)
        xg_ref[...] = (c * r * g_ref[...]).astype(BF16)

    hw = MLA_NOPE_DIM + MLA_V_DIM
    kvg = jnp.dot(xg_ref[...], w_ref[...].astype(BF16), preferred_element_type=F32)
    lane = lax.broadcasted_iota(jnp.int32, kr_ref.shape, 1)
    kr = jnp.where(lane < MLA_ROPE_DIM, kr_ref[...].astype(F32), 0.0)
    kr_ss = jnp.sum(kr * kr, axis=-1, keepdims=True)
    for hh in range(MLA_HEAD_GROUP):
        kn = kvg[:, hh * hw:hh * hw + MLA_NOPE_DIM]
        ms = (jnp.sum(kn * kn, axis=-1, keepdims=True) + kr_ss) * (1.0 / MLA_QK_DIM)
        rs = lax.rsqrt(ms + NORM_EPS)
        k_ref[hh, :, :MLA_NOPE_DIM] = (kn * rs * kgn_ref[...]).astype(k_ref.dtype)
        k_ref[hh, :, MLA_NOPE_DIM:] = _rope_apply(kr * rs * kgr_ref[...], c_ref, s1_ref, s2_ref).astype(k_ref.dtype)
        vt_ref[hh] = kvg[:, hh * hw + MLA_NOPE_DIM:(hh + 1) * hw].T.astype(vt_ref.dtype)


def _mla_kv_prep(rest, ckv_g, w_ukv, kg_nope, kg_rope_pad, rope_tabs, tm):
    s = rest.shape[0]
    hg = MLA_HEAD_GROUP
    tab = pl.BlockSpec((tm, LANE), lambda i, h: (i, 0))
    hw = MLA_NOPE_DIM + MLA_V_DIM
    return pl.pallas_call(
        _mla_kv_kernel,
        grid=(s // tm, MLA_HEADS // hg),
        in_specs=[pl.BlockSpec((tm, MLA_KV_RANK), lambda i, h: (i, R_CKV // MLA_KV_RANK)),
                  pl.BlockSpec((tm, LANE), lambda i, h: (i, R_KROPE // LANE)),
                  pl.BlockSpec((1, MLA_KV_RANK), lambda i, h: (0, 0)),
                  pl.BlockSpec((MLA_KV_RANK, hg * hw), lambda i, h: (0, h)),
                  pl.BlockSpec((1, LANE), lambda i, h: (0, 0)),
                  pl.BlockSpec((1, LANE), lambda i, h: (0, 0)),
                  tab, tab, tab],
        out_specs=[pl.BlockSpec((hg, tm, MLA_QK_PAD), lambda i, h: (h, i, 0)),
                   pl.BlockSpec((hg, MLA_V_DIM, tm), lambda i, h: (h, 0, i))],
        out_shape=[jax.ShapeDtypeStruct((MLA_HEADS, s, MLA_QK_PAD), BF16),
                   jax.ShapeDtypeStruct((MLA_HEADS, MLA_V_DIM, s), BF16)],
        scratch_shapes=[pltpu.VMEM((tm, MLA_KV_RANK), BF16)],
        compiler_params=_cp(("arbitrary", "arbitrary")),
        name="mla_kv_prep",
    )(rest, rest, ckv_g, w_ukv, kg_nope, kg_rope_pad, *rope_tabs)


def _mla_attn_kernel(qt_ref, k_ref, vt_ref, o_ref, sa_ref, sb_ref, m_ref, l_ref, acc_ref):
    tk, tq = sa_ref.shape
    i = pl.program_id(1)
    qt = qt_ref[0]
    _init_stats(m_ref, l_ref, acc_ref)

    def scores(kb, dst):
        off = _block_offset(kb, tk)
        dst[...] = jnp.dot(k_ref[0, pl.ds(off, tk), :], qt, preferred_element_type=F32)

    def scores_right(kb, dst):
        off = _block_offset(kb, tk)
        dst[:, tk:] = jnp.dot(k_ref[0, pl.ds(off, tk), :], qt[:, tk:], preferred_element_type=F32)

    def consume(kb, src, diag, col0=0):
        off = _block_offset(kb, tk)
        s = src[:, col0:]
        if diag:
            krow = lax.broadcasted_iota(jnp.int32, s.shape, 0)
            qcol = lax.broadcasted_iota(jnp.int32, s.shape, 1)
            s = jnp.where(krow <= qcol, s, NEG_INF)
        _softmax_pv(0, s, vt_ref[0, :, pl.ds(off, tk)], m_ref, l_ref, acc_ref, col0)

    scores(0, sa_ref)

    def far_pair(j, c):
        kb = 2 * j
        scores(kb + 1, sb_ref)
        consume(kb, sa_ref, False)
        scores(kb + 2, sa_ref)
        consume(kb + 1, sb_ref, False)
        return c

    lax.fori_loop(0, i, far_pair, 0)
    scores_right(2 * i + 1, sb_ref)
    consume(2 * i, sa_ref, True)
    consume(2 * i + 1, sb_ref, True, col0=tk)
    o_ref[...] = (acc_ref[0] / l_ref[0]).T.astype(o_ref.dtype)


def _mla_attention(q_t, k, v_t):
    s = k.shape[1]
    tk = ATT_T
    tq = 2 * tk
    assert s % tq == 0
    return pl.pallas_call(
        _mla_attn_kernel,
        grid=(MLA_HEADS, s // tq),
        in_specs=[pl.BlockSpec((1, MLA_QK_PAD, tq), lambda h, i: (h, 0, i)),
                  pl.BlockSpec((1, s, MLA_QK_PAD), lambda h, i: (h, 0, 0)),
                  pl.BlockSpec((1, MLA_V_DIM, s), lambda h, i: (h, 0, 0))],
        out_specs=pl.BlockSpec((tq, MLA_V_DIM), lambda h, i: (i, h)),
        out_shape=jax.ShapeDtypeStruct((s, MLA_WIDTH), BF16),
        scratch_shapes=[pltpu.VMEM((tk, tq), F32), pltpu.VMEM((tk, tq), F32),
                        pltpu.VMEM((1, 1, tq), F32), pltpu.VMEM((1, 1, tq), F32),
                        pltpu.VMEM((1, MLA_V_DIM, tq), F32)],
        compiler_params=_cp(("arbitrary", "arbitrary")),
        name="mla_attention",
    )(q_t, k, v_t)


def _qk_nt(q, k_blk):
    return lax.dot_general(q, k_blk, (((1,), (1,)), ((), ())), preferred_element_type=F32)


def _mem_attn_kernel(q_ref, k_ref, v_ref, qg_ref, o_ref):
    shift = R_MQ % MEM_WINDOW
    qall = q_ref[...].astype(F32)[:, shift:shift + MEM_WIDTH]
    for h in range(MEM_HEADS):
        lo = h * MEM_HEAD_DIM
        qh = qall[:, lo:lo + MEM_HEAD_DIM]
        ms = jnp.mean(qh * qh, axis=-1, keepdims=True)
        qn = (qh * lax.rsqrt(ms + NORM_EPS) * qg_ref[...]).astype(BF16)
        s = _qk_nt(qn, k_ref[:, lo:lo + MEM_HEAD_DIM])
        p = jnp.exp(s - jnp.max(s, axis=-1, keepdims=True))
        l = jnp.sum(p, axis=-1, keepdims=True)
        o = jnp.dot(p.astype(BF16), v_ref[:, lo:lo + MEM_HEAD_DIM], preferred_element_type=F32)
        o_ref[:, lo:lo + MEM_HEAD_DIM] = (o / l).astype(o_ref.dtype)


def _mem_attention(rest, k_mem, v_mem, qg_scaled, tm):
    s = rest.shape[0]
    n_mem = k_mem.shape[0]
    assert R_MQ % MEM_WINDOW + MEM_WIDTH <= MEM_WINDOW
    return pl.pallas_call(
        _mem_attn_kernel,
        grid=(s // tm,),
        in_specs=[pl.BlockSpec((tm, MEM_WINDOW), lambda i: (i, R_MQ // MEM_WINDOW)),
                  pl.BlockSpec((n_mem, MEM_WIDTH), lambda i: (0, 0)),
                  pl.BlockSpec((n_mem, MEM_WIDTH), lambda i: (0, 0)),
                  pl.BlockSpec((1, MEM_HEAD_DIM), lambda i: (0, 0))],
        out_specs=pl.BlockSpec((tm, MEM_WIDTH), lambda i: (i, 0)),
        out_shape=jax.ShapeDtypeStruct((s, MEM_WIDTH), BF16),
        compiler_params=_cp(("parallel",)),
        name="mem_attention",
    )(rest, k_mem, v_mem, qg_scaled)


def _mix_kernel(od_ref, om_ref, oc_ref, wd_ref, wm_ref, wc_ref, g0_ref, g1_ref, g2_ref, o_ref,
                wdb_ref, wmb_ref, wcb_ref):
    @pl.when(pl.program_id(1) == 0)
    def _():
        wdb_ref[...] = wd_ref[...].astype(BF16)
        wmb_ref[...] = wm_ref[...].astype(BF16)
        wcb_ref[...] = wc_ref[...].astype(BF16)

    yd = jnp.dot(od_ref[...], wdb_ref[...], preferred_element_type=F32)
    ym = jnp.dot(om_ref[...], wmb_ref[...], preferred_element_type=F32)
    yc = jnp.dot(oc_ref[...], wcb_ref[...], preferred_element_type=F32)
    mixed = (g0_ref[...].astype(F32) * yd + g1_ref[...].astype(F32) * ym) + g2_ref[...].astype(F32) * yc
    o_ref[...] = mixed.astype(o_ref.dtype)


def _mix(o_diff, o_mla, o_mem, w_d, w_m, w_c, gates, tm, tn):
    s = o_diff.shape[0]
    d = w_d.shape[1]
    nj = d // tn
    return pl.pallas_call(
        _mix_kernel,
        grid=(nj, s // tm),
        in_specs=[pl.BlockSpec((tm, o_diff.shape[1]), lambda j, i: (i, 0)),
                  pl.BlockSpec((tm, o_mla.shape[1]), lambda j, i: (i, 0)),
                  pl.BlockSpec((tm, o_mem.shape[1]), lambda j, i: (i, 0)),
                  pl.BlockSpec((w_d.shape[0], tn), lambda j, i: (0, j)),
                  pl.BlockSpec((w_m.shape[0], tn), lambda j, i: (0, j)),
                  pl.BlockSpec((w_c.shape[0], tn), lambda j, i: (0, j)),
                  pl.BlockSpec((tm, tn), lambda j, i: (i, j)),
                  pl.BlockSpec((tm, tn), lambda j, i: (i, nj + j)),
                  pl.BlockSpec((tm, tn), lambda j, i: (i, 2 * nj + j))],
        out_specs=pl.BlockSpec((tm, tn), lambda j, i: (i, j)),
        out_shape=jax.ShapeDtypeStruct((s, d), BF16),
        scratch_shapes=[pltpu.VMEM((w_d.shape[0], tn), BF16), pltpu.VMEM((w_m.shape[0], tn), BF16),
                        pltpu.VMEM((w_c.shape[0], tn), BF16)],
        compiler_params=_cp(("arbitrary", "arbitrary")),
        name="gated_mix",
    )(o_diff, o_mla, o_mem, w_d, w_m, w_c, gates, gates, gates)


def _router_kernel(x_ref, g_ref, w_ref, b_ref, h_ref, r_ref):
    x = x_ref[...]
    ms = jnp.mean(x * x, axis=-1, keepdims=True)
    h = x * lax.rsqrt(ms + NORM_EPS) * g_ref[...]
    h_ref[...] = _pack_bf16_pairs(h)
    h_hi = h.astype(BF16)
    h_lo = (h - h_hi.astype(F32)).astype(BF16)
    logits = (jnp.dot(h_hi, w_ref[0], preferred_element_type=F32)
              + (jnp.dot(h_lo, w_ref[0], preferred_element_type=F32)
                 + jnp.dot(h_hi, w_ref[1], preferred_element_type=F32))) + b_ref[...]
    lane = lax.broadcasted_iota(jnp.int32, logits.shape, 1)
    lane_f = lane.astype(F32)
    big = float(4 * ROUTE_W)
    lg = jnp.where(lane < N_GROUPS, logits, -jnp.inf)
    gmax = jnp.max(lg, axis=-1, keepdims=True)
    gidx = jnp.min(jnp.where(lg == gmax, lane_f, big), axis=-1, keepdims=True)
    pg_top = 1.0 / jnp.sum(jnp.exp(lg - gmax), axis=-1, keepdims=True)
    e_lane = lane - N_GROUPS
    lane_group = jnp.right_shift(e_lane, 3).astype(F32)
    in_group = (e_lane >= 0) & (e_lane < N_EXPERTS) & (lane_group == gidx)
    le = jnp.where(in_group, logits, -jnp.inf)
    e1 = jnp.max(le, axis=-1, keepdims=True)
    i1 = jnp.min(jnp.where(le == e1, lane_f, big), axis=-1, keepdims=True)
    le2 = jnp.where(lane_f == i1, -jnp.inf, le)
    e2 = jnp.max(le2, axis=-1, keepdims=True)
    i2 = jnp.min(jnp.where(le2 == e2, lane_f, big), axis=-1, keepdims=True)
    w2 = jnp.exp(e2 - e1)
    inv = 1.0 / (1.0 + w2)
    gate1 = pg_top * inv
    gate2 = pg_top * (w2 * inv)
    out = jnp.where(lane == 0, i1 - N_GROUPS,
                    jnp.where(lane == 1, i2 - N_GROUPS,
                              jnp.where(lane == 2, gate1, jnp.where(lane == 3, gate2, 0.0))))
    r_ref[...] = out


def _router(x1, g, w_r, b_r, tm):
    s, d = x1.shape
    return pl.pallas_call(
        _router_kernel,
        grid=(s // tm,),
        in_specs=[pl.BlockSpec((tm, d), lambda i: (i, 0)),
                  pl.BlockSpec((1, d), lambda i: (0, 0)),
                  pl.BlockSpec((2, d, ROUTE_W), lambda i: (0, 0, 0)),
                  pl.BlockSpec((1, ROUTE_W), lambda i: (0, 0))],
        out_specs=[pl.BlockSpec((tm, d // 2), lambda i: (i, 0)),
                   pl.BlockSpec((tm, ROUTE_W), lambda i: (i, 0))],
        out_shape=[jax.ShapeDtypeStruct((s, d // 2), jnp.uint32),
                   jax.ShapeDtypeStruct((s, ROUTE_W), F32)],
        compiler_params=_cp(("parallel",)),
        name="ffn_norm_router",
    )(x1, g, w_r, b_r)


def _pack_bf16_pairs(v):
    n = v.shape[1] // 2
    bits = lax.bitcast_convert_type(v.astype(BF16).astype(F32), jnp.uint32)
    return jnp.right_shift(bits[:, :n], jnp.uint32(16)) | bits[:, n:]


def _unpack_bf16_pairs(words):
    lo = lax.bitcast_convert_type(jnp.left_shift(words, jnp.uint32(16)), F32)
    hi = lax.bitcast_convert_type(words & jnp.uint32(0xFFFF0000), F32)
    return lo, hi


def _moe_kernel(be_ref, nr_ref, nu_ref, tok_ref, tokn_ref, h_ref, wg_ref, wu_ref, wd_ref, o_ref,
                xg_ref, xb_ref, gp_ref, a_ref, wgb_ref, wub_ref, wdb_ref, sem):
    b = pl.program_id(0)
    c = pl.program_id(1)
    nb = pl.num_programs(0)
    nr = nr_ref[b]

    def row_copy(src_row, slot, r):
        return pltpu.make_async_copy(h_ref.at[pl.ds(src_row, 1), :], xg_ref.at[slot, pl.ds(r, 1), :], sem.at[slot])

    def start_gather(tok, n_rows, slot):
        unroll = 8

        def issue(i, carry):
            for j in range(unroll):
                r = i * unroll + j
                row_copy(tok[0, 0, r], slot, r).start(priority=1)
            return carry
        lax.fori_loop(0, n_rows // unroll, issue, 0)

    def for_row_count(fn):
        for units in range(1, MOE_TB // MOE_RU + 1):
            @pl.when(nr == units)
            def _():
                fn(units * MOE_RU)

    @pl.when(c == 0)
    def _():
        slot = b % 2

        @pl.when(b == 0)
        def _():
            start_gather(tok_ref, nr * MOE_RU, 0)

        def drain(u, carry):
            pltpu.make_async_copy(h_ref.at[pl.ds(0, MOE_RU), :], xg_ref.at[slot, pl.ds(0, MOE_RU), :],
                                  sem.at[slot]).wait()
            return carry
        lax.fori_loop(0, nr, drain, 0)

        @pl.when(b + 1 < nb)
        def _():
            start_gather(tokn_ref, nr_ref[jnp.minimum(b + 1, nb - 1)] * MOE_RU, 1 - slot)

        def unpack(u, carry):
            r0 = pl.multiple_of(u * MOE_RU, MOE_RU)
            lo, hi = _unpack_bf16_pairs(xg_ref[slot, pl.ds(r0, MOE_RU), :])
            xb_ref[0, pl.ds(r0, MOE_RU), :] = lo.astype(BF16)
            xb_ref[1, pl.ds(r0, MOE_RU), :] = hi.astype(BF16)
            return carry
        lax.fori_loop(0, nr, unpack, 0)

    @pl.when((c < 2) & (nr > 0))
    def _():
        wgb_ref[...] = wg_ref[0].astype(BF16)
        wub_ref[...] = wu_ref[0].astype(BF16)

        def gate_up(m):
            x = xb_ref[c, :m, :]
            g = jnp.dot(x, wgb_ref[...], preferred_element_type=F32)
            u = jnp.dot(x, wub_ref[...], preferred_element_type=F32)

            @pl.when(c == 0)
            def _():
                gp_ref[0, :m, :] = g
                gp_ref[1, :m, :] = u

            @pl.when(c == 1)
            def _():
                gs = gp_ref[0, :m, :] + g
                a_ref[:m, :] = ((gs * jax.nn.sigmoid(gs)) * (gp_ref[1, :m, :] + u)).astype(BF16)

        for_row_count(gate_up)

    @pl.when((c >= 2) & (nr > 0))
    def _():
        wdb_ref[...] = wd_ref[0].astype(BF16)

        def down(m):
            yv = jnp.dot(a_ref[:m, :], wdb_ref[...], preferred_element_type=F32)
            o_ref[:m, :] = _pack_bf16_pairs(yv)
            if m < MOE_TB:
                o_ref[m:, :] = jnp.zeros((MOE_TB - m, o_ref.shape[1]), jnp.uint32)

        for_row_count(down)

    @pl.when((c >= 2) & (nr == 0))
    def _():
        o_ref[...] = jnp.zeros(o_ref.shape, jnp.uint32)


def _moe_experts(block_e, nsub, n_used, buf_tok, h2p, w_gate, w_up, w_down):
    n_blocks = block_e.shape[0]
    half = h2p.shape[1]
    d = 2 * half
    ff = w_gate.shape[2]
    dh = d // 2
    tok = buf_tok.reshape(n_blocks, 1, MOE_TB)

    def gate_up_idx(b, c, be, nr, nu):
        live = b < nu[0]
        return be[jnp.minimum(b, nu[0] - 1)], jnp.where(live, jnp.minimum(c, 1), 1), 0

    def down_idx(b, c, be, nr, nu):
        bb = jnp.minimum(b, nu[0] - 1)
        cc = jnp.where(b < nu[0], c, MOE_PH - 1)
        e = jnp.where(cc >= 2, be[bb], be[jnp.maximum(bb - 1, 0)])
        return e, 0, jnp.where(cc == 2, 0, 1)

    def out_idx(b, c, be, nr, nu):
        ob = jnp.where(c >= 2, b, jnp.maximum(b - 1, 0))
        oc = jnp.where(c >= 2, c - 2, jnp.where(b > 0, 1, 0))
        return ob, oc

    grid_spec = pltpu.PrefetchScalarGridSpec(
        num_scalar_prefetch=3,
        grid=(n_blocks, MOE_PH),
        in_specs=[pl.BlockSpec((1, 1, MOE_TB), lambda b, c, be, nr, nu: (b, 0, 0), memory_space=pltpu.SMEM),
                  pl.BlockSpec((1, 1, MOE_TB), lambda b, c, be, nr, nu: (jnp.minimum(b + 1, n_blocks - 1), 0, 0),
                               memory_space=pltpu.SMEM),
                  pl.BlockSpec(memory_space=pl.ANY),
                  pl.BlockSpec((1, half, ff), gate_up_idx),
                  pl.BlockSpec((1, half, ff), gate_up_idx),
                  pl.BlockSpec((1, ff, dh), down_idx)],
        out_specs=pl.BlockSpec((MOE_TB, dh // 2), out_idx),
        scratch_shapes=[pltpu.VMEM((2, MOE_TB, half), jnp.uint32), pltpu.VMEM((2, MOE_TB, half), BF16),
                        pltpu.VMEM((2, MOE_TB, ff), F32), pltpu.VMEM((MOE_TB, ff), BF16),
                        pltpu.VMEM((half, ff), BF16), pltpu.VMEM((half, ff), BF16), pltpu.VMEM((ff, dh), BF16),
                        pltpu.SemaphoreType.DMA((2,))],
    )
    return pl.pallas_call(
        _moe_kernel,
        grid_spec=grid_spec,
        out_shape=jax.ShapeDtypeStruct((n_blocks * MOE_TB, half), jnp.uint32),
        compiler_params=_cp(("arbitrary", "arbitrary")),
        name="moe_experts",
    )(block_e, nsub, n_used, tok, tok, h2p, w_gate, w_up, w_down)


def _combine_kernel(slot_ref, slotn_ref, r_ref, x_ref, yb_ref, o_ref, g_ref, sem):
    tm = x_ref.shape[0]
    n = g_ref.shape[3]
    step = pl.program_id(0)
    buf = step % 2
    unroll = 4

    def start_gather(slots, dst):
        def issue(i, c):
            for j in range(unroll):
                r = i * unroll + j
                for k in range(TOP_K):
                    sl = slots[0, 0, r * TOP_K + k]
                    pltpu.make_async_copy(yb_ref.at[pl.ds(sl, 1), :], g_ref.at[dst, k, pl.ds(r, 1), :],
                                          sem.at[dst, k]).start(priority=k)
            return c
        lax.fori_loop(0, tm // unroll, issue, 0)

    @pl.when(step == 0)
    def _():
        start_gather(slot_ref, 0)

    @pl.when(step + 1 < pl.num_programs(0))
    def _():
        start_gather(slotn_ref, 1 - buf)

    for k in range(TOP_K):
        pltpu.make_async_copy(yb_ref.at[pl.ds(0, tm), :], g_ref.at[buf, k], sem.at[buf, k]).wait()

    route = r_ref[...]
    w0 = route[:, TOP_K:TOP_K + 1]
    w1 = route[:, TOP_K + 1:TOP_K + 2]
    q = n // 2
    for hf in range(2):
        lo0, hi0 = _unpack_bf16_pairs(g_ref[buf, 0, :, hf * q:(hf + 1) * q])
        lo1, hi1 = _unpack_bf16_pairs(g_ref[buf, 1, :, hf * q:(hf + 1) * q])
        c0 = hf * n
        o_ref[:, c0:c0 + q] = x_ref[:, c0:c0 + q] + (lo0 * w0 + lo1 * w1)
        o_ref[:, c0 + q:c0 + n] = x_ref[:, c0 + q:c0 + n] + (hi0 * w0 + hi1 * w1)


def _combine(slots, route, x1, yb, tm):
    s, d = x1.shape
    n_steps = s // tm
    slot_blocks = slots.reshape(n_steps, 1, tm * TOP_K)
    return pl.pallas_call(
        _combine_kernel,
        grid=(n_steps,),
        in_specs=[pl.BlockSpec((1, 1, tm * TOP_K), lambda i: (i, 0, 0), memory_space=pltpu.SMEM),
                  pl.BlockSpec((1, 1, tm * TOP_K), lambda i: (jnp.minimum(i + 1, n_steps - 1), 0, 0),
                               memory_space=pltpu.SMEM),
                  pl.BlockSpec((tm, ROUTE_W), lambda i: (i, 0)),
                  pl.BlockSpec((tm, d), lambda i: (i, 0)),
                  pl.BlockSpec(memory_space=pl.ANY)],
        out_specs=pl.BlockSpec((tm, d), lambda i: (i, 0)),
        out_shape=jax.ShapeDtypeStruct((s, d), F32),
        scratch_shapes=[pltpu.VMEM((2, TOP_K, tm, d // 2), jnp.uint32), pltpu.SemaphoreType.DMA((2, TOP_K))],
        compiler_params=_cp(("arbitrary",)),
        name="moe_combine",
    )(slot_blocks, slot_blocks, route, x1, yb)


def _dispatch_plan(route, s):
    a = s * TOP_K
    flat_e = route[:, :TOP_K].astype(jnp.int32).reshape(a)
    chunk = 128
    onehot = (flat_e[:, None] == jnp.arange(N_EXPERTS, dtype=jnp.int32)[None, :]).astype(F32)
    oh = onehot.reshape(a // chunk, chunk, N_EXPERTS)
    strict_lower = jnp.tril(jnp.ones((chunk, chunk), F32), -1)
    within = jnp.einsum('ij,bjk->bik', strict_lower, oh, precision=lax.Precision.HIGHEST)
    totals = jnp.sum(oh, axis=1)
    before = jnp.cumsum(totals, axis=0) - totals
    rank = jnp.sum((within + before[:, None, :]) * oh, axis=2).reshape(a).astype(jnp.int32)
    counts = jnp.sum(totals, axis=0).astype(jnp.int32)
    padded = (counts + MOE_TB - 1) // MOE_TB * MOE_TB
    pad_end = jnp.cumsum(padded)
    pad_start = pad_end - padded
    dest = pad_start[flat_e] + rank
    n_blocks = a // MOE_TB + N_EXPERTS
    p_rows = n_blocks * MOE_TB
    buf_tok = jnp.zeros((p_rows,), jnp.int32).at[dest].set(jnp.arange(a, dtype=jnp.int32) // TOP_K)
    starts = jnp.arange(n_blocks, dtype=jnp.int32) * MOE_TB
    block_e = jnp.minimum(jnp.searchsorted(pad_end, starts, side='right'), N_EXPERTS - 1).astype(jnp.int32)
    valid = jnp.clip(counts[block_e] - (starts - pad_start[block_e]), 0, MOE_TB)
    valid = jnp.where(starts < pad_end[-1], valid, 0)
    nsub = ((valid + MOE_RU - 1) // MOE_RU).astype(jnp.int32)
    n_used = (pad_end[-1:] // MOE_TB).astype(jnp.int32)
    return block_e, nsub, n_used, buf_tok, dest.astype(jnp.int32)


def _rope_tables(positions):
    half = MLA_ROPE_DIM // 2
    inv_freq = ROPE_THETA ** (-jnp.arange(half, dtype=F32) / half)
    ang = positions.astype(F32)[:, None] * inv_freq[None, :]
    cos, sin = jnp.cos(ang), jnp.sin(ang)
    z = jnp.zeros_like(cos)
    c = jnp.concatenate([cos, cos, z, z], axis=-1)
    s1 = jnp.concatenate([-sin, z, z, z], axis=-1)
    s2 = jnp.concatenate([z, sin, z, z], axis=-1)
    return c, s1, s2


def kernel(x, mem, positions, rel_bias, mix_norm_g, w_in, diff_q_norm_g, diff_k_norm_g, diff_lambda_q1, diff_lambda_k1, diff_lambda_q2, diff_lambda_k2, diff_subln_g, mla_cq_norm_g, mla_ckv_norm_g, mla_w_uq, mla_w_ukv, mla_q_norm_g, mla_k_norm_g, mem_norm_g, mem_w_kv, mem_q_norm_g, mem_k_norm_g, w_o_diff, w_o_mla, w_o_mem, w_out, ffn_norm_g, w_route_group, b_route_group, w_route_expert, b_route_expert, w_exp_gate, w_exp_up, w_exp_down):
    b, s, d = x.shape
    assert b == 1 and s % ATT_T == 0
    depth = mix_norm_g.shape[0]
    xs = x.reshape(s, d)
    pos = positions.reshape(s)
    rope_tabs = _rope_tables(pos)
    row = lambda v: v.reshape(1, -1).astype(F32)

    for l in range(depth):
        lam_init = 0.8 - 0.6 * math.exp(-0.3 * l)
        h = _rmsnorm_rows(xs, mix_norm_g[l], 256)
        tn = 512
        q_gain = jnp.tile(diff_q_norm_g[l] * (DIFF_HEAD_DIM ** -0.5 * LOG2E), DIFF_MAPS)
        w_in_t = jnp.transpose(w_in[l])
        dq_t = _matmul_nt(h, w_in_t, row0=OFF_DQ, n_cols=DIFF_QK_WIDTH, tm=MM_TM, tn=tn, out_dtype=BF16,
                          mode="groupnorm", extra=q_gain, group=DIFF_HEAD_DIM, transpose_out=True,
                          name="diff_q_proj")
        dk = _matmul_nt(h, w_in_t, row0=OFF_DK, n_cols=DIFF_QK_WIDTH, tm=MM_TM, tn=tn, out_dtype=BF16,
                        mode="groupnorm", extra=jnp.tile(diff_k_norm_g[l], DIFF_MAPS), group=DIFF_HEAD_DIM,
                        name="diff_k_proj")
        dv_t = _matmul_nt(h, w_in_t, row0=OFF_DV, n_cols=DIFF_WIDTH, tm=MM_TM, tn=tn, out_dtype=BF16,
                          transpose_out=True, name="diff_v_proj")
        rest = _matmul_nt(h, w_in_t, row0=OFF_CQ, n_cols=REST_WIDTH, tm=MM_TM, tn=tn, out_dtype=BF16,
                          name="rest_proj")
        gates = _matmul_nt(h, w_in_t, row0=OFF_GATES, n_cols=3 * d, tm=MM_TM, tn=tn, out_dtype=BF16,
                           mode="sigmoid", name="gate_proj")

        bias_strips = _diff_bias_strips(rel_bias, ATT_T)
        lam_vecs = [row(diff_lambda_q1[l]), row(diff_lambda_k1[l]), row(diff_lambda_q2[l]), row(diff_lambda_k2[l])]
        o_diff = _diff_attention(dq_t, dk, dv_t, bias_strips, lam_vecs, diff_subln_g[l], lam_init)

        w_uq_heads = jnp.pad(
            mla_w_uq[l].reshape(MLA_Q_RANK, MLA_HEADS, MLA_QK_DIM),
            ((0, 0), (0, 0), (0, MLA_QK_PAD - MLA_QK_DIM))).reshape(MLA_Q_RANK, MLA_HEADS * MLA_QK_PAD).astype(BF16)
        qg_pad = jnp.pad(mla_q_norm_g[l] * (MLA_QK_DIM ** -0.5 * LOG2E),
                         (0, MLA_QK_PAD - MLA_QK_DIM)).reshape(1, -1).astype(F32)
        q_mla_t = _mla_q_prep(rest, row(mla_cq_norm_g[l]), w_uq_heads, qg_pad, rope_tabs, 512)
        kg = mla_k_norm_g[l]
        kg_nope = row(kg[:MLA_NOPE_DIM])
        kg_rope = jnp.pad(kg[MLA_NOPE_DIM:], (0, LANE - MLA_ROPE_DIM)).reshape(1, -1).astype(F32)
        k_mla, v_mla_t = _mla_kv_prep(rest, row(mla_ckv_norm_g[l]), mla_w_ukv[l], kg_nope, kg_rope, rope_tabs, 512)
        o_mla = _mla_attention(q_mla_t, k_mla, v_mla_t)

        n_mem = mem.shape[1]
        mem_h = _rmsnorm_rows(mem.reshape(n_mem, d), mem_norm_g[l], n_mem)
        k_mem = _matmul(mem_h, mem_w_kv[l], col0=0, n_cols=MEM_WIDTH, tm=n_mem, tn=512, out_dtype=BF16,
                        mode="groupnorm", extra=jnp.tile(mem_k_norm_g[l], MEM_HEADS), group=MEM_HEAD_DIM,
                        name="mem_k_proj")
        v_mem = _matmul(mem_h, mem_w_kv[l], col0=MEM_WIDTH, n_cols=MEM_WIDTH, tm=n_mem, tn=512, out_dtype=BF16,
                        name="mem_v_proj")
        o_mem = _mem_attention(rest, k_mem, v_mem, row(mem_q_norm_g[l] * MEM_HEAD_DIM ** -0.5), 512)

        mixed = _mix(o_diff, o_mla, o_mem, w_o_diff[l], w_o_mla[l], w_o_mem[l], gates, MM_TM, 512)
        x1 = _matmul(mixed, w_out[l], n_cols=d, tm=MM_TM, tn=512, out_dtype=F32, mode="residual", extra=xs,
                     name="out_proj")

        w_r = jnp.pad(jnp.concatenate([w_route_group[l], w_route_expert[l]], axis=1),
                      ((0, 0), (0, ROUTE_W - N_GROUPS - N_EXPERTS))).astype(F32)
        w_r_hi = w_r.astype(BF16)
        w_r = jnp.stack([w_r_hi, (w_r - w_r_hi.astype(F32)).astype(BF16)])
        b_r = jnp.pad(jnp.concatenate([b_route_group[l], b_route_expert[l]]),
                      (0, ROUTE_W - N_GROUPS - N_EXPERTS)).reshape(1, -1).astype(F32)
        h2, route = _router(x1, row(ffn_norm_g[l]), w_r, b_r, 256)
        block_e, nsub, n_used, buf_tok, slots = _dispatch_plan(route, s)
        yb = _moe_experts(block_e, nsub, n_used, buf_tok, h2, w_exp_gate[l], w_exp_up[l], w_exp_down[l])
        xs = _combine(slots, route, x1, yb, 256)
    return xs.reshape(b, s, d)
```

```python
import functools
import math

import jax
import jax.numpy as jnp
from jax import lax
from jax.experimental import pallas as pl
from jax.experimental.pallas import tpu as pltpu

F32 = jnp.float32
BF16 = jnp.bfloat16

NORM_EPS = 1e-6
NEG_INF = -1e30
LOG2E = math.log2(math.e)

DIFF_HEADS = 6
DIFF_HEAD_DIM = 128
DIFF_V_DIM = 256
DIFF_MAPS = 12
DIFF_QK_WIDTH = 1536
DIFF_WIDTH = 1536
MLA_HEADS = 12
MLA_Q_RANK = 1536
MLA_KV_RANK = 512
MLA_NOPE_DIM = 128
MLA_ROPE_DIM = 64
MLA_QK_DIM = 192
MLA_QK_PAD = 256
MLA_V_DIM = 128
MLA_WIDTH = 1536
ROPE_THETA = 10000.0
MEM_HEADS = 4
MEM_HEAD_DIM = 256
MEM_WIDTH = 1024
REL_BUCKETS = 32
REL_MAX_DIST = 128
N_GROUPS = 8
EXPERTS_PER_GROUP = 8
N_EXPERTS = 64
TOP_K = 2
EXPERT_FF = 512

OFF_DQ = 0
OFF_DK = 1536
OFF_DV = 3072
OFF_CQ = 4608
OFF_CKV = 6144
OFF_KROPE = 6656
OFF_MQ = 6720
OFF_GATES = 7744
REST_WIDTH = 3584
R_CQ = OFF_CQ - OFF_CQ
R_CKV = OFF_CKV - OFF_CQ
R_KROPE = OFF_KROPE - OFF_CQ
R_MQ = OFF_MQ - OFF_CQ

LANE = 128
VMEM_LIMIT = 52 * 1024 * 1024

MM_TM = 1024
MLA_HEAD_GROUP = 4
ATT_T = 512
MOE_TB = 512
MOE_RU = 128
MOE_PH = 4
ROUTE_W = 128
MEM_WIN_START = 2048
MEM_WINDOW = 1152


def _cp(sem, vmem=VMEM_LIMIT):
    return pltpu.CompilerParams(dimension_semantics=sem, vmem_limit_bytes=vmem)


def _rmsnorm_kernel(x_ref, g_ref, o_ref):
    x = x_ref[...].astype(F32)
    ms = jnp.mean(x * x, axis=-1, keepdims=True)
    o_ref[...] = (x * lax.rsqrt(ms + NORM_EPS) * g_ref[...]).astype(o_ref.dtype)


def _rmsnorm_rows(x, g, tm, out_dtype=BF16):
    m, d = x.shape
    return pl.pallas_call(
        _rmsnorm_kernel,
        grid=(m // tm,),
        in_specs=[pl.BlockSpec((tm, d), lambda i: (i, 0)),
                  pl.BlockSpec((1, d), lambda i: (0, 0))],
        out_specs=pl.BlockSpec((tm, d), lambda i: (i, 0)),
        out_shape=jax.ShapeDtypeStruct((m, d), out_dtype),
        compiler_params=_cp(("parallel",)),
        name="rmsnorm_rows",
    )(x, g.reshape(1, d).astype(F32))


def _cast_shifted(w_ref, w2_ref, wb_ref, shift):
    k = w_ref.shape[0]
    rows = 256

    def body(c, carry):
        r0 = pl.multiple_of(c * rows, rows)
        main = w_ref[pl.ds(r0, rows), :]
        tail = w2_ref[pl.ds(r0, rows), :]
        wb_ref[pl.ds(r0, rows), :] = jnp.concatenate([main[:, shift:], tail[:, :shift]], axis=1).astype(BF16)
        return carry

    lax.fori_loop(0, k // rows, body, 0)


def _mm_kernel(*refs, mode, cast, group, shift, transpose_out):
    a_ref, w_ref = refs[0], refs[1]
    pos = 2
    w2_ref = None
    if shift:
        w2_ref = refs[pos]
        pos += 1
    extra = None
    if mode in ("groupnorm", "residual"):
        extra = refs[pos]
        pos += 1
    o_ref = refs[pos]
    wb_ref = refs[pos + 1] if cast else None
    i = pl.program_id(1)
    if cast:
        @pl.when(i == 0)
        def _():
            if shift:
                _cast_shifted(w_ref, w2_ref, wb_ref, shift)
            else:
                wb_ref[...] = w_ref[...].astype(BF16)
        w = wb_ref[...]
    else:
        w = w_ref[...]
    acc = jnp.dot(a_ref[...], w, preferred_element_type=F32)
    if transpose_out:
        acc = acc.T
    if mode == "plain":
        o_ref[...] = acc.astype(o_ref.dtype)
    elif mode == "sigmoid":
        o_ref[...] = jax.nn.sigmoid(acc).astype(o_ref.dtype)
    elif mode == "residual":
        o_ref[...] = (extra[...] + acc).astype(o_ref.dtype)
    elif mode == "groupnorm":
        tn = w.shape[1]
        for c in range(tn // group):
            sl = slice(c * group, (c + 1) * group)
            if transpose_out:
                blk = acc[sl, :]
                ms = jnp.mean(blk * blk, axis=0, keepdims=True)
                o_ref[sl, :] = (blk * lax.rsqrt(ms + NORM_EPS) * extra[sl, :]).astype(o_ref.dtype)
            else:
                blk = acc[:, sl]
                ms = jnp.mean(blk * blk, axis=-1, keepdims=True)
                o_ref[:, sl] = (blk * lax.rsqrt(ms + NORM_EPS) * extra[:, sl]).astype(o_ref.dtype)


def _matmul(a, w, *, n_cols, tm, tn, out_dtype, col0=0, mode="plain", extra=None, group=LANE,
            transpose_out=False, name="matmul"):
    m, k = a.shape
    assert m % tm == 0 and n_cols % tn == 0 and w.shape[0] == k
    cast = w.dtype != BF16
    base, shift = divmod(col0, tn)
    assert shift <= LANE and (shift == 0 or cast)
    in_specs = [pl.BlockSpec((tm, k), lambda j, i: (i, 0)),
                pl.BlockSpec((k, tn), lambda j, i: (0, base + j))]
    args = [a, w]
    if shift:
        in_specs.append(pl.BlockSpec((k, LANE), lambda j, i: (0, (base + j + 1) * (tn // LANE))))
        args.append(w)
    if mode == "groupnorm":
        gain = extra.reshape(-1, 1) if transpose_out else extra.reshape(1, -1)
        in_specs.append(pl.BlockSpec((tn, 1), lambda j, i: (j, 0)) if transpose_out
                        else pl.BlockSpec((1, tn), lambda j, i: (0, j)))
        args.append(gain.astype(F32))
    elif mode == "residual":
        assert not transpose_out
        in_specs.append(pl.BlockSpec((tm, tn), lambda j, i: (i, j)))
        args.append(extra)
    if transpose_out:
        out_spec = pl.BlockSpec((tn, tm), lambda j, i: (j, i))
        out_shape = jax.ShapeDtypeStruct((n_cols, m), out_dtype)
    else:
        out_spec = pl.BlockSpec((tm, tn), lambda j, i: (i, j))
        out_shape = jax.ShapeDtypeStruct((m, n_cols), out_dtype)
    scratch = [pltpu.VMEM((k, tn), BF16)] if cast else []
    return pl.pallas_call(
        functools.partial(_mm_kernel, mode=mode, cast=cast, group=group, shift=shift, transpose_out=transpose_out),
        grid=(n_cols // tn, m // tm),
        in_specs=in_specs,
        out_specs=out_spec,
        out_shape=out_shape,
        scratch_shapes=scratch,
        compiler_params=_cp(("arbitrary", "arbitrary")),
        name=name,
    )(*args)


def _mm_nt_kernel(*refs, mode, group, transpose_out):
    a_ref, wt_ref = refs[0], refs[1]
    extra = refs[2] if mode == "groupnorm" else None
    o_ref, wb_ref = refs[-2], refs[-1]
    i = pl.program_id(1)

    @pl.when(i == 0)
    def _():
        wb_ref[...] = wt_ref[...].astype(BF16)

    contract_last = (((1,), (1,)), ((), ()))
    if transpose_out:
        acc = lax.dot_general(wb_ref[...], a_ref[...], contract_last, preferred_element_type=F32)
    else:
        acc = lax.dot_general(a_ref[...], wb_ref[...], contract_last, preferred_element_type=F32)
    if mode == "plain":
        o_ref[...] = acc.astype(o_ref.dtype)
    elif mode == "sigmoid":
        o_ref[...] = jax.nn.sigmoid(acc).astype(o_ref.dtype)
    elif mode == "groupnorm":
        tn = wb_ref.shape[0]
        for c in range(tn // group):
            sl = slice(c * group, (c + 1) * group)
            if transpose_out:
                blk = acc[sl, :]
                ms = jnp.mean(blk * blk, axis=0, keepdims=True)
                o_ref[sl, :] = (blk * lax.rsqrt(ms + NORM_EPS) * extra[sl, :]).astype(o_ref.dtype)
            else:
                blk = acc[:, sl]
                ms = jnp.mean(blk * blk, axis=-1, keepdims=True)
                o_ref[:, sl] = (blk * lax.rsqrt(ms + NORM_EPS) * extra[:, sl]).astype(o_ref.dtype)


def _matmul_nt(a, wt, *, row0, n_cols, tm, tn, out_dtype, mode="plain", extra=None, group=LANE,
               transpose_out=False, name="matmul_nt"):
    m, k = a.shape
    assert m % tm == 0 and n_cols % tn == 0 and wt.shape[1] == k and row0 % 8 == 0
    in_specs = [pl.BlockSpec((tm, k), lambda j, i: (i, 0)),
                pl.BlockSpec((pl.Element(tn), pl.Element(k)), lambda j, i: (pl.multiple_of(row0 + j * tn, 8), 0))]
    args = [a, wt]
    if mode == "groupnorm":
        gain = extra.reshape(-1, 1) if transpose_out else extra.reshape(1, -1)
        in_specs.append(pl.BlockSpec((tn, 1), lambda j, i: (j, 0)) if transpose_out
                        else pl.BlockSpec((1, tn), lambda j, i: (0, j)))
        args.append(gain.astype(F32))
    if transpose_out:
        out_spec = pl.BlockSpec((tn, tm), lambda j, i: (j, i))
        out_shape = jax.ShapeDtypeStruct((n_cols, m), out_dtype)
    else:
        out_spec = pl.BlockSpec((tm, tn), lambda j, i: (i, j))
        out_shape = jax.ShapeDtypeStruct((m, n_cols), out_dtype)
    return pl.pallas_call(
        functools.partial(_mm_nt_kernel, mode=mode, group=group, transpose_out=transpose_out),
        grid=(n_cols // tn, m // tm),
        in_specs=in_specs,
        out_specs=out_spec,
        out_shape=out_shape,
        scratch_shapes=[pltpu.VMEM((tn, k), BF16)],
        compiler_params=_cp(("arbitrary", "arbitrary")),
        name=name,
    )(*args)


def _softmax_pv(idx, s, vt_blk, m_ref, l_ref, acc_ref, col0=0):
    cols = slice(col0, col0 + s.shape[1])
    m_prev = m_ref[idx, :, cols]
    m_new = jnp.maximum(m_prev, jnp.max(s, axis=0, keepdims=True))
    alpha = jnp.exp2(m_prev - m_new)
    p = jnp.exp2(s - m_new)
    l_ref[idx, :, cols] = alpha * l_ref[idx, :, cols] + jnp.sum(p, axis=0, keepdims=True)
    acc_ref[idx, :, cols] = (alpha * acc_ref[idx, :, cols]
                             + jnp.dot(vt_blk, p.astype(BF16), preferred_element_type=F32))
    m_ref[idx, :, cols] = m_new


def _block_offset(kb, t):
    return kb * t if isinstance(kb, int) else pl.multiple_of(kb * t, t)


def _init_stats(m_ref, l_ref, acc_ref):
    m_ref[...] = jnp.full(m_ref.shape, NEG_INF, F32)
    l_ref[...] = jnp.zeros(l_ref.shape, F32)
    acc_ref[...] = jnp.zeros(acc_ref.shape, F32)


def _diff_attn_kernel(lq1_ref, lk1_ref, lq2_ref, lk2_ref, q1_ref, q2_ref, k1_ref, k2_ref, vt_ref,
                      b1_ref, b2_ref, g_ref, o_ref, sa_ref, sb_ref, m_ref, l_ref, acc_ref, *, lam_init):
    _, tk, tq = sa_ref.shape
    i = pl.program_id(1)
    qts = (q1_ref[...], q2_ref[...])
    ks = (k1_ref, k2_ref)
    bs = (b1_ref, b2_ref)
    _init_stats(m_ref, l_ref, acc_ref)

    def scores(kb, dst):
        off = _block_offset(kb, tk)
        for mp in range(2):
            dst[mp] = jnp.dot(ks[mp][pl.ds(off, tk), :], qts[mp], preferred_element_type=F32)

    def scores_right(kb, dst):
        off = _block_offset(kb, tk)
        for mp in range(2):
            dst[mp, :, tk:] = jnp.dot(ks[mp][pl.ds(off, tk), :], qts[mp][:, tk:], preferred_element_type=F32)

    def consume(kb, src, strip_off, col0=0):
        off = _block_offset(kb, tk)
        vt_blk = vt_ref[:, pl.ds(off, tk)]
        for mp in range(2):
            s = src[mp, :, col0:]
            if strip_off is not None:
                s = s + bs[mp][0, :, strip_off + col0:strip_off + tq]
            _softmax_pv(mp, s, vt_blk, m_ref, l_ref, acc_ref, col0)

    scores(0, sa_ref)

    def far_pair(j, c):
        kb = 2 * j
        scores(kb + 1, sb_ref)
        consume(kb, sa_ref, None)
        scores(kb + 2, sa_ref)
        consume(kb + 1, sb_ref, None)
        return c

    lax.fori_loop(0, jnp.maximum(i - 1, 0), far_pair, 0)

    @pl.when(i >= 1)
    def _():
        scores(2 * i - 1, sb_ref)
        consume(2 * i - 2, sa_ref, None)
        scores(2 * i, sa_ref)
        consume(2 * i - 1, sb_ref, 2 * tk)

    scores_right(2 * i + 1, sb_ref)
    consume(2 * i, sa_ref, tk)
    consume(2 * i + 1, sb_ref, 0, col0=tk)

    lam = (jnp.exp(jnp.sum(lq1_ref[...] * lk1_ref[...], axis=-1, keepdims=True))
           - jnp.exp(jnp.sum(lq2_ref[...] * lk2_ref[...], axis=-1, keepdims=True)) + lam_init)
    o = acc_ref[0] / l_ref[0] - lam * (acc_ref[1] / l_ref[1])
    ms = jnp.mean(o * o, axis=0, keepdims=True)
    o = (o * lax.rsqrt(ms + NORM_EPS) * g_ref[...]) * (1.0 - lam_init)
    o_ref[...] = o.T.astype(o_ref.dtype)


def _diff_attention(q_t, k, v_t, bias_strips, lam_vecs, subln_g, lam_init):
    s = k.shape[0]
    tk = ATT_T
    tq = 2 * tk
    assert s % tq == 0 and bias_strips.shape[1:] == (tk, 4 * tk)
    hd, vd = DIFF_HEAD_DIM, DIFF_V_DIM
    vec = pl.BlockSpec((1, hd), lambda h, i: (0, 0))
    in_specs = [vec, vec, vec, vec,
                pl.BlockSpec((hd, tq), lambda h, i: (h, i)),
                pl.BlockSpec((hd, tq), lambda h, i: (DIFF_HEADS + h, i)),
                pl.BlockSpec((s, hd), lambda h, i: (0, h)),
                pl.BlockSpec((s, hd), lambda h, i: (0, DIFF_HEADS + h)),
                pl.BlockSpec((vd, s), lambda h, i: (h, 0)),
                pl.BlockSpec((1, tk, 4 * tk), lambda h, i: (h, 0, 0), pipeline_mode=pl.Buffered(1)),
                pl.BlockSpec((1, tk, 4 * tk), lambda h, i: (DIFF_HEADS + h, 0, 0), pipeline_mode=pl.Buffered(1)),
                pl.BlockSpec((vd, 1), lambda h, i: (0, 0))]
    return pl.pallas_call(
        functools.partial(_diff_attn_kernel, lam_init=lam_init),
        grid=(DIFF_HEADS, s // tq),
        in_specs=in_specs,
        out_specs=pl.BlockSpec((tq, vd), lambda h, i: (i, h)),
        out_shape=jax.ShapeDtypeStruct((s, DIFF_WIDTH), BF16),
        scratch_shapes=[pltpu.VMEM((2, tk, tq), F32), pltpu.VMEM((2, tk, tq), F32),
                        pltpu.VMEM((2, 1, tq), F32), pltpu.VMEM((2, 1, tq), F32),
                        pltpu.VMEM((2, vd, tq), F32)],
        compiler_params=_cp(("arbitrary", "arbitrary")),
        name="diff_attention",
    )(*lam_vecs, q_t, q_t, k, k, v_t, bias_strips, bias_strips, subln_g.reshape(vd, 1).astype(F32))


def _t5_bucket(dist):
    n = jnp.maximum(dist, 0)
    max_exact = REL_BUCKETS // 2
    nf = jnp.maximum(n, 1).astype(F32)
    large = max_exact + (jnp.log(nf / max_exact) / math.log(REL_MAX_DIST / max_exact)
                         * (REL_BUCKETS - max_exact)).astype(jnp.int32)
    large = jnp.minimum(large, REL_BUCKETS - 1)
    return jnp.where(n < max_exact, n, large)


def _diff_bias_strips(rel_bias, tk):
    assert tk >= REL_MAX_DIST
    width = 4 * tk
    n = width + tk
    table = rel_bias.astype(F32)
    table = (table - table[REL_BUCKETS - 1:REL_BUCKETS]) * LOG2E
    kk = jnp.arange(n, dtype=jnp.int32)
    dist = kk - tk
    onehot = (_t5_bucket(dist)[:, None] == jnp.arange(REL_BUCKETS, dtype=jnp.int32)[None, :]).astype(F32)
    g = jnp.einsum('kb,bm->mk', onehot, table, precision=lax.Precision.HIGHEST)
    w = jnp.where(((dist >= 0) & (kk < width))[None, :], g, NEG_INF)
    maps = w.shape[0]

    def strip_kernel(w_ref, o_ref):
        rows = jnp.broadcast_to(w_ref[0], (tk, n))
        o_ref[0] = pltpu.roll(rows, 0, 1, stride=1, stride_axis=0)[:, :width]

    return pl.pallas_call(
        strip_kernel,
        grid=(maps,),
        in_specs=[pl.BlockSpec((1, 1, n), lambda m: (m, 0, 0))],
        out_specs=pl.BlockSpec((1, tk, width), lambda m: (m, 0, 0)),
        out_shape=jax.ShapeDtypeStruct((maps, tk, width), F32),
        compiler_params=_cp(("parallel",)),
        name="bias_strips",
    )(w.reshape(maps, 1, n))


def _rope_apply(tv, c_ref, s1_ref, s2_ref):
    return (tv * c_ref[...] + pltpu.roll(tv, 96, 1) * s1_ref[...] + pltpu.roll(tv, 32, 1) * s2_ref[...])


def _mla_q_kernel(cq_ref, g_ref, w_ref, qg_ref, c_ref, s1_ref, s2_ref, o_ref, xg_ref):
    h = pl.program_id(1)

    @pl.when(h == 0)
    def _():
        c = cq_ref[...].astype(F32)
        r = lax.rsqrt(jnp.mean(c * c, axis=-1, keepdims=True) + NORM_EPS)
        xg_ref[...] = (c * r * g_ref[...]).astype(BF16)

    ug = jnp.dot(xg_ref[...], w_ref[...], preferred_element_type=F32)
    for hh in range(MLA_HEAD_GROUP):
        u = ug[:, hh * MLA_QK_PAD:(hh + 1) * MLA_QK_PAD]
        ms = jnp.sum(u * u, axis=-1, keepdims=True) * (1.0 / MLA_QK_DIM)
        qn = u * lax.rsqrt(ms + NORM_EPS) * qg_ref[...]
        o_ref[hh, :MLA_NOPE_DIM, :] = qn[:, :MLA_NOPE_DIM].T.astype(o_ref.dtype)
        o_ref[hh, MLA_NOPE_DIM:, :] = _rope_apply(qn[:, MLA_NOPE_DIM:], c_ref, s1_ref, s2_ref).T.astype(o_ref.dtype)


def _mla_q_prep(rest, cq_g, w_uq_pad, qg_pad, rope_tabs, tm):
    s = rest.shape[0]
    hg = MLA_HEAD_GROUP
    tab = pl.BlockSpec((tm, LANE), lambda i, h: (i, 0))
    return pl.pallas_call(
        _mla_q_kernel,
        grid=(s // tm, MLA_HEADS // hg),
        in_specs=[pl.BlockSpec((tm, MLA_Q_RANK), lambda i, h: (i, R_CQ // MLA_Q_RANK)),
                  pl.BlockSpec((1, MLA_Q_RANK), lambda i, h: (0, 0)),
                  pl.BlockSpec((MLA_Q_RANK, hg * MLA_QK_PAD), lambda i, h: (0, h)),
                  pl.BlockSpec((1, MLA_QK_PAD), lambda i, h: (0, 0)),
                  tab, tab, tab],
        out_specs=pl.BlockSpec((hg, MLA_QK_PAD, tm), lambda i, h: (h, 0, i)),
        out_shape=jax.ShapeDtypeStruct((MLA_HEADS, MLA_QK_PAD, s), BF16),
        scratch_shapes=[pltpu.VMEM((tm, MLA_Q_RANK), BF16)],
        compiler_params=_cp(("arbitrary", "arbitrary")),
        name="mla_q_prep",
    )(rest, cq_g, w_uq_pad, qg_pad, *rope_tabs)


def _mla_kv_kernel(ckv_ref, kr_ref, g_ref, w_ref, kgn_ref, kgr_ref, c_ref, s1_ref, s2_ref,
                   k_ref, vt_ref, xg_ref):
    h = pl.program_id(1)

    @pl.when(h == 0)
    def _():
        c = ckv_ref[...].astype(F32)
        r = lax.rsqrt(jnp.mean(c * c, axis=-1, keepdims=True) + NORM_EPS)
        xg_ref[...] = (c * r * g_ref[...]).astype(BF16)

    hw = MLA_NOPE_DIM + MLA_V_DIM
    kvg = jnp.dot(xg_ref[...], w_ref[...].astype(BF16), preferred_element_type=F32)
    lane = lax.broadcasted_iota(jnp.int32, kr_ref.shape, 1)
    kr = jnp.where(lane < MLA_ROPE_DIM, kr_ref[...].astype(F32), 0.0)
    kr_ss = jnp.sum(kr * kr, axis=-1, keepdims=True)
    for hh in range(MLA_HEAD_GROUP):
        kn = kvg[:, hh * hw:hh * hw + MLA_NOPE_DIM]
        ms = (jnp.sum(kn * kn, axis=-1, keepdims=True) + kr_ss) * (1.0 / MLA_QK_DIM)
        rs = lax.rsqrt(ms + NORM_EPS)
        k_ref[hh, :, :MLA_NOPE_DIM] = (kn * rs * kgn_ref[...]).astype(k_ref.dtype)
        k_ref[hh, :, MLA_NOPE_DIM:] = _rope_apply(kr * rs * kgr_ref[...], c_ref, s1_ref, s2_ref).astype(k_ref.dtype)
        vt_ref[hh] = kvg[:, hh * hw + MLA_NOPE_DIM:(hh + 1) * hw].T.astype(vt_ref.dtype)


def _mla_kv_prep(rest, ckv_g, w_ukv, kg_nope, kg_rope_pad, rope_tabs, tm):
    s = rest.shape[0]
    hg = MLA_HEAD_GROUP
    tab = pl.BlockSpec((tm, LANE), lambda i, h: (i, 0))
    hw = MLA_NOPE_DIM + MLA_V_DIM
    return pl.pallas_call(
        _mla_kv_kernel,
        grid=(s // tm, MLA_HEADS // hg),
        in_specs=[pl.BlockSpec((tm, MLA_KV_RANK), lambda i, h: (i, R_CKV // MLA_KV_RANK)),
                  pl.BlockSpec((tm, LANE), lambda i, h: (i, R_KROPE // LANE)),
                  pl.BlockSpec((1, MLA_KV_RANK), lambda i, h: (0, 0)),
                  pl.BlockSpec((MLA_KV_RANK, hg * hw), lambda i, h: (0, h)),
                  pl.BlockSpec((1, LANE), lambda i, h: (0, 0)),
                  pl.BlockSpec((1, LANE), lambda i, h: (0, 0)),
                  tab, tab, tab],
        out_specs=[pl.BlockSpec((hg, tm, MLA_QK_PAD), lambda i, h: (h, i, 0)),
                   pl.BlockSpec((hg, MLA_V_DIM, tm), lambda i, h: (h, 0, i))],
        out_shape=[jax.ShapeDtypeStruct((MLA_HEADS, s, MLA_QK_PAD), BF16),
                   jax.ShapeDtypeStruct((MLA_HEADS, MLA_V_DIM, s), BF16)],
        scratch_shapes=[pltpu.VMEM((tm, MLA_KV_RANK), BF16)],
        compiler_params=_cp(("arbitrary", "arbitrary")),
        name="mla_kv_prep",
    )(rest, rest, ckv_g, w_ukv, kg_nope, kg_rope_pad, *rope_tabs)


def _mla_attn_kernel(qt_ref, k_ref, vt_ref, o_ref, sa_ref, sb_ref, m_ref, l_ref, acc_ref):
    tk, tq = sa_ref.shape
    i = pl.program_id(1)
    qt = qt_ref[0]
    _init_stats(m_ref, l_ref, acc_ref)

    def scores(kb, dst):
        off = _block_offset(kb, tk)
        dst[...] = jnp.dot(k_ref[0, pl.ds(off, tk), :], qt, preferred_element_type=F32)

    def scores_right(kb, dst):
        off = _block_offset(kb, tk)
        dst[:, tk:] = jnp.dot(k_ref[0, pl.ds(off, tk), :], qt[:, tk:], preferred_element_type=F32)

    def consume(kb, src, diag, col0=0):
        off = _block_offset(kb, tk)
        s = src[:, col0:]
        if diag:
            krow = lax.broadcasted_iota(jnp.int32, s.shape, 0)
            qcol = lax.broadcasted_iota(jnp.int32, s.shape, 1)
            s = jnp.where(krow <= qcol, s, NEG_INF)
        _softmax_pv(0, s, vt_ref[0, :, pl.ds(off, tk)], m_ref, l_ref, acc_ref, col0)

    scores(0, sa_ref)

    def far_pair(j, c):
        kb = 2 * j
        scores(kb + 1, sb_ref)
        consume(kb, sa_ref, False)
        scores(kb + 2, sa_ref)
        consume(kb + 1, sb_ref, False)
        return c

    lax.fori_loop(0, i, far_pair, 0)
    scores_right(2 * i + 1, sb_ref)
    consume(2 * i, sa_ref, True)
    consume(2 * i + 1, sb_ref, True, col0=tk)
    o_ref[...] = (acc_ref[0] / l_ref[0]).T.astype(o_ref.dtype)


def _mla_attention(q_t, k, v_t):
    s = k.shape[1]
    tk = ATT_T
    tq = 2 * tk
    assert s % tq == 0
    return pl.pallas_call(
        _mla_attn_kernel,
        grid=(MLA_HEADS, s // tq),
        in_specs=[pl.BlockSpec((1, MLA_QK_PAD, tq), lambda h, i: (h, 0, i)),
                  pl.BlockSpec((1, s, MLA_QK_PAD), lambda h, i: (h, 0, 0)),
                  pl.BlockSpec((1, MLA_V_DIM, s), lambda h, i: (h, 0, 0))],
        out_specs=pl.BlockSpec((tq, MLA_V_DIM), lambda h, i: (i, h)),
        out_shape=jax.ShapeDtypeStruct((s, MLA_WIDTH), BF16),
        scratch_shapes=[pltpu.VMEM((tk, tq), F32), pltpu.VMEM((tk, tq), F32),
                        pltpu.VMEM((1, 1, tq), F32), pltpu.VMEM((1, 1, tq), F32),
                        pltpu.VMEM((1, MLA_V_DIM, tq), F32)],
        compiler_params=_cp(("arbitrary", "arbitrary")),
        name="mla_attention",
    )(q_t, k, v_t)


def _qk_nt(q, k_blk):
    return lax.dot_general(q, k_blk, (((1,), (1,)), ((), ())), preferred_element_type=F32)


def _mem_attn_kernel(q_ref, k_ref, v_ref, qg_ref, o_ref):
    shift = R_MQ - MEM_WIN_START
    qall = q_ref[...].astype(F32)[:, shift:shift + MEM_WIDTH]
    for h in range(MEM_HEADS):
        lo = h * MEM_HEAD_DIM
        qh = qall[:, lo:lo + MEM_HEAD_DIM]
        ms = jnp.mean(qh * qh, axis=-1, keepdims=True)
        qn = (qh * lax.rsqrt(ms + NORM_EPS) * qg_ref[...]).astype(BF16)
        s = _qk_nt(qn, k_ref[:, lo:lo + MEM_HEAD_DIM])
        p = jnp.exp(s - jnp.max(s, axis=-1, keepdims=True))
        l = jnp.sum(p, axis=-1, keepdims=True)
        o = jnp.dot(p.astype(BF16), v_ref[:, lo:lo + MEM_HEAD_DIM], preferred_element_type=F32)
        o_ref[:, lo:lo + MEM_HEAD_DIM] = (o / l).astype(o_ref.dtype)


def _mem_attention(rest, k_mem, v_mem, qg_scaled, tm):
    s = rest.shape[0]
    n_mem = k_mem.shape[0]
    assert 0 <= R_MQ - MEM_WIN_START and R_MQ - MEM_WIN_START + MEM_WIDTH <= MEM_WINDOW
    assert MEM_WIN_START % LANE == 0 and MEM_WIN_START + MEM_WINDOW <= rest.shape[1]
    return pl.pallas_call(
        _mem_attn_kernel,
        grid=(s // tm,),
        in_specs=[pl.BlockSpec((pl.Element(tm), pl.Element(MEM_WINDOW)),
                               lambda i: (pl.multiple_of(i * tm, tm), MEM_WIN_START)),
                  pl.BlockSpec((n_mem, MEM_WIDTH), lambda i: (0, 0)),
                  pl.BlockSpec((n_mem, MEM_WIDTH), lambda i: (0, 0)),
                  pl.BlockSpec((1, MEM_HEAD_DIM), lambda i: (0, 0))],
        out_specs=pl.BlockSpec((tm, MEM_WIDTH), lambda i: (i, 0)),
        out_shape=jax.ShapeDtypeStruct((s, MEM_WIDTH), BF16),
        compiler_params=_cp(("parallel",)),
        name="mem_attention",
    )(rest, k_mem, v_mem, qg_scaled)


def _mix_kernel(od_ref, om_ref, oc_ref, wd_ref, wm_ref, wc_ref, g0_ref, g1_ref, g2_ref, o_ref,
                wdb_ref, wmb_ref, wcb_ref):
    @pl.when(pl.program_id(1) == 0)
    def _():
        wdb_ref[...] = wd_ref[...].astype(BF16)
        wmb_ref[...] = wm_ref[...].astype(BF16)
        wcb_ref[...] = wc_ref[...].astype(BF16)

    yd = jnp.dot(od_ref[...], wdb_ref[...], preferred_element_type=F32)
    ym = jnp.dot(om_ref[...], wmb_ref[...], preferred_element_type=F32)
    yc = jnp.dot(oc_ref[...], wcb_ref[...], preferred_element_type=F32)
    mixed = (g0_ref[...].astype(F32) * yd + g1_ref[...].astype(F32) * ym) + g2_ref[...].astype(F32) * yc
    o_ref[...] = mixed.astype(o_ref.dtype)


def _mix(o_diff, o_mla, o_mem, w_d, w_m, w_c, gates, tm, tn):
    s = o_diff.shape[0]
    d = w_d.shape[1]
    nj = d // tn
    return pl.pallas_call(
        _mix_kernel,
        grid=(nj, s // tm),
        in_specs=[pl.BlockSpec((tm, o_diff.shape[1]), lambda j, i: (i, 0)),
                  pl.BlockSpec((tm, o_mla.shape[1]), lambda j, i: (i, 0)),
                  pl.BlockSpec((tm, o_mem.shape[1]), lambda j, i: (i, 0)),
                  pl.BlockSpec((w_d.shape[0], tn), lambda j, i: (0, j)),
                  pl.BlockSpec((w_m.shape[0], tn), lambda j, i: (0, j)),
                  pl.BlockSpec((w_c.shape[0], tn), lambda j, i: (0, j)),
                  pl.BlockSpec((tm, tn), lambda j, i: (i, j)),
                  pl.BlockSpec((tm, tn), lambda j, i: (i, nj + j)),
                  pl.BlockSpec((tm, tn), lambda j, i: (i, 2 * nj + j))],
        out_specs=pl.BlockSpec((tm, tn), lambda j, i: (i, j)),
        out_shape=jax.ShapeDtypeStruct((s, d), BF16),
        scratch_shapes=[pltpu.VMEM((w_d.shape[0], tn), BF16), pltpu.VMEM((w_m.shape[0], tn), BF16),
                        pltpu.VMEM((w_c.shape[0], tn), BF16)],
        compiler_params=_cp(("arbitrary", "arbitrary")),
        name="gated_mix",
    )(o_diff, o_mla, o_mem, w_d, w_m, w_c, gates, gates, gates)


def _router_kernel(x_ref, g_ref, w_ref, b_ref, h_ref, r_ref):
    x = x_ref[...]
    ms = jnp.mean(x * x, axis=-1, keepdims=True)
    h = x * lax.rsqrt(ms + NORM_EPS) * g_ref[...]
    h_ref[...] = _pack_bf16_pairs(h)
    h_hi = h.astype(BF16)
    h_lo = (h - h_hi.astype(F32)).astype(BF16)
    logits = (jnp.dot(h_hi, w_ref[0], preferred_element_type=F32)
              + (jnp.dot(h_lo, w_ref[0], preferred_element_type=F32)
                 + jnp.dot(h_hi, w_ref[1], preferred_element_type=F32))) + b_ref[...]
    lane = lax.broadcasted_iota(jnp.int32, logits.shape, 1)
    lane_f = lane.astype(F32)
    big = float(4 * ROUTE_W)
    lg = jnp.where(lane < N_GROUPS, logits, -jnp.inf)
    gmax = jnp.max(lg, axis=-1, keepdims=True)
    gidx = jnp.min(jnp.where(lg == gmax, lane_f, big), axis=-1, keepdims=True)
    pg_top = 1.0 / jnp.sum(jnp.exp(lg - gmax), axis=-1, keepdims=True)
    e_lane = lane - N_GROUPS
    lane_group = jnp.right_shift(e_lane, 3).astype(F32)
    in_group = (e_lane >= 0) & (e_lane < N_EXPERTS) & (lane_group == gidx)
    le = jnp.where(in_group, logits, -jnp.inf)
    e1 = jnp.max(le, axis=-1, keepdims=True)
    i1 = jnp.min(jnp.where(le == e1, lane_f, big), axis=-1, keepdims=True)
    le2 = jnp.where(lane_f == i1, -jnp.inf, le)
    e2 = jnp.max(le2, axis=-1, keepdims=True)
    i2 = jnp.min(jnp.where(le2 == e2, lane_f, big), axis=-1, keepdims=True)
    w2 = jnp.exp(e2 - e1)
    inv = 1.0 / (1.0 + w2)
    gate1 = pg_top * inv
    gate2 = pg_top * (w2 * inv)
    out = jnp.where(lane == 0, i1 - N_GROUPS,
                    jnp.where(lane == 1, i2 - N_GROUPS,
                              jnp.where(lane == 2, gate1, jnp.where(lane == 3, gate2, 0.0))))
    r_ref[...] = out


def _router(x1, g, w_r, b_r, tm):
    s, d = x1.shape
    return pl.pallas_call(
        _router_kernel,
        grid=(s // tm,),
        in_specs=[pl.BlockSpec((tm, d), lambda i: (i, 0)),
                  pl.BlockSpec((1, d), lambda i: (0, 0)),
                  pl.BlockSpec((2, d, ROUTE_W), lambda i: (0, 0, 0)),
                  pl.BlockSpec((1, ROUTE_W), lambda i: (0, 0))],
        out_specs=[pl.BlockSpec((tm, d // 2), lambda i: (i, 0)),
                   pl.BlockSpec((tm, ROUTE_W), lambda i: (i, 0))],
        out_shape=[jax.ShapeDtypeStruct((s, d // 2), jnp.uint32),
                   jax.ShapeDtypeStruct((s, ROUTE_W), F32)],
        compiler_params=_cp(("parallel",)),
        name="ffn_norm_router",
    )(x1, g, w_r, b_r)


def _pack_bf16_pairs(v):
    n = v.shape[1] // 2
    bits = lax.bitcast_convert_type(v.astype(BF16).astype(F32), jnp.uint32)
    return jnp.right_shift(bits[:, :n], jnp.uint32(16)) | bits[:, n:]


def _unpack_bf16_pairs(words):
    lo = lax.bitcast_convert_type(jnp.left_shift(words, jnp.uint32(16)), F32)
    hi = lax.bitcast_convert_type(words & jnp.uint32(0xFFFF0000), F32)
    return lo, hi


def _moe_kernel(be_ref, nr_ref, nu_ref, tok_ref, tokn_ref, h_ref, wg_ref, wu_ref, wd_ref, o_ref,
                xg_ref, xb_ref, gp_ref, a_ref, wgb_ref, wub_ref, wdb_ref, sem):
    b = pl.program_id(0)
    c = pl.program_id(1)
    nb = pl.num_programs(0)
    nr = nr_ref[b]

    def row_copy(src_row, slot, r):
        return pltpu.make_async_copy(h_ref.at[pl.ds(src_row, 1), :], xg_ref.at[slot, pl.ds(r, 1), :], sem.at[slot])

    def start_gather(tok, n_rows, slot):
        unroll = 8

        def issue(i, carry):
            for j in range(unroll):
                r = i * unroll + j
                row_copy(tok[0, 0, r], slot, r).start(priority=1)
            return carry
        lax.fori_loop(0, n_rows // unroll, issue, 0)

    def for_row_count(fn):
        for units in range(1, MOE_TB // MOE_RU + 1):
            @pl.when(nr == units)
            def _():
                fn(units * MOE_RU)

    @pl.when(c == 0)
    def _():
        slot = b % 2

        @pl.when(b == 0)
        def _():
            start_gather(tok_ref, nr * MOE_RU, 0)

        def drain(u, carry):
            pltpu.make_async_copy(h_ref.at[pl.ds(0, MOE_RU), :], xg_ref.at[slot, pl.ds(0, MOE_RU), :],
                                  sem.at[slot]).wait()
            return carry
        lax.fori_loop(0, nr, drain, 0)

        @pl.when(b + 1 < nb)
        def _():
            start_gather(tokn_ref, nr_ref[jnp.minimum(b + 1, nb - 1)] * MOE_RU, 1 - slot)

        def unpack(u, carry):
            r0 = pl.multiple_of(u * MOE_RU, MOE_RU)
            lo, hi = _unpack_bf16_pairs(xg_ref[slot, pl.ds(r0, MOE_RU), :])
            xb_ref[0, pl.ds(r0, MOE_RU), :] = lo.astype(BF16)
            xb_ref[1, pl.ds(r0, MOE_RU), :] = hi.astype(BF16)
            return carry
        lax.fori_loop(0, nr, unpack, 0)

    @pl.when((c < 2) & (nr > 0))
    def _():
        wgb_ref[...] = wg_ref[0].astype(BF16)
        wub_ref[...] = wu_ref[0].astype(BF16)

        def gate_up(m):
            x = xb_ref[c, :m, :]
            g = jnp.dot(x, wgb_ref[...], preferred_element_type=F32)
            u = jnp.dot(x, wub_ref[...], preferred_element_type=F32)

            @pl.when(c == 0)
            def _():
                gp_ref[0, :m, :] = g
                gp_ref[1, :m, :] = u

            @pl.when(c == 1)
            def _():
                gs = gp_ref[0, :m, :] + g
                a_ref[:m, :] = ((gs * jax.nn.sigmoid(gs)) * (gp_ref[1, :m, :] + u)).astype(BF16)

        for_row_count(gate_up)

    @pl.when((c >= 2) & (nr > 0))
    def _():
        wdb_ref[...] = wd_ref[0].astype(BF16)

        def down(m):
            yv = jnp.dot(a_ref[:m, :], wdb_ref[...], preferred_element_type=F32)
            o_ref[:m, :] = _pack_bf16_pairs(yv)
            if m < MOE_TB:
                o_ref[m:, :] = jnp.zeros((MOE_TB - m, o_ref.shape[1]), jnp.uint32)

        for_row_count(down)

    @pl.when((c >= 2) & (nr == 0))
    def _():
        o_ref[...] = jnp.zeros(o_ref.shape, jnp.uint32)


def _moe_experts(block_e, nsub, n_used, buf_tok, h2p, w_gate, w_up, w_down):
    n_blocks = block_e.shape[0]
    half = h2p.shape[1]
    d = 2 * half
    ff = w_gate.shape[2]
    dh = d // 2
    tok = buf_tok.reshape(n_blocks, 1, MOE_TB)

    def gate_up_idx(b, c, be, nr, nu):
        live = b < nu[0]
        return be[jnp.minimum(b, nu[0] - 1)], jnp.where(live, jnp.minimum(c, 1), 1), 0

    def down_idx(b, c, be, nr, nu):
        bb = jnp.minimum(b, nu[0] - 1)
        cc = jnp.where(b < nu[0], c, MOE_PH - 1)
        e = jnp.where(cc >= 2, be[bb], be[jnp.maximum(bb - 1, 0)])
        return e, 0, jnp.where(cc == 2, 0, 1)

    def out_idx(b, c, be, nr, nu):
        ob = jnp.where(c >= 2, b, jnp.maximum(b - 1, 0))
        oc = jnp.where(c >= 2, c - 2, jnp.where(b > 0, 1, 0))
        return ob, oc

    grid_spec = pltpu.PrefetchScalarGridSpec(
        num_scalar_prefetch=3,
        grid=(n_blocks, MOE_PH),
        in_specs=[pl.BlockSpec((1, 1, MOE_TB), lambda b, c, be, nr, nu: (b, 0, 0), memory_space=pltpu.SMEM),
                  pl.BlockSpec((1, 1, MOE_TB), lambda b, c, be, nr, nu: (jnp.minimum(b + 1, n_blocks - 1), 0, 0),
                               memory_space=pltpu.SMEM),
                  pl.BlockSpec(memory_space=pl.ANY),
                  pl.BlockSpec((1, half, ff), gate_up_idx),
                  pl.BlockSpec((1, half, ff), gate_up_idx),
                  pl.BlockSpec((1, ff, dh), down_idx)],
        out_specs=pl.BlockSpec((MOE_TB, dh // 2), out_idx),
        scratch_shapes=[pltpu.VMEM((2, MOE_TB, half), jnp.uint32), pltpu.VMEM((2, MOE_TB, half), BF16),
                        pltpu.VMEM((2, MOE_TB, ff), F32), pltpu.VMEM((MOE_TB, ff), BF16),
                        pltpu.VMEM((half, ff), BF16), pltpu.VMEM((half, ff), BF16), pltpu.VMEM((ff, dh), BF16),
                        pltpu.SemaphoreType.DMA((2,))],
    )
    return pl.pallas_call(
        _moe_kernel,
        grid_spec=grid_spec,
        out_shape=jax.ShapeDtypeStruct((n_blocks * MOE_TB, half), jnp.uint32),
        compiler_params=_cp(("arbitrary", "arbitrary")),
        name="moe_experts",
    )(block_e, nsub, n_used, tok, tok, h2p, w_gate, w_up, w_down)


def _combine_kernel(slot_ref, slotn_ref, r_ref, x_ref, yb_ref, o_ref, g_ref, sem):
    tm = x_ref.shape[0]
    n = g_ref.shape[3]
    step = pl.program_id(0)
    buf = step % 2
    unroll = 4

    def start_gather(slots, dst):
        def issue(i, c):
            for j in range(unroll):
                r = i * unroll + j
                for k in range(TOP_K):
                    sl = slots[0, 0, r * TOP_K + k]
                    pltpu.make_async_copy(yb_ref.at[pl.ds(sl, 1), :], g_ref.at[dst, k, pl.ds(r, 1), :],
                                          sem.at[dst, k]).start(priority=k)
            return c
        lax.fori_loop(0, tm // unroll, issue, 0)

    @pl.when(step == 0)
    def _():
        start_gather(slot_ref, 0)

    @pl.when(step + 1 < pl.num_programs(0))
    def _():
        start_gather(slotn_ref, 1 - buf)

    for k in range(TOP_K):
        pltpu.make_async_copy(yb_ref.at[pl.ds(0, tm), :], g_ref.at[buf, k], sem.at[buf, k]).wait()

    route = r_ref[...]
    w0 = route[:, TOP_K:TOP_K + 1]
    w1 = route[:, TOP_K + 1:TOP_K + 2]
    q = n // 2
    for hf in range(2):
        lo0, hi0 = _unpack_bf16_pairs(g_ref[buf, 0, :, hf * q:(hf + 1) * q])
        lo1, hi1 = _unpack_bf16_pairs(g_ref[buf, 1, :, hf * q:(hf + 1) * q])
        c0 = hf * n
        o_ref[:, c0:c0 + q] = x_ref[:, c0:c0 + q] + (lo0 * w0 + lo1 * w1)
        o_ref[:, c0 + q:c0 + n] = x_ref[:, c0 + q:c0 + n] + (hi0 * w0 + hi1 * w1)


def _combine(slots, route, x1, yb, tm):
    s, d = x1.shape
    n_steps = s // tm
    slot_blocks = slots.reshape(n_steps, 1, tm * TOP_K)
    return pl.pallas_call(
        _combine_kernel,
        grid=(n_steps,),
        in_specs=[pl.BlockSpec((1, 1, tm * TOP_K), lambda i: (i, 0, 0), memory_space=pltpu.SMEM),
                  pl.BlockSpec((1, 1, tm * TOP_K), lambda i: (jnp.minimum(i + 1, n_steps - 1), 0, 0),
                               memory_space=pltpu.SMEM),
                  pl.BlockSpec((tm, ROUTE_W), lambda i: (i, 0)),
                  pl.BlockSpec((tm, d), lambda i: (i, 0)),
                  pl.BlockSpec(memory_space=pl.ANY)],
        out_specs=pl.BlockSpec((tm, d), lambda i: (i, 0)),
        out_shape=jax.ShapeDtypeStruct((s, d), F32),
        scratch_shapes=[pltpu.VMEM((2, TOP_K, tm, d // 2), jnp.uint32), pltpu.SemaphoreType.DMA((2, TOP_K))],
        compiler_params=_cp(("arbitrary",)),
        name="moe_combine",
    )(slot_blocks, slot_blocks, route, x1, yb)


def _dispatch_plan(route, s):
    a = s * TOP_K
    flat_e = route[:, :TOP_K].astype(jnp.int32).reshape(a)
    chunk = 128
    onehot = (flat_e[:, None] == jnp.arange(N_EXPERTS, dtype=jnp.int32)[None, :]).astype(F32)
    oh = onehot.reshape(a // chunk, chunk, N_EXPERTS)
    strict_lower = jnp.tril(jnp.ones((chunk, chunk), F32), -1)
    within = jnp.einsum('ij,bjk->bik', strict_lower, oh, precision=lax.Precision.HIGHEST)
    totals = jnp.sum(oh, axis=1)
    before = jnp.cumsum(totals, axis=0) - totals
    rank = jnp.sum((within + before[:, None, :]) * oh, axis=2).reshape(a).astype(jnp.int32)
    counts = jnp.sum(totals, axis=0).astype(jnp.int32)
    padded = (counts + MOE_TB - 1) // MOE_TB * MOE_TB
    pad_end = jnp.cumsum(padded)
    pad_start = pad_end - padded
    dest = jnp.sum(onehot * pad_start.astype(F32)[None, :], axis=1).astype(jnp.int32) + rank
    n_blocks = a // MOE_TB + N_EXPERTS
    p_rows = n_blocks * MOE_TB
    buf_tok = jnp.zeros((p_rows,), jnp.int32).at[dest].set(jnp.arange(a, dtype=jnp.int32) // TOP_K)
    starts = jnp.arange(n_blocks, dtype=jnp.int32) * MOE_TB
    block_e = jnp.minimum(jnp.searchsorted(pad_end, starts, side='right'), N_EXPERTS - 1).astype(jnp.int32)
    valid = jnp.clip(counts[block_e] - (starts - pad_start[block_e]), 0, MOE_TB)
    valid = jnp.where(starts < pad_end[-1], valid, 0)
    nsub = ((valid + MOE_RU - 1) // MOE_RU).astype(jnp.int32)
    n_used = (pad_end[-1:] // MOE_TB).astype(jnp.int32)
    return block_e, nsub, n_used, buf_tok, dest.astype(jnp.int32)


def _rope_tables(positions):
    half = MLA_ROPE_DIM // 2
    inv_freq = ROPE_THETA ** (-jnp.arange(half, dtype=F32) / half)
    ang = positions.astype(F32)[:, None] * inv_freq[None, :]
    cos, sin = jnp.cos(ang), jnp.sin(ang)
    z = jnp.zeros_like(cos)
    c = jnp.concatenate([cos, cos, z, z], axis=-1)
    s1 = jnp.concatenate([-sin, z, z, z], axis=-1)
    s2 = jnp.concatenate([z, sin, z, z], axis=-1)
    return c, s1, s2


def kernel(x, mem, positions, rel_bias, mix_norm_g, w_in, diff_q_norm_g, diff_k_norm_g, diff_lambda_q1, diff_lambda_k1, diff_lambda_q2, diff_lambda_k2, diff_subln_g, mla_cq_norm_g, mla_ckv_norm_g, mla_w_uq, mla_w_ukv, mla_q_norm_g, mla_k_norm_g, mem_norm_g, mem_w_kv, mem_q_norm_g, mem_k_norm_g, w_o_diff, w_o_mla, w_o_mem, w_out, ffn_norm_g, w_route_group, b_route_group, w_route_expert, b_route_expert, w_exp_gate, w_exp_up, w_exp_down):
    b, s, d = x.shape
    assert b == 1 and s % ATT_T == 0
    depth = mix_norm_g.shape[0]
    xs = x.reshape(s, d)
    pos = positions.reshape(s)
    rope_tabs = _rope_tables(pos)
    row = lambda v: v.reshape(1, -1).astype(F32)

    for l in range(depth):
        lam_init = 0.8 - 0.6 * math.exp(-0.3 * l)
        h = _rmsnorm_rows(xs, mix_norm_g[l], 256)
        tn = 512
        q_gain = jnp.tile(diff_q_norm_g[l] * (DIFF_HEAD_DIM ** -0.5 * LOG2E), DIFF_MAPS)
        w_in_t = jnp.transpose(w_in[l])
        dq_t = _matmul_nt(h, w_in_t, row0=OFF_DQ, n_cols=DIFF_QK_WIDTH, tm=MM_TM, tn=tn, out_dtype=BF16,
                          mode="groupnorm", extra=q_gain, group=DIFF_HEAD_DIM, transpose_out=True,
                          name="diff_q_proj")
        dk = _matmul_nt(h, w_in_t, row0=OFF_DK, n_cols=DIFF_QK_WIDTH, tm=MM_TM, tn=tn, out_dtype=BF16,
                        mode="groupnorm", extra=jnp.tile(diff_k_norm_g[l], DIFF_MAPS), group=DIFF_HEAD_DIM,
                        name="diff_k_proj")
        dv_t = _matmul_nt(h, w_in_t, row0=OFF_DV, n_cols=DIFF_WIDTH, tm=MM_TM, tn=tn, out_dtype=BF16,
                          transpose_out=True, name="diff_v_proj")
        rest = _matmul_nt(h, w_in_t, row0=OFF_CQ, n_cols=REST_WIDTH, tm=MM_TM, tn=tn, out_dtype=BF16,
                          name="rest_proj")
        gates = _matmul_nt(h, w_in_t, row0=OFF_GATES, n_cols=3 * d, tm=MM_TM, tn=tn, out_dtype=BF16,
                           mode="sigmoid", name="gate_proj")

        bias_strips = _diff_bias_strips(rel_bias, ATT_T)
        lam_vecs = [row(diff_lambda_q1[l]), row(diff_lambda_k1[l]), row(diff_lambda_q2[l]), row(diff_lambda_k2[l])]
        o_diff = _diff_attention(dq_t, dk, dv_t, bias_strips, lam_vecs, diff_subln_g[l], lam_init)

        w_uq_heads = jnp.pad(
            mla_w_uq[l].reshape(MLA_Q_RANK, MLA_HEADS, MLA_QK_DIM),
            ((0, 0), (0, 0), (0, MLA_QK_PAD - MLA_QK_DIM))).reshape(MLA_Q_RANK, MLA_HEADS * MLA_QK_PAD).astype(BF16)
        qg_pad = jnp.pad(mla_q_norm_g[l] * (MLA_QK_DIM ** -0.5 * LOG2E),
                         (0, MLA_QK_PAD - MLA_QK_DIM)).reshape(1, -1).astype(F32)
        q_mla_t = _mla_q_prep(rest, row(mla_cq_norm_g[l]), w_uq_heads, qg_pad, rope_tabs, 512)
        kg = mla_k_norm_g[l]
        kg_nope = row(kg[:MLA_NOPE_DIM])
        kg_rope = jnp.pad(kg[MLA_NOPE_DIM:], (0, LANE - MLA_ROPE_DIM)).reshape(1, -1).astype(F32)
        k_mla, v_mla_t = _mla_kv_prep(rest, row(mla_ckv_norm_g[l]), mla_w_ukv[l], kg_nope, kg_rope, rope_tabs, 512)
        o_mla = _mla_attention(q_mla_t, k_mla, v_mla_t)

        n_mem = mem.shape[1]
        mem_h = _rmsnorm_rows(mem.reshape(n_mem, d), mem_norm_g[l], n_mem)
        k_mem = _matmul(mem_h, mem_w_kv[l], col0=0, n_cols=MEM_WIDTH, tm=n_mem, tn=512, out_dtype=BF16,
                        mode="groupnorm", extra=jnp.tile(mem_k_norm_g[l], MEM_HEADS), group=MEM_HEAD_DIM,
                        name="mem_k_proj")
        v_mem = _matmul(mem_h, mem_w_kv[l], col0=MEM_WIDTH, n_cols=MEM_WIDTH, tm=n_mem, tn=512, out_dtype=BF16,
                        name="mem_v_proj")
        o_mem = _mem_attention(rest, k_mem, v_mem, row(mem_q_norm_g[l] * MEM_HEAD_DIM ** -0.5), 512)

        mixed = _mix(o_diff, o_mla, o_mem, w_o_diff[l], w_o_mla[l], w_o_mem[l], gates, MM_TM, 512)
        x1 = _matmul(mixed, w_out[l], n_cols=d, tm=MM_TM, tn=512, out_dtype=F32, mode="residual", extra=xs,
                     name="out_proj")

        w_r = jnp.pad(jnp.concatenate([w_route_group[l], w_route_expert[l]], axis=1),
                      ((0, 0), (0, ROUTE_W - N_GROUPS - N_EXPERTS))).astype(F32)
        w_r_hi = w_r.astype(BF16)
        w_r = jnp.stack([w_r_hi, (w_r - w_r_hi.astype(F32)).astype(BF16)])
        b_r = jnp.pad(jnp.concatenate([b_route_group[l], b_route_expert[l]]),
                      (0, ROUTE_W - N_GROUPS - N_EXPERTS)).reshape(1, -1).astype(F32)
        h2, route = _router(x1, row(ffn_norm_g[l]), w_r, b_r, 256)
        block_e, nsub, n_used, buf_tok, slots = _dispatch_plan(route, s)
        yb = _moe_experts(block_e, nsub, n_used, buf_tok, h2, w_exp_gate[l], w_exp_up[l], w_exp_down[l])
        xs = _combine(slots, route, x1, yb, 256)
    return xs.reshape(b, s, d)
```

```python
import functools
import math

import jax
import jax.numpy as jnp
from jax import lax
from jax.experimental import pallas as pl
from jax.experimental.pallas import tpu as pltpu

F32 = jnp.float32
BF16 = jnp.bfloat16

NORM_EPS = 1e-6
NEG_INF = -1e30
LOG2E = math.log2(math.e)

DIFF_HEADS = 6
DIFF_HEAD_DIM = 128
DIFF_V_DIM = 256
DIFF_MAPS = 12
DIFF_QK_WIDTH = 1536
DIFF_WIDTH = 1536
MLA_HEADS = 12
MLA_Q_RANK = 1536
MLA_KV_RANK = 512
MLA_NOPE_DIM = 128
MLA_ROPE_DIM = 64
MLA_QK_DIM = 192
MLA_QK_PAD = 256
MLA_V_DIM = 128
MLA_WIDTH = 1536
ROPE_THETA = 10000.0
MEM_HEADS = 4
MEM_HEAD_DIM = 256
MEM_WIDTH = 1024
REL_BUCKETS = 32
REL_MAX_DIST = 128
N_GROUPS = 8
EXPERTS_PER_GROUP = 8
N_EXPERTS = 64
TOP_K = 2
EXPERT_FF = 512

OFF_DQ = 0
OFF_DK = 1536
OFF_DV = 3072
OFF_CQ = 4608
OFF_CKV = 6144
OFF_KROPE = 6656
OFF_MQ = 6720
OFF_GATES = 7744
REST_WIDTH = 3584
R_CQ = OFF_CQ - OFF_CQ
R_CKV = OFF_CKV - OFF_CQ
R_KROPE = OFF_KROPE - OFF_CQ
R_MQ = OFF_MQ - OFF_CQ

LANE = 128
VMEM_LIMIT = 52 * 1024 * 1024

MM_TM = 1024
MLA_HEAD_GROUP = 4
ATT_T = 512
MOE_TB = 512
MOE_RU = 128
MOE_PH = 4
ROUTE_W = 128
MEM_WIN_START = 2048
MEM_WINDOW = 1152


def _cp(sem, vmem=VMEM_LIMIT):
    return pltpu.CompilerParams(dimension_semantics=sem, vmem_limit_bytes=vmem)


def _rmsnorm_kernel(x_ref, g_ref, o_ref):
    x = x_ref[...].astype(F32)
    ms = jnp.mean(x * x, axis=-1, keepdims=True)
    o_ref[...] = (x * lax.rsqrt(ms + NORM_EPS) * g_ref[...]).astype(o_ref.dtype)


def _rmsnorm_rows(x, g, tm, out_dtype=BF16):
    m, d = x.shape
    return pl.pallas_call(
        _rmsnorm_kernel,
        grid=(m // tm,),
        in_specs=[pl.BlockSpec((tm, d), lambda i: (i, 0)),
                  pl.BlockSpec((1, d), lambda i: (0, 0))],
        out_specs=pl.BlockSpec((tm, d), lambda i: (i, 0)),
        out_shape=jax.ShapeDtypeStruct((m, d), out_dtype),
        compiler_params=_cp(("parallel",)),
        name="rmsnorm_rows",
    )(x, g.reshape(1, d).astype(F32))


def _cast_shifted(w_ref, w2_ref, wb_ref, shift):
    k = w_ref.shape[0]
    rows = 256

    def body(c, carry):
        r0 = pl.multiple_of(c * rows, rows)
        main = w_ref[pl.ds(r0, rows), :]
        tail = w2_ref[pl.ds(r0, rows), :]
        wb_ref[pl.ds(r0, rows), :] = jnp.concatenate([main[:, shift:], tail[:, :shift]], axis=1).astype(BF16)
        return carry

    lax.fori_loop(0, k // rows, body, 0)


def _mm_kernel(*refs, mode, cast, group, shift, transpose_out):
    a_ref, w_ref = refs[0], refs[1]
    pos = 2
    w2_ref = None
    if shift:
        w2_ref = refs[pos]
        pos += 1
    extra = None
    if mode in ("groupnorm", "residual"):
        extra = refs[pos]
        pos += 1
    o_ref = refs[pos]
    wb_ref = refs[pos + 1] if cast else None
    i = pl.program_id(1)
    if cast:
        @pl.when(i == 0)
        def _():
            if shift:
                _cast_shifted(w_ref, w2_ref, wb_ref, shift)
            else:
                wb_ref[...] = w_ref[...].astype(BF16)
        w = wb_ref[...]
    else:
        w = w_ref[...]
    acc = jnp.dot(a_ref[...], w, preferred_element_type=F32)
    if transpose_out:
        acc = acc.T
    if mode == "plain":
        o_ref[...] = acc.astype(o_ref.dtype)
    elif mode == "sigmoid":
        o_ref[...] = jax.nn.sigmoid(acc).astype(o_ref.dtype)
    elif mode == "residual":
        o_ref[...] = (extra[...] + acc).astype(o_ref.dtype)
    elif mode == "groupnorm":
        tn = w.shape[1]
        for c in range(tn // group):
            sl = slice(c * group, (c + 1) * group)
            if transpose_out:
                blk = acc[sl, :]
                ms = jnp.mean(blk * blk, axis=0, keepdims=True)
                o_ref[sl, :] = (blk * lax.rsqrt(ms + NORM_EPS) * extra[sl, :]).astype(o_ref.dtype)
            else:
                blk = acc[:, sl]
                ms = jnp.mean(blk * blk, axis=-1, keepdims=True)
                o_ref[:, sl] = (blk * lax.rsqrt(ms + NORM_EPS) * extra[:, sl]).astype(o_ref.dtype)


def _matmul(a, w, *, n_cols, tm, tn, out_dtype, col0=0, mode="plain", extra=None, group=LANE,
            transpose_out=False, name="matmul"):
    m, k = a.shape
    assert m % tm == 0 and n_cols % tn == 0 and w.shape[0] == k
    cast = w.dtype != BF16
    base, shift = divmod(col0, tn)
    assert shift <= LANE and (shift == 0 or cast)
    in_specs = [pl.BlockSpec((tm, k), lambda j, i: (i, 0)),
                pl.BlockSpec((k, tn), lambda j, i: (0, base + j))]
    args = [a, w]
    if shift:
        in_specs.append(pl.BlockSpec((k, LANE), lambda j, i: (0, (base + j + 1) * (tn // LANE))))
        args.append(w)
    if mode == "groupnorm":
        gain = extra.reshape(-1, 1) if transpose_out else extra.reshape(1, -1)
        in_specs.append(pl.BlockSpec((tn, 1), lambda j, i: (j, 0)) if transpose_out
                        else pl.BlockSpec((1, tn), lambda j, i: (0, j)))
        args.append(gain.astype(F32))
    elif mode == "residual":
        assert not transpose_out
        in_specs.append(pl.BlockSpec((tm, tn), lambda j, i: (i, j)))
        args.append(extra)
    if transpose_out:
        out_spec = pl.BlockSpec((tn, tm), lambda j, i: (j, i))
        out_shape = jax.ShapeDtypeStruct((n_cols, m), out_dtype)
    else:
        out_spec = pl.BlockSpec((tm, tn), lambda j, i: (i, j))
        out_shape = jax.ShapeDtypeStruct((m, n_cols), out_dtype)
    scratch = [pltpu.VMEM((k, tn), BF16)] if cast else []
    return pl.pallas_call(
        functools.partial(_mm_kernel, mode=mode, cast=cast, group=group, shift=shift, transpose_out=transpose_out),
        grid=(n_cols // tn, m // tm),
        in_specs=in_specs,
        out_specs=out_spec,
        out_shape=out_shape,
        scratch_shapes=scratch,
        compiler_params=_cp(("arbitrary", "arbitrary")),
        name=name,
    )(*args)


def _mm_nt_kernel(*refs, mode, group, transpose_out):
    a_ref, wt_ref = refs[0], refs[1]
    extra = refs[2] if mode == "groupnorm" else None
    o_ref, wb_ref = refs[-2], refs[-1]
    i = pl.program_id(1)

    @pl.when(i == 0)
    def _():
        wb_ref[...] = wt_ref[...].astype(BF16)

    contract_last = (((1,), (1,)), ((), ()))
    if transpose_out:
        acc = lax.dot_general(wb_ref[...], a_ref[...], contract_last, preferred_element_type=F32)
    else:
        acc = lax.dot_general(a_ref[...], wb_ref[...], contract_last, preferred_element_type=F32)
    if mode == "plain":
        o_ref[...] = acc.astype(o_ref.dtype)
    elif mode == "sigmoid":
        o_ref[...] = jax.nn.sigmoid(acc).astype(o_ref.dtype)
    elif mode == "groupnorm":
        tn = wb_ref.shape[0]
        for c in range(tn // group):
            sl = slice(c * group, (c + 1) * group)
            if transpose_out:
                blk = acc[sl, :]
                ms = jnp.mean(blk * blk, axis=0, keepdims=True)
                o_ref[sl, :] = (blk * lax.rsqrt(ms + NORM_EPS) * extra[sl, :]).astype(o_ref.dtype)
            else:
                blk = acc[:, sl]
                ms = jnp.mean(blk * blk, axis=-1, keepdims=True)
                o_ref[:, sl] = (blk * lax.rsqrt(ms + NORM_EPS) * extra[:, sl]).astype(o_ref.dtype)


def _matmul_nt(a, wt, *, row0, n_cols, tm, tn, out_dtype, mode="plain", extra=None, group=LANE,
               transpose_out=False, name="matmul_nt"):
    m, k = a.shape
    assert m % tm == 0 and n_cols % tn == 0 and wt.shape[1] == k and row0 % 8 == 0
    in_specs = [pl.BlockSpec((tm, k), lambda j, i: (i, 0)),
                pl.BlockSpec((pl.Element(tn), pl.Element(k)), lambda j, i: (pl.multiple_of(row0 + j * tn, 8), 0))]
    args = [a, wt]
    if mode == "groupnorm":
        gain = extra.reshape(-1, 1) if transpose_out else extra.reshape(1, -1)
        in_specs.append(pl.BlockSpec((tn, 1), lambda j, i: (j, 0)) if transpose_out
                        else pl.BlockSpec((1, tn), lambda j, i: (0, j)))
        args.append(gain.astype(F32))
    if transpose_out:
        out_spec = pl.BlockSpec((tn, tm), lambda j, i: (j, i))
        out_shape = jax.ShapeDtypeStruct((n_cols, m), out_dtype)
    else:
        out_spec = pl.BlockSpec((tm, tn), lambda j, i: (i, j))
        out_shape = jax.ShapeDtypeStruct((m, n_cols), out_dtype)
    return pl.pallas_call(
        functools.partial(_mm_nt_kernel, mode=mode, group=group, transpose_out=transpose_out),
        grid=(n_cols // tn, m // tm),
        in_specs=in_specs,
        out_specs=out_spec,
        out_shape=out_shape,
        scratch_shapes=[pltpu.VMEM((tn, k), BF16)],
        compiler_params=_cp(("arbitrary", "arbitrary")),
        name=name,
    )(*args)


def _softmax_pv(idx, s, vt_blk, m_ref, l_ref, acc_ref, col0=0):
    cols = slice(col0, col0 + s.shape[1])
    m_prev = m_ref[idx, :, cols]
    m_new = jnp.maximum(m_prev, jnp.max(s, axis=0, keepdims=True))
    alpha = jnp.exp2(m_prev - m_new)
    p = jnp.exp2(s - m_new)
    l_ref[idx, :, cols] = alpha * l_ref[idx, :, cols] + jnp.sum(p, axis=0, keepdims=True)
    acc_ref[idx, :, cols] = (alpha * acc_ref[idx, :, cols]
                             + jnp.dot(vt_blk, p.astype(BF16), preferred_element_type=F32))
    m_ref[idx, :, cols] = m_new


def _block_offset(kb, t):
    return kb * t if isinstance(kb, int) else pl.multiple_of(kb * t, t)


def _init_stats(m_ref, l_ref, acc_ref):
    m_ref[...] = jnp.full(m_ref.shape, NEG_INF, F32)
    l_ref[...] = jnp.zeros(l_ref.shape, F32)
    acc_ref[...] = jnp.zeros(acc_ref.shape, F32)


def _diff_attn_kernel(lq1_ref, lk1_ref, lq2_ref, lk2_ref, q1_ref, q2_ref, k1_ref, k2_ref, vt_ref,
                      b1_ref, b2_ref, g_ref, o_ref, sa_ref, sb_ref, m_ref, l_ref, acc_ref, *, lam_init):
    _, tk, tq = sa_ref.shape
    i = pl.program_id(1)
    qts = (q1_ref[...], q2_ref[...])
    ks = (k1_ref, k2_ref)
    bs = (b1_ref, b2_ref)
    _init_stats(m_ref, l_ref, acc_ref)

    def scores(kb, dst):
        off = _block_offset(kb, tk)
        for mp in range(2):
            dst[mp] = jnp.dot(ks[mp][pl.ds(off, tk), :], qts[mp], preferred_element_type=F32)

    def scores_right(kb, dst):
        off = _block_offset(kb, tk)
        for mp in range(2):
            dst[mp, :, tk:] = jnp.dot(ks[mp][pl.ds(off, tk), :], qts[mp][:, tk:], preferred_element_type=F32)

    def consume(kb, src, strip_off, col0=0):
        off = _block_offset(kb, tk)
        vt_blk = vt_ref[:, pl.ds(off, tk)]
        for mp in range(2):
            s = src[mp, :, col0:]
            if strip_off is not None:
                s = s + bs[mp][0, :, strip_off + col0:strip_off + tq]
            _softmax_pv(mp, s, vt_blk, m_ref, l_ref, acc_ref, col0)

    scores(0, sa_ref)

    def far_pair(j, c):
        kb = 2 * j
        scores(kb + 1, sb_ref)
        consume(kb, sa_ref, None)
        scores(kb + 2, sa_ref)
        consume(kb + 1, sb_ref, None)
        return c

    lax.fori_loop(0, jnp.maximum(i - 1, 0), far_pair, 0)

    @pl.when(i >= 1)
    def _():
        scores(2 * i - 1, sb_ref)
        consume(2 * i - 2, sa_ref, None)
        scores(2 * i, sa_ref)
        consume(2 * i - 1, sb_ref, 2 * tk)

    scores_right(2 * i + 1, sb_ref)
    consume(2 * i, sa_ref, tk)
    consume(2 * i + 1, sb_ref, 0, col0=tk)

    lam = (jnp.exp(jnp.sum(lq1_ref[...] * lk1_ref[...], axis=-1, keepdims=True))
           - jnp.exp(jnp.sum(lq2_ref[...] * lk2_ref[...], axis=-1, keepdims=True)) + lam_init)
    o = acc_ref[0] / l_ref[0] - lam * (acc_ref[1] / l_ref[1])
    ms = jnp.mean(o * o, axis=0, keepdims=True)
    o = (o * lax.rsqrt(ms + NORM_EPS) * g_ref[...]) * (1.0 - lam_init)
    o_ref[...] = o.T.astype(o_ref.dtype)


def _diff_attention(q_t, k, v_t, bias_strips, lam_vecs, subln_g, lam_init):
    s = k.shape[0]
    tk = ATT_T
    tq = 2 * tk
    assert s % tq == 0 and bias_strips.shape[1:] == (tk, 4 * tk)
    hd, vd = DIFF_HEAD_DIM, DIFF_V_DIM
    vec = pl.BlockSpec((1, hd), lambda h, i: (0, 0))
    in_specs = [vec, vec, vec, vec,
                pl.BlockSpec((hd, tq), lambda h, i: (h, i)),
                pl.BlockSpec((hd, tq), lambda h, i: (DIFF_HEADS + h, i)),
                pl.BlockSpec((s, hd), lambda h, i: (0, h)),
                pl.BlockSpec((s, hd), lambda h, i: (0, DIFF_HEADS + h)),
                pl.BlockSpec((vd, s), lambda h, i: (h, 0)),
                pl.BlockSpec((1, tk, 4 * tk), lambda h, i: (h, 0, 0), pipeline_mode=pl.Buffered(1)),
                pl.BlockSpec((1, tk, 4 * tk), lambda h, i: (DIFF_HEADS + h, 0, 0), pipeline_mode=pl.Buffered(1)),
                pl.BlockSpec((vd, 1), lambda h, i: (0, 0))]
    return pl.pallas_call(
        functools.partial(_diff_attn_kernel, lam_init=lam_init),
        grid=(DIFF_HEADS, s // tq),
        in_specs=in_specs,
        out_specs=pl.BlockSpec((tq, vd), lambda h, i: (i, h)),
        out_shape=jax.ShapeDtypeStruct((s, DIFF_WIDTH), BF16),
        scratch_shapes=[pltpu.VMEM((2, tk, tq), F32), pltpu.VMEM((2, tk, tq), F32),
                        pltpu.VMEM((2, 1, tq), F32), pltpu.VMEM((2, 1, tq), F32),
                        pltpu.VMEM((2, vd, tq), F32)],
        compiler_params=_cp(("arbitrary", "arbitrary")),
        name="diff_attention",
    )(*lam_vecs, q_t, q_t, k, k, v_t, bias_strips, bias_strips, subln_g.reshape(vd, 1).astype(F32))


def _t5_bucket(dist):
    n = jnp.maximum(dist, 0)
    max_exact = REL_BUCKETS // 2
    nf = jnp.maximum(n, 1).astype(F32)
    large = max_exact + (jnp.log(nf / max_exact) / math.log(REL_MAX_DIST / max_exact)
                         * (REL_BUCKETS - max_exact)).astype(jnp.int32)
    large = jnp.minimum(large, REL_BUCKETS - 1)
    return jnp.where(n < max_exact, n, large)


def _diff_bias_strips(rel_bias, tk):
    assert tk >= REL_MAX_DIST
    width = 4 * tk
    n = width + tk
    table = rel_bias.astype(F32)
    table = (table - table[REL_BUCKETS - 1:REL_BUCKETS]) * LOG2E
    kk = jnp.arange(n, dtype=jnp.int32)
    dist = kk - tk
    onehot = (_t5_bucket(dist)[:, None] == jnp.arange(REL_BUCKETS, dtype=jnp.int32)[None, :]).astype(F32)
    g = jnp.einsum('kb,bm->mk', onehot, table, precision=lax.Precision.HIGHEST)
    w = jnp.where(((dist >= 0) & (kk < width))[None, :], g, NEG_INF)
    maps = w.shape[0]

    def strip_kernel(w_ref, o_ref):
        rows = jnp.broadcast_to(w_ref[0], (tk, n))
        o_ref[0] = pltpu.roll(rows, 0, 1, stride=1, stride_axis=0)[:, :width]

    return pl.pallas_call(
        strip_kernel,
        grid=(maps,),
        in_specs=[pl.BlockSpec((1, 1, n), lambda m: (m, 0, 0))],
        out_specs=pl.BlockSpec((1, tk, width), lambda m: (m, 0, 0)),
        out_shape=jax.ShapeDtypeStruct((maps, tk, width), F32),
        compiler_params=_cp(("parallel",)),
        name="bias_strips",
    )(w.reshape(maps, 1, n))


def _rope_apply(tv, c_ref, s1_ref, s2_ref):
    return (tv * c_ref[...] + pltpu.roll(tv, 96, 1) * s1_ref[...] + pltpu.roll(tv, 32, 1) * s2_ref[...])


def _mla_q_kernel(cq_ref, g_ref, w_ref, qg_ref, c_ref, s1_ref, s2_ref, o_ref, xg_ref):
    h = pl.program_id(1)

    @pl.when(h == 0)
    def _():
        c = cq_ref[...].astype(F32)
        r = lax.rsqrt(jnp.mean(c * c, axis=-1, keepdims=True) + NORM_EPS)
        xg_ref[...] = (c * r * g_ref[...]).astype(BF16)

    ug = jnp.dot(xg_ref[...], w_ref[...], preferred_element_type=F32)
    for hh in range(MLA_HEAD_GROUP):
        u = ug[:, hh * MLA_QK_PAD:(hh + 1) * MLA_QK_PAD]
        ms = jnp.sum(u * u, axis=-1, keepdims=True) * (1.0 / MLA_QK_DIM)
        qn = u * lax.rsqrt(ms + NORM_EPS) * qg_ref[...]
        o_ref[hh, :MLA_NOPE_DIM, :] = qn[:, :MLA_NOPE_DIM].T.astype(o_ref.dtype)
        o_ref[hh, MLA_NOPE_DIM:, :] = _rope_apply(qn[:, MLA_NOPE_DIM:], c_ref, s1_ref, s2_ref).T.astype(o_ref.dtype)


def _mla_q_prep(rest, cq_g, w_uq_pad, qg_pad, rope_tabs, tm):
    s = rest.shape[0]
    hg = MLA_HEAD_GROUP
    tab = pl.BlockSpec((tm, LANE), lambda i, h: (i, 0))
    return pl.pallas_call(
        _mla_q_kernel,
        grid=(s // tm, MLA_HEADS // hg),
        in_specs=[pl.BlockSpec((tm, MLA_Q_RANK), lambda i, h: (i, R_CQ // MLA_Q_RANK)),
                  pl.BlockSpec((1, MLA_Q_RANK), lambda i, h: (0, 0)),
                  pl.BlockSpec((MLA_Q_RANK, hg * MLA_QK_PAD), lambda i, h: (0, h)),
                  pl.BlockSpec((1, MLA_QK_PAD), lambda i, h: (0, 0)),
                  tab, tab, tab],
        out_specs=pl.BlockSpec((hg, MLA_QK_PAD, tm), lambda i, h: (h, 0, i)),
        out_shape=jax.ShapeDtypeStruct((MLA_HEADS, MLA_QK_PAD, s), BF16),
        scratch_shapes=[pltpu.VMEM((tm, MLA_Q_RANK), BF16)],
        compiler_params=_cp(("arbitrary", "arbitrary")),
        name="mla_q_prep",
    )(rest, cq_g, w_uq_pad, qg_pad, *rope_tabs)


def _mla_kv_kernel(ckv_ref, kr_ref, g_ref, w_ref, kgn_ref, kgr_ref, c_ref, s1_ref, s2_ref,
                   k_ref, vt_ref, xg_ref):
    h = pl.program_id(1)

    @pl.when(h == 0)
    def _():
        c = ckv_ref[...].astype(F32)
        r = lax.rsqrt(jnp.mean(c * c, axis=-1, keepdims=True) + NORM_EPS)
        xg_ref[...] = (c * r * g_ref[...]).astype(BF16)

    hw = MLA_NOPE_DIM + MLA_V_DIM
    kvg = jnp.dot(xg_ref[...], w_ref[...].astype(BF16), preferred_element_type=F32)
    lane = lax.broadcasted_iota(jnp.int32, kr_ref.shape, 1)
    kr = jnp.where(lane < MLA_ROPE_DIM, kr_ref[...].astype(F32), 0.0)
    kr_ss = jnp.sum(kr * kr, axis=-1, keepdims=True)
    for hh in range(MLA_HEAD_GROUP):
        kn = kvg[:, hh * hw:hh * hw + MLA_NOPE_DIM]
        ms = (jnp.sum(kn * kn, axis=-1, keepdims=True) + kr_ss) * (1.0 / MLA_QK_DIM)
        rs = lax.rsqrt(ms + NORM_EPS)
        k_ref[hh, :, :MLA_NOPE_DIM] = (kn * rs * kgn_ref[...]).astype(k_ref.dtype)
        k_ref[hh, :, MLA_NOPE_DIM:] = _rope_apply(kr * rs * kgr_ref[...], c_ref, s1_ref, s2_ref).astype(k_ref.dtype)
        vt_ref[hh] = kvg[:, hh * hw + MLA_NOPE_DIM:(hh + 1) * hw].T.astype(vt_ref.dtype)


def _mla_kv_prep(rest, ckv_g, w_ukv, kg_nope, kg_rope_pad, rope_tabs, tm):
    s = rest.shape[0]
    hg = MLA_HEAD_GROUP
    tab = pl.BlockSpec((tm, LANE), lambda i, h: (i, 0))
    hw = MLA_NOPE_DIM + MLA_V_DIM
    return pl.pallas_call(
        _mla_kv_kernel,
        grid=(s // tm, MLA_HEADS // hg),
        in_specs=[pl.BlockSpec((tm, MLA_KV_RANK), lambda i, h: (i, R_CKV // MLA_KV_RANK)),
                  pl.BlockSpec((tm, LANE), lambda i, h: (i, R_KROPE // LANE)),
                  pl.BlockSpec((1, MLA_KV_RANK), lambda i, h: (0, 0)),
                  pl.BlockSpec((MLA_KV_RANK, hg * hw), lambda i, h: (0, h)),
                  pl.BlockSpec((1, LANE), lambda i, h: (0, 0)),
                  pl.BlockSpec((1, LANE), lambda i, h: (0, 0)),
                  tab, tab, tab],
        out_specs=[pl.BlockSpec((hg, tm, MLA_QK_PAD), lambda i, h: (h, i, 0)),
                   pl.BlockSpec((hg, MLA_V_DIM, tm), lambda i, h: (h, 0, i))],
        out_shape=[jax.ShapeDtypeStruct((MLA_HEADS, s, MLA_QK_PAD), BF16),
                   jax.ShapeDtypeStruct((MLA_HEADS, MLA_V_DIM, s), BF16)],
        scratch_shapes=[pltpu.VMEM((tm, MLA_KV_RANK), BF16)],
        compiler_params=_cp(("arbitrary", "arbitrary")),
        name="mla_kv_prep",
    )(rest, rest, ckv_g, w_ukv, kg_nope, kg_rope_pad, *rope_tabs)


def _mla_attn_kernel(qt_ref, k_ref, vt_ref, o_ref, sa_ref, sb_ref, m_ref, l_ref, acc_ref):
    tk, tq = sa_ref.shape
    i = pl.program_id(1)
    qt = qt_ref[0]
    _init_stats(m_ref, l_ref, acc_ref)

    def scores(kb, dst):
        off = _block_offset(kb, tk)
        dst[...] = jnp.dot(k_ref[0, pl.ds(off, tk), :], qt, preferred_element_type=F32)

    def scores_right(kb, dst):
        off = _block_offset(kb, tk)
        dst[:, tk:] = jnp.dot(k_ref[0, pl.ds(off, tk), :], qt[:, tk:], preferred_element_type=F32)

    def consume(kb, src, diag, col0=0):
        off = _block_offset(kb, tk)
        s = src[:, col0:]
        if diag:
            krow = lax.broadcasted_iota(jnp.int32, s.shape, 0)
            qcol = lax.broadcasted_iota(jnp.int32, s.shape, 1)
            s = jnp.where(krow <= qcol, s, NEG_INF)
        _softmax_pv(0, s, vt_ref[0, :, pl.ds(off, tk)], m_ref, l_ref, acc_ref, col0)

    scores(0, sa_ref)

    def far_pair(j, c):
        kb = 2 * j
        scores(kb + 1, sb_ref)
        consume(kb, sa_ref, False)
        scores(kb + 2, sa_ref)
        consume(kb + 1, sb_ref, False)
        return c

    lax.fori_loop(0, i, far_pair, 0)
    scores_right(2 * i + 1, sb_ref)
    consume(2 * i, sa_ref, True)
    consume(2 * i + 1, sb_ref, True, col0=tk)
    o_ref[...] = (acc_ref[0] / l_ref[0]).T.astype(o_ref.dtype)


def _mla_attention(q_t, k, v_t):
    s = k.shape[1]
    tk = ATT_T
    tq = 2 * tk
    assert s % tq == 0
    return pl.pallas_call(
        _mla_attn_kernel,
        grid=(MLA_HEADS, s // tq),
        in_specs=[pl.BlockSpec((1, MLA_QK_PAD, tq), lambda h, i: (h, 0, i)),
                  pl.BlockSpec((1, s, MLA_QK_PAD), lambda h, i: (h, 0, 0)),
                  pl.BlockSpec((1, MLA_V_DIM, s), lambda h, i: (h, 0, 0))],
        out_specs=pl.BlockSpec((tq, MLA_V_DIM), lambda h, i: (i, h)),
        out_shape=jax.ShapeDtypeStruct((s, MLA_WIDTH), BF16),
        scratch_shapes=[pltpu.VMEM((tk, tq), F32), pltpu.VMEM((tk, tq), F32),
                        pltpu.VMEM((1, 1, tq), F32), pltpu.VMEM((1, 1, tq), F32),
                        pltpu.VMEM((1, MLA_V_DIM, tq), F32)],
        compiler_params=_cp(("arbitrary", "arbitrary")),
        name="mla_attention",
    )(q_t, k, v_t)


def _qk_nt(q, k_blk):
    return lax.dot_general(q, k_blk, (((1,), (1,)), ((), ())), preferred_element_type=F32)


def _mem_attn_kernel(q_ref, k_ref, v_ref, qg_ref, o_ref):
    shift = R_MQ - MEM_WIN_START
    qall = q_ref[...].astype(F32)[:, shift:shift + MEM_WIDTH]
    for h in range(MEM_HEADS):
        lo = h * MEM_HEAD_DIM
        qh = qall[:, lo:lo + MEM_HEAD_DIM]
        ms = jnp.mean(qh * qh, axis=-1, keepdims=True)
        qn = (qh * lax.rsqrt(ms + NORM_EPS) * qg_ref[...]).astype(BF16)
        s = _qk_nt(qn, k_ref[:, lo:lo + MEM_HEAD_DIM])
        p = jnp.exp(s - jnp.max(s, axis=-1, keepdims=True))
        l = jnp.sum(p, axis=-1, keepdims=True)
        o = jnp.dot(p.astype(BF16), v_ref[:, lo:lo + MEM_HEAD_DIM], preferred_element_type=F32)
        o_ref[:, lo:lo + MEM_HEAD_DIM] = (o / l).astype(o_ref.dtype)


def _mem_attention(rest, k_mem, v_mem, qg_scaled, tm):
    s = rest.shape[0]
    n_mem = k_mem.shape[0]
    assert 0 <= R_MQ - MEM_WIN_START and R_MQ - MEM_WIN_START + MEM_WIDTH <= MEM_WINDOW
    assert MEM_WIN_START % LANE == 0 and MEM_WIN_START + MEM_WINDOW <= rest.shape[1]
    return pl.pallas_call(
        _mem_attn_kernel,
        grid=(s // tm,),
        in_specs=[pl.BlockSpec((pl.Element(tm), pl.Element(MEM_WINDOW)),
                               lambda i: (pl.multiple_of(i * tm, tm), MEM_WIN_START)),
                  pl.BlockSpec((n_mem, MEM_WIDTH), lambda i: (0, 0)),
                  pl.BlockSpec((n_mem, MEM_WIDTH), lambda i: (0, 0)),
                  pl.BlockSpec((1, MEM_HEAD_DIM), lambda i: (0, 0))],
        out_specs=pl.BlockSpec((tm, MEM_WIDTH), lambda i: (i, 0)),
        out_shape=jax.ShapeDtypeStruct((s, MEM_WIDTH), BF16),
        compiler_params=_cp(("parallel",)),
        name="mem_attention",
    )(rest, k_mem, v_mem, qg_scaled)


def _mix_kernel(od_ref, om_ref, oc_ref, wd_ref, wm_ref, wc_ref, g0_ref, g1_ref, g2_ref, o_ref,
                wdb_ref, wmb_ref, wcb_ref):
    @pl.when(pl.program_id(1) == 0)
    def _():
        wdb_ref[...] = wd_ref[...].astype(BF16)
        wmb_ref[...] = wm_ref[...].astype(BF16)
        wcb_ref[...] = wc_ref[...].astype(BF16)

    yd = jnp.dot(od_ref[...], wdb_ref[...], preferred_element_type=F32)
    ym = jnp.dot(om_ref[...], wmb_ref[...], preferred_element_type=F32)
    yc = jnp.dot(oc_ref[...], wcb_ref[...], preferred_element_type=F32)
    mixed = (g0_ref[...].astype(F32) * yd + g1_ref[...].astype(F32) * ym) + g2_ref[...].astype(F32) * yc
    o_ref[...] = mixed.astype(o_ref.dtype)


def _mix(o_diff, o_mla, o_mem, w_d, w_m, w_c, gates, tm, tn):
    s = o_diff.shape[0]
    d = w_d.shape[1]
    nj = d // tn
    return pl.pallas_call(
        _mix_kernel,
        grid=(nj, s // tm),
        in_specs=[pl.BlockSpec((tm, o_diff.shape[1]), lambda j, i: (i, 0)),
                  pl.BlockSpec((tm, o_mla.shape[1]), lambda j, i: (i, 0)),
                  pl.BlockSpec((tm, o_mem.shape[1]), lambda j, i: (i, 0)),
                  pl.BlockSpec((w_d.shape[0], tn), lambda j, i: (0, j)),
                  pl.BlockSpec((w_m.shape[0], tn), lambda j, i: (0, j)),
                  pl.BlockSpec((w_c.shape[0], tn), lambda j, i: (0, j)),
                  pl.BlockSpec((tm, tn), lambda j, i: (i, j)),
                  pl.BlockSpec((tm, tn), lambda j, i: (i, nj + j)),
                  pl.BlockSpec((tm, tn), lambda j, i: (i, 2 * nj + j))],
        out_specs=pl.BlockSpec((tm, tn), lambda j, i: (i, j)),
        out_shape=jax.ShapeDtypeStruct((s, d), BF16),
        scratch_shapes=[pltpu.VMEM((w_d.shape[0], tn), BF16), pltpu.VMEM((w_m.shape[0], tn), BF16),
                        pltpu.VMEM((w_c.shape[0], tn), BF16)],
        compiler_params=_cp(("arbitrary", "arbitrary")),
        name="gated_mix",
    )(o_diff, o_mla, o_mem, w_d, w_m, w_c, gates, gates, gates)


def _router_kernel(x_ref, g_ref, w_ref, b_ref, h_ref, r_ref):
    x = x_ref[...]
    ms = jnp.mean(x * x, axis=-1, keepdims=True)
    h = x * lax.rsqrt(ms + NORM_EPS) * g_ref[...]
    h_ref[...] = _pack_bf16_pairs(h)
    h_hi = h.astype(BF16)
    h_lo = (h - h_hi.astype(F32)).astype(BF16)
    logits = (jnp.dot(h_hi, w_ref[0], preferred_element_type=F32)
              + (jnp.dot(h_lo, w_ref[0], preferred_element_type=F32)
                 + jnp.dot(h_hi, w_ref[1], preferred_element_type=F32))) + b_ref[...]
    lane = lax.broadcasted_iota(jnp.int32, logits.shape, 1)
    lane_f = lane.astype(F32)
    big = float(4 * ROUTE_W)
    lg = jnp.where(lane < N_GROUPS, logits, -jnp.inf)
    gmax = jnp.max(lg, axis=-1, keepdims=True)
    gidx = jnp.min(jnp.where(lg == gmax, lane_f, big), axis=-1, keepdims=True)
    pg_top = 1.0 / jnp.sum(jnp.exp(lg - gmax), axis=-1, keepdims=True)
    e_lane = lane - N_GROUPS
    lane_group = jnp.right_shift(e_lane, 3).astype(F32)
    in_group = (e_lane >= 0) & (e_lane < N_EXPERTS) & (lane_group == gidx)
    le = jnp.where(in_group, logits, -jnp.inf)
    e1 = jnp.max(le, axis=-1, keepdims=True)
    i1 = jnp.min(jnp.where(le == e1, lane_f, big), axis=-1, keepdims=True)
    le2 = jnp.where(lane_f == i1, -jnp.inf, le)
    e2 = jnp.max(le2, axis=-1, keepdims=True)
    i2 = jnp.min(jnp.where(le2 == e2, lane_f, big), axis=-1, keepdims=True)
    w2 = jnp.exp(e2 - e1)
    inv = 1.0 / (1.0 + w2)
    gate1 = pg_top * inv
    gate2 = pg_top * (w2 * inv)
    out = jnp.where(lane == 0, i1 - N_GROUPS,
                    jnp.where(lane == 1, i2 - N_GROUPS,
                              jnp.where(lane == 2, gate1, jnp.where(lane == 3, gate2, 0.0))))
    r_ref[...] = out


def _router(x1, g, w_r, b_r, tm):
    s, d = x1.shape
    return pl.pallas_call(
        _router_kernel,
        grid=(s // tm,),
        in_specs=[pl.BlockSpec((tm, d), lambda i: (i, 0)),
                  pl.BlockSpec((1, d), lambda i: (0, 0)),
                  pl.BlockSpec((2, d, ROUTE_W), lambda i: (0, 0, 0)),
                  pl.BlockSpec((1, ROUTE_W), lambda i: (0, 0))],
        out_specs=[pl.BlockSpec((tm, d // 2), lambda i: (i, 0)),
                   pl.BlockSpec((tm, ROUTE_W), lambda i: (i, 0))],
        out_shape=[jax.ShapeDtypeStruct((s, d // 2), jnp.uint32),
                   jax.ShapeDtypeStruct((s, ROUTE_W), F32)],
        compiler_params=_cp(("parallel",)),
        name="ffn_norm_router",
    )(x1, g, w_r, b_r)


def _pack_bf16_pairs(v):
    n = v.shape[1] // 2
    bits = lax.bitcast_convert_type(v.astype(BF16).astype(F32), jnp.uint32)
    return jnp.right_shift(bits[:, :n], jnp.uint32(16)) | bits[:, n:]


def _unpack_bf16_pairs(words):
    lo = lax.bitcast_convert_type(jnp.left_shift(words, jnp.uint32(16)), F32)
    hi = lax.bitcast_convert_type(words & jnp.uint32(0xFFFF0000), F32)
    return lo, hi


def _moe_kernel(be_ref, nr_ref, nu_ref, tok_ref, tokn_ref, h_ref, wg_ref, wu_ref, wd_ref, o_ref,
                xg_ref, xb_ref, gp_ref, a_ref, wgb_ref, wub_ref, wdb_ref, sem):
    b = pl.program_id(0)
    c = pl.program_id(1)
    nb = pl.num_programs(0)
    nr = nr_ref[b]

    def row_copy(src_row, slot, r):
        return pltpu.make_async_copy(h_ref.at[pl.ds(src_row, 1), :], xg_ref.at[slot, pl.ds(r, 1), :], sem.at[slot])

    def start_gather(tok, n_rows, slot):
        unroll = 8

        def issue(i, carry):
            for j in range(unroll):
                r = i * unroll + j
                row_copy(tok[0, 0, r], slot, r).start(priority=1)
            return carry
        lax.fori_loop(0, n_rows // unroll, issue, 0)

    def for_row_count(fn):
        for units in range(1, MOE_TB // MOE_RU + 1):
            @pl.when(nr == units)
            def _():
                fn(units * MOE_RU)

    @pl.when(c == 0)
    def _():
        slot = b % 2

        @pl.when(b == 0)
        def _():
            start_gather(tok_ref, nr * MOE_RU, 0)

        def drain(u, carry):
            pltpu.make_async_copy(h_ref.at[pl.ds(0, MOE_RU), :], xg_ref.at[slot, pl.ds(0, MOE_RU), :],
                                  sem.at[slot]).wait()
            return carry
        lax.fori_loop(0, nr, drain, 0)

        @pl.when(b + 1 < nb)
        def _():
            start_gather(tokn_ref, nr_ref[jnp.minimum(b + 1, nb - 1)] * MOE_RU, 1 - slot)

        def unpack(u, carry):
            r0 = pl.multiple_of(u * MOE_RU, MOE_RU)
            lo, hi = _unpack_bf16_pairs(xg_ref[slot, pl.ds(r0, MOE_RU), :])
            xb_ref[0, pl.ds(r0, MOE_RU), :] = lo.astype(BF16)
            xb_ref[1, pl.ds(r0, MOE_RU), :] = hi.astype(BF16)
            return carry
        lax.fori_loop(0, nr, unpack, 0)

    @pl.when((c < 2) & (nr > 0))
    def _():
        wgb_ref[...] = wg_ref[0].astype(BF16)
        wub_ref[...] = wu_ref[0].astype(BF16)

        def gate_up(m):
            x = xb_ref[c, :m, :]
            g = jnp.dot(x, wgb_ref[...], preferred_element_type=F32)
            u = jnp.dot(x, wub_ref[...], preferred_element_type=F32)

            @pl.when(c == 0)
            def _():
                gp_ref[0, :m, :] = g
                gp_ref[1, :m, :] = u

            @pl.when(c == 1)
            def _():
                gs = gp_ref[0, :m, :] + g
                a_ref[:m, :] = ((gs * jax.nn.sigmoid(gs)) * (gp_ref[1, :m, :] + u)).astype(BF16)

        for_row_count(gate_up)

    @pl.when((c >= 2) & (nr > 0))
    def _():
        wdb_ref[...] = wd_ref[0].astype(BF16)

        def down(m):
            yv = jnp.dot(a_ref[:m, :], wdb_ref[...], preferred_element_type=F32)
            o_ref[:m, :] = _pack_bf16_pairs(yv)
            if m < MOE_TB:
                o_ref[m:, :] = jnp.zeros((MOE_TB - m, o_ref.shape[1]), jnp.uint32)

        for_row_count(down)

    @pl.when((c >= 2) & (nr == 0))
    def _():
        o_ref[...] = jnp.zeros(o_ref.shape, jnp.uint32)


def _moe_experts(block_e, nsub, n_used, buf_tok, h2p, w_gate, w_up, w_down):
    n_blocks = block_e.shape[0]
    half = h2p.shape[1]
    d = 2 * half
    ff = w_gate.shape[2]
    dh = d // 2
    tok = buf_tok.reshape(n_blocks, 1, MOE_TB)

    def gate_up_idx(b, c, be, nr, nu):
        live = b < nu[0]
        return be[jnp.minimum(b, nu[0] - 1)], jnp.where(live, jnp.minimum(c, 1), 1), 0

    def down_idx(b, c, be, nr, nu):
        bb = jnp.minimum(b, nu[0] - 1)
        cc = jnp.where(b < nu[0], c, MOE_PH - 1)
        e = jnp.where(cc >= 2, be[bb], be[jnp.maximum(bb - 1, 0)])
        return e, 0, jnp.where(cc == 2, 0, 1)

    def out_idx(b, c, be, nr, nu):
        ob = jnp.where(c >= 2, b, jnp.maximum(b - 1, 0))
        oc = jnp.where(c >= 2, c - 2, jnp.where(b > 0, 1, 0))
        return ob, oc

    grid_spec = pltpu.PrefetchScalarGridSpec(
        num_scalar_prefetch=3,
        grid=(n_blocks, MOE_PH),
        in_specs=[pl.BlockSpec((1, 1, MOE_TB), lambda b, c, be, nr, nu: (b, 0, 0), memory_space=pltpu.SMEM),
                  pl.BlockSpec((1, 1, MOE_TB), lambda b, c, be, nr, nu: (jnp.minimum(b + 1, n_blocks - 1), 0, 0),
                               memory_space=pltpu.SMEM),
                  pl.BlockSpec(memory_space=pl.ANY),
                  pl.BlockSpec((1, half, ff), gate_up_idx),
                  pl.BlockSpec((1, half, ff), gate_up_idx),
                  pl.BlockSpec((1, ff, dh), down_idx)],
        out_specs=pl.BlockSpec((MOE_TB, dh // 2), out_idx),
        scratch_shapes=[pltpu.VMEM((2, MOE_TB, half), jnp.uint32), pltpu.VMEM((2, MOE_TB, half), BF16),
                        pltpu.VMEM((2, MOE_TB, ff), F32), pltpu.VMEM((MOE_TB, ff), BF16),
                        pltpu.VMEM((half, ff), BF16), pltpu.VMEM((half, ff), BF16), pltpu.VMEM((ff, dh), BF16),
                        pltpu.SemaphoreType.DMA((2,))],
    )
    return pl.pallas_call(
        _moe_kernel,
        grid_spec=grid_spec,
        out_shape=jax.ShapeDtypeStruct((n_blocks * MOE_TB, half), jnp.uint32),
        compiler_params=_cp(("arbitrary", "arbitrary")),
        name="moe_experts",
    )(block_e, nsub, n_used, tok, tok, h2p, w_gate, w_up, w_down)


def _combine_kernel(slot_ref, slotn_ref, r_ref, x_ref, yb_ref, o_ref, g_ref, sem):
    tm = x_ref.shape[0]
    n = g_ref.shape[3]
    step = pl.program_id(0)
    buf = step % 2
    unroll = 4

    def start_gather(slots, dst):
        def issue(i, c):
            for j in range(unroll):
                r = i * unroll + j
                for k in range(TOP_K):
                    sl = slots[0, 0, r * TOP_K + k]
                    pltpu.make_async_copy(yb_ref.at[pl.ds(sl, 1), :], g_ref.at[dst, k, pl.ds(r, 1), :],
                                          sem.at[dst, k]).start(priority=k)
            return c
        lax.fori_loop(0, tm // unroll, issue, 0)

    @pl.when(step == 0)
    def _():
        start_gather(slot_ref, 0)

    @pl.when(step + 1 < pl.num_programs(0))
    def _():
        start_gather(slotn_ref, 1 - buf)

    for k in range(TOP_K):
        pltpu.make_async_copy(yb_ref.at[pl.ds(0, tm), :], g_ref.at[buf, k], sem.at[buf, k]).wait()

    route = r_ref[...]
    w0 = route[:, TOP_K:TOP_K + 1]
    w1 = route[:, TOP_K + 1:TOP_K + 2]
    q = n // 2
    for hf in range(2):
        lo0, hi0 = _unpack_bf16_pairs(g_ref[buf, 0, :, hf * q:(hf + 1) * q])
        lo1, hi1 = _unpack_bf16_pairs(g_ref[buf, 1, :, hf * q:(hf + 1) * q])
        c0 = hf * n
        o_ref[:, c0:c0 + q] = x_ref[:, c0:c0 + q] + (lo0 * w0 + lo1 * w1)
        o_ref[:, c0 + q:c0 + n] = x_ref[:, c0 + q:c0 + n] + (hi0 * w0 + hi1 * w1)


def _combine(slots, route, x1, yb, tm):
    s, d = x1.shape
    n_steps = s // tm
    slot_blocks = slots.reshape(n_steps, 1, tm * TOP_K)
    return pl.pallas_call(
        _combine_kernel,
        grid=(n_steps,),
        in_specs=[pl.BlockSpec((1, 1, tm * TOP_K), lambda i: (i, 0, 0), memory_space=pltpu.SMEM),
                  pl.BlockSpec((1, 1, tm * TOP_K), lambda i: (jnp.minimum(i + 1, n_steps - 1), 0, 0),
                               memory_space=pltpu.SMEM),
                  pl.BlockSpec((tm, ROUTE_W), lambda i: (i, 0)),
                  pl.BlockSpec((tm, d), lambda i: (i, 0)),
                  pl.BlockSpec(memory_space=pl.ANY)],
        out_specs=pl.BlockSpec((tm, d), lambda i: (i, 0)),
        out_shape=jax.ShapeDtypeStruct((s, d), F32),
        scratch_shapes=[pltpu.VMEM((2, TOP_K, tm, d // 2), jnp.uint32), pltpu.SemaphoreType.DMA((2, TOP_K))],
        compiler_params=_cp(("arbitrary",)),
        name="moe_combine",
    )(slot_blocks, slot_blocks, route, x1, yb)


def _dispatch_plan(route, s):
    a = s * TOP_K
    flat_e = route[:, :TOP_K].astype(jnp.int32).reshape(a)
    chunk = 128
    onehot = (flat_e[:, None] == jnp.arange(N_EXPERTS, dtype=jnp.int32)[None, :]).astype(F32)
    oh = onehot.reshape(a // chunk, chunk, N_EXPERTS)
    strict_lower = jnp.tril(jnp.ones((chunk, chunk), F32), -1)
    within = jnp.einsum('ij,bjk->bik', strict_lower, oh, precision=lax.Precision.HIGHEST)
    totals = jnp.sum(oh, axis=1)
    before = jnp.cumsum(totals, axis=0) - totals
    rank = jnp.sum((within + before[:, None, :]) * oh, axis=2).reshape(a).astype(jnp.int32)
    counts = jnp.sum(totals, axis=0).astype(jnp.int32)
    padded = (counts + MOE_TB - 1) // MOE_TB * MOE_TB
    pad_end = jnp.cumsum(padded)
    pad_start = pad_end - padded
    dest = jnp.sum(onehot * pad_start.astype(F32)[None, :], axis=1).astype(jnp.int32) + rank
    n_blocks = a // MOE_TB + N_EXPERTS
    p_rows = n_blocks * MOE_TB
    buf_tok = jnp.zeros((p_rows,), jnp.int32).at[dest].set(jnp.arange(a, dtype=jnp.int32) // TOP_K)
    starts = jnp.arange(n_blocks, dtype=jnp.int32) * MOE_TB
    block_e = jnp.minimum(jnp.searchsorted(pad_end, starts, side='right'), N_EXPERTS - 1).astype(jnp.int32)
    valid = jnp.clip(counts[block_e] - (starts - pad_start[block_e]), 0, MOE_TB)
    valid = jnp.where(starts < pad_end[-1], valid, 0)
    nsub = ((valid + MOE_RU - 1) // MOE_RU).astype(jnp.int32)
    n_used = (pad_end[-1:] // MOE_TB).astype(jnp.int32)
    return block_e, nsub, n_used, buf_tok, dest.astype(jnp.int32)


def _rope_tables(positions):
    half = MLA_ROPE_DIM // 2
    inv_freq = ROPE_THETA ** (-jnp.arange(half, dtype=F32) / half)
    ang = positions.astype(F32)[:, None] * inv_freq[None, :]
    cos, sin = jnp.cos(ang), jnp.sin(ang)
    z = jnp.zeros_like(cos)
    c = jnp.concatenate([cos, cos, z, z], axis=-1)
    s1 = jnp.concatenate([-sin, z, z, z], axis=-1)
    s2 = jnp.concatenate([z, sin, z, z], axis=-1)
    return c, s1, s2


def kernel(x, mem, positions, rel_bias, mix_norm_g, w_in, diff_q_norm_g, diff_k_norm_g, diff_lambda_q1, diff_lambda_k1, diff_lambda_q2, diff_lambda_k2, diff_subln_g, mla_cq_norm_g, mla_ckv_norm_g, mla_w_uq, mla_w_ukv, mla_q_norm_g, mla_k_norm_g, mem_norm_g, mem_w_kv, mem_q_norm_g, mem_k_norm_g, w_o_diff, w_o_mla, w_o_mem, w_out, ffn_norm_g, w_route_group, b_route_group, w_route_expert, b_route_expert, w_exp_gate, w_exp_up, w_exp_down):
    b, s, d = x.shape
    assert b == 1 and s % ATT_T == 0
    depth = mix_norm_g.shape[0]
    xs = x.reshape(s, d)
    pos = positions.reshape(s)
    rope_tabs = _rope_tables(pos)
    row = lambda v: v.reshape(1, -1).astype(F32)

    for l in range(depth):
        lam_init = 0.8 - 0.6 * math.exp(-0.3 * l)
        h = _rmsnorm_rows(xs, mix_norm_g[l], 256)
        tn = 512
        q_gain = jnp.tile(diff_q_norm_g[l] * (DIFF_HEAD_DIM ** -0.5 * LOG2E), DIFF_MAPS)
        w_in_t = jnp.transpose(w_in[l])
        dq_t = _matmul_nt(h, w_in_t, row0=OFF_DQ, n_cols=DIFF_QK_WIDTH, tm=MM_TM, tn=tn, out_dtype=BF16,
                          mode="groupnorm", extra=q_gain, group=DIFF_HEAD_DIM, transpose_out=True,
                          name="diff_q_proj")
        dk = _matmul_nt(h, w_in_t, row0=OFF_DK, n_cols=DIFF_QK_WIDTH, tm=MM_TM, tn=tn, out_dtype=BF16,
                        mode="groupnorm", extra=jnp.tile(diff_k_norm_g[l], DIFF_MAPS), group=DIFF_HEAD_DIM,
                        name="diff_k_proj")
        dv_t = _matmul_nt(h, w_in_t, row0=OFF_DV, n_cols=DIFF_WIDTH, tm=MM_TM, tn=tn, out_dtype=BF16,
                          transpose_out=True, name="diff_v_proj")
        rest = _matmul_nt(h, w_in_t, row0=OFF_CQ, n_cols=REST_WIDTH, tm=MM_TM, tn=tn, out_dtype=BF16,
                          name="rest_proj")
        gates = _matmul_nt(h, w_in_t, row0=OFF_GATES, n_cols=3 * d, tm=MM_TM, tn=tn, out_dtype=BF16,
                           mode="sigmoid", name="gate_proj")

        bias_strips = _diff_bias_strips(rel_bias, ATT_T)
        lam_vecs = [row(diff_lambda_q1[l]), row(diff_lambda_k1[l]), row(diff_lambda_q2[l]), row(diff_lambda_k2[l])]
        o_diff = _diff_attention(dq_t, dk, dv_t, bias_strips, lam_vecs, diff_subln_g[l], lam_init)

        w_uq_heads = jnp.pad(
            mla_w_uq[l].reshape(MLA_Q_RANK, MLA_HEADS, MLA_QK_DIM),
            ((0, 0), (0, 0), (0, MLA_QK_PAD - MLA_QK_DIM))).reshape(MLA_Q_RANK, MLA_HEADS * MLA_QK_PAD).astype(BF16)
        qg_pad = jnp.pad(mla_q_norm_g[l] * (MLA_QK_DIM ** -0.5 * LOG2E),
                         (0, MLA_QK_PAD - MLA_QK_DIM)).reshape(1, -1).astype(F32)
        q_mla_t = _mla_q_prep(rest, row(mla_cq_norm_g[l]), w_uq_heads, qg_pad, rope_tabs, MM_TM)
        kg = mla_k_norm_g[l]
        kg_nope = row(kg[:MLA_NOPE_DIM])
        kg_rope = jnp.pad(kg[MLA_NOPE_DIM:], (0, LANE - MLA_ROPE_DIM)).reshape(1, -1).astype(F32)
        k_mla, v_mla_t = _mla_kv_prep(rest, row(mla_ckv_norm_g[l]), mla_w_ukv[l], kg_nope, kg_rope, rope_tabs, MM_TM)
        o_mla = _mla_attention(q_mla_t, k_mla, v_mla_t)

        n_mem = mem.shape[1]
        mem_h = _rmsnorm_rows(mem.reshape(n_mem, d), mem_norm_g[l], n_mem)
        k_mem = _matmul(mem_h, mem_w_kv[l], col0=0, n_cols=MEM_WIDTH, tm=n_mem, tn=512, out_dtype=BF16,
                        mode="groupnorm", extra=jnp.tile(mem_k_norm_g[l], MEM_HEADS), group=MEM_HEAD_DIM,
                        name="mem_k_proj")
        v_mem = _matmul(mem_h, mem_w_kv[l], col0=MEM_WIDTH, n_cols=MEM_WIDTH, tm=n_mem, tn=512, out_dtype=BF16,
                        name="mem_v_proj")
        o_mem = _mem_attention(rest, k_mem, v_mem, row(mem_q_norm_g[l] * MEM_HEAD_DIM ** -0.5), 512)

        mixed = _mix(o_diff, o_mla, o_mem, w_o_diff[l], w_o_mla[l], w_o_mem[l], gates, MM_TM, 512)
        x1 = _matmul(mixed, w_out[l], n_cols=d, tm=MM_TM, tn=512, out_dtype=F32, mode="residual", extra=xs,
                     name="out_proj")

        w_r = jnp.pad(jnp.concatenate([w_route_group[l], w_route_expert[l]], axis=1),
                      ((0, 0), (0, ROUTE_W - N_GROUPS - N_EXPERTS))).astype(F32)
        w_r_hi = w_r.astype(BF16)
        w_r = jnp.stack([w_r_hi, (w_r - w_r_hi.astype(F32)).astype(BF16)])
        b_r = jnp.pad(jnp.concatenate([b_route_group[l], b_route_expert[l]]),
                      (0, ROUTE_W - N_GROUPS - N_EXPERTS)).reshape(1, -1).astype(F32)
        h2, route = _router(x1, row(ffn_norm_g[l]), w_r, b_r, 256)
        block_e, nsub, n_used, buf_tok, slots = _dispatch_plan(route, s)
        yb = _moe_experts(block_e, nsub, n_used, buf_tok, h2, w_exp_gate[l], w_exp_up[l], w_exp_down[l])
        xs = _combine(slots, route, x1, yb, 256)
    return xs.reshape(b, s, d)
```

```python
import functools
import math

import jax
import jax.numpy as jnp
from jax import lax
from jax.experimental import pallas as pl
from jax.experimental.pallas import tpu as pltpu

F32 = jnp.float32
BF16 = jnp.bfloat16

NORM_EPS = 1e-6
NEG_INF = -1e30
LOG2E = math.log2(math.e)

DIFF_HEADS = 6
DIFF_HEAD_DIM = 128
DIFF_V_DIM = 256
DIFF_MAPS = 12
DIFF_QK_WIDTH = 1536
DIFF_WIDTH = 1536
MLA_HEADS = 12
MLA_Q_RANK = 1536
MLA_KV_RANK = 512
MLA_NOPE_DIM = 128
MLA_ROPE_DIM = 64
MLA_QK_DIM = 192
MLA_QK_PAD = 256
MLA_V_DIM = 128
MLA_WIDTH = 1536
ROPE_THETA = 10000.0
MEM_HEADS = 4
MEM_HEAD_DIM = 256
MEM_WIDTH = 1024
REL_BUCKETS = 32
REL_MAX_DIST = 128
N_GROUPS = 8
EXPERTS_PER_GROUP = 8
N_EXPERTS = 64
TOP_K = 2
EXPERT_FF = 512

OFF_DQ = 0
OFF_DK = OFF_DQ + DIFF_QK_WIDTH
OFF_DV = OFF_DK + DIFF_QK_WIDTH
OFF_CQ = OFF_DV + DIFF_WIDTH
OFF_CKV = OFF_CQ + MLA_Q_RANK
OFF_KROPE = OFF_CKV + MLA_KV_RANK
OFF_MQ = OFF_KROPE + MLA_ROPE_DIM
OFF_GATES = OFF_MQ + MEM_WIDTH
REST_WIDTH = 3584
R_CQ = OFF_CQ - OFF_CQ
R_CKV = OFF_CKV - OFF_CQ
R_KROPE = OFF_KROPE - OFF_CQ
R_MQ = OFF_MQ - OFF_CQ

LANE = 128
VMEM_LIMIT = 52 * 1024 * 1024

MM_TM = 1024
MLA_HEAD_GROUP = 4
ATT_T = 512
MOE_TB = 512
MOE_RU = 128
MOE_PH = 4
ROUTE_W = 128
MEM_WIN_START = R_MQ // LANE * LANE
MEM_WINDOW = (R_MQ - MEM_WIN_START + MEM_WIDTH + LANE - 1) // LANE * LANE


def _cp(sem, vmem=VMEM_LIMIT):
    return pltpu.CompilerParams(dimension_semantics=sem, vmem_limit_bytes=vmem)


def _rmsnorm_kernel(x_ref, g_ref, o_ref):
    x = x_ref[...].astype(F32)
    ms = jnp.mean(x * x, axis=-1, keepdims=True)
    o_ref[...] = (x * lax.rsqrt(ms + NORM_EPS) * g_ref[...]).astype(o_ref.dtype)


def _rmsnorm_rows(x, g, tm, out_dtype=BF16):
    m, d = x.shape
    return pl.pallas_call(
        _rmsnorm_kernel,
        grid=(m // tm,),
        in_specs=[pl.BlockSpec((tm, d), lambda i: (i, 0)),
                  pl.BlockSpec((1, d), lambda i: (0, 0))],
        out_specs=pl.BlockSpec((tm, d), lambda i: (i, 0)),
        out_shape=jax.ShapeDtypeStruct((m, d), out_dtype),
        compiler_params=_cp(("parallel",)),
        name="rmsnorm_rows",
    )(x, g.reshape(1, d).astype(F32))


def _mm_kernel(*refs, mode, group, nt, transpose_out):
    a_ref, w_ref = refs[0], refs[1]
    extra = refs[2] if mode in ("groupnorm", "residual") else None
    o_ref, wb_ref = refs[-2], refs[-1]

    @pl.when(pl.program_id(1) == 0)
    def _():
        wb_ref[...] = w_ref[...].astype(BF16)

    if nt:
        contract_last = (((1,), (1,)), ((), ()))
        if transpose_out:
            acc = lax.dot_general(wb_ref[...], a_ref[...], contract_last, preferred_element_type=F32)
        else:
            acc = lax.dot_general(a_ref[...], wb_ref[...], contract_last, preferred_element_type=F32)
    else:
        acc = jnp.dot(a_ref[...], wb_ref[...], preferred_element_type=F32)
    if mode == "plain":
        o_ref[...] = acc.astype(o_ref.dtype)
    elif mode == "sigmoid":
        o_ref[...] = jax.nn.sigmoid(acc).astype(o_ref.dtype)
    elif mode == "residual":
        o_ref[...] = (extra[...] + acc).astype(o_ref.dtype)
    elif mode == "groupnorm":
        tn = acc.shape[0] if transpose_out else acc.shape[1]
        for c in range(tn // group):
            sl = slice(c * group, (c + 1) * group)
            if transpose_out:
                blk = acc[sl, :]
                ms = jnp.mean(blk * blk, axis=0, keepdims=True)
                o_ref[sl, :] = (blk * lax.rsqrt(ms + NORM_EPS) * extra[sl, :]).astype(o_ref.dtype)
            else:
                blk = acc[:, sl]
                ms = jnp.mean(blk * blk, axis=-1, keepdims=True)
                o_ref[:, sl] = (blk * lax.rsqrt(ms + NORM_EPS) * extra[:, sl]).astype(o_ref.dtype)


def _matmul(a, w, *, first, n_out, tm, tn, out_dtype, w_rows_are_outputs, mode="plain", extra=None, group=LANE,
            transpose_out=False, name="matmul"):
    m, k = a.shape
    assert m % tm == 0 and n_out % tn == 0 and w.dtype == F32
    if w_rows_are_outputs:
        assert w.shape[1] == k and first % 8 == 0
        w_spec = pl.BlockSpec((pl.Element(tn), pl.Element(k)),
                              lambda j, i: (pl.multiple_of(first + j * tn, 8), 0))
        wb_shape = (tn, k)
    else:
        assert w.shape[0] == k and first % tn == 0 and not transpose_out
        w_spec = pl.BlockSpec((k, tn), lambda j, i: (0, first // tn + j))
        wb_shape = (k, tn)
    in_specs = [pl.BlockSpec((tm, k), lambda j, i: (i, 0)), w_spec]
    args = [a, w]
    if mode == "groupnorm":
        gain = extra.reshape(-1, 1) if transpose_out else extra.reshape(1, -1)
        in_specs.append(pl.BlockSpec((tn, 1), lambda j, i: (j, 0)) if transpose_out
                        else pl.BlockSpec((1, tn), lambda j, i: (0, j)))
        args.append(gain.astype(F32))
    elif mode == "residual":
        assert not transpose_out
        in_specs.append(pl.BlockSpec((tm, tn), lambda j, i: (i, j)))
        args.append(extra)
    if transpose_out:
        out_spec = pl.BlockSpec((tn, tm), lambda j, i: (j, i))
        out_shape = jax.ShapeDtypeStruct((n_out, m), out_dtype)
    else:
        out_spec = pl.BlockSpec((tm, tn), lambda j, i: (i, j))
        out_shape = jax.ShapeDtypeStruct((m, n_out), out_dtype)
    return pl.pallas_call(
        functools.partial(_mm_kernel, mode=mode, group=group, nt=w_rows_are_outputs, transpose_out=transpose_out),
        grid=(n_out // tn, m // tm),
        in_specs=in_specs,
        out_specs=out_spec,
        out_shape=out_shape,
        scratch_shapes=[pltpu.VMEM(wb_shape, BF16)],
        compiler_params=_cp(("arbitrary", "arbitrary")),
        name=name,
    )(*args)


def _softmax_pv(idx, s, vt_blk, m_ref, l_ref, acc_ref, col0=0):
    cols = slice(col0, col0 + s.shape[1])
    m_prev = m_ref[idx, :, cols]
    m_new = jnp.maximum(m_prev, jnp.max(s, axis=0, keepdims=True))
    alpha = jnp.exp2(m_prev - m_new)
    p = jnp.exp2(s - m_new)
    l_ref[idx, :, cols] = alpha * l_ref[idx, :, cols] + jnp.sum(p, axis=0, keepdims=True)
    acc_ref[idx, :, cols] = (alpha * acc_ref[idx, :, cols]
                             + jnp.dot(vt_blk, p.astype(BF16), preferred_element_type=F32))
    m_ref[idx, :, cols] = m_new


def _block_offset(kb, t):
    return kb * t if isinstance(kb, int) else pl.multiple_of(kb * t, t)


def _init_stats(m_ref, l_ref, acc_ref):
    m_ref[...] = jnp.full(m_ref.shape, NEG_INF, F32)
    l_ref[...] = jnp.zeros(l_ref.shape, F32)
    acc_ref[...] = jnp.zeros(acc_ref.shape, F32)


def _diff_attn_kernel(lq1_ref, lk1_ref, lq2_ref, lk2_ref, q1_ref, q2_ref, k1_ref, k2_ref, vt_ref,
                      b1_ref, b2_ref, g_ref, o_ref, sa_ref, sb_ref, m_ref, l_ref, acc_ref, *, lam_init):
    _, tk, tq = sa_ref.shape
    i = pl.program_id(1)
    qts = (q1_ref[...], q2_ref[...])
    ks = (k1_ref, k2_ref)
    bs = (b1_ref, b2_ref)
    _init_stats(m_ref, l_ref, acc_ref)

    def scores(kb, dst):
        off = _block_offset(kb, tk)
        for mp in range(2):
            dst[mp] = jnp.dot(ks[mp][pl.ds(off, tk), :], qts[mp], preferred_element_type=F32)

    def scores_right(kb, dst):
        off = _block_offset(kb, tk)
        for mp in range(2):
            dst[mp, :, tk:] = jnp.dot(ks[mp][pl.ds(off, tk), :], qts[mp][:, tk:], preferred_element_type=F32)

    def consume(kb, src, strip_off, col0=0):
        off = _block_offset(kb, tk)
        vt_blk = vt_ref[:, pl.ds(off, tk)]
        for mp in range(2):
            s = src[mp, :, col0:]
            if strip_off is not None:
                s = s + bs[mp][0, :, strip_off + col0:strip_off + tq]
            _softmax_pv(mp, s, vt_blk, m_ref, l_ref, acc_ref, col0)

    scores(0, sa_ref)

    def far_pair(j, c):
        kb = 2 * j
        scores(kb + 1, sb_ref)
        consume(kb, sa_ref, None)
        scores(kb + 2, sa_ref)
        consume(kb + 1, sb_ref, None)
        return c

    lax.fori_loop(0, jnp.maximum(i - 1, 0), far_pair, 0)

    @pl.when(i >= 1)
    def _():
        scores(2 * i - 1, sb_ref)
        consume(2 * i - 2, sa_ref, None)
        scores(2 * i, sa_ref)
        consume(2 * i - 1, sb_ref, 2 * tk)

    scores_right(2 * i + 1, sb_ref)
    consume(2 * i, sa_ref, tk)
    consume(2 * i + 1, sb_ref, 0, col0=tk)

    lam = (jnp.exp(jnp.sum(lq1_ref[...] * lk1_ref[...], axis=-1, keepdims=True))
           - jnp.exp(jnp.sum(lq2_ref[...] * lk2_ref[...], axis=-1, keepdims=True)) + lam_init)
    o = acc_ref[0] / l_ref[0] - lam * (acc_ref[1] / l_ref[1])
    ms = jnp.mean(o * o, axis=0, keepdims=True)
    o = (o * lax.rsqrt(ms + NORM_EPS) * g_ref[...]) * (1.0 - lam_init)
    o_ref[...] = o.T.astype(o_ref.dtype)


def _diff_attention(q_t, k, v_t, bias_strips, lam_vecs, subln_g, lam_init):
    s = k.shape[0]
    tk = ATT_T
    tq = 2 * tk
    assert s % tq == 0 and bias_strips.shape[1:] == (tk, 4 * tk)
    hd, vd = DIFF_HEAD_DIM, DIFF_V_DIM
    vec = pl.BlockSpec((1, hd), lambda h, i: (0, 0))
    in_specs = [vec, vec, vec, vec,
                pl.BlockSpec((hd, tq), lambda h, i: (h, i)),
                pl.BlockSpec((hd, tq), lambda h, i: (DIFF_HEADS + h, i)),
                pl.BlockSpec((s, hd), lambda h, i: (0, h)),
                pl.BlockSpec((s, hd), lambda h, i: (0, DIFF_HEADS + h)),
                pl.BlockSpec((vd, s), lambda h, i: (h, 0)),
                pl.BlockSpec((1, tk, 4 * tk), lambda h, i: (h, 0, 0), pipeline_mode=pl.Buffered(1)),
                pl.BlockSpec((1, tk, 4 * tk), lambda h, i: (DIFF_HEADS + h, 0, 0), pipeline_mode=pl.Buffered(1)),
                pl.BlockSpec((vd, 1), lambda h, i: (0, 0))]
    return pl.pallas_call(
        functools.partial(_diff_attn_kernel, lam_init=lam_init),
        grid=(DIFF_HEADS, s // tq),
        in_specs=in_specs,
        out_specs=pl.BlockSpec((tq, vd), lambda h, i: (i, h)),
        out_shape=jax.ShapeDtypeStruct((s, DIFF_WIDTH), BF16),
        scratch_shapes=[pltpu.VMEM((2, tk, tq), F32), pltpu.VMEM((2, tk, tq), F32),
                        pltpu.VMEM((2, 1, tq), F32), pltpu.VMEM((2, 1, tq), F32),
                        pltpu.VMEM((2, vd, tq), F32)],
        compiler_params=_cp(("arbitrary", "arbitrary")),
        name="diff_attention",
    )(*lam_vecs, q_t, q_t, k, k, v_t, bias_strips, bias_strips, subln_g.reshape(vd, 1).astype(F32))


def _t5_bucket(dist):
    n = jnp.maximum(dist, 0)
    max_exact = REL_BUCKETS // 2
    nf = jnp.maximum(n, 1).astype(F32)
    large = max_exact + (jnp.log(nf / max_exact) / math.log(REL_MAX_DIST / max_exact)
                         * (REL_BUCKETS - max_exact)).astype(jnp.int32)
    large = jnp.minimum(large, REL_BUCKETS - 1)
    return jnp.where(n < max_exact, n, large)


def _diff_bias_strips(rel_bias, tk):
    assert tk >= REL_MAX_DIST
    width = 4 * tk
    n = width + tk
    table = rel_bias.astype(F32)
    table = (table - table[REL_BUCKETS - 1:REL_BUCKETS]) * LOG2E
    kk = jnp.arange(n, dtype=jnp.int32)
    dist = kk - tk
    onehot = (_t5_bucket(dist)[:, None] == jnp.arange(REL_BUCKETS, dtype=jnp.int32)[None, :]).astype(F32)
    g = jnp.einsum('kb,bm->mk', onehot, table, precision=lax.Precision.HIGHEST)
    w = jnp.where(((dist >= 0) & (kk < width))[None, :], g, NEG_INF)
    maps = w.shape[0]

    def strip_kernel(w_ref, o_ref):
        rows = jnp.broadcast_to(w_ref[0], (tk, n))
        o_ref[0] = pltpu.roll(rows, 0, 1, stride=1, stride_axis=0)[:, :width]

    return pl.pallas_call(
        strip_kernel,
        grid=(maps,),
        in_specs=[pl.BlockSpec((1, 1, n), lambda m: (m, 0, 0))],
        out_specs=pl.BlockSpec((1, tk, width), lambda m: (m, 0, 0)),
        out_shape=jax.ShapeDtypeStruct((maps, tk, width), F32),
        compiler_params=_cp(("parallel",)),
        name="bias_strips",
    )(w.reshape(maps, 1, n))


def _rope_apply(tv, c_ref, s1_ref, s2_ref):
    return (tv * c_ref[...] + pltpu.roll(tv, 96, 1) * s1_ref[...] + pltpu.roll(tv, 32, 1) * s2_ref[...])


def _mla_q_kernel(cq_ref, g_ref, w_ref, qg_ref, c_ref, s1_ref, s2_ref, o_ref, xg_ref):
    h = pl.program_id(1)

    @pl.when(h == 0)
    def _():
        c = cq_ref[...].astype(F32)
        r = lax.rsqrt(jnp.mean(c * c, axis=-1, keepdims=True) + NORM_EPS)
        xg_ref[...] = (c * r * g_ref[...]).astype(BF16)

    ug = jnp.dot(xg_ref[...], w_ref[...], preferred_element_type=F32)
    for hh in range(MLA_HEAD_GROUP):
        u = ug[:, hh * MLA_QK_PAD:(hh + 1) * MLA_QK_PAD]
        ms = jnp.sum(u * u, axis=-1, keepdims=True) * (1.0 / MLA_QK_DIM)
        qn = u * lax.rsqrt(ms + NORM_EPS) * qg_ref[...]
        o_ref[hh, :MLA_NOPE_DIM, :] = qn[:, :MLA_NOPE_DIM].T.astype(o_ref.dtype)
        o_ref[hh, MLA_NOPE_DIM:, :] = _rope_apply(qn[:, MLA_NOPE_DIM:], c_ref, s1_ref, s2_ref).T.astype(o_ref.dtype)


def _mla_q_prep(rest, cq_g, w_uq_pad, qg_pad, rope_tabs, tm):
    s = rest.shape[0]
    hg = MLA_HEAD_GROUP
    tab = pl.BlockSpec((tm, LANE), lambda i, h: (i, 0))
    return pl.pallas_call(
        _mla_q_kernel,
        grid=(s // tm, MLA_HEADS // hg),
        in_specs=[pl.BlockSpec((tm, MLA_Q_RANK), lambda i, h: (i, R_CQ // MLA_Q_RANK)),
                  pl.BlockSpec((1, MLA_Q_RANK), lambda i, h: (0, 0)),
                  pl.BlockSpec((MLA_Q_RANK, hg * MLA_QK_PAD), lambda i, h: (0, h)),
                  pl.BlockSpec((1, MLA_QK_PAD), lambda i, h: (0, 0)),
                  tab, tab, tab],
        out_specs=pl.BlockSpec((hg, MLA_QK_PAD, tm), lambda i, h: (h, 0, i)),
        out_shape=jax.ShapeDtypeStruct((MLA_HEADS, MLA_QK_PAD, s), BF16),
        scratch_shapes=[pltpu.VMEM((tm, MLA_Q_RANK), BF16)],
        compiler_params=_cp(("arbitrary", "arbitrary")),
        name="mla_q_prep",
    )(rest, cq_g, w_uq_pad, qg_pad, *rope_tabs)


def _mla_kv_kernel(ckv_ref, kr_ref, g_ref, w_ref, kgn_ref, kgr_ref, c_ref, s1_ref, s2_ref,
                   k_ref, vt_ref, xg_ref):
    h = pl.program_id(1)

    @pl.when(h == 0)
    def _():
        c = ckv_ref[...].astype(F32)
        r = lax.rsqrt(jnp.mean(c * c, axis=-1, keepdims=True) + NORM_EPS)
        xg_ref[...] = (c * r * g_ref[...]).astype(BF16)

    hw = MLA_NOPE_DIM + MLA_V_DIM
    kvg = jnp.dot(xg_ref[...], w_ref[...].astype(BF16), preferred_element_type=F32)
    lane = lax.broadcasted_iota(jnp.int32, kr_ref.shape, 1)
    kr = jnp.where(lane < MLA_ROPE_DIM, kr_ref[...].astype(F32), 0.0)
    kr_ss = jnp.sum(kr * kr, axis=-1, keepdims=True)
    for hh in range(MLA_HEAD_GROUP):
        kn = kvg[:, hh * hw:hh * hw + MLA_NOPE_DIM]
        ms = (jnp.sum(kn * kn, axis=-1, keepdims=True) + kr_ss) * (1.0 / MLA_QK_DIM)
        rs = lax.rsqrt(ms + NORM_EPS)
        k_ref[hh, :, :MLA_NOPE_DIM] = (kn * rs * kgn_ref[...]).astype(k_ref.dtype)
        k_ref[hh, :, MLA_NOPE_DIM:] = _rope_apply(kr * rs * kgr_ref[...], c_ref, s1_ref, s2_ref).astype(k_ref.dtype)
        vt_ref[hh] = kvg[:, hh * hw + MLA_NOPE_DIM:(hh + 1) * hw].T.astype(vt_ref.dtype)


def _mla_kv_prep(rest, ckv_g, w_ukv, kg_nope, kg_rope_pad, rope_tabs, tm):
    s = rest.shape[0]
    hg = MLA_HEAD_GROUP
    tab = pl.BlockSpec((tm, LANE), lambda i, h: (i, 0))
    hw = MLA_NOPE_DIM + MLA_V_DIM
    return pl.pallas_call(
        _mla_kv_kernel,
        grid=(s // tm, MLA_HEADS // hg),
        in_specs=[pl.BlockSpec((tm, MLA_KV_RANK), lambda i, h: (i, R_CKV // MLA_KV_RANK)),
                  pl.BlockSpec((tm, LANE), lambda i, h: (i, R_KROPE // LANE)),
                  pl.BlockSpec((1, MLA_KV_RANK), lambda i, h: (0, 0)),
                  pl.BlockSpec((MLA_KV_RANK, hg * hw), lambda i, h: (0, h)),
                  pl.BlockSpec((1, LANE), lambda i, h: (0, 0)),
                  pl.BlockSpec((1, LANE), lambda i, h: (0, 0)),
                  tab, tab, tab],
        out_specs=[pl.BlockSpec((hg, tm, MLA_QK_PAD), lambda i, h: (h, i, 0)),
                   pl.BlockSpec((hg, MLA_V_DIM, tm), lambda i, h: (h, 0, i))],
        out_shape=[jax.ShapeDtypeStruct((MLA_HEADS, s, MLA_QK_PAD), BF16),
                   jax.ShapeDtypeStruct((MLA_HEADS, MLA_V_DIM, s), BF16)],
        scratch_shapes=[pltpu.VMEM((tm, MLA_KV_RANK), BF16)],
        compiler_params=_cp(("arbitrary", "arbitrary")),
        name="mla_kv_prep",
    )(rest, rest, ckv_g, w_ukv, kg_nope, kg_rope_pad, *rope_tabs)


def _mla_attn_kernel(qt_ref, k_ref, vt_ref, o_ref, sa_ref, sb_ref, m_ref, l_ref, acc_ref):
    tk, tq = sa_ref.shape
    i = pl.program_id(1)
    qt = qt_ref[0]
    _init_stats(m_ref, l_ref, acc_ref)

    def scores(kb, dst):
        off = _block_offset(kb, tk)
        dst[...] = jnp.dot(k_ref[0, pl.ds(off, tk), :], qt, preferred_element_type=F32)

    def scores_right(kb, dst):
        off = _block_offset(kb, tk)
        dst[:, tk:] = jnp.dot(k_ref[0, pl.ds(off, tk), :], qt[:, tk:], preferred_element_type=F32)

    def consume(kb, src, diag, col0=0):
        off = _block_offset(kb, tk)
        s = src[:, col0:]
        if diag:
            krow = lax.broadcasted_iota(jnp.int32, s.shape, 0)
            qcol = lax.broadcasted_iota(jnp.int32, s.shape, 1)
            s = jnp.where(krow <= qcol, s, NEG_INF)
        _softmax_pv(0, s, vt_ref[0, :, pl.ds(off, tk)], m_ref, l_ref, acc_ref, col0)

    scores(0, sa_ref)

    def far_pair(j, c):
        kb = 2 * j
        scores(kb + 1, sb_ref)
        consume(kb, sa_ref, False)
        scores(kb + 2, sa_ref)
        consume(kb + 1, sb_ref, False)
        return c

    lax.fori_loop(0, i, far_pair, 0)
    scores_right(2 * i + 1, sb_ref)
    consume(2 * i, sa_ref, True)
    consume(2 * i + 1, sb_ref, True, col0=tk)
    o_ref[...] = (acc_ref[0] / l_ref[0]).T.astype(o_ref.dtype)


def _mla_attention(q_t, k, v_t):
    s = k.shape[1]
    tk = ATT_T
    tq = 2 * tk
    assert s % tq == 0
    return pl.pallas_call(
        _mla_attn_kernel,
        grid=(MLA_HEADS, s // tq),
        in_specs=[pl.BlockSpec((1, MLA_QK_PAD, tq), lambda h, i: (h, 0, i)),
                  pl.BlockSpec((1, s, MLA_QK_PAD), lambda h, i: (h, 0, 0)),
                  pl.BlockSpec((1, MLA_V_DIM, s), lambda h, i: (h, 0, 0))],
        out_specs=pl.BlockSpec((tq, MLA_V_DIM), lambda h, i: (i, h)),
        out_shape=jax.ShapeDtypeStruct((s, MLA_WIDTH), BF16),
        scratch_shapes=[pltpu.VMEM((tk, tq), F32), pltpu.VMEM((tk, tq), F32),
                        pltpu.VMEM((1, 1, tq), F32), pltpu.VMEM((1, 1, tq), F32),
                        pltpu.VMEM((1, MLA_V_DIM, tq), F32)],
        compiler_params=_cp(("arbitrary", "arbitrary")),
        name="mla_attention",
    )(q_t, k, v_t)


def _qk_nt(q, k_blk):
    return lax.dot_general(q, k_blk, (((1,), (1,)), ((), ())), preferred_element_type=F32)


def _mem_attn_kernel(q_ref, k_ref, v_ref, qg_ref, o_ref):
    shift = R_MQ - MEM_WIN_START
    qall = q_ref[...].astype(F32)[:, shift:shift + MEM_WIDTH]
    for h in range(MEM_HEADS):
        lo = h * MEM_HEAD_DIM
        qh = qall[:, lo:lo + MEM_HEAD_DIM]
        ms = jnp.mean(qh * qh, axis=-1, keepdims=True)
        qn = (qh * lax.rsqrt(ms + NORM_EPS) * qg_ref[...]).astype(BF16)
        s = _qk_nt(qn, k_ref[:, lo:lo + MEM_HEAD_DIM])
        p = jnp.exp(s - jnp.max(s, axis=-1, keepdims=True))
        l = jnp.sum(p, axis=-1, keepdims=True)
        o = jnp.dot(p.astype(BF16), v_ref[:, lo:lo + MEM_HEAD_DIM], preferred_element_type=F32)
        o_ref[:, lo:lo + MEM_HEAD_DIM] = (o / l).astype(o_ref.dtype)


def _mem_attention(rest, k_mem, v_mem, qg_scaled, tm):
    s = rest.shape[0]
    n_mem = k_mem.shape[0]
    assert 0 <= R_MQ - MEM_WIN_START and R_MQ - MEM_WIN_START + MEM_WIDTH <= MEM_WINDOW
    assert MEM_WIN_START % LANE == 0 and MEM_WIN_START + MEM_WINDOW <= rest.shape[1]
    return pl.pallas_call(
        _mem_attn_kernel,
        grid=(s // tm,),
        in_specs=[pl.BlockSpec((pl.Element(tm), pl.Element(MEM_WINDOW)),
                               lambda i: (pl.multiple_of(i * tm, tm), MEM_WIN_START)),
                  pl.BlockSpec((n_mem, MEM_WIDTH), lambda i: (0, 0)),
                  pl.BlockSpec((n_mem, MEM_WIDTH), lambda i: (0, 0)),
                  pl.BlockSpec((1, MEM_HEAD_DIM), lambda i: (0, 0))],
        out_specs=pl.BlockSpec((tm, MEM_WIDTH), lambda i: (i, 0)),
        out_shape=jax.ShapeDtypeStruct((s, MEM_WIDTH), BF16),
        compiler_params=_cp(("parallel",)),
        name="mem_attention",
    )(rest, k_mem, v_mem, qg_scaled)


def _mix_kernel(od_ref, om_ref, oc_ref, wd_ref, wm_ref, wc_ref, g0_ref, g1_ref, g2_ref, o_ref,
                wdb_ref, wmb_ref, wcb_ref):
    @pl.when(pl.program_id(1) == 0)
    def _():
        wdb_ref[...] = wd_ref[...].astype(BF16)
        wmb_ref[...] = wm_ref[...].astype(BF16)
        wcb_ref[...] = wc_ref[...].astype(BF16)

    yd = jnp.dot(od_ref[...], wdb_ref[...], preferred_element_type=F32)
    ym = jnp.dot(om_ref[...], wmb_ref[...], preferred_element_type=F32)
    yc = jnp.dot(oc_ref[...], wcb_ref[...], preferred_element_type=F32)
    mixed = (g0_ref[...].astype(F32) * yd + g1_ref[...].astype(F32) * ym) + g2_ref[...].astype(F32) * yc
    o_ref[...] = mixed.astype(o_ref.dtype)


def _mix(o_diff, o_mla, o_mem, w_d, w_m, w_c, gates, tm, tn):
    s = o_diff.shape[0]
    d = w_d.shape[1]
    nj = d // tn
    return pl.pallas_call(
        _mix_kernel,
        grid=(nj, s // tm),
        in_specs=[pl.BlockSpec((tm, o_diff.shape[1]), lambda j, i: (i, 0)),
                  pl.BlockSpec((tm, o_mla.shape[1]), lambda j, i: (i, 0)),
                  pl.BlockSpec((tm, o_mem.shape[1]), lambda j, i: (i, 0)),
                  pl.BlockSpec((w_d.shape[0], tn), lambda j, i: (0, j)),
                  pl.BlockSpec((w_m.shape[0], tn), lambda j, i: (0, j)),
                  pl.BlockSpec((w_c.shape[0], tn), lambda j, i: (0, j)),
                  pl.BlockSpec((tm, tn), lambda j, i: (i, j)),
                  pl.BlockSpec((tm, tn), lambda j, i: (i, nj + j)),
                  pl.BlockSpec((tm, tn), lambda j, i: (i, 2 * nj + j))],
        out_specs=pl.BlockSpec((tm, tn), lambda j, i: (i, j)),
        out_shape=jax.ShapeDtypeStruct((s, d), BF16),
        scratch_shapes=[pltpu.VMEM((w_d.shape[0], tn), BF16), pltpu.VMEM((w_m.shape[0], tn), BF16),
                        pltpu.VMEM((w_c.shape[0], tn), BF16)],
        compiler_params=_cp(("arbitrary", "arbitrary")),
        name="gated_mix",
    )(o_diff, o_mla, o_mem, w_d, w_m, w_c, gates, gates, gates)


def _router_kernel(x_ref, g_ref, w_ref, b_ref, h_ref, r_ref):
    x = x_ref[...]
    ms = jnp.mean(x * x, axis=-1, keepdims=True)
    h = x * lax.rsqrt(ms + NORM_EPS) * g_ref[...]
    h_ref[...] = _pack_bf16_pairs(h)
    h_hi = h.astype(BF16)
    h_lo = (h - h_hi.astype(F32)).astype(BF16)
    logits = (jnp.dot(h_hi, w_ref[0], preferred_element_type=F32)
              + (jnp.dot(h_lo, w_ref[0], preferred_element_type=F32)
                 + jnp.dot(h_hi, w_ref[1], preferred_element_type=F32))) + b_ref[...]
    lane = lax.broadcasted_iota(jnp.int32, logits.shape, 1)
    lane_f = lane.astype(F32)
    big = float(4 * ROUTE_W)
    lg = jnp.where(lane < N_GROUPS, logits, -jnp.inf)
    gmax = jnp.max(lg, axis=-1, keepdims=True)
    gidx = jnp.min(jnp.where(lg == gmax, lane_f, big), axis=-1, keepdims=True)
    pg_top = 1.0 / jnp.sum(jnp.exp(lg - gmax), axis=-1, keepdims=True)
    e_lane = lane - N_GROUPS
    lane_group = jnp.right_shift(e_lane, 3).astype(F32)
    in_group = (e_lane >= 0) & (e_lane < N_EXPERTS) & (lane_group == gidx)
    le = jnp.where(in_group, logits, -jnp.inf)
    e1 = jnp.max(le, axis=-1, keepdims=True)
    i1 = jnp.min(jnp.where(le == e1, lane_f, big), axis=-1, keepdims=True)
    le2 = jnp.where(lane_f == i1, -jnp.inf, le)
    e2 = jnp.max(le2, axis=-1, keepdims=True)
    i2 = jnp.min(jnp.where(le2 == e2, lane_f, big), axis=-1, keepdims=True)
    w2 = jnp.exp(e2 - e1)
    inv = 1.0 / (1.0 + w2)
    gate1 = pg_top * inv
    gate2 = pg_top * (w2 * inv)
    out = jnp.where(lane == 0, i1 - N_GROUPS,
                    jnp.where(lane == 1, i2 - N_GROUPS,
                              jnp.where(lane == 2, gate1, jnp.where(lane == 3, gate2, 0.0))))
    r_ref[...] = out


def _router(x1, g, w_r, b_r, tm):
    s, d = x1.shape
    return pl.pallas_call(
        _router_kernel,
        grid=(s // tm,),
        in_specs=[pl.BlockSpec((tm, d), lambda i: (i, 0)),
                  pl.BlockSpec((1, d), lambda i: (0, 0)),
                  pl.BlockSpec((2, d, ROUTE_W), lambda i: (0, 0, 0)),
                  pl.BlockSpec((1, ROUTE_W), lambda i: (0, 0))],
        out_specs=[pl.BlockSpec((tm, d // 2), lambda i: (i, 0)),
                   pl.BlockSpec((tm, ROUTE_W), lambda i: (i, 0))],
        out_shape=[jax.ShapeDtypeStruct((s, d // 2), jnp.uint32),
                   jax.ShapeDtypeStruct((s, ROUTE_W), F32)],
        compiler_params=_cp(("parallel",)),
        name="ffn_norm_router",
    )(x1, g, w_r, b_r)


def _pack_bf16_pairs(v):
    n = v.shape[1] // 2
    bits = lax.bitcast_convert_type(v.astype(BF16).astype(F32), jnp.uint32)
    return jnp.right_shift(bits[:, :n], jnp.uint32(16)) | bits[:, n:]


def _unpack_bf16_pairs(words):
    lo = lax.bitcast_convert_type(jnp.left_shift(words, jnp.uint32(16)), F32)
    hi = lax.bitcast_convert_type(words & jnp.uint32(0xFFFF0000), F32)
    return lo, hi


def _moe_kernel(be_ref, nr_ref, nu_ref, tok_ref, tokn_ref, h_ref, wg_ref, wu_ref, wd_ref, o_ref,
                xg_ref, xb_ref, gp_ref, a_ref, wgb_ref, wub_ref, wdb_ref, sem):
    b = pl.program_id(0)
    c = pl.program_id(1)
    nb = pl.num_programs(0)
    nr = nr_ref[b]

    def row_copy(src_row, slot, r):
        return pltpu.make_async_copy(h_ref.at[pl.ds(src_row, 1), :], xg_ref.at[slot, pl.ds(r, 1), :], sem.at[slot])

    def start_gather(tok, n_rows, slot):
        unroll = 8

        def issue(i, carry):
            for j in range(unroll):
                r = i * unroll + j
                row_copy(tok[0, 0, r], slot, r).start(priority=1)
            return carry
        lax.fori_loop(0, n_rows // unroll, issue, 0)

    def for_row_count(fn):
        for units in range(1, MOE_TB // MOE_RU + 1):
            @pl.when(nr == units)
            def _():
                fn(units * MOE_RU)

    @pl.when(c == 0)
    def _():
        slot = b % 2

        @pl.when(b == 0)
        def _():
            start_gather(tok_ref, nr * MOE_RU, 0)

        def drain(u, carry):
            pltpu.make_async_copy(h_ref.at[pl.ds(0, MOE_RU), :], xg_ref.at[slot, pl.ds(0, MOE_RU), :],
                                  sem.at[slot]).wait()
            return carry
        lax.fori_loop(0, nr, drain, 0)

        @pl.when(b + 1 < nb)
        def _():
            start_gather(tokn_ref, nr_ref[jnp.minimum(b + 1, nb - 1)] * MOE_RU, 1 - slot)

        def unpack(u, carry):
            r0 = pl.multiple_of(u * MOE_RU, MOE_RU)
            lo, hi = _unpack_bf16_pairs(xg_ref[slot, pl.ds(r0, MOE_RU), :])
            xb_ref[0, pl.ds(r0, MOE_RU), :] = lo.astype(BF16)
            xb_ref[1, pl.ds(r0, MOE_RU), :] = hi.astype(BF16)
            return carry
        lax.fori_loop(0, nr, unpack, 0)

    @pl.when((c < 2) & (nr > 0))
    def _():
        wgb_ref[...] = wg_ref[0].astype(BF16)
        wub_ref[...] = wu_ref[0].astype(BF16)

        def gate_up(m):
            x = xb_ref[c, :m, :]
            g = jnp.dot(x, wgb_ref[...], preferred_element_type=F32)
            u = jnp.dot(x, wub_ref[...], preferred_element_type=F32)

            @pl.when(c == 0)
            def _():
                gp_ref[0, :m, :] = g
                gp_ref[1, :m, :] = u

            @pl.when(c == 1)
            def _():
                gs = gp_ref[0, :m, :] + g
                a_ref[:m, :] = ((gs * jax.nn.sigmoid(gs)) * (gp_ref[1, :m, :] + u)).astype(BF16)

        for_row_count(gate_up)

    @pl.when((c >= 2) & (nr > 0))
    def _():
        wdb_ref[...] = wd_ref[0].astype(BF16)

        def down(m):
            yv = jnp.dot(a_ref[:m, :], wdb_ref[...], preferred_element_type=F32)
            o_ref[:m, :] = _pack_bf16_pairs(yv)
            if m < MOE_TB:
                o_ref[m:, :] = jnp.zeros((MOE_TB - m, o_ref.shape[1]), jnp.uint32)

        for_row_count(down)

    @pl.when((c >= 2) & (nr == 0))
    def _():
        o_ref[...] = jnp.zeros(o_ref.shape, jnp.uint32)


def _moe_experts(block_e, nsub, n_used, buf_tok, h2p, w_gate, w_up, w_down):
    n_blocks = block_e.shape[0]
    half = h2p.shape[1]
    d = 2 * half
    ff = w_gate.shape[2]
    dh = d // 2
    tok = buf_tok.reshape(n_blocks, 1, MOE_TB)

    def gate_up_idx(b, c, be, nr, nu):
        live = b < nu[0]
        return be[jnp.minimum(b, nu[0] - 1)], jnp.where(live, jnp.minimum(c, 1), 1), 0

    def down_idx(b, c, be, nr, nu):
        bb = jnp.minimum(b, nu[0] - 1)
        cc = jnp.where(b < nu[0], c, MOE_PH - 1)
        e = jnp.where(cc >= 2, be[bb], be[jnp.maximum(bb - 1, 0)])
        return e, 0, jnp.where(cc == 2, 0, 1)

    def out_idx(b, c, be, nr, nu):
        ob = jnp.where(c >= 2, b, jnp.maximum(b - 1, 0))
        oc = jnp.where(c >= 2, c - 2, jnp.where(b > 0, 1, 0))
        return ob, oc

    grid_spec = pltpu.PrefetchScalarGridSpec(
        num_scalar_prefetch=3,
        grid=(n_blocks, MOE_PH),
        in_specs=[pl.BlockSpec((1, 1, MOE_TB), lambda b, c, be, nr, nu: (b, 0, 0), memory_space=pltpu.SMEM),
                  pl.BlockSpec((1, 1, MOE_TB), lambda b, c, be, nr, nu: (jnp.minimum(b + 1, n_blocks - 1), 0, 0),
                               memory_space=pltpu.SMEM),
                  pl.BlockSpec(memory_space=pl.ANY),
                  pl.BlockSpec((1, half, ff), gate_up_idx),
                  pl.BlockSpec((1, half, ff), gate_up_idx),
                  pl.BlockSpec((1, ff, dh), down_idx)],
        out_specs=pl.BlockSpec((MOE_TB, dh // 2), out_idx),
        scratch_shapes=[pltpu.VMEM((2, MOE_TB, half), jnp.uint32), pltpu.VMEM((2, MOE_TB, half), BF16),
                        pltpu.VMEM((2, MOE_TB, ff), F32), pltpu.VMEM((MOE_TB, ff), BF16),
                        pltpu.VMEM((half, ff), BF16), pltpu.VMEM((half, ff), BF16), pltpu.VMEM((ff, dh), BF16),
                        pltpu.SemaphoreType.DMA((2,))],
    )
    return pl.pallas_call(
        _moe_kernel,
        grid_spec=grid_spec,
        out_shape=jax.ShapeDtypeStruct((n_blocks * MOE_TB, half), jnp.uint32),
        compiler_params=_cp(("arbitrary", "arbitrary")),
        name="moe_experts",
    )(block_e, nsub, n_used, tok, tok, h2p, w_gate, w_up, w_down)


def _combine_kernel(slot_ref, slotn_ref, r_ref, x_ref, yb_ref, o_ref, g_ref, sem):
    tm = x_ref.shape[0]
    n = g_ref.shape[3]
    step = pl.program_id(0)
    buf = step % 2
    unroll = 4

    def start_gather(slots, dst):
        def issue(i, c):
            for j in range(unroll):
                r = i * unroll + j
                for k in range(TOP_K):
                    sl = slots[0, 0, r * TOP_K + k]
                    pltpu.make_async_copy(yb_ref.at[pl.ds(sl, 1), :], g_ref.at[dst, k, pl.ds(r, 1), :],
                                          sem.at[dst, k]).start(priority=k)
            return c
        lax.fori_loop(0, tm // unroll, issue, 0)

    @pl.when(step == 0)
    def _():
        start_gather(slot_ref, 0)

    @pl.when(step + 1 < pl.num_programs(0))
    def _():
        start_gather(slotn_ref, 1 - buf)

    for k in range(TOP_K):
        pltpu.make_async_copy(yb_ref.at[pl.ds(0, tm), :], g_ref.at[buf, k], sem.at[buf, k]).wait()

    route = r_ref[...]
    w0 = route[:, TOP_K:TOP_K + 1]
    w1 = route[:, TOP_K + 1:TOP_K + 2]
    q = n // 2
    for hf in range(2):
        lo0, hi0 = _unpack_bf16_pairs(g_ref[buf, 0, :, hf * q:(hf + 1) * q])
        lo1, hi1 = _unpack_bf16_pairs(g_ref[buf, 1, :, hf * q:(hf + 1) * q])
        c0 = hf * n
        o_ref[:, c0:c0 + q] = x_ref[:, c0:c0 + q] + (lo0 * w0 + lo1 * w1)
        o_ref[:, c0 + q:c0 + n] = x_ref[:, c0 + q:c0 + n] + (hi0 * w0 + hi1 * w1)


def _combine(slots, route, x1, yb, tm):
    s, d = x1.shape
    n_steps = s // tm
    slot_blocks = slots.reshape(n_steps, 1, tm * TOP_K)
    return pl.pallas_call(
        _combine_kernel,
        grid=(n_steps,),
        in_specs=[pl.BlockSpec((1, 1, tm * TOP_K), lambda i: (i, 0, 0), memory_space=pltpu.SMEM),
                  pl.BlockSpec((1, 1, tm * TOP_K), lambda i: (jnp.minimum(i + 1, n_steps - 1), 0, 0),
                               memory_space=pltpu.SMEM),
                  pl.BlockSpec((tm, ROUTE_W), lambda i: (i, 0)),
                  pl.BlockSpec((tm, d), lambda i: (i, 0)),
                  pl.BlockSpec(memory_space=pl.ANY)],
        out_specs=pl.BlockSpec((tm, d), lambda i: (i, 0)),
        out_shape=jax.ShapeDtypeStruct((s, d), F32),
        scratch_shapes=[pltpu.VMEM((2, TOP_K, tm, d // 2), jnp.uint32), pltpu.SemaphoreType.DMA((2, TOP_K))],
        compiler_params=_cp(("arbitrary",)),
        name="moe_combine",
    )(slot_blocks, slot_blocks, route, x1, yb)


def _dispatch_plan(route, s):
    a = s * TOP_K
    flat_e = route[:, :TOP_K].astype(jnp.int32).reshape(a)
    chunk = 128
    onehot = (flat_e[:, None] == jnp.arange(N_EXPERTS, dtype=jnp.int32)[None, :]).astype(F32)
    oh = onehot.reshape(a // chunk, chunk, N_EXPERTS)
    strict_lower = jnp.tril(jnp.ones((chunk, chunk), F32), -1)
    within = jnp.einsum('ij,bjk->bik', strict_lower, oh, precision=lax.Precision.HIGHEST)
    totals = jnp.sum(oh, axis=1)
    before = jnp.cumsum(totals, axis=0) - totals
    rank = jnp.sum((within + before[:, None, :]) * oh, axis=2).reshape(a).astype(jnp.int32)
    counts = jnp.sum(totals, axis=0).astype(jnp.int32)
    padded = (counts + MOE_TB - 1) // MOE_TB * MOE_TB
    pad_end = jnp.cumsum(padded)
    pad_start = pad_end - padded
    dest = jnp.sum(onehot * pad_start.astype(F32)[None, :], axis=1).astype(jnp.int32) + rank
    n_blocks = a // MOE_TB + N_EXPERTS
    p_rows = n_blocks * MOE_TB
    buf_tok = jnp.zeros((p_rows,), jnp.int32).at[dest].set(jnp.arange(a, dtype=jnp.int32) // TOP_K)
    starts = jnp.arange(n_blocks, dtype=jnp.int32) * MOE_TB
    block_e = jnp.minimum(jnp.searchsorted(pad_end, starts, side='right'), N_EXPERTS - 1).astype(jnp.int32)
    valid = jnp.clip(counts[block_e] - (starts - pad_start[block_e]), 0, MOE_TB)
    valid = jnp.where(starts < pad_end[-1], valid, 0)
    nsub = ((valid + MOE_RU - 1) // MOE_RU).astype(jnp.int32)
    n_used = (pad_end[-1:] // MOE_TB).astype(jnp.int32)
    return block_e, nsub, n_used, buf_tok, dest.astype(jnp.int32)


def _rope_tables(positions):
    half = MLA_ROPE_DIM // 2
    inv_freq = ROPE_THETA ** (-jnp.arange(half, dtype=F32) / half)
    ang = positions.astype(F32)[:, None] * inv_freq[None, :]
    cos, sin = jnp.cos(ang), jnp.sin(ang)
    z = jnp.zeros_like(cos)
    c = jnp.concatenate([cos, cos, z, z], axis=-1)
    s1 = jnp.concatenate([-sin, z, z, z], axis=-1)
    s2 = jnp.concatenate([z, sin, z, z], axis=-1)
    return c, s1, s2


def kernel(x, mem, positions, rel_bias, mix_norm_g, w_in, diff_q_norm_g, diff_k_norm_g, diff_lambda_q1, diff_lambda_k1, diff_lambda_q2, diff_lambda_k2, diff_subln_g, mla_cq_norm_g, mla_ckv_norm_g, mla_w_uq, mla_w_ukv, mla_q_norm_g, mla_k_norm_g, mem_norm_g, mem_w_kv, mem_q_norm_g, mem_k_norm_g, w_o_diff, w_o_mla, w_o_mem, w_out, ffn_norm_g, w_route_group, b_route_group, w_route_expert, b_route_expert, w_exp_gate, w_exp_up, w_exp_down):
    b, s, d = x.shape
    assert b == 1 and s % (2 * ATT_T) == 0 and s % MM_TM == 0
    depth = mix_norm_g.shape[0]
    xs = x.reshape(s, d)
    pos = positions.reshape(s)
    rope_tabs = _rope_tables(pos)
    row = lambda v: v.reshape(1, -1).astype(F32)

    for l in range(depth):
        lam_init = 0.8 - 0.6 * math.exp(-0.3 * l)
        h = _rmsnorm_rows(xs, mix_norm_g[l], 256)
        w_in_t = jnp.transpose(w_in[l])
        in_proj = functools.partial(_matmul, h, w_in_t, w_rows_are_outputs=True, tm=MM_TM, tn=512, out_dtype=BF16)
        q_gain = jnp.tile(diff_q_norm_g[l] * (DIFF_HEAD_DIM ** -0.5 * LOG2E), DIFF_MAPS)
        dq_t = in_proj(first=OFF_DQ, n_out=DIFF_QK_WIDTH, mode="groupnorm", extra=q_gain, group=DIFF_HEAD_DIM,
                       transpose_out=True, name="diff_q_proj")
        dk = in_proj(first=OFF_DK, n_out=DIFF_QK_WIDTH, mode="groupnorm", extra=jnp.tile(diff_k_norm_g[l], DIFF_MAPS),
                     group=DIFF_HEAD_DIM, name="diff_k_proj")
        dv_t = in_proj(first=OFF_DV, n_out=DIFF_WIDTH, transpose_out=True, name="diff_v_proj")
        rest = in_proj(first=OFF_CQ, n_out=REST_WIDTH, name="rest_proj")
        gates = in_proj(first=OFF_GATES, n_out=3 * d, mode="sigmoid", name="gate_proj")

        bias_strips = _diff_bias_strips(rel_bias, ATT_T)
        lam_vecs = [row(diff_lambda_q1[l]), row(diff_lambda_k1[l]), row(diff_lambda_q2[l]), row(diff_lambda_k2[l])]
        o_diff = _diff_attention(dq_t, dk, dv_t, bias_strips, lam_vecs, diff_subln_g[l], lam_init)

        w_uq_heads = jnp.pad(
            mla_w_uq[l].reshape(MLA_Q_RANK, MLA_HEADS, MLA_QK_DIM),
            ((0, 0), (0, 0), (0, MLA_QK_PAD - MLA_QK_DIM))).reshape(MLA_Q_RANK, MLA_HEADS * MLA_QK_PAD).astype(BF16)
        qg_pad = jnp.pad(mla_q_norm_g[l] * (MLA_QK_DIM ** -0.5 * LOG2E),
                         (0, MLA_QK_PAD - MLA_QK_DIM)).reshape(1, -1).astype(F32)
        q_mla_t = _mla_q_prep(rest, row(mla_cq_norm_g[l]), w_uq_heads, qg_pad, rope_tabs, 512)
        kg = mla_k_norm_g[l]
        kg_nope = row(kg[:MLA_NOPE_DIM])
        kg_rope = jnp.pad(kg[MLA_NOPE_DIM:], (0, LANE - MLA_ROPE_DIM)).reshape(1, -1).astype(F32)
        k_mla, v_mla_t = _mla_kv_prep(rest, row(mla_ckv_norm_g[l]), mla_w_ukv[l], kg_nope, kg_rope, rope_tabs, MM_TM)
        o_mla = _mla_attention(q_mla_t, k_mla, v_mla_t)

        n_mem = mem.shape[1]
        mem_h = _rmsnorm_rows(mem.reshape(n_mem, d), mem_norm_g[l], n_mem)
        mem_proj = functools.partial(_matmul, mem_h, mem_w_kv[l], n_out=MEM_WIDTH, w_rows_are_outputs=False,
                                     tm=n_mem, tn=512, out_dtype=BF16)
        k_mem = mem_proj(first=0, mode="groupnorm", extra=jnp.tile(mem_k_norm_g[l], MEM_HEADS), group=MEM_HEAD_DIM,
                         name="mem_k_proj")
        v_mem = mem_proj(first=MEM_WIDTH, name="mem_v_proj")
        o_mem = _mem_attention(rest, k_mem, v_mem, row(mem_q_norm_g[l] * MEM_HEAD_DIM ** -0.5), 512)

        mixed = _mix(o_diff, o_mla, o_mem, w_o_diff[l], w_o_mla[l], w_o_mem[l], gates, MM_TM, 512)
        x1 = _matmul(mixed, w_out[l], first=0, n_out=d, w_rows_are_outputs=False, tm=MM_TM, tn=512, out_dtype=F32,
                     mode="residual", extra=xs, name="out_proj")

        w_r = jnp.pad(jnp.concatenate([w_route_group[l], w_route_expert[l]], axis=1),
                      ((0, 0), (0, ROUTE_W - N_GROUPS - N_EXPERTS))).astype(F32)
        w_r_hi = w_r.astype(BF16)
        w_r = jnp.stack([w_r_hi, (w_r - w_r_hi.astype(F32)).astype(BF16)])
        b_r = jnp.pad(jnp.concatenate([b_route_group[l], b_route_expert[l]]),
                      (0, ROUTE_W - N_GROUPS - N_EXPERTS)).reshape(1, -1).astype(F32)
        h2, route = _router(x1, row(ffn_norm_g[l]), w_r, b_r, 256)
        block_e, nsub, n_used, buf_tok, slots = _dispatch_plan(route, s)
        yb = _moe_experts(block_e, nsub, n_used, buf_tok, h2, w_exp_gate[l], w_exp_up[l], w_exp_down[l])
        xs = _combine(slots, route, x1, yb, 256)
    return xs.reshape(b, s, d)
```

```python
import functools
import math

import jax
import jax.numpy as jnp
from jax import lax
from jax.experimental import pallas as pl
from jax.experimental.pallas import tpu as pltpu

F32 = jnp.float32
BF16 = jnp.bfloat16

NORM_EPS = 1e-6
NEG_INF = -1e30
LOG2E = math.log2(math.e)

DIFF_HEADS = 6
DIFF_HEAD_DIM = 128
DIFF_V_DIM = 256
DIFF_MAPS = 12
DIFF_QK_WIDTH = 1536
DIFF_WIDTH = 1536
MLA_HEADS = 12
MLA_Q_RANK = 1536
MLA_KV_RANK = 512
MLA_NOPE_DIM = 128
MLA_ROPE_DIM = 64
MLA_QK_DIM = 192
MLA_QK_PAD = 256
MLA_V_DIM = 128
MLA_WIDTH = 1536
ROPE_THETA = 10000.0
MEM_HEADS = 4
MEM_HEAD_DIM = 256
MEM_WIDTH = 1024
REL_BUCKETS = 32
REL_MAX_DIST = 128
N_GROUPS = 8
EXPERTS_PER_GROUP = 8
N_EXPERTS = 64
TOP_K = 2
EXPERT_FF = 512

OFF_DQ = 0
OFF_DK = OFF_DQ + DIFF_QK_WIDTH
OFF_DV = OFF_DK + DIFF_QK_WIDTH
OFF_CQ = OFF_DV + DIFF_WIDTH
OFF_CKV = OFF_CQ + MLA_Q_RANK
OFF_KROPE = OFF_CKV + MLA_KV_RANK
OFF_MQ = OFF_KROPE + MLA_ROPE_DIM
OFF_GATES = OFF_MQ + MEM_WIDTH
REST_WIDTH = 3584
R_CQ = OFF_CQ - OFF_CQ
R_CKV = OFF_CKV - OFF_CQ
R_KROPE = OFF_KROPE - OFF_CQ
R_MQ = OFF_MQ - OFF_CQ

LANE = 128
VMEM_LIMIT = 52 * 1024 * 1024

MM_TM = 1024
MLA_HEAD_GROUP = 4
ATT_T = 512
MOE_TB = 512
MOE_RU = 128
MOE_PH = 4
ROUTE_W = 128
MEM_WIN_START = R_MQ // LANE * LANE
MEM_WINDOW = (R_MQ - MEM_WIN_START + MEM_WIDTH + LANE - 1) // LANE * LANE


def _cp(sem, vmem=VMEM_LIMIT):
    return pltpu.CompilerParams(dimension_semantics=sem, vmem_limit_bytes=vmem)


def _rmsnorm_kernel(x_ref, g_ref, o_ref):
    x = x_ref[...].astype(F32)
    ms = jnp.mean(x * x, axis=-1, keepdims=True)
    o_ref[...] = (x * lax.rsqrt(ms + NORM_EPS) * g_ref[...]).astype(o_ref.dtype)


def _rmsnorm_rows(x, g, tm, out_dtype=BF16):
    m, d = x.shape
    return pl.pallas_call(
        _rmsnorm_kernel,
        grid=(m // tm,),
        in_specs=[pl.BlockSpec((tm, d), lambda i: (i, 0)),
                  pl.BlockSpec((1, d), lambda i: (0, 0))],
        out_specs=pl.BlockSpec((tm, d), lambda i: (i, 0)),
        out_shape=jax.ShapeDtypeStruct((m, d), out_dtype),
        compiler_params=_cp(("parallel",)),
        name="rmsnorm_rows",
    )(x, g.reshape(1, d).astype(F32))


def _mm_kernel(*refs, mode, group, nt, transpose_out):
    a_ref, w_ref = refs[0], refs[1]
    extra = refs[2] if mode in ("groupnorm", "residual") else None
    o_ref, wb_ref = refs[-2], refs[-1]

    @pl.when(pl.program_id(1) == 0)
    def _():
        wb_ref[...] = w_ref[...].astype(BF16)

    if nt:
        contract_last = (((1,), (1,)), ((), ()))
        if transpose_out:
            acc = lax.dot_general(wb_ref[...], a_ref[...], contract_last, preferred_element_type=F32)
        else:
            acc = lax.dot_general(a_ref[...], wb_ref[...], contract_last, preferred_element_type=F32)
    else:
        acc = jnp.dot(a_ref[...], wb_ref[...], preferred_element_type=F32)
    if mode == "plain":
        o_ref[...] = acc.astype(o_ref.dtype)
    elif mode == "sigmoid":
        o_ref[...] = jax.nn.sigmoid(acc).astype(o_ref.dtype)
    elif mode == "residual":
        o_ref[...] = (extra[...] + acc).astype(o_ref.dtype)
    elif mode == "groupnorm":
        tn = acc.shape[0] if transpose_out else acc.shape[1]
        for c in range(tn // group):
            sl = slice(c * group, (c + 1) * group)
            if transpose_out:
                blk = acc[sl, :]
                ms = jnp.mean(blk * blk, axis=0, keepdims=True)
                o_ref[sl, :] = (blk * lax.rsqrt(ms + NORM_EPS) * extra[sl, :]).astype(o_ref.dtype)
            else:
                blk = acc[:, sl]
                ms = jnp.mean(blk * blk, axis=-1, keepdims=True)
                o_ref[:, sl] = (blk * lax.rsqrt(ms + NORM_EPS) * extra[:, sl]).astype(o_ref.dtype)


def _matmul(a, w, *, first, n_out, tm, tn, out_dtype, w_rows_are_outputs, mode="plain", extra=None, group=LANE,
            transpose_out=False, name="matmul"):
    m, k = a.shape
    assert m % tm == 0 and n_out % tn == 0 and w.dtype == F32
    if w_rows_are_outputs:
        assert w.shape[1] == k and first % 8 == 0
        w_spec = pl.BlockSpec((pl.Element(tn), pl.Element(k)),
                              lambda j, i: (pl.multiple_of(first + j * tn, 8), 0))
        wb_shape = (tn, k)
    else:
        assert w.shape[0] == k and first % tn == 0 and not transpose_out
        w_spec = pl.BlockSpec((k, tn), lambda j, i: (0, first // tn + j))
        wb_shape = (k, tn)
    in_specs = [pl.BlockSpec((tm, k), lambda j, i: (i, 0)), w_spec]
    args = [a, w]
    if mode == "groupnorm":
        gain = extra.reshape(-1, 1) if transpose_out else extra.reshape(1, -1)
        in_specs.append(pl.BlockSpec((tn, 1), lambda j, i: (j, 0)) if transpose_out
                        else pl.BlockSpec((1, tn), lambda j, i: (0, j)))
        args.append(gain.astype(F32))
    elif mode == "residual":
        assert not transpose_out
        in_specs.append(pl.BlockSpec((tm, tn), lambda j, i: (i, j)))
        args.append(extra)
    if transpose_out:
        out_spec = pl.BlockSpec((tn, tm), lambda j, i: (j, i))
        out_shape = jax.ShapeDtypeStruct((n_out, m), out_dtype)
    else:
        out_spec = pl.BlockSpec((tm, tn), lambda j, i: (i, j))
        out_shape = jax.ShapeDtypeStruct((m, n_out), out_dtype)
    return pl.pallas_call(
        functools.partial(_mm_kernel, mode=mode, group=group, nt=w_rows_are_outputs, transpose_out=transpose_out),
        grid=(n_out // tn, m // tm),
        in_specs=in_specs,
        out_specs=out_spec,
        out_shape=out_shape,
        scratch_shapes=[pltpu.VMEM(wb_shape, BF16)],
        compiler_params=_cp(("arbitrary", "arbitrary")),
        name=name,
    )(*args)


def _softmax_pv(idx, s, vt_blk, m_ref, l_ref, acc_ref, col0=0):
    cols = slice(col0, col0 + s.shape[1])
    m_prev = m_ref[idx, :, cols]
    m_new = jnp.maximum(m_prev, jnp.max(s, axis=0, keepdims=True))
    alpha = jnp.exp2(m_prev - m_new)
    p = jnp.exp2(s - m_new)
    l_ref[idx, :, cols] = alpha * l_ref[idx, :, cols] + jnp.sum(p, axis=0, keepdims=True)
    acc_ref[idx, :, cols] = (alpha * acc_ref[idx, :, cols]
                             + jnp.dot(vt_blk, p.astype(BF16), preferred_element_type=F32))
    m_ref[idx, :, cols] = m_new


def _block_offset(kb, t):
    return kb * t if isinstance(kb, int) else pl.multiple_of(kb * t, t)


def _init_stats(m_ref, l_ref, acc_ref):
    m_ref[...] = jnp.full(m_ref.shape, NEG_INF, F32)
    l_ref[...] = jnp.zeros(l_ref.shape, F32)
    acc_ref[...] = jnp.zeros(acc_ref.shape, F32)


def _diff_attn_kernel(lq1_ref, lk1_ref, lq2_ref, lk2_ref, q1_ref, q2_ref, k1_ref, k2_ref, vt_ref,
                      b1_ref, b2_ref, g_ref, o_ref, sa_ref, sb_ref, m_ref, l_ref, acc_ref, *, lam_init):
    _, tk, tq = sa_ref.shape
    i = pl.program_id(1)
    qts = (q1_ref[...], q2_ref[...])
    ks = (k1_ref, k2_ref)
    bs = (b1_ref, b2_ref)
    _init_stats(m_ref, l_ref, acc_ref)

    def scores(kb, dst):
        off = _block_offset(kb, tk)
        for mp in range(2):
            dst[mp] = jnp.dot(ks[mp][pl.ds(off, tk), :], qts[mp], preferred_element_type=F32)

    def scores_right(kb, dst):
        off = _block_offset(kb, tk)
        for mp in range(2):
            dst[mp, :, tk:] = jnp.dot(ks[mp][pl.ds(off, tk), :], qts[mp][:, tk:], preferred_element_type=F32)

    def consume(kb, src, strip_off, col0=0):
        off = _block_offset(kb, tk)
        vt_blk = vt_ref[:, pl.ds(off, tk)]
        for mp in range(2):
            s = src[mp, :, col0:]
            if strip_off is not None:
                s = s + bs[mp][0, :, strip_off + col0:strip_off + tq]
            _softmax_pv(mp, s, vt_blk, m_ref, l_ref, acc_ref, col0)

    scores(0, sa_ref)

    def far_pair(j, c):
        kb = 2 * j
        scores(kb + 1, sb_ref)
        consume(kb, sa_ref, None)
        scores(kb + 2, sa_ref)
        consume(kb + 1, sb_ref, None)
        return c

    lax.fori_loop(0, jnp.maximum(i - 1, 0), far_pair, 0)

    @pl.when(i >= 1)
    def _():
        scores(2 * i - 1, sb_ref)
        consume(2 * i - 2, sa_ref, None)
        scores(2 * i, sa_ref)
        consume(2 * i - 1, sb_ref, 2 * tk)

    scores_right(2 * i + 1, sb_ref)
    consume(2 * i, sa_ref, tk)
    consume(2 * i + 1, sb_ref, 0, col0=tk)

    lam = (jnp.exp(jnp.sum(lq1_ref[...] * lk1_ref[...], axis=-1, keepdims=True))
           - jnp.exp(jnp.sum(lq2_ref[...] * lk2_ref[...], axis=-1, keepdims=True)) + lam_init)
    o = acc_ref[0] / l_ref[0] - lam * (acc_ref[1] / l_ref[1])
    ms = jnp.mean(o * o, axis=0, keepdims=True)
    o = (o * lax.rsqrt(ms + NORM_EPS) * g_ref[...]) * (1.0 - lam_init)
    o_ref[...] = o.T.astype(o_ref.dtype)


def _diff_attention(q_t, k, v_t, bias_strips, lam_vecs, subln_g, lam_init):
    s = k.shape[0]
    tk = ATT_T
    tq = 2 * tk
    assert s % tq == 0 and bias_strips.shape[1:] == (tk, 4 * tk)
    hd, vd = DIFF_HEAD_DIM, DIFF_V_DIM
    vec = pl.BlockSpec((1, hd), lambda h, i: (0, 0))
    in_specs = [vec, vec, vec, vec,
                pl.BlockSpec((hd, tq), lambda h, i: (h, i)),
                pl.BlockSpec((hd, tq), lambda h, i: (DIFF_HEADS + h, i)),
                pl.BlockSpec((s, hd), lambda h, i: (0, h)),
                pl.BlockSpec((s, hd), lambda h, i: (0, DIFF_HEADS + h)),
                pl.BlockSpec((vd, s), lambda h, i: (h, 0)),
                pl.BlockSpec((1, tk, 4 * tk), lambda h, i: (h, 0, 0), pipeline_mode=pl.Buffered(1)),
                pl.BlockSpec((1, tk, 4 * tk), lambda h, i: (DIFF_HEADS + h, 0, 0), pipeline_mode=pl.Buffered(1)),
                pl.BlockSpec((vd, 1), lambda h, i: (0, 0))]
    return pl.pallas_call(
        functools.partial(_diff_attn_kernel, lam_init=lam_init),
        grid=(DIFF_HEADS, s // tq),
        in_specs=in_specs,
        out_specs=pl.BlockSpec((tq, vd), lambda h, i: (i, h)),
        out_shape=jax.ShapeDtypeStruct((s, DIFF_WIDTH), BF16),
        scratch_shapes=[pltpu.VMEM((2, tk, tq), F32), pltpu.VMEM((2, tk, tq), F32),
                        pltpu.VMEM((2, 1, tq), F32), pltpu.VMEM((2, 1, tq), F32),
                        pltpu.VMEM((2, vd, tq), F32)],
        compiler_params=_cp(("arbitrary", "arbitrary")),
        name="diff_attention",
    )(*lam_vecs, q_t, q_t, k, k, v_t, bias_strips, bias_strips, subln_g.reshape(vd, 1).astype(F32))


def _t5_bucket(dist):
    n = jnp.maximum(dist, 0)
    max_exact = REL_BUCKETS // 2
    nf = jnp.maximum(n, 1).astype(F32)
    large = max_exact + (jnp.log(nf / max_exact) / math.log(REL_MAX_DIST / max_exact)
                         * (REL_BUCKETS - max_exact)).astype(jnp.int32)
    large = jnp.minimum(large, REL_BUCKETS - 1)
    return jnp.where(n < max_exact, n, large)


def _diff_bias_strips(rel_bias, tk):
    assert tk >= REL_MAX_DIST
    width = 4 * tk
    n = width + tk
    table = rel_bias.astype(F32)
    table = (table - table[REL_BUCKETS - 1:REL_BUCKETS]) * LOG2E
    kk = jnp.arange(n, dtype=jnp.int32)
    dist = kk - tk
    onehot = (_t5_bucket(dist)[:, None] == jnp.arange(REL_BUCKETS, dtype=jnp.int32)[None, :]).astype(F32)
    g = jnp.einsum('kb,bm->mk', onehot, table, precision=lax.Precision.HIGHEST)
    w = jnp.where(((dist >= 0) & (kk < width))[None, :], g, NEG_INF)
    maps = w.shape[0]

    def strip_kernel(w_ref, o_ref):
        rows = jnp.broadcast_to(w_ref[0], (tk, n))
        o_ref[0] = pltpu.roll(rows, 0, 1, stride=1, stride_axis=0)[:, :width]

    return pl.pallas_call(
        strip_kernel,
        grid=(maps,),
        in_specs=[pl.BlockSpec((1, 1, n), lambda m: (m, 0, 0))],
        out_specs=pl.BlockSpec((1, tk, width), lambda m: (m, 0, 0)),
        out_shape=jax.ShapeDtypeStruct((maps, tk, width), F32),
        compiler_params=_cp(("parallel",)),
        name="bias_strips",
    )(w.reshape(maps, 1, n))


def _rope_apply(tv, c_ref, s1_ref, s2_ref):
    return (tv * c_ref[...] + pltpu.roll(tv, 96, 1) * s1_ref[...] + pltpu.roll(tv, 32, 1) * s2_ref[...])


def _mla_q_kernel(cq_ref, g_ref, w_ref, qg_ref, c_ref, s1_ref, s2_ref, o_ref, xg_ref):
    h = pl.program_id(1)

    @pl.when(h == 0)
    def _():
        c = cq_ref[...].astype(F32)
        r = lax.rsqrt(jnp.mean(c * c, axis=-1, keepdims=True) + NORM_EPS)
        xg_ref[...] = (c * r * g_ref[...]).astype(BF16)

    ug = jnp.dot(xg_ref[...], w_ref[...], preferred_element_type=F32)
    for hh in range(MLA_HEAD_GROUP):
        u = ug[:, hh * MLA_QK_PAD:(hh + 1) * MLA_QK_PAD]
        ms = jnp.sum(u * u, axis=-1, keepdims=True) * (1.0 / MLA_QK_DIM)
        qn = u * lax.rsqrt(ms + NORM_EPS) * qg_ref[...]
        o_ref[hh, :MLA_NOPE_DIM, :] = qn[:, :MLA_NOPE_DIM].T.astype(o_ref.dtype)
        o_ref[hh, MLA_NOPE_DIM:, :] = _rope_apply(qn[:, MLA_NOPE_DIM:], c_ref, s1_ref, s2_ref).T.astype(o_ref.dtype)


def _mla_q_prep(rest, cq_g, w_uq_pad, qg_pad, rope_tabs, tm):
    s = rest.shape[0]
    hg = MLA_HEAD_GROUP
    tab = pl.BlockSpec((tm, LANE), lambda i, h: (i, 0))
    return pl.pallas_call(
        _mla_q_kernel,
        grid=(s // tm, MLA_HEADS // hg),
        in_specs=[pl.BlockSpec((tm, MLA_Q_RANK), lambda i, h: (i, R_CQ // MLA_Q_RANK)),
                  pl.BlockSpec((1, MLA_Q_RANK), lambda i, h: (0, 0)),
                  pl.BlockSpec((MLA_Q_RANK, hg * MLA_QK_PAD), lambda i, h: (0, h)),
                  pl.BlockSpec((1, MLA_QK_PAD), lambda i, h: (0, 0)),
                  tab, tab, tab],
        out_specs=pl.BlockSpec((hg, MLA_QK_PAD, tm), lambda i, h: (h, 0, i)),
        out_shape=jax.ShapeDtypeStruct((MLA_HEADS, MLA_QK_PAD, s), BF16),
        scratch_shapes=[pltpu.VMEM((tm, MLA_Q_RANK), BF16)],
        compiler_params=_cp(("arbitrary", "arbitrary")),
        name="mla_q_prep",
    )(rest, cq_g, w_uq_pad, qg_pad, *rope_tabs)


def _mla_kv_kernel(ckv_ref, kr_ref, g_ref, w_ref, kgn_ref, kgr_ref, c_ref, s1_ref, s2_ref,
                   k_ref, vt_ref, xg_ref):
    h = pl.program_id(1)

    @pl.when(h == 0)
    def _():
        c = ckv_ref[...].astype(F32)
        r = lax.rsqrt(jnp.mean(c * c, axis=-1, keepdims=True) + NORM_EPS)
        xg_ref[...] = (c * r * g_ref[...]).astype(BF16)

    hw = MLA_NOPE_DIM + MLA_V_DIM
    kvg = jnp.dot(xg_ref[...], w_ref[...].astype(BF16), preferred_element_type=F32)
    lane = lax.broadcasted_iota(jnp.int32, kr_ref.shape, 1)
    kr = jnp.where(lane < MLA_ROPE_DIM, kr_ref[...].astype(F32), 0.0)
    kr_ss = jnp.sum(kr * kr, axis=-1, keepdims=True)
    for hh in range(MLA_HEAD_GROUP):
        kn = kvg[:, hh * hw:hh * hw + MLA_NOPE_DIM]
        ms = (jnp.sum(kn * kn, axis=-1, keepdims=True) + kr_ss) * (1.0 / MLA_QK_DIM)
        rs = lax.rsqrt(ms + NORM_EPS)
        k_ref[hh, :, :MLA_NOPE_DIM] = (kn * rs * kgn_ref[...]).astype(k_ref.dtype)
        k_ref[hh, :, MLA_NOPE_DIM:] = _rope_apply(kr * rs * kgr_ref[...], c_ref, s1_ref, s2_ref).astype(k_ref.dtype)
        vt_ref[hh] = kvg[:, hh * hw + MLA_NOPE_DIM:(hh + 1) * hw].T.astype(vt_ref.dtype)


def _mla_kv_prep(rest, ckv_g, w_ukv, kg_nope, kg_rope_pad, rope_tabs, tm):
    s = rest.shape[0]
    hg = MLA_HEAD_GROUP
    tab = pl.BlockSpec((tm, LANE), lambda i, h: (i, 0))
    hw = MLA_NOPE_DIM + MLA_V_DIM
    return pl.pallas_call(
        _mla_kv_kernel,
        grid=(s // tm, MLA_HEADS // hg),
        in_specs=[pl.BlockSpec((tm, MLA_KV_RANK), lambda i, h: (i, R_CKV // MLA_KV_RANK)),
                  pl.BlockSpec((tm, LANE), lambda i, h: (i, R_KROPE // LANE)),
                  pl.BlockSpec((1, MLA_KV_RANK), lambda i, h: (0, 0)),
                  pl.BlockSpec((MLA_KV_RANK, hg * hw), lambda i, h: (0, h)),
                  pl.BlockSpec((1, LANE), lambda i, h: (0, 0)),
                  pl.BlockSpec((1, LANE), lambda i, h: (0, 0)),
                  tab, tab, tab],
        out_specs=[pl.BlockSpec((hg, tm, MLA_QK_PAD), lambda i, h: (h, i, 0)),
                   pl.BlockSpec((hg, MLA_V_DIM, tm), lambda i, h: (h, 0, i))],
        out_shape=[jax.ShapeDtypeStruct((MLA_HEADS, s, MLA_QK_PAD), BF16),
                   jax.ShapeDtypeStruct((MLA_HEADS, MLA_V_DIM, s), BF16)],
        scratch_shapes=[pltpu.VMEM((tm, MLA_KV_RANK), BF16)],
        compiler_params=_cp(("arbitrary", "arbitrary")),
        name="mla_kv_prep",
    )(rest, rest, ckv_g, w_ukv, kg_nope, kg_rope_pad, *rope_tabs)


def _mla_attn_kernel(qt_ref, k_ref, vt_ref, o_ref, sa_ref, sb_ref, m_ref, l_ref, acc_ref):
    tk, tq = sa_ref.shape
    i = pl.program_id(1)
    qt = qt_ref[0]
    _init_stats(m_ref, l_ref, acc_ref)

    def scores(kb, dst):
        off = _block_offset(kb, tk)
        dst[...] = jnp.dot(k_ref[0, pl.ds(off, tk), :], qt, preferred_element_type=F32)

    def scores_right(kb, dst):
        off = _block_offset(kb, tk)
        dst[:, tk:] = jnp.dot(k_ref[0, pl.ds(off, tk), :], qt[:, tk:], preferred_element_type=F32)

    def consume(kb, src, diag, col0=0):
        off = _block_offset(kb, tk)
        s = src[:, col0:]
        if diag:
            krow = lax.broadcasted_iota(jnp.int32, s.shape, 0)
            qcol = lax.broadcasted_iota(jnp.int32, s.shape, 1)
            s = jnp.where(krow <= qcol, s, NEG_INF)
        _softmax_pv(0, s, vt_ref[0, :, pl.ds(off, tk)], m_ref, l_ref, acc_ref, col0)

    scores(0, sa_ref)

    def far_pair(j, c):
        kb = 2 * j
        scores(kb + 1, sb_ref)
        consume(kb, sa_ref, False)
        scores(kb + 2, sa_ref)
        consume(kb + 1, sb_ref, False)
        return c

    lax.fori_loop(0, i, far_pair, 0)
    scores_right(2 * i + 1, sb_ref)
    consume(2 * i, sa_ref, True)
    consume(2 * i + 1, sb_ref, True, col0=tk)
    o_ref[...] = (acc_ref[0] / l_ref[0]).T.astype(o_ref.dtype)


def _mla_attention(q_t, k, v_t):
    s = k.shape[1]
    tk = ATT_T
    tq = 2 * tk
    assert s % tq == 0
    return pl.pallas_call(
        _mla_attn_kernel,
        grid=(MLA_HEADS, s // tq),
        in_specs=[pl.BlockSpec((1, MLA_QK_PAD, tq), lambda h, i: (h, 0, i)),
                  pl.BlockSpec((1, s, MLA_QK_PAD), lambda h, i: (h, 0, 0)),
                  pl.BlockSpec((1, MLA_V_DIM, s), lambda h, i: (h, 0, 0))],
        out_specs=pl.BlockSpec((tq, MLA_V_DIM), lambda h, i: (i, h)),
        out_shape=jax.ShapeDtypeStruct((s, MLA_WIDTH), BF16),
        scratch_shapes=[pltpu.VMEM((tk, tq), F32), pltpu.VMEM((tk, tq), F32),
                        pltpu.VMEM((1, 1, tq), F32), pltpu.VMEM((1, 1, tq), F32),
                        pltpu.VMEM((1, MLA_V_DIM, tq), F32)],
        compiler_params=_cp(("arbitrary", "arbitrary")),
        name="mla_attention",
    )(q_t, k, v_t)


def _qk_nt(q, k_blk):
    return lax.dot_general(q, k_blk, (((1,), (1,)), ((), ())), preferred_element_type=F32)


def _mem_attn_kernel(q_ref, k_ref, v_ref, qg_ref, o_ref):
    shift = R_MQ - MEM_WIN_START
    qall = q_ref[...].astype(F32)[:, shift:shift + MEM_WIDTH]
    for h in range(MEM_HEADS):
        lo = h * MEM_HEAD_DIM
        qh = qall[:, lo:lo + MEM_HEAD_DIM]
        ms = jnp.mean(qh * qh, axis=-1, keepdims=True)
        qn = (qh * lax.rsqrt(ms + NORM_EPS) * qg_ref[...]).astype(BF16)
        s = _qk_nt(qn, k_ref[:, lo:lo + MEM_HEAD_DIM])
        p = jnp.exp(s - jnp.max(s, axis=-1, keepdims=True))
        l = jnp.sum(p, axis=-1, keepdims=True)
        o = jnp.dot(p.astype(BF16), v_ref[:, lo:lo + MEM_HEAD_DIM], preferred_element_type=F32)
        o_ref[:, lo:lo + MEM_HEAD_DIM] = (o / l).astype(o_ref.dtype)


def _mem_attention(rest, k_mem, v_mem, qg_scaled, tm):
    s = rest.shape[0]
    n_mem = k_mem.shape[0]
    assert 0 <= R_MQ - MEM_WIN_START and R_MQ - MEM_WIN_START + MEM_WIDTH <= MEM_WINDOW
    assert MEM_WIN_START % LANE == 0 and MEM_WIN_START + MEM_WINDOW <= rest.shape[1]
    return pl.pallas_call(
        _mem_attn_kernel,
        grid=(s // tm,),
        in_specs=[pl.BlockSpec((pl.Element(tm), pl.Element(MEM_WINDOW)),
                               lambda i: (pl.multiple_of(i * tm, tm), MEM_WIN_START)),
                  pl.BlockSpec((n_mem, MEM_WIDTH), lambda i: (0, 0)),
                  pl.BlockSpec((n_mem, MEM_WIDTH), lambda i: (0, 0)),
                  pl.BlockSpec((1, MEM_HEAD_DIM), lambda i: (0, 0))],
        out_specs=pl.BlockSpec((tm, MEM_WIDTH), lambda i: (i, 0)),
        out_shape=jax.ShapeDtypeStruct((s, MEM_WIDTH), BF16),
        compiler_params=_cp(("parallel",)),
        name="mem_attention",
    )(rest, k_mem, v_mem, qg_scaled)


def _mix_kernel(od_ref, om_ref, oc_ref, wd_ref, wm_ref, wc_ref, g0_ref, g1_ref, g2_ref, o_ref,
                wdb_ref, wmb_ref, wcb_ref):
    @pl.when(pl.program_id(1) == 0)
    def _():
        wdb_ref[...] = wd_ref[...].astype(BF16)
        wmb_ref[...] = wm_ref[...].astype(BF16)
        wcb_ref[...] = wc_ref[...].astype(BF16)

    yd = jnp.dot(od_ref[...], wdb_ref[...], preferred_element_type=F32)
    ym = jnp.dot(om_ref[...], wmb_ref[...], preferred_element_type=F32)
    yc = jnp.dot(oc_ref[...], wcb_ref[...], preferred_element_type=F32)
    mixed = (g0_ref[...].astype(F32) * yd + g1_ref[...].astype(F32) * ym) + g2_ref[...].astype(F32) * yc
    o_ref[...] = mixed.astype(o_ref.dtype)


def _mix(o_diff, o_mla, o_mem, w_d, w_m, w_c, gates, tm, tn):
    s = o_diff.shape[0]
    d = w_d.shape[1]
    nj = d // tn
    return pl.pallas_call(
        _mix_kernel,
        grid=(nj, s // tm),
        in_specs=[pl.BlockSpec((tm, o_diff.shape[1]), lambda j, i: (i, 0)),
                  pl.BlockSpec((tm, o_mla.shape[1]), lambda j, i: (i, 0)),
                  pl.BlockSpec((tm, o_mem.shape[1]), lambda j, i: (i, 0)),
                  pl.BlockSpec((w_d.shape[0], tn), lambda j, i: (0, j)),
                  pl.BlockSpec((w_m.shape[0], tn), lambda j, i: (0, j)),
                  pl.BlockSpec((w_c.shape[0], tn), lambda j, i: (0, j)),
                  pl.BlockSpec((tm, tn), lambda j, i: (i, j)),
                  pl.BlockSpec((tm, tn), lambda j, i: (i, nj + j)),
                  pl.BlockSpec((tm, tn), lambda j, i: (i, 2 * nj + j))],
        out_specs=pl.BlockSpec((tm, tn), lambda j, i: (i, j)),
        out_shape=jax.ShapeDtypeStruct((s, d), BF16),
        scratch_shapes=[pltpu.VMEM((w_d.shape[0], tn), BF16), pltpu.VMEM((w_m.shape[0], tn), BF16),
                        pltpu.VMEM((w_c.shape[0], tn), BF16)],
        compiler_params=_cp(("arbitrary", "arbitrary")),
        name="gated_mix",
    )(o_diff, o_mla, o_mem, w_d, w_m, w_c, gates, gates, gates)


def _router_kernel(x_ref, g_ref, w_ref, b_ref, h_ref, r_ref):
    x = x_ref[...]
    ms = jnp.mean(x * x, axis=-1, keepdims=True)
    h = x * lax.rsqrt(ms + NORM_EPS) * g_ref[...]
    h_ref[...] = _pack_bf16_pairs(h)
    h_hi = h.astype(BF16)
    h_lo = (h - h_hi.astype(F32)).astype(BF16)
    logits = (jnp.dot(h_hi, w_ref[0], preferred_element_type=F32)
              + (jnp.dot(h_lo, w_ref[0], preferred_element_type=F32)
                 + jnp.dot(h_hi, w_ref[1], preferred_element_type=F32))) + b_ref[...]
    lane = lax.broadcasted_iota(jnp.int32, logits.shape, 1)
    lane_f = lane.astype(F32)
    big = float(4 * ROUTE_W)
    lg = jnp.where(lane < N_GROUPS, logits, -jnp.inf)
    gmax = jnp.max(lg, axis=-1, keepdims=True)
    gidx = jnp.min(jnp.where(lg == gmax, lane_f, big), axis=-1, keepdims=True)
    pg_top = 1.0 / jnp.sum(jnp.exp(lg - gmax), axis=-1, keepdims=True)
    e_lane = lane - N_GROUPS
    lane_group = jnp.right_shift(e_lane, 3).astype(F32)
    in_group = (e_lane >= 0) & (e_lane < N_EXPERTS) & (lane_group == gidx)
    le = jnp.where(in_group, logits, -jnp.inf)
    e1 = jnp.max(le, axis=-1, keepdims=True)
    i1 = jnp.min(jnp.where(le == e1, lane_f, big), axis=-1, keepdims=True)
    le2 = jnp.where(lane_f == i1, -jnp.inf, le)
    e2 = jnp.max(le2, axis=-1, keepdims=True)
    i2 = jnp.min(jnp.where(le2 == e2, lane_f, big), axis=-1, keepdims=True)
    w2 = jnp.exp(e2 - e1)
    inv = 1.0 / (1.0 + w2)
    gate1 = pg_top * inv
    gate2 = pg_top * (w2 * inv)
    out = jnp.where(lane == 0, i1 - N_GROUPS,
                    jnp.where(lane == 1, i2 - N_GROUPS,
                              jnp.where(lane == 2, gate1, jnp.where(lane == 3, gate2, 0.0))))
    r_ref[...] = out


def _router(x1, g, w_r, b_r, tm):
    s, d = x1.shape
    return pl.pallas_call(
        _router_kernel,
        grid=(s // tm,),
        in_specs=[pl.BlockSpec((tm, d), lambda i: (i, 0)),
                  pl.BlockSpec((1, d), lambda i: (0, 0)),
                  pl.BlockSpec((2, d, ROUTE_W), lambda i: (0, 0, 0)),
                  pl.BlockSpec((1, ROUTE_W), lambda i: (0, 0))],
        out_specs=[pl.BlockSpec((tm, d // 2), lambda i: (i, 0)),
                   pl.BlockSpec((tm, ROUTE_W), lambda i: (i, 0))],
        out_shape=[jax.ShapeDtypeStruct((s, d // 2), jnp.uint32),
                   jax.ShapeDtypeStruct((s, ROUTE_W), F32)],
        compiler_params=_cp(("parallel",)),
        name="ffn_norm_router",
    )(x1, g, w_r, b_r)


def _pack_bf16_pairs(v):
    n = v.shape[1] // 2
    bits = lax.bitcast_convert_type(v.astype(BF16).astype(F32), jnp.uint32)
    return jnp.right_shift(bits[:, :n], jnp.uint32(16)) | bits[:, n:]


def _unpack_bf16_pairs(words):
    lo = lax.bitcast_convert_type(jnp.left_shift(words, jnp.uint32(16)), F32)
    hi = lax.bitcast_convert_type(words & jnp.uint32(0xFFFF0000), F32)
    return lo, hi


def _moe_kernel(be_ref, nr_ref, nu_ref, tok_ref, tokn_ref, h_ref, wg_ref, wu_ref, wd_ref, o_ref,
                xg_ref, xb_ref, gp_ref, a_ref, wgb_ref, wub_ref, wdb_ref, sem):
    b = pl.program_id(0)
    c = pl.program_id(1)
    nb = pl.num_programs(0)
    nr = nr_ref[b]

    def row_copy(src_row, slot, r):
        return pltpu.make_async_copy(h_ref.at[pl.ds(src_row, 1), :], xg_ref.at[slot, pl.ds(r, 1), :], sem.at[slot])

    def start_gather(tok, n_rows, slot):
        unroll = 8

        def issue(i, carry):
            for j in range(unroll):
                r = i * unroll + j
                row_copy(tok[0, 0, r], slot, r).start(priority=1)
            return carry
        lax.fori_loop(0, n_rows // unroll, issue, 0)

    def for_row_count(fn):
        for units in range(1, MOE_TB // MOE_RU + 1):
            @pl.when(nr == units)
            def _():
                fn(units * MOE_RU)

    @pl.when(c == 0)
    def _():
        slot = b % 2

        @pl.when(b == 0)
        def _():
            start_gather(tok_ref, nr * MOE_RU, 0)

        def drain(u, carry):
            pltpu.make_async_copy(h_ref.at[pl.ds(0, MOE_RU), :], xg_ref.at[slot, pl.ds(0, MOE_RU), :],
                                  sem.at[slot]).wait()
            return carry
        lax.fori_loop(0, nr, drain, 0)

        @pl.when(b + 1 < nb)
        def _():
            start_gather(tokn_ref, nr_ref[jnp.minimum(b + 1, nb - 1)] * MOE_RU, 1 - slot)

        def unpack(u, carry):
            r0 = pl.multiple_of(u * MOE_RU, MOE_RU)
            lo, hi = _unpack_bf16_pairs(xg_ref[slot, pl.ds(r0, MOE_RU), :])
            xb_ref[0, pl.ds(r0, MOE_RU), :] = lo.astype(BF16)
            xb_ref[1, pl.ds(r0, MOE_RU), :] = hi.astype(BF16)
            return carry
        lax.fori_loop(0, nr, unpack, 0)

    @pl.when((c < 2) & (nr > 0))
    def _():
        wgb_ref[...] = wg_ref[0].astype(BF16)
        wub_ref[...] = wu_ref[0].astype(BF16)

        def gate_up(m):
            x = xb_ref[c, :m, :]
            g = jnp.dot(x, wgb_ref[...], preferred_element_type=F32)
            u = jnp.dot(x, wub_ref[...], preferred_element_type=F32)

            @pl.when(c == 0)
            def _():
                gp_ref[0, :m, :] = g
                gp_ref[1, :m, :] = u

            @pl.when(c == 1)
            def _():
                gs = gp_ref[0, :m, :] + g
                a_ref[:m, :] = ((gs * jax.nn.sigmoid(gs)) * (gp_ref[1, :m, :] + u)).astype(BF16)

        for_row_count(gate_up)

    @pl.when((c >= 2) & (nr > 0))
    def _():
        wdb_ref[...] = wd_ref[0].astype(BF16)

        def down(m):
            yv = jnp.dot(a_ref[:m, :], wdb_ref[...], preferred_element_type=F32)
            o_ref[:m, :] = _pack_bf16_pairs(yv)
            if m < MOE_TB:
                o_ref[m:, :] = jnp.zeros((MOE_TB - m, o_ref.shape[1]), jnp.uint32)

        for_row_count(down)

    @pl.when((c >= 2) & (nr == 0))
    def _():
        o_ref[...] = jnp.zeros(o_ref.shape, jnp.uint32)


def _moe_experts(block_e, nsub, n_used, buf_tok, h2p, w_gate, w_up, w_down):
    n_blocks = block_e.shape[0]
    half = h2p.shape[1]
    d = 2 * half
    ff = w_gate.shape[2]
    dh = d // 2
    tok = buf_tok.reshape(n_blocks, 1, MOE_TB)

    def gate_idx(b, c, be, nr, nu):
        bb = jnp.minimum(b, nu[0] - 1)
        own = (b < nu[0]) & (c < 2)
        e = jnp.where(own, be[bb], be[jnp.minimum(bb + 1, nu[0] - 1)])
        return e, jnp.where(own, c, 0), 0

    def up_idx(b, c, be, nr, nu):
        live = b < nu[0]
        return be[jnp.minimum(b, nu[0] - 1)], jnp.where(live, jnp.minimum(c, 1), 1), 0

    def down_idx(b, c, be, nr, nu):
        bb = jnp.minimum(b, nu[0] - 1)
        cc = jnp.where(b < nu[0], c, MOE_PH - 1)
        e = jnp.where(cc >= 1, be[bb], be[jnp.maximum(bb - 1, 0)])
        return e, 0, jnp.where((cc == 1) | (cc == 2), 0, 1)

    def out_idx(b, c, be, nr, nu):
        ob = jnp.where(c >= 2, b, jnp.maximum(b - 1, 0))
        oc = jnp.where(c >= 2, c - 2, jnp.where(b > 0, 1, 0))
        return ob, oc

    grid_spec = pltpu.PrefetchScalarGridSpec(
        num_scalar_prefetch=3,
        grid=(n_blocks, MOE_PH),
        in_specs=[pl.BlockSpec((1, 1, MOE_TB), lambda b, c, be, nr, nu: (b, 0, 0), memory_space=pltpu.SMEM),
                  pl.BlockSpec((1, 1, MOE_TB), lambda b, c, be, nr, nu: (jnp.minimum(b + 1, n_blocks - 1), 0, 0),
                               memory_space=pltpu.SMEM),
                  pl.BlockSpec(memory_space=pl.ANY),
                  pl.BlockSpec((1, half, ff), gate_idx),
                  pl.BlockSpec((1, half, ff), up_idx),
                  pl.BlockSpec((1, ff, dh), down_idx)],
        out_specs=pl.BlockSpec((MOE_TB, dh // 2), out_idx),
        scratch_shapes=[pltpu.VMEM((2, MOE_TB, half), jnp.uint32), pltpu.VMEM((2, MOE_TB, half), BF16),
                        pltpu.VMEM((2, MOE_TB, ff), F32), pltpu.VMEM((MOE_TB, ff), BF16),
                        pltpu.VMEM((half, ff), BF16), pltpu.VMEM((half, ff), BF16), pltpu.VMEM((ff, dh), BF16),
                        pltpu.SemaphoreType.DMA((2,))],
    )
    return pl.pallas_call(
        _moe_kernel,
        grid_spec=grid_spec,
        out_shape=jax.ShapeDtypeStruct((n_blocks * MOE_TB, half), jnp.uint32),
        compiler_params=_cp(("arbitrary", "arbitrary")),
        name="moe_experts",
    )(block_e, nsub, n_used, tok, tok, h2p, w_gate, w_up, w_down)


def _combine_kernel(slot_ref, slotn_ref, r_ref, x_ref, yb_ref, o_ref, g_ref, sem):
    tm = x_ref.shape[0]
    n = g_ref.shape[3]
    step = pl.program_id(0)
    buf = step % 2
    unroll = 4

    def start_gather(slots, dst):
        def issue(i, c):
            for j in range(unroll):
                r = i * unroll + j
                for k in range(TOP_K):
                    sl = slots[0, 0, r * TOP_K + k]
                    pltpu.make_async_copy(yb_ref.at[pl.ds(sl, 1), :], g_ref.at[dst, k, pl.ds(r, 1), :],
                                          sem.at[dst, k]).start(priority=k)
            return c
        lax.fori_loop(0, tm // unroll, issue, 0)

    @pl.when(step == 0)
    def _():
        start_gather(slot_ref, 0)

    @pl.when(step + 1 < pl.num_programs(0))
    def _():
        start_gather(slotn_ref, 1 - buf)

    for k in range(TOP_K):
        pltpu.make_async_copy(yb_ref.at[pl.ds(0, tm), :], g_ref.at[buf, k], sem.at[buf, k]).wait()

    route = r_ref[...]
    w0 = route[:, TOP_K:TOP_K + 1]
    w1 = route[:, TOP_K + 1:TOP_K + 2]
    q = n // 2
    for hf in range(2):
        lo0, hi0 = _unpack_bf16_pairs(g_ref[buf, 0, :, hf * q:(hf + 1) * q])
        lo1, hi1 = _unpack_bf16_pairs(g_ref[buf, 1, :, hf * q:(hf + 1) * q])
        c0 = hf * n
        o_ref[:, c0:c0 + q] = x_ref[:, c0:c0 + q] + (lo0 * w0 + lo1 * w1)
        o_ref[:, c0 + q:c0 + n] = x_ref[:, c0 + q:c0 + n] + (hi0 * w0 + hi1 * w1)


def _combine(slots, route, x1, yb, tm):
    s, d = x1.shape
    n_steps = s // tm
    slot_blocks = slots.reshape(n_steps, 1, tm * TOP_K)
    return pl.pallas_call(
        _combine_kernel,
        grid=(n_steps,),
        in_specs=[pl.BlockSpec((1, 1, tm * TOP_K), lambda i: (i, 0, 0), memory_space=pltpu.SMEM),
                  pl.BlockSpec((1, 1, tm * TOP_K), lambda i: (jnp.minimum(i + 1, n_steps - 1), 0, 0),
                               memory_space=pltpu.SMEM),
                  pl.BlockSpec((tm, ROUTE_W), lambda i: (i, 0)),
                  pl.BlockSpec((tm, d), lambda i: (i, 0)),
                  pl.BlockSpec(memory_space=pl.ANY)],
        out_specs=pl.BlockSpec((tm, d), lambda i: (i, 0)),
        out_shape=jax.ShapeDtypeStruct((s, d), F32),
        scratch_shapes=[pltpu.VMEM((2, TOP_K, tm, d // 2), jnp.uint32), pltpu.SemaphoreType.DMA((2, TOP_K))],
        compiler_params=_cp(("arbitrary",)),
        name="moe_combine",
    )(slot_blocks, slot_blocks, route, x1, yb)


def _dispatch_plan(route, s):
    a = s * TOP_K
    flat_e = route[:, :TOP_K].astype(jnp.int32).reshape(a)
    chunk = 128
    onehot = (flat_e[:, None] == jnp.arange(N_EXPERTS, dtype=jnp.int32)[None, :]).astype(F32)
    oh = onehot.reshape(a // chunk, chunk, N_EXPERTS)
    strict_lower = jnp.tril(jnp.ones((chunk, chunk), F32), -1)
    within = jnp.einsum('ij,bjk->bik', strict_lower, oh, precision=lax.Precision.HIGHEST)
    totals = jnp.sum(oh, axis=1)
    before = jnp.cumsum(totals, axis=0) - totals
    rank = jnp.sum((within + before[:, None, :]) * oh, axis=2).reshape(a).astype(jnp.int32)
    counts = jnp.sum(totals, axis=0).astype(jnp.int32)
    padded = (counts + MOE_TB - 1) // MOE_TB * MOE_TB
    pad_end = jnp.cumsum(padded)
    pad_start = pad_end - padded
    dest = jnp.sum(onehot * pad_start.astype(F32)[None, :], axis=1).astype(jnp.int32) + rank
    n_blocks = a // MOE_TB + N_EXPERTS
    p_rows = n_blocks * MOE_TB
    buf_tok = jnp.zeros((p_rows,), jnp.int32).at[dest].set(jnp.arange(a, dtype=jnp.int32) // TOP_K)
    starts = jnp.arange(n_blocks, dtype=jnp.int32) * MOE_TB
    block_e = jnp.minimum(jnp.searchsorted(pad_end, starts, side='right'), N_EXPERTS - 1).astype(jnp.int32)
    valid = jnp.clip(counts[block_e] - (starts - pad_start[block_e]), 0, MOE_TB)
    valid = jnp.where(starts < pad_end[-1], valid, 0)
    nsub = ((valid + MOE_RU - 1) // MOE_RU).astype(jnp.int32)
    n_used = (pad_end[-1:] // MOE_TB).astype(jnp.int32)
    return block_e, nsub, n_used, buf_tok, dest.astype(jnp.int32)


def _rope_tables(positions):
    half = MLA_ROPE_DIM // 2
    inv_freq = ROPE_THETA ** (-jnp.arange(half, dtype=F32) / half)
    ang = positions.astype(F32)[:, None] * inv_freq[None, :]
    cos, sin = jnp.cos(ang), jnp.sin(ang)
    z = jnp.zeros_like(cos)
    c = jnp.concatenate([cos, cos, z, z], axis=-1)
    s1 = jnp.concatenate([-sin, z, z, z], axis=-1)
    s2 = jnp.concatenate([z, sin, z, z], axis=-1)
    return c, s1, s2


def kernel(x, mem, positions, rel_bias, mix_norm_g, w_in, diff_q_norm_g, diff_k_norm_g, diff_lambda_q1, diff_lambda_k1, diff_lambda_q2, diff_lambda_k2, diff_subln_g, mla_cq_norm_g, mla_ckv_norm_g, mla_w_uq, mla_w_ukv, mla_q_norm_g, mla_k_norm_g, mem_norm_g, mem_w_kv, mem_q_norm_g, mem_k_norm_g, w_o_diff, w_o_mla, w_o_mem, w_out, ffn_norm_g, w_route_group, b_route_group, w_route_expert, b_route_expert, w_exp_gate, w_exp_up, w_exp_down):
    b, s, d = x.shape
    assert b == 1 and s % (2 * ATT_T) == 0 and s % MM_TM == 0
    depth = mix_norm_g.shape[0]
    xs = x.reshape(s, d)
    pos = positions.reshape(s)
    rope_tabs = _rope_tables(pos)
    row = lambda v: v.reshape(1, -1).astype(F32)

    for l in range(depth):
        lam_init = 0.8 - 0.6 * math.exp(-0.3 * l)
        h = _rmsnorm_rows(xs, mix_norm_g[l], 256)
        w_in_t = jnp.transpose(w_in[l])
        in_proj = functools.partial(_matmul, h, w_in_t, w_rows_are_outputs=True, tm=MM_TM, tn=512, out_dtype=BF16)
        q_gain = jnp.tile(diff_q_norm_g[l] * (DIFF_HEAD_DIM ** -0.5 * LOG2E), DIFF_MAPS)
        dq_t = in_proj(first=OFF_DQ, n_out=DIFF_QK_WIDTH, mode="groupnorm", extra=q_gain, group=DIFF_HEAD_DIM,
                       transpose_out=True, name="diff_q_proj")
        dk = in_proj(first=OFF_DK, n_out=DIFF_QK_WIDTH, mode="groupnorm", extra=jnp.tile(diff_k_norm_g[l], DIFF_MAPS),
                     group=DIFF_HEAD_DIM, name="diff_k_proj")
        dv_t = in_proj(first=OFF_DV, n_out=DIFF_WIDTH, transpose_out=True, name="diff_v_proj")
        rest = in_proj(first=OFF_CQ, n_out=REST_WIDTH, name="rest_proj")
        gates = in_proj(first=OFF_GATES, n_out=3 * d, mode="sigmoid", name="gate_proj")

        bias_strips = _diff_bias_strips(rel_bias, ATT_T)
        lam_vecs = [row(diff_lambda_q1[l]), row(diff_lambda_k1[l]), row(diff_lambda_q2[l]), row(diff_lambda_k2[l])]
        o_diff = _diff_attention(dq_t, dk, dv_t, bias_strips, lam_vecs, diff_subln_g[l], lam_init)

        w_uq_heads = jnp.pad(
            mla_w_uq[l].reshape(MLA_Q_RANK, MLA_HEADS, MLA_QK_DIM),
            ((0, 0), (0, 0), (0, MLA_QK_PAD - MLA_QK_DIM))).reshape(MLA_Q_RANK, MLA_HEADS * MLA_QK_PAD).astype(BF16)
        qg_pad = jnp.pad(mla_q_norm_g[l] * (MLA_QK_DIM ** -0.5 * LOG2E),
                         (0, MLA_QK_PAD - MLA_QK_DIM)).reshape(1, -1).astype(F32)
        q_mla_t = _mla_q_prep(rest, row(mla_cq_norm_g[l]), w_uq_heads, qg_pad, rope_tabs, 512)
        kg = mla_k_norm_g[l]
        kg_nope = row(kg[:MLA_NOPE_DIM])
        kg_rope = jnp.pad(kg[MLA_NOPE_DIM:], (0, LANE - MLA_ROPE_DIM)).reshape(1, -1).astype(F32)
        k_mla, v_mla_t = _mla_kv_prep(rest, row(mla_ckv_norm_g[l]), mla_w_ukv[l], kg_nope, kg_rope, rope_tabs, MM_TM)
        o_mla = _mla_attention(q_mla_t, k_mla, v_mla_t)

        n_mem = mem.shape[1]
        mem_h = _rmsnorm_rows(mem.reshape(n_mem, d), mem_norm_g[l], n_mem)
        mem_proj = functools.partial(_matmul, mem_h, mem_w_kv[l], n_out=MEM_WIDTH, w_rows_are_outputs=False,
                                     tm=n_mem, tn=512, out_dtype=BF16)
        k_mem = mem_proj(first=0, mode="groupnorm", extra=jnp.tile(mem_k_norm_g[l], MEM_HEADS), group=MEM_HEAD_DIM,
                         name="mem_k_proj")
        v_mem = mem_proj(first=MEM_WIDTH, name="mem_v_proj")
        o_mem = _mem_attention(rest, k_mem, v_mem, row(mem_q_norm_g[l] * MEM_HEAD_DIM ** -0.5), 512)

        mixed = _mix(o_diff, o_mla, o_mem, w_o_diff[l], w_o_mla[l], w_o_mem[l], gates, MM_TM, 512)
        x1 = _matmul(mixed, w_out[l], first=0, n_out=d, w_rows_are_outputs=False, tm=MM_TM, tn=512, out_dtype=F32,
                     mode="residual", extra=xs, name="out_proj")

        w_r = jnp.pad(jnp.concatenate([w_route_group[l], w_route_expert[l]], axis=1),
                      ((0, 0), (0, ROUTE_W - N_GROUPS - N_EXPERTS))).astype(F32)
        w_r_hi = w_r.astype(BF16)
        w_r = jnp.stack([w_r_hi, (w_r - w_r_hi.astype(F32)).astype(BF16)])
        b_r = jnp.pad(jnp.concatenate([b_route_group[l], b_route_expert[l]]),
                      (0, ROUTE_W - N_GROUPS - N_EXPERTS)).reshape(1, -1).astype(F32)
        h2, route = _router(x1, row(ffn_norm_g[l]), w_r, b_r, 256)
        block_e, nsub, n_used, buf_tok, slots = _dispatch_plan(route, s)
        yb = _moe_experts(block_e, nsub, n_used, buf_tok, h2, w_exp_gate[l], w_exp_up[l], w_exp_down[l])
        xs = _combine(slots, route, x1, yb, 256)
    return xs.reshape(b, s, d)
```
